```python
import math
import jax, jax.numpy as jnp
from jax import lax
import numpy as np

D_MODEL = 2048
BATCH = 4
SEQ = 4096
DEPTH = 2

HEAD_DIM = 128
ATTN_WIDTH = D_MODEL // 2
N_HEADS_A = ATTN_WIDTH // HEAD_DIM
IDX_HEADS = 16
IDX_DIM = 64
INDEX_TOPK = 256
Q_BLOCK = 128
REL_BUCKETS = 32
REL_MAX_DIST = 128
CONV_CH = D_MODEL // 2
CONV_WIDTH = 31
POOL_WINDOWS = (2, 4, 8, 16)
POOL_GROUPS = len(POOL_WINDOWS)
POOL_CH = D_MODEL // POOL_GROUPS
N_GROUPS = 4
EXPERTS_PER_GROUP = 4
N_EXPERTS = N_GROUPS * EXPERTS_PER_GROUP
EXPERT_FF = D_MODEL // 4
INNER_TOPK = 2
IN_SPLITS = (ATTN_WIDTH, HEAD_DIM, HEAD_DIM, IDX_HEADS * IDX_DIM, IDX_DIM, IDX_HEADS, 2 * CONV_CH)
IN_WIDTH = sum(IN_SPLITS)
MIX_WIDTH = ATTN_WIDTH + CONV_CH
N_EVEN = (DEPTH + 1) // 2
N_ODD = DEPTH // 2
NORM_EPS = 1e-6

kernel_name = "dsa_conformer_pool_hmoe_hybrid"


def rms_norm(x, w):
    xf = x.astype(jnp.float32)
    y = xf * lax.rsqrt(jnp.mean(xf * xf, axis=-1, keepdims=True) + NORM_EPS)
    return (y * w.astype(jnp.float32)).astype(x.dtype)


def rel_bucket(dist):
    n = jnp.maximum(dist, 0)
    max_exact = REL_BUCKETS // 2
    nf = jnp.maximum(n, 1).astype(jnp.float32)
    large = max_exact + (jnp.log(nf / max_exact) / math.log(REL_MAX_DIST / max_exact)
                         * (REL_BUCKETS - max_exact)).astype(jnp.int32)
    large = jnp.minimum(large, REL_BUCKETS - 1)
    return jnp.where(n < max_exact, n, large)


def dsa_attention(q, k, v, iq, ik, iw, rel_bias, k_top):
    B, S = q.shape[0], q.shape[1]
    nb = S // Q_BLOCK

    def to_blocks(a):
        return jnp.moveaxis(a.reshape((B, nb, Q_BLOCK) + a.shape[2:]), 1, 0)

    pos = jnp.arange(S)
    starts = jnp.arange(nb) * Q_BLOCK

    def block(args):
        qb, iqb, iwb, t0 = args
        t = t0 + jnp.arange(Q_BLOCK)
        il = jnp.einsum('bqhd,bsd->bqhs', iqb, ik, preferred_element_type=jnp.float32) * (IDX_DIM ** -0.5)
        score = jnp.einsum('bqhs,bqh->bqs', jax.nn.relu(il), iwb.astype(jnp.float32))
        causal = pos[None, :] <= t[:, None]
        score = jnp.where(causal[None], score, -jnp.inf)
        _, sel = lax.top_k(score, k_top)
        valid = sel <= t[None, :, None]
        k_sel = jax.vmap(lambda a, i: a[i])(k, sel)
        v_sel = jax.vmap(lambda a, i: a[i])(v, sel)
        logits = jnp.einsum('bqhd,bqkd->bqhk', qb, k_sel, preferred_element_type=jnp.float32) * (HEAD_DIM ** -0.5)
        bias = rel_bias[rel_bucket(t[None, :, None] - sel)]
        logits = logits + jnp.moveaxis(bias, -1, 2).astype(jnp.float32)
        logits = jnp.where(valid[:, :, None, :], logits, -jnp.inf)
        p = jax.nn.softmax(logits, axis=-1)
        o = jnp.einsum('bqhk,bqkd->bqhd', p.astype(v.dtype), v_sel)
        return o.reshape(B, Q_BLOCK, N_HEADS_A * HEAD_DIM)

    out = lax.map(block, (to_blocks(q), to_blocks(iq), to_blocks(iw), starts))
    return jnp.moveaxis(out, 0, 1).reshape(B, S, N_HEADS_A * HEAD_DIM)


def conformer_conv(glu_in, conv_w, conv_b, ln_g, ln_b):
    a, g = jnp.split(glu_in, 2, axis=-1)
    u = a * jax.nn.sigmoid(g)
    u = lax.conv_general_dilated(u, conv_w[:, None, :].astype(u.dtype), window_strides=(1,),
                                 padding=[(CONV_WIDTH - 1, 0)],
                                 dimension_numbers=('NWC', 'WIO', 'NWC'),
                                 feature_group_count=CONV_CH)
    uf = u.astype(jnp.float32) + conv_b.astype(jnp.float32)
    mu = jnp.mean(uf, axis=-1, keepdims=True)
    var = jnp.mean(jnp.square(uf - mu), axis=-1, keepdims=True)
    uf = (uf - mu) * lax.rsqrt(var + NORM_EPS) * ln_g.astype(jnp.float32) + ln_b.astype(jnp.float32)
    return jax.nn.silu(uf).astype(glu_in.dtype)


def even_layer(x, rel_bias, norm_w, w_in, q_norm_w, k_norm_w, conv_w, conv_b, ln_g, ln_b, w_out, k_top):
    B, S, _ = x.shape
    h = rms_norm(x, norm_w)
    proj = h @ w_in
    cuts = [int(c) for c in np.cumsum(IN_SPLITS)[:-1]]
    q, k, v, iq, ik, iw, glu_in = jnp.split(proj, cuts, axis=-1)
    q = rms_norm(q.reshape(B, S, N_HEADS_A, HEAD_DIM), q_norm_w)
    k = rms_norm(k, k_norm_w)
    iq = iq.reshape(B, S, IDX_HEADS, IDX_DIM)
    iw = iw * (IDX_HEADS ** -0.5)
    attn = dsa_attention(q, k, v, iq, ik, iw, rel_bias, k_top)
    conv = conformer_conv(glu_in, conv_w, conv_b, ln_g, ln_b)
    return x + jnp.concatenate([attn, conv], axis=-1) @ w_out


def odd_layer(x, norm_w, pool_w, pool_scale):
    B, S, _ = x.shape
    h = rms_norm(x, norm_w)
    hf = h.astype(jnp.float32).reshape(B, S, POOL_GROUPS, POOL_CH)
    c = jnp.concatenate([jnp.zeros((B, 1, POOL_GROUPS, POOL_CH), jnp.float32),
                         jnp.cumsum(hf, axis=1)], axis=1)
    t = jnp.arange(S)
    outs = []
    for g, w in enumerate(POOL_WINDOWS):
        cg = c[:, :, g]
        prev = jnp.maximum(t + 1 - w, 0)
        count = jnp.minimum(t + 1, w).astype(jnp.float32)[None, :, None]
        mean = (cg[:, 1:] - cg[:, prev]) / count
        outs.append(mean - hf[:, :, g])
    d = jnp.stack(outs, axis=2).astype(x.dtype)
    mixed = jnp.einsum('bsgc,gce->bsge', d, pool_w).reshape(B, S, D_MODEL)
    return x + pool_scale * mixed


def hier_moe(x, norm_w, wg, bg, we, be, w_gate, w_up, w_down):
    B, S, D = x.shape
    h = rms_norm(x, norm_w).reshape(-1, D)
    T = h.shape[0]
    g_prob = jax.nn.softmax((h @ wg).astype(jnp.float32) + bg.astype(jnp.float32), axis=-1)
    g_p, g_idx = lax.top_k(g_prob, 1)
    e_logits = ((h @ we).astype(jnp.float32) + be.astype(jnp.float32)).reshape(T, N_GROUPS, EXPERTS_PER_GROUP)
    e_logits = jnp.take_along_axis(e_logits, g_idx[:, :, None], axis=1)[:, 0]
    e_p, e_idx = lax.top_k(jax.nn.softmax(e_logits, axis=-1), INNER_TOPK)
    e_p = e_p / jnp.sum(e_p, axis=-1, keepdims=True)
    gate = g_p * e_p
    flat = g_idx * EXPERTS_PER_GROUP + e_idx
    combine = jnp.sum(jax.nn.one_hot(flat, N_EXPERTS, dtype=jnp.float32) * gate[..., None], axis=1)
    y = jnp.zeros((T, D), jnp.float32)
    for e in range(N_EXPERTS):
        a = jax.nn.silu(h @ w_gate[e]) * (h @ w_up[e])
        y = y + combine[:, e:e + 1] * (a @ w_down[e]).astype(jnp.float32)
    return x + y.reshape(B, S, D).astype(x.dtype)


def setup_inputs(seed: int = 0) -> dict:
    key = jax.random.key(seed)
    ks = jax.random.split(key, 24)

    def nrm(k, shape, scale):
        return jax.random.normal(k, shape, jnp.float32) * scale

    def gain(k, shape):
        return 1.0 + 0.05 * jax.random.normal(k, shape, jnp.float32)

    return {
        "x": nrm(ks[0], (BATCH, SEQ, D_MODEL), 1.0),
        "rel_bias": nrm(ks[1], (REL_BUCKETS, N_HEADS_A), 0.5),
        "mix_norm_e": gain(ks[2], (N_EVEN, D_MODEL)),
        "w_in_e": nrm(ks[3], (N_EVEN, D_MODEL, IN_WIDTH), D_MODEL ** -0.5),
        "q_norm_e": gain(ks[4], (N_EVEN, HEAD_DIM)),
        "k_norm_e": gain(ks[5], (N_EVEN, HEAD_DIM)),
        "conv_w_e": nrm(ks[6], (N_EVEN, CONV_WIDTH, CONV_CH), CONV_WIDTH ** -0.5),
        "conv_b_e": nrm(ks[7], (N_EVEN, CONV_CH), 0.02),
        "conv_ln_g_e": gain(ks[8], (N_EVEN, CONV_CH)),
        "conv_ln_b_e": nrm(ks[9], (N_EVEN, CONV_CH), 0.02),
        "w_out_e": nrm(ks[10], (N_EVEN, MIX_WIDTH, D_MODEL), MIX_WIDTH ** -0.5),
        "mix_norm_o": gain(ks[11], (N_ODD, D_MODEL)),
        "pool_w_o": nrm(ks[12], (N_ODD, POOL_GROUPS, POOL_CH, POOL_CH), POOL_CH ** -0.5),
        "pool_scale_o": gain(ks[13], (N_ODD, D_MODEL)),
        "ffn_norm": gain(ks[14], (DEPTH, D_MODEL)),
        "router_group_w": nrm(ks[15], (DEPTH, D_MODEL, N_GROUPS), D_MODEL ** -0.5),
        "router_group_b": nrm(ks[16], (DEPTH, N_GROUPS), 0.01),
        "router_expert_w": nrm(ks[17], (DEPTH, D_MODEL, N_EXPERTS), D_MODEL ** -0.5),
        "router_expert_b": nrm(ks[18], (DEPTH, N_EXPERTS), 0.01),
        "w_gate": nrm(ks[19], (DEPTH, N_EXPERTS, D_MODEL, EXPERT_FF), D_MODEL ** -0.5),
        "w_up": nrm(ks[20], (DEPTH, N_EXPERTS, D_MODEL, EXPERT_FF), D_MODEL ** -0.5),
        "w_down": nrm(ks[21], (DEPTH, N_EXPERTS, EXPERT_FF, D_MODEL), EXPERT_FF ** -0.5),
    }


def reference(x, rel_bias, mix_norm_e, w_in_e, q_norm_e, k_norm_e, conv_w_e, conv_b_e,
              conv_ln_g_e, conv_ln_b_e, w_out_e, mix_norm_o, pool_w_o, pool_scale_o,
              ffn_norm, router_group_w, router_group_b, router_expert_w, router_expert_b,
              w_gate, w_up, w_down):
    k_top = min(INDEX_TOPK, x.shape[1] // 4)
    for l in range(DEPTH):
        i = l // 2
        if l % 2 == 0:
            x = even_layer(x, rel_bias, mix_norm_e[i], w_in_e[i], q_norm_e[i], k_norm_e[i],
                           conv_w_e[i], conv_b_e[i], conv_ln_g_e[i], conv_ln_b_e[i], w_out_e[i], k_top)
        else:
            x = odd_layer(x, mix_norm_o[i], pool_w_o[i], pool_scale_o[i])
        x = hier_moe(x, ffn_norm[l], router_group_w[l], router_group_b[l], router_expert_w[l],
                     router_expert_b[l], w_gate[l], w_up[l], w_down[l])
    return x
```

```python
import functools
import math

import jax
import jax.numpy as jnp
from jax import lax
from jax.experimental import pallas as pl
from jax.experimental.pallas import tpu as pltpu

F32 = jnp.float32
BF16 = jnp.bfloat16

NORM_EPS = 1e-6
HEAD_DIM = 128
IDX_DIM = 64
INDEX_TOPK = 256
REL_BUCKETS = 32
REL_MAX_DIST = 128
POOL_WINDOWS = (2, 4, 8, 16)
EXPERTS_PER_GROUP = 4
LANES = 128
VMEM_LIMIT = 56 * 1024 * 1024
NEG = -1e30
INT_MIN = -(2 ** 31)


def _dot(a, b):
    return jnp.dot(a, b, preferred_element_type=F32)


def _dot_nt(a, b):
    return lax.dot_general(a, b, (((1,), (1,)), ((), ())), preferred_element_type=F32)


def _rms(x, w):
    return x * lax.rsqrt(jnp.mean(x * x, axis=-1, keepdims=True) + NORM_EPS) * w


def _params(*sem):
    return pltpu.CompilerParams(dimension_semantics=sem, vmem_limit_bytes=VMEM_LIMIT)


def _resident(shape):
    nd = len(shape)
    return pl.BlockSpec(shape, lambda *_: (0,) * nd, pipeline_mode=pl.Buffered(1))


def _in_proj_kernel(x_ref, nw_ref, qn_ref, kn_ref, wq_ref, wkv_ref, wiq_ref, wikw_ref, wa_ref, wg_ref,
                    q_ref, k_ref, v_ref, iq_ref, ik_ref, iw_ref, u_ref, *, q_scale, iw_scale):
    h = _rms(x_ref[...], nw_ref[...]).astype(BF16)
    n_pairs = q_ref.shape[0] // 2
    qn = qn_ref[...] * q_scale
    for c in range(n_pairs):
        qq = _dot(h, wq_ref[:, c * 256:(c + 1) * 256])
        for s in range(2):
            qh = qq[:, s * HEAD_DIM:(s + 1) * HEAD_DIM]
            q_ref[2 * c + s] = _rms(qh, qn).astype(BF16)
    kv = _dot(h, wkv_ref[...])
    k_ref[...] = _rms(kv[:, :HEAD_DIM], kn_ref[...]).astype(BF16)
    v_ref[...] = kv[:, HEAD_DIM:].astype(BF16)
    for c in range(iq_ref.shape[0] // 2):
        r = _dot(h, wiq_ref[:, c * 256:(c + 1) * 256])
        iq_ref[2 * c] = r[:, :LANES].astype(BF16)
        iq_ref[2 * c + 1] = r[:, LANES:].astype(BF16)
    r = _dot(h, wikw_ref[...])
    ik_ref[...] = r[:, :LANES].astype(BF16)
    iw_ref[...] = r[:, LANES:] * iw_scale
    for c in range(u_ref.shape[1] // 256):
        cs = slice(c * 256, (c + 1) * 256)
        a = _dot(h, wa_ref[:, cs])
        g = _dot(h, wg_ref[:, cs])
        u_ref[:, cs] = (a * jax.nn.sigmoid(g)).astype(BF16)


def _in_proj(x, norm_w, w_in, q_norm, k_norm, n_heads, idx_heads, conv_ch, tm=256):
    T, D = x.shape
    attn_w = n_heads * HEAD_DIM
    iq_w = idx_heads * IDX_DIM
    o = 0
    wq = w_in[:, o:o + attn_w]; o += attn_w
    wkv = w_in[:, o:o + 2 * HEAD_DIM]; o += 2 * HEAD_DIM
    wiq = w_in[:, o:o + iq_w]; o += iq_w
    wik = w_in[:, o:o + IDX_DIM]; o += IDX_DIM
    wiw = w_in[:, o:o + idx_heads]; o += idx_heads
    wa = w_in[:, o:o + conv_ch]; o += conv_ch
    wg = w_in[:, o:o + conv_ch]; o += conv_ch
    assert o == w_in.shape[1] and 2 * IDX_DIM == LANES
    wikw = jnp.concatenate([wik, wik, wiw, jnp.zeros((D, LANES - idx_heads), w_in.dtype)], axis=1)
    ws = [w.astype(BF16) for w in (wq, wkv, wiq, wikw, wa, wg)]
    row = lambda w: pl.BlockSpec((tm, w), lambda i: (i, 0))
    heads = lambda n: pl.BlockSpec((n, tm, LANES), lambda i: (0, i, 0))
    kern = functools.partial(_in_proj_kernel, q_scale=HEAD_DIM ** -0.5,
                             iw_scale=(idx_heads ** -0.5) * (IDX_DIM ** -0.5))
    return pl.pallas_call(
        kern,
        grid=(T // tm,),
        in_specs=[row(D), _resident((1, D)), _resident((1, HEAD_DIM)), _resident((1, HEAD_DIM))]
                 + [_resident(w.shape) for w in ws],
        out_specs=[heads(n_heads), row(HEAD_DIM), row(HEAD_DIM), heads(iq_w // LANES), row(LANES), row(LANES),
                   row(conv_ch)],
        out_shape=[jax.ShapeDtypeStruct((n_heads, T, HEAD_DIM), BF16),
                   jax.ShapeDtypeStruct((T, HEAD_DIM), BF16),
                   jax.ShapeDtypeStruct((T, HEAD_DIM), BF16),
                   jax.ShapeDtypeStruct((iq_w // LANES, T, LANES), BF16),
                   jax.ShapeDtypeStruct((T, LANES), BF16),
                   jax.ShapeDtypeStruct((T, LANES), F32),
                   jax.ShapeDtypeStruct((T, conv_ch), BF16)],
        compiler_params=_params("arbitrary"),
    )(x, norm_w.reshape(1, D), q_norm.reshape(1, HEAD_DIM), k_norm.reshape(1, HEAD_DIM), *ws)


def _rel_bias_kernel(rb_ref, o_ref):
    _, n_heads, C, _ = o_ref.shape
    tau = lax.broadcasted_iota(jnp.int32, (C, C), 0)
    sig = lax.broadcasted_iota(jnp.int32, (C, C), 1)
    max_exact = REL_BUCKETS // 2
    for kind in range(2):
        d = tau - sig + kind * C
        n = jnp.maximum(d, 0)
        nf = jnp.maximum(n, 1).astype(F32)
        large = max_exact + (jnp.log(nf / max_exact) / math.log(REL_MAX_DIST / max_exact)
                             * (REL_BUCKETS - max_exact)).astype(jnp.int32)
        large = jnp.minimum(large, REL_BUCKETS - 1)
        bucket = jnp.where(n < max_exact, n, large)
        for h in range(n_heads):
            b = jnp.zeros((C, C), F32)
            for bk in range(REL_BUCKETS):
                b = jnp.where(bucket == bk, rb_ref[bk, h], b)
            b = b - rb_ref[REL_BUCKETS - 1, h]
            if kind == 0:
                b = jnp.where(d < 0, NEG, b)
            o_ref[kind, h] = b
    o_ref[2] = jnp.zeros(o_ref.shape[1:], F32)


def _rel_bias_tiles(rel_bias, C):
    n_heads = rel_bias.shape[1]
    assert C >= REL_MAX_DIST
    return pl.pallas_call(
        _rel_bias_kernel,
        in_specs=[pl.BlockSpec(memory_space=pltpu.SMEM)],
        out_shape=jax.ShapeDtypeStruct((3, n_heads, C, C), F32),
        compiler_params=pltpu.CompilerParams(vmem_limit_bytes=VMEM_LIMIT),
    )(rel_bias)


def _attn_kernel(q_ref, k_ref, v_ref, iq_ref, ik_ref, iw_ref, bias_ref, o_ref,
                 key_sc, iqm_sc, m_sc, l_sc, acc_sc, *, k_top):
    n_heads, C, _ = q_ref.shape
    idx_heads = iqm_sc.shape[0]
    qi = pl.program_id(1)
    nkv = qi + 1

    lane = lax.broadcasted_iota(jnp.int32, (C, LANES), 1)
    for p in range(idx_heads // 2):
        qp = iq_ref[p].astype(F32)
        iqm_sc[2 * p] = jnp.where(lane < IDX_DIM, qp, 0.0).astype(BF16)
        iqm_sc[2 * p + 1] = jnp.where(lane >= IDX_DIM, qp, 0.0).astype(BF16)
    iw = iw_ref[...]
    tau = lax.broadcasted_iota(jnp.int32, (C, C), 0)
    sig = lax.broadcasted_iota(jnp.int32, (C, C), 1)

    def score_body(j, carry):
        off = pl.multiple_of(j * C, C)
        ikc = ik_ref[pl.ds(off, C), :]
        acc = jnp.zeros((C, C), F32)
        for hh in range(idx_heads):
            s = _dot_nt(iqm_sc[hh], ikc)
            acc = acc + jnp.maximum(s, 0.0) * iw[:, hh:hh + 1]
        acc = jnp.where(jnp.logical_and(j == qi, sig > tau), -jnp.inf, acc)
        bits = pltpu.bitcast(acc, jnp.int32)
        key_sc[j] = bits ^ ((bits >> 31) & 0x7FFFFFFF)
        return carry

    lax.fori_loop(0, nkv, score_body, 0)

    def bit_body(b, res):
        cand = res ^ lax.shift_left(jnp.int32(1), 31 - b)

        def cnt_body(j, cnt):
            ge = jnp.where(key_sc[j] >= cand, 1.0, 0.0)
            for t in range(C // LANES):
                cnt = cnt + ge[:, t * LANES:(t + 1) * LANES]
            return cnt

        cnt = lax.fori_loop(0, nkv, cnt_body, jnp.zeros((C, LANES), F32))
        tot = jnp.sum(cnt, axis=-1, keepdims=True)
        return jnp.where(tot >= k_top, cand, res)

    thr = lax.fori_loop(0, 32, bit_body, jnp.full((C, 1), INT_MIN, jnp.int32))

    m_sc[...] = jnp.full(m_sc.shape, NEG, F32)
    l_sc[...] = jnp.zeros(l_sc.shape, F32)
    acc_sc[...] = jnp.zeros(acc_sc.shape, F32)

    def attn_body(j, carry):
        off = pl.multiple_of(j * C, C)
        kc = k_ref[pl.ds(off, C), :]
        vc = v_ref[pl.ds(off, C), :]
        sel = key_sc[j] >= thr
        kind = jnp.minimum(qi - j, 2)
        for h in range(n_heads):
            lg = _dot_nt(q_ref[h], kc) + bias_ref[kind, h]
            lg = jnp.where(sel, lg, NEG)
            m_old = m_sc[h]
            m_new = jnp.maximum(m_old, jnp.max(lg, axis=-1, keepdims=True))
            p = jnp.exp(lg - m_new)
            alpha = jnp.exp(m_old - m_new)
            l_sc[h] = alpha * l_sc[h] + jnp.sum(p, axis=-1, keepdims=True)
            acc_sc[h] = alpha * acc_sc[h] + _dot(p.astype(BF16), vc)
            m_sc[h] = m_new
        return carry

    lax.fori_loop(0, nkv, attn_body, 0)
    for h in range(n_heads):
        o_ref[:, h * HEAD_DIM:(h + 1) * HEAD_DIM] = (acc_sc[h] / l_sc[h]).astype(BF16)


def _attention(q, k, v, iq, ik, iw, bias_tiles, B, S, k_top, C):
    n_heads, T, _ = q.shape
    n_pairs = iq.shape[0]
    nq = S // C
    kern = functools.partial(_attn_kernel, k_top=k_top)
    heads = lambda n: pl.BlockSpec((n, C, LANES), lambda b, i: (0, b * nq + i, 0))
    seq = pl.BlockSpec((S, LANES), lambda b, i: (b, 0))
    return pl.pallas_call(
        kern,
        grid=(B, nq),
        in_specs=[heads(n_heads), seq, seq, heads(n_pairs), seq,
                  pl.BlockSpec((C, LANES), lambda b, i: (b * nq + i, 0)),
                  _resident(bias_tiles.shape)],
        out_specs=pl.BlockSpec((C, n_heads * HEAD_DIM), lambda b, i: (b * nq + i, 0)),
        out_shape=jax.ShapeDtypeStruct((T, n_heads * HEAD_DIM), BF16),
        scratch_shapes=[pltpu.VMEM((nq, C, C), jnp.int32),
                        pltpu.VMEM((2 * n_pairs, C, LANES), BF16),
                        pltpu.VMEM((n_heads, C, 1), F32),
                        pltpu.VMEM((n_heads, C, 1), F32),
                        pltpu.VMEM((n_heads, C, HEAD_DIM), F32)],
        compiler_params=_params("arbitrary", "arbitrary"),
    )(q, k, v, iq, ik, iw, bias_tiles)


CONV_HALO = 32


def _conv_kernel(u_ref, halo_ref, w_ref, cb_ref, g_ref, b_ref, o_ref, buf, y_sc, *, width):
    TS, CH = u_ref.shape
    i = pl.program_id(1)
    buf[0:CONV_HALO] = jnp.where(i > 0, halo_ref[...].astype(F32), 0.0)
    buf[CONV_HALO:] = u_ref[...].astype(F32)
    base = CONV_HALO - (width - 1)
    for c in range(CH // LANES):
        cs = slice(c * LANES, (c + 1) * LANES)
        acc = jnp.zeros((TS, LANES), F32)
        for j in range(width):
            acc = acc + w_ref[j:j + 1, cs] * buf[base + j:base + j + TS, cs]
        y_sc[:, cs] = acc + cb_ref[:, cs]
    y = y_sc[...]
    mu = jnp.mean(y, axis=-1, keepdims=True)
    yc = y - mu
    var = jnp.mean(yc * yc, axis=-1, keepdims=True)
    yn = yc * lax.rsqrt(var + NORM_EPS) * g_ref[...] + b_ref[...]
    o_ref[...] = (yn * jax.nn.sigmoid(yn)).astype(BF16)


def _conv_module(u, conv_w, conv_b, ln_g, ln_b, B, S, TS=256):
    T, CH = u.shape
    width = conv_w.shape[0]
    assert width - 1 <= CONV_HALO
    ns = S // TS
    r = TS // CONV_HALO
    wpad = jnp.zeros((CONV_HALO, CH), F32).at[:width].set(conv_w)
    vec = lambda a: a.reshape(1, CH)
    kern = functools.partial(_conv_kernel, width=width)
    return pl.pallas_call(
        kern,
        grid=(B, ns),
        in_specs=[pl.BlockSpec((TS, CH), lambda b, i: (b * ns + i, 0)),
                  pl.BlockSpec((CONV_HALO, CH), lambda b, i: (jnp.maximum((b * ns + i) * r - 1, 0), 0)),
                  _resident((CONV_HALO, CH)), _resident((1, CH)), _resident((1, CH)), _resident((1, CH))],
        out_specs=pl.BlockSpec((TS, CH), lambda b, i: (b * ns + i, 0)),
        out_shape=jax.ShapeDtypeStruct((T, CH), BF16),
        scratch_shapes=[pltpu.VMEM((TS + CONV_HALO, CH), F32), pltpu.VMEM((TS, CH), F32)],
        compiler_params=_params("arbitrary", "arbitrary"),
    )(u, u, wpad, vec(conv_b), vec(ln_g), vec(ln_b))


def _out_proj_kernel(x_ref, a_ref, c_ref, wa_ref, wc_ref, o_ref):
    o_ref[...] = x_ref[...] + _dot(a_ref[...], wa_ref[...]) + _dot(c_ref[...], wc_ref[...])


def _out_proj(x, attn, conv, w_out, tm=512):
    T, D = x.shape
    aw, cw = attn.shape[1], conv.shape[1]
    wa = w_out[:aw].astype(BF16)
    wc = w_out[aw:].astype(BF16)
    row = lambda w: pl.BlockSpec((tm, w), lambda i: (i, 0))
    return pl.pallas_call(
        _out_proj_kernel,
        grid=(T // tm,),
        in_specs=[row(D), row(aw), row(cw), _resident(wa.shape), _resident(wc.shape)],
        out_specs=row(D),
        out_shape=jax.ShapeDtypeStruct((T, D), F32),
        compiler_params=_params("arbitrary"),
    )(x, attn, conv, wa, wc)


def _first_max(vals):
    m = vals[0]
    for v in vals[1:]:
        m = jnp.maximum(m, v)
    idx = jnp.full(m.shape, len(vals) - 1, jnp.int32)
    for k in range(len(vals) - 2, -1, -1):
        idx = jnp.where(vals[k] == m, k, idx)
    return m, idx


def _softmax_cols(cols):
    m = cols[0]
    for c in cols[1:]:
        m = jnp.maximum(m, c)
    e = [jnp.exp(c - m) for c in cols]
    s = e[0]
    for c in e[1:]:
        s = s + c
    return [c / s for c in e]


def _router_kernel(x_ref, nw_ref, wr_ref, br_ref, h_ref, comb_ref, *, n_groups):
    h = _rms(x_ref[...], nw_ref[...]).astype(BF16)
    h_ref[...] = h
    lg = _dot(h, wr_ref[...]) + br_ref[...]
    col = lambda k: lg[:, k:k + 1]
    g_prob = _softmax_cols([col(g) for g in range(n_groups)])
    g_p, g_idx = _first_max(g_prob)
    e_logit = []
    for k in range(EXPERTS_PER_GROUP):
        v = col(n_groups + k)
        for g in range(1, n_groups):
            v = jnp.where(g_idx == g, col(n_groups + g * EXPERTS_PER_GROUP + k), v)
        e_logit.append(v)
    e_prob = _softmax_cols(e_logit)
    p1, i1 = _first_max(e_prob)
    rest = [jnp.where(i1 == k, -1.0, e_prob[k]) for k in range(EXPERTS_PER_GROUP)]
    p2, i2 = _first_max(rest)
    den = p1 + p2
    lane = lax.broadcasted_iota(jnp.int32, lg.shape, 1)
    base = g_idx * EXPERTS_PER_GROUP
    comb_ref[...] = (jnp.where(lane == base + i1, g_p * (p1 / den), 0.0)
                     + jnp.where(lane == base + i2, g_p * (p2 / den), 0.0))


def _router(x, norm_w, wg, bg, we, be, tm=512):
    T, D = x.shape
    n_groups, n_exp = wg.shape[1], we.shape[1]
    pad = LANES - n_groups - n_exp
    wr = jnp.concatenate([wg, we, jnp.zeros((D, pad), F32)], axis=1).astype(BF16)
    br = jnp.concatenate([bg, be, jnp.zeros((pad,), F32)]).reshape(1, LANES)
    row = lambda w: pl.BlockSpec((tm, w), lambda i: (i, 0))
    return pl.pallas_call(
        functools.partial(_router_kernel, n_groups=n_groups),
        grid=(T // tm,),
        in_specs=[row(D), _resident((1, D)), _resident((D, LANES)), _resident((1, LANES))],
        out_specs=[row(D), row(LANES)],
        out_shape=[jax.ShapeDtypeStruct((T, D), BF16), jax.ShapeDtypeStruct((T, LANES), F32)],
        compiler_params=_params("arbitrary"),
    )(x, norm_w.reshape(1, D), wr, br)


def _moe_kernel(x_ref, h_ref, comb_ref, wg_ref, wu_ref, wd_ref, o_ref, acc_sc):
    e = pl.program_id(1)

    @pl.when(e == 0)
    def _():
        acc_sc[...] = x_ref[...]

    h = h_ref[...]
    lane = lax.broadcasted_iota(jnp.int32, comb_ref.shape, 1)
    c = jnp.sum(jnp.where(lane == e, comb_ref[...], 0.0), axis=-1, keepdims=True)
    a = jax.nn.silu(_dot(h, wg_ref[0])) * _dot(h, wu_ref[0])
    acc_sc[...] += _dot((a * c).astype(BF16), wd_ref[0])

    @pl.when(e == pl.num_programs(1) - 1)
    def _():
        o_ref[...] = acc_sc[...]


def _moe(x, h, comb, w_gate, w_up, w_down, tm=512):
    T, D = x.shape
    n_exp, _, ff = w_gate.shape
    row = lambda w: pl.BlockSpec((tm, w), lambda i, e: (i, 0))
    return pl.pallas_call(
        _moe_kernel,
        grid=(T // tm, n_exp),
        in_specs=[row(D), row(D), row(LANES),
                  pl.BlockSpec((1, D, ff), lambda i, e: (e, 0, 0)),
                  pl.BlockSpec((1, D, ff), lambda i, e: (e, 0, 0)),
                  pl.BlockSpec((1, ff, D), lambda i, e: (e, 0, 0))],
        out_specs=row(D),
        out_shape=jax.ShapeDtypeStruct((T, D), F32),
        scratch_shapes=[pltpu.VMEM((tm, D), F32)],
        compiler_params=_params("arbitrary", "arbitrary"),
    )(x, h, comb, w_gate, w_up, w_down)


POOL_HALO = 16


def _pool_kernel(x_ref, halo_ref, nw_ref, pw_ref, ps_ref, o_ref, hb):
    TS, D = x_ref.shape
    n_groups, pc, _ = pw_ref.shape
    i = pl.program_id(1)
    nw = nw_ref[...]
    hb[0:POOL_HALO] = jnp.where(i > 0, _rms(halo_ref[...], nw), 0.0)
    hb[POOL_HALO:] = _rms(x_ref[...], nw)
    t = i * TS + lax.broadcasted_iota(jnp.int32, (TS, 1), 0)
    for g, w in enumerate(POOL_WINDOWS):
        cs = slice(g * pc, (g + 1) * pc)
        cur = hb[POOL_HALO:, cs]
        s = cur
        for r in range(1, w):
            s = s + hb[POOL_HALO - r:POOL_HALO - r + TS, cs]
        count = jnp.minimum(t + 1, w).astype(F32)
        d = s / count - cur
        mixed = _dot(d.astype(BF16), pw_ref[g])
        o_ref[:, cs] = x_ref[:, cs] + ps_ref[:, cs] * mixed


def _pool_layer(x, norm_w, pool_w, pool_scale, B, S, TS=256):
    T, D = x.shape
    assert len(POOL_WINDOWS) == pool_w.shape[0] and max(POOL_WINDOWS) - 1 <= POOL_HALO
    ns = S // TS
    r = TS // POOL_HALO
    pw = pool_w.astype(BF16)
    return pl.pallas_call(
        _pool_kernel,
        grid=(B, ns),
        in_specs=[pl.BlockSpec((TS, D), lambda b, i: (b * ns + i, 0)),
                  pl.BlockSpec((POOL_HALO, D), lambda b, i: (jnp.maximum((b * ns + i) * r - 1, 0), 0)),
                  _resident((1, D)), _resident(pw.shape), _resident((1, D))],
        out_specs=pl.BlockSpec((TS, D), lambda b, i: (b * ns + i, 0)),
        out_shape=jax.ShapeDtypeStruct((T, D), F32),
        scratch_shapes=[pltpu.VMEM((TS + POOL_HALO, D), F32)],
        compiler_params=_params("arbitrary", "arbitrary"),
    )(x, x, norm_w.reshape(1, D), pw, pool_scale.reshape(1, D))


def _chunk(S):
    return 256 if S % 256 == 0 else 128


def kernel(x, rel_bias, mix_norm_e, w_in_e, q_norm_e, k_norm_e, conv_w_e, conv_b_e, conv_ln_g_e, conv_ln_b_e,
           w_out_e, mix_norm_o, pool_w_o, pool_scale_o, ffn_norm, router_group_w, router_group_b,
           router_expert_w, router_expert_b, w_gate, w_up, w_down):
    B, S, D = x.shape
    T = B * S
    depth = ffn_norm.shape[0]
    n_heads = rel_bias.shape[1]
    idx_heads = (w_in_e.shape[2] - n_heads * HEAD_DIM - 2 * HEAD_DIM - IDX_DIM - 2 * conv_w_e.shape[2]) \
        // (IDX_DIM + 1)
    k_top = min(INDEX_TOPK, S // 4)
    C = _chunk(S)
    xf = x.reshape(T, D)
    bias_tiles = _rel_bias_tiles(rel_bias, C)
    for l in range(depth):
        i = l // 2
        if l % 2 == 0:
            q, k, v, iq, ik, iw, u = _in_proj(xf, mix_norm_e[i], w_in_e[i], q_norm_e[i], k_norm_e[i],
                                              n_heads, idx_heads, conv_w_e.shape[2])
            attn = _attention(q, k, v, iq, ik, iw, bias_tiles, B, S, k_top, C)
            conv = _conv_module(u, conv_w_e[i], conv_b_e[i], conv_ln_g_e[i], conv_ln_b_e[i], B, S)
            xf = _out_proj(xf, attn, conv, w_out_e[i])
        else:
            xf = _pool_layer(xf, mix_norm_o[i], pool_w_o[i], pool_scale_o[i], B, S)
        h, comb = _router(xf, ffn_norm[l], router_group_w[l], router_group_b[l],
                          router_expert_w[l], router_expert_b[l])
        xf = _moe(xf, h, comb, w_gate[l].astype(BF16), w_up[l].astype(BF16), w_down[l].astype(BF16))
    return xf.reshape(B, S, D)
```

```python
import functools
import math

import jax
import jax.numpy as jnp
from jax import lax
from jax.experimental import pallas as pl
from jax.experimental.pallas import tpu as pltpu

F32 = jnp.float32
BF16 = jnp.bfloat16

NORM_EPS = 1e-6
HEAD_DIM = 128
IDX_DIM = 64
INDEX_TOPK = 256
REL_BUCKETS = 32
REL_MAX_DIST = 128
POOL_WINDOWS = (2, 4, 8, 16)
EXPERTS_PER_GROUP = 4
LANES = 128
VMEM_LIMIT = 56 * 1024 * 1024
NEG = -1e30
INT_MIN = -(2 ** 31)


def _dot(a, b):
    return jnp.dot(a, b, preferred_element_type=F32)


def _dot_nt(a, b):
    return lax.dot_general(a, b, (((1,), (1,)), ((), ())), preferred_element_type=F32)


def _rms(x, w):
    return x * lax.rsqrt(jnp.mean(x * x, axis=-1, keepdims=True) + NORM_EPS) * w


def _params(*sem):
    return pltpu.CompilerParams(dimension_semantics=sem, vmem_limit_bytes=VMEM_LIMIT)


def _resident(shape):
    nd = len(shape)
    return pl.BlockSpec(shape, lambda *_: (0,) * nd, pipeline_mode=pl.Buffered(1))


def _in_proj_kernel(x_ref, nw_ref, qn_ref, kn_ref, wq_ref, wkv_ref, wiq_ref, wikw_ref, wa_ref, wg_ref,
                    q_ref, k_ref, v_ref, iq_ref, ik_ref, iw_ref, u_ref, *, q_scale, iw_scale):
    h = _rms(x_ref[...], nw_ref[...]).astype(BF16)
    n_pairs = q_ref.shape[0] // 2
    qn = qn_ref[...] * q_scale
    for c in range(n_pairs):
        qq = _dot(h, wq_ref[:, c * 256:(c + 1) * 256])
        for s in range(2):
            qh = qq[:, s * HEAD_DIM:(s + 1) * HEAD_DIM]
            q_ref[2 * c + s] = _rms(qh, qn).astype(BF16)
    kv = _dot(h, wkv_ref[...])
    k_ref[...] = _rms(kv[:, :HEAD_DIM], kn_ref[...]).astype(BF16)
    v_ref[...] = kv[:, HEAD_DIM:].astype(BF16)
    for c in range(iq_ref.shape[0] // 2):
        r = _dot(h, wiq_ref[:, c * 256:(c + 1) * 256])
        iq_ref[2 * c] = r[:, :LANES].astype(BF16)
        iq_ref[2 * c + 1] = r[:, LANES:].astype(BF16)
    r = _dot(h, wikw_ref[...])
    ik_ref[...] = r[:, :LANES].astype(BF16)
    iw_ref[...] = r[:, LANES:] * iw_scale
    for c in range(u_ref.shape[1] // 256):
        cs = slice(c * 256, (c + 1) * 256)
        a = _dot(h, wa_ref[:, cs])
        g = _dot(h, wg_ref[:, cs])
        u_ref[:, cs] = (a * jax.nn.sigmoid(g)).astype(BF16)


def _in_proj(x, norm_w, w_in, q_norm, k_norm, n_heads, idx_heads, conv_ch, tm=256):
    T, D = x.shape
    attn_w = n_heads * HEAD_DIM
    iq_w = idx_heads * IDX_DIM
    o = 0
    wq = w_in[:, o:o + attn_w]; o += attn_w
    wkv = w_in[:, o:o + 2 * HEAD_DIM]; o += 2 * HEAD_DIM
    wiq = w_in[:, o:o + iq_w]; o += iq_w
    wik = w_in[:, o:o + IDX_DIM]; o += IDX_DIM
    wiw = w_in[:, o:o + idx_heads]; o += idx_heads
    wa = w_in[:, o:o + conv_ch]; o += conv_ch
    wg = w_in[:, o:o + conv_ch]; o += conv_ch
    assert o == w_in.shape[1] and 2 * IDX_DIM == LANES
    wikw = jnp.concatenate([wik, wik, wiw, jnp.zeros((D, LANES - idx_heads), w_in.dtype)], axis=1)
    ws = [w.astype(BF16) for w in (wq, wkv, wiq, wikw, wa, wg)]
    row = lambda w: pl.BlockSpec((tm, w), lambda i: (i, 0))
    heads = lambda n: pl.BlockSpec((n, tm, LANES), lambda i: (0, i, 0))
    kern = functools.partial(_in_proj_kernel, q_scale=HEAD_DIM ** -0.5,
                             iw_scale=(idx_heads ** -0.5) * (IDX_DIM ** -0.5))
    return pl.pallas_call(
        kern,
        grid=(T // tm,),
        in_specs=[row(D), _resident((1, D)), _resident((1, HEAD_DIM)), _resident((1, HEAD_DIM))]
                 + [_resident(w.shape) for w in ws],
        out_specs=[heads(n_heads), row(HEAD_DIM), row(HEAD_DIM), heads(iq_w // LANES), row(LANES), row(LANES),
                   row(conv_ch)],
        out_shape=[jax.ShapeDtypeStruct((n_heads, T, HEAD_DIM), BF16),
                   jax.ShapeDtypeStruct((T, HEAD_DIM), BF16),
                   jax.ShapeDtypeStruct((T, HEAD_DIM), BF16),
                   jax.ShapeDtypeStruct((iq_w // LANES, T, LANES), BF16),
                   jax.ShapeDtypeStruct((T, LANES), BF16),
                   jax.ShapeDtypeStruct((T, LANES), F32),
                   jax.ShapeDtypeStruct((T, conv_ch), BF16)],
        compiler_params=_params("arbitrary"),
    )(x, norm_w.reshape(1, D), q_norm.reshape(1, HEAD_DIM), k_norm.reshape(1, HEAD_DIM), *ws)


def _rel_bias_kernel(rb_ref, o_ref, stat_ref):
    _, n_heads, C, _ = o_ref.shape
    tau = lax.broadcasted_iota(jnp.int32, (C, C), 0)
    sig = lax.broadcasted_iota(jnp.int32, (C, C), 1)
    max_exact = REL_BUCKETS // 2
    for kind in range(2):
        d = tau - sig + kind * C
        n = jnp.maximum(d, 0)
        nf = jnp.maximum(n, 1).astype(F32)
        large = max_exact + (jnp.log(nf / max_exact) / math.log(REL_MAX_DIST / max_exact)
                             * (REL_BUCKETS - max_exact)).astype(jnp.int32)
        large = jnp.minimum(large, REL_BUCKETS - 1)
        bucket = jnp.where(n < max_exact, n, large)
        for h in range(n_heads):
            b = jnp.zeros((C, C), F32)
            for bk in range(REL_BUCKETS):
                b = jnp.where(bucket == bk, rb_ref[bk, h], b)
            b = b - rb_ref[REL_BUCKETS - 1, h]
            if kind == 0:
                b = jnp.where(d < 0, NEG, b)
            o_ref[kind, h] = b
    for h in range(n_heads):
        hi = rb_ref[0, h]
        lo = rb_ref[0, h]
        for bk in range(1, REL_BUCKETS):
            hi = jnp.maximum(hi, rb_ref[bk, h])
            lo = jnp.minimum(lo, rb_ref[bk, h])
        stat_ref[0, h] = hi - rb_ref[REL_BUCKETS - 1, h]
        stat_ref[1, h] = lo - rb_ref[REL_BUCKETS - 1, h]


def _rel_bias_tiles(rel_bias, C):
    n_heads = rel_bias.shape[1]
    assert C >= REL_MAX_DIST
    return pl.pallas_call(
        _rel_bias_kernel,
        in_specs=[pl.BlockSpec(memory_space=pltpu.SMEM)],
        out_specs=[pl.BlockSpec(memory_space=pltpu.VMEM), pl.BlockSpec(memory_space=pltpu.SMEM)],
        out_shape=[jax.ShapeDtypeStruct((2, n_heads, C, C), F32), jax.ShapeDtypeStruct((2, n_heads), F32)],
        compiler_params=pltpu.CompilerParams(vmem_limit_bytes=VMEM_LIMIT),
    )(rel_bias)


SHIFT_SPAN_LIMIT = 60.0


def _attn_kernel(bstat_ref, q_ref, k_ref, v_ref, iq_ref, ik_ref, iw_ref, bias_ref, o_ref,
                 key_sc, iqm_sc, vx_sc, kmax_sc, shift_sc, acc_sc, *, k_top):
    n_heads, C, _ = q_ref.shape
    idx_heads = iqm_sc.shape[0]
    qi = pl.program_id(1)
    nkv = qi + 1

    @pl.when(qi == 0)
    def _():
        vx_sc[:, :HEAD_DIM] = v_ref[...]
        vx_sc[:, HEAD_DIM:] = jnp.ones((vx_sc.shape[0], HEAD_DIM), BF16)
        kf = k_ref[...].astype(F32)
        k2 = jnp.sum(kf * kf, axis=-1, keepdims=True)
        kmax_sc[...] = jnp.broadcast_to(jnp.sqrt(jnp.max(k2, axis=0, keepdims=True)), kmax_sc.shape)

    lane = lax.broadcasted_iota(jnp.int32, (C, LANES), 1)
    for p in range(idx_heads // 2):
        qp = iq_ref[p].astype(F32)
        iqm_sc[2 * p] = jnp.where(lane < IDX_DIM, qp, 0.0).astype(BF16)
        iqm_sc[2 * p + 1] = jnp.where(lane >= IDX_DIM, qp, 0.0).astype(BF16)
    iw = iw_ref[...]
    tau = lax.broadcasted_iota(jnp.int32, (C, C), 0)
    sig = lax.broadcasted_iota(jnp.int32, (C, C), 1)

    def score_body(j, carry):
        off = pl.multiple_of(j * C, C)
        ikc = ik_ref[pl.ds(off, C), :]
        acc = jnp.zeros((C, C), F32)
        for hh in range(idx_heads):
            s = _dot_nt(iqm_sc[hh], ikc)
            acc = acc + jnp.maximum(s, 0.0) * iw[:, hh:hh + 1]
        acc = jnp.where(jnp.logical_and(j == qi, sig > tau), -jnp.inf, acc)
        bits = pltpu.bitcast(acc, jnp.int32)
        key_sc[j] = bits ^ ((bits >> 31) & 0x7FFFFFFF)
        return carry

    lax.fori_loop(0, nkv, score_body, 0)

    def bit_body(b, res):
        cand = res ^ lax.shift_left(jnp.int32(1), 31 - b)

        def cnt_body(j, cnt):
            ge = jnp.where(key_sc[j] >= cand, 1.0, 0.0)
            for t in range(C // LANES):
                cnt = cnt + ge[:, t * LANES:(t + 1) * LANES]
            return cnt

        cnt = lax.fori_loop(0, nkv, cnt_body, jnp.zeros((C, LANES), F32))
        tot = jnp.sum(cnt, axis=-1, keepdims=True)
        return jnp.where(tot >= k_top, cand, res)

    n_bits = jnp.where(nkv * C <= k_top, 0, 32)
    thr = lax.fori_loop(0, n_bits, bit_body, jnp.full((C, 1), INT_MIN, jnp.int32))

    kmax = kmax_sc[0:1, 0:1] * 1.001
    worst = jnp.zeros((C, 1), F32)
    for h in range(n_heads):
        qf = q_ref[h].astype(F32)
        bound = jnp.sqrt(jnp.sum(qf * qf, axis=-1, keepdims=True)) * kmax
        shift_sc[h] = jnp.broadcast_to(bound + bstat_ref[0, h], (C, LANES))
        worst = jnp.maximum(worst, 2.0 * bound + (bstat_ref[0, h] - bstat_ref[1, h]))
    loose = jnp.max(worst) > SHIFT_SPAN_LIMIT

    def logits(j, h, kind):
        off = pl.multiple_of(j * C, C)
        lg = _dot_nt(q_ref[h], k_ref[pl.ds(off, C), :])
        return lg if kind is None else lg + bias_ref[kind, h]

    def near_tiles(fn):
        @pl.when(qi > 0)
        def _():
            fn(qi - 1, 1)
        fn(qi, 0)

    def far_tiles(fn):
        def body(j, carry):
            fn(j, None)
            return carry
        lax.fori_loop(0, qi - 1, body, 0)

    @pl.when(loose)
    def _():
        for h in range(n_heads):
            shift_sc[h] = jnp.full((C, LANES), NEG, F32)

        def max_tile(j, kind):
            sel = key_sc[j] >= thr
            for h in range(n_heads):
                lg = jnp.where(sel, logits(j, h, kind), NEG)
                m = jnp.max(lg, axis=-1, keepdims=True)
                shift_sc[h] = jnp.maximum(shift_sc[h], jnp.broadcast_to(m, (C, LANES)))

        far_tiles(max_tile)
        near_tiles(max_tile)

    acc_sc[...] = jnp.zeros(acc_sc.shape, F32)

    def attn_tile(j, kind):
        off = pl.multiple_of(j * C, C)
        vx = vx_sc[pl.ds(off, C), :]
        sel = key_sc[j] >= thr
        for h in range(n_heads):
            sh = jnp.concatenate([shift_sc[h]] * (C // LANES), axis=1)
            p = jnp.where(sel, jnp.exp(logits(j, h, kind) - sh), 0.0)
            acc_sc[h] += _dot(p.astype(BF16), vx)

    far_tiles(attn_tile)
    near_tiles(attn_tile)
    for h in range(n_heads):
        a = acc_sc[h]
        o_ref[:, h * HEAD_DIM:(h + 1) * HEAD_DIM] = (a[:, :HEAD_DIM] / a[:, HEAD_DIM:]).astype(BF16)


def _attention(q, k, v, iq, ik, iw, bias_tiles, bias_stat, B, S, k_top, C):
    n_heads, T, _ = q.shape
    n_pairs = iq.shape[0]
    nq = S // C
    kern = functools.partial(_attn_kernel, k_top=k_top)
    heads = lambda n: pl.BlockSpec((n, C, LANES), lambda b, i: (0, b * nq + i, 0))
    seq = pl.BlockSpec((S, LANES), lambda b, i: (b, 0))
    return pl.pallas_call(
        kern,
        grid=(B, nq),
        in_specs=[pl.BlockSpec(memory_space=pltpu.SMEM),
                  heads(n_heads), seq, seq, heads(n_pairs), seq,
                  pl.BlockSpec((C, LANES), lambda b, i: (b * nq + i, 0)),
                  _resident(bias_tiles.shape)],
        out_specs=pl.BlockSpec((C, n_heads * HEAD_DIM), lambda b, i: (b * nq + i, 0)),
        out_shape=jax.ShapeDtypeStruct((T, n_heads * HEAD_DIM), BF16),
        scratch_shapes=[pltpu.VMEM((nq, C, C), jnp.int32),
                        pltpu.VMEM((2 * n_pairs, C, LANES), BF16),
                        pltpu.VMEM((S, 2 * HEAD_DIM), BF16),
                        pltpu.VMEM((8, LANES), F32),
                        pltpu.VMEM((n_heads, C, LANES), F32),
                        pltpu.VMEM((n_heads, C, 2 * HEAD_DIM), F32)],
        compiler_params=_params("arbitrary", "arbitrary"),
    )(bias_stat, q, k, v, iq, ik, iw, bias_tiles)


CONV_HALO = 32


def _conv_kernel(u_ref, halo_ref, w_ref, cb_ref, g_ref, b_ref, o_ref, buf, y_sc, *, width):
    TS, CH = u_ref.shape
    i = pl.program_id(1)
    buf[0:CONV_HALO] = jnp.where(i > 0, halo_ref[...].astype(F32), 0.0)
    buf[CONV_HALO:] = u_ref[...].astype(F32)
    base = CONV_HALO - (width - 1)
    for c in range(CH // LANES):
        cs = slice(c * LANES, (c + 1) * LANES)
        acc = jnp.zeros((TS, LANES), F32)
        for j in range(width):
            acc = acc + w_ref[j:j + 1, cs] * buf[base + j:base + j + TS, cs]
        y_sc[:, cs] = acc + cb_ref[:, cs]
    y = y_sc[...]
    mu = jnp.mean(y, axis=-1, keepdims=True)
    yc = y - mu
    var = jnp.mean(yc * yc, axis=-1, keepdims=True)
    yn = yc * lax.rsqrt(var + NORM_EPS) * g_ref[...] + b_ref[...]
    o_ref[...] = (yn * jax.nn.sigmoid(yn)).astype(BF16)


def _conv_module(u, conv_w, conv_b, ln_g, ln_b, B, S, TS=256):
    T, CH = u.shape
    width = conv_w.shape[0]
    assert width - 1 <= CONV_HALO
    ns = S // TS
    r = TS // CONV_HALO
    wpad = jnp.zeros((CONV_HALO, CH), F32).at[:width].set(conv_w)
    vec = lambda a: a.reshape(1, CH)
    kern = functools.partial(_conv_kernel, width=width)
    return pl.pallas_call(
        kern,
        grid=(B, ns),
        in_specs=[pl.BlockSpec((TS, CH), lambda b, i: (b * ns + i, 0)),
                  pl.BlockSpec((CONV_HALO, CH), lambda b, i: (jnp.maximum((b * ns + i) * r - 1, 0), 0)),
                  _resident((CONV_HALO, CH)), _resident((1, CH)), _resident((1, CH)), _resident((1, CH))],
        out_specs=pl.BlockSpec((TS, CH), lambda b, i: (b * ns + i, 0)),
        out_shape=jax.ShapeDtypeStruct((T, CH), BF16),
        scratch_shapes=[pltpu.VMEM((TS + CONV_HALO, CH), F32), pltpu.VMEM((TS, CH), F32)],
        compiler_params=_params("arbitrary", "arbitrary"),
    )(u, u, wpad, vec(conv_b), vec(ln_g), vec(ln_b))


def _out_proj_kernel(x_ref, a_ref, c_ref, wa_ref, wc_ref, o_ref):
    o_ref[...] = x_ref[...] + _dot(a_ref[...], wa_ref[...]) + _dot(c_ref[...], wc_ref[...])


def _out_proj(x, attn, conv, w_out, tm=512):
    T, D = x.shape
    aw, cw = attn.shape[1], conv.shape[1]
    wa = w_out[:aw].astype(BF16)
    wc = w_out[aw:].astype(BF16)
    row = lambda w: pl.BlockSpec((tm, w), lambda i: (i, 0))
    return pl.pallas_call(
        _out_proj_kernel,
        grid=(T // tm,),
        in_specs=[row(D), row(aw), row(cw), _resident(wa.shape), _resident(wc.shape)],
        out_specs=row(D),
        out_shape=jax.ShapeDtypeStruct((T, D), F32),
        compiler_params=_params("arbitrary"),
    )(x, attn, conv, wa, wc)


def _first_max(vals):
    m = vals[0]
    for v in vals[1:]:
        m = jnp.maximum(m, v)
    idx = jnp.full(m.shape, len(vals) - 1, jnp.int32)
    for k in range(len(vals) - 2, -1, -1):
        idx = jnp.where(vals[k] == m, k, idx)
    return m, idx


def _softmax_cols(cols):
    m = cols[0]
    for c in cols[1:]:
        m = jnp.maximum(m, c)
    e = [jnp.exp(c - m) for c in cols]
    s = e[0]
    for c in e[1:]:
        s = s + c
    return [c / s for c in e]


def _router_kernel(x_ref, nw_ref, wr_ref, br_ref, h_ref, comb_ref, *, n_groups):
    h = _rms(x_ref[...], nw_ref[...]).astype(BF16)
    h_ref[...] = h
    lg = _dot(h, wr_ref[...]) + br_ref[...]
    col = lambda k: lg[:, k:k + 1]
    g_prob = _softmax_cols([col(g) for g in range(n_groups)])
    g_p, g_idx = _first_max(g_prob)
    e_logit = []
    for k in range(EXPERTS_PER_GROUP):
        v = col(n_groups + k)
        for g in range(1, n_groups):
            v = jnp.where(g_idx == g, col(n_groups + g * EXPERTS_PER_GROUP + k), v)
        e_logit.append(v)
    e_prob = _softmax_cols(e_logit)
    p1, i1 = _first_max(e_prob)
    rest = [jnp.where(i1 == k, -1.0, e_prob[k]) for k in range(EXPERTS_PER_GROUP)]
    p2, i2 = _first_max(rest)
    den = p1 + p2
    lane = lax.broadcasted_iota(jnp.int32, lg.shape, 1)
    base = g_idx * EXPERTS_PER_GROUP
    comb_ref[...] = (jnp.where(lane == base + i1, g_p * (p1 / den), 0.0)
                     + jnp.where(lane == base + i2, g_p * (p2 / den), 0.0))


def _router(x, norm_w, wg, bg, we, be, tm=512):
    T, D = x.shape
    n_groups, n_exp = wg.shape[1], we.shape[1]
    pad = LANES - n_groups - n_exp
    wr = jnp.concatenate([wg, we, jnp.zeros((D, pad), F32)], axis=1).astype(BF16)
    br = jnp.concatenate([bg, be, jnp.zeros((pad,), F32)]).reshape(1, LANES)
    row = lambda w: pl.BlockSpec((tm, w), lambda i: (i, 0))
    return pl.pallas_call(
        functools.partial(_router_kernel, n_groups=n_groups),
        grid=(T // tm,),
        in_specs=[row(D), _resident((1, D)), _resident((D, LANES)), _resident((1, LANES))],
        out_specs=[row(D), row(LANES)],
        out_shape=[jax.ShapeDtypeStruct((T, D), BF16), jax.ShapeDtypeStruct((T, LANES), F32)],
        compiler_params=_params("arbitrary"),
    )(x, norm_w.reshape(1, D), wr, br)


def _moe_kernel(x_ref, h_ref, comb_ref, wg_ref, wu_ref, wd_ref, o_ref, acc_sc):
    e = pl.program_id(1)

    @pl.when(e == 0)
    def _():
        acc_sc[...] = x_ref[...]

    h = h_ref[...]
    lane = lax.broadcasted_iota(jnp.int32, comb_ref.shape, 1)
    c = jnp.sum(jnp.where(lane == e, comb_ref[...], 0.0), axis=-1, keepdims=True)
    a = jax.nn.silu(_dot(h, wg_ref[0])) * _dot(h, wu_ref[0])
    acc_sc[...] += _dot((a * c).astype(BF16), wd_ref[0])

    @pl.when(e == pl.num_programs(1) - 1)
    def _():
        o_ref[...] = acc_sc[...]


def _moe(x, h, comb, w_gate, w_up, w_down, tm=512):
    T, D = x.shape
    n_exp, _, ff = w_gate.shape
    row = lambda w: pl.BlockSpec((tm, w), lambda i, e: (i, 0))
    return pl.pallas_call(
        _moe_kernel,
        grid=(T // tm, n_exp),
        in_specs=[row(D), row(D), row(LANES),
                  pl.BlockSpec((1, D, ff), lambda i, e: (e, 0, 0)),
                  pl.BlockSpec((1, D, ff), lambda i, e: (e, 0, 0)),
                  pl.BlockSpec((1, ff, D), lambda i, e: (e, 0, 0))],
        out_specs=row(D),
        out_shape=jax.ShapeDtypeStruct((T, D), F32),
        scratch_shapes=[pltpu.VMEM((tm, D), F32)],
        compiler_params=_params("arbitrary", "arbitrary"),
    )(x, h, comb, w_gate, w_up, w_down)


POOL_HALO = 16


def _pool_kernel(x_ref, halo_ref, nw_ref, pw_ref, ps_ref, o_ref, hb):
    TS, D = x_ref.shape
    n_groups, pc, _ = pw_ref.shape
    i = pl.program_id(1)
    nw = nw_ref[...]
    hb[0:POOL_HALO] = jnp.where(i > 0, _rms(halo_ref[...], nw), 0.0)
    hb[POOL_HALO:] = _rms(x_ref[...], nw)
    t = i * TS + lax.broadcasted_iota(jnp.int32, (TS, 1), 0)
    for g, w in enumerate(POOL_WINDOWS):
        cs = slice(g * pc, (g + 1) * pc)
        cur = hb[POOL_HALO:, cs]
        s = cur
        for r in range(1, w):
            s = s + hb[POOL_HALO - r:POOL_HALO - r + TS, cs]
        count = jnp.minimum(t + 1, w).astype(F32)
        d = s / count - cur
        mixed = _dot(d.astype(BF16), pw_ref[g])
        o_ref[:, cs] = x_ref[:, cs] + ps_ref[:, cs] * mixed


def _pool_layer(x, norm_w, pool_w, pool_scale, B, S, TS=256):
    T, D = x.shape
    assert len(POOL_WINDOWS) == pool_w.shape[0] and max(POOL_WINDOWS) - 1 <= POOL_HALO
    ns = S // TS
    r = TS // POOL_HALO
    pw = pool_w.astype(BF16)
    return pl.pallas_call(
        _pool_kernel,
        grid=(B, ns),
        in_specs=[pl.BlockSpec((TS, D), lambda b, i: (b * ns + i, 0)),
                  pl.BlockSpec((POOL_HALO, D), lambda b, i: (jnp.maximum((b * ns + i) * r - 1, 0), 0)),
                  _resident((1, D)), _resident(pw.shape), _resident((1, D))],
        out_specs=pl.BlockSpec((TS, D), lambda b, i: (b * ns + i, 0)),
        out_shape=jax.ShapeDtypeStruct((T, D), F32),
        scratch_shapes=[pltpu.VMEM((TS + POOL_HALO, D), F32)],
        compiler_params=_params("arbitrary", "arbitrary"),
    )(x, x, norm_w.reshape(1, D), pw, pool_scale.reshape(1, D))


def _chunk(S):
    return 256 if S % 256 == 0 else 128


def kernel(x, rel_bias, mix_norm_e, w_in_e, q_norm_e, k_norm_e, conv_w_e, conv_b_e, conv_ln_g_e, conv_ln_b_e,
           w_out_e, mix_norm_o, pool_w_o, pool_scale_o, ffn_norm, router_group_w, router_group_b,
           router_expert_w, router_expert_b, w_gate, w_up, w_down):
    B, S, D = x.shape
    T = B * S
    depth = ffn_norm.shape[0]
    n_heads = rel_bias.shape[1]
    idx_heads = (w_in_e.shape[2] - n_heads * HEAD_DIM - 2 * HEAD_DIM - IDX_DIM - 2 * conv_w_e.shape[2]) \
        // (IDX_DIM + 1)
    k_top = min(INDEX_TOPK, S // 4)
    C = _chunk(S)
    xf = x.reshape(T, D)
    bias_tiles, bias_stat = _rel_bias_tiles(rel_bias, C)
    for l in range(depth):
        i = l // 2
        if l % 2 == 0:
            q, k, v, iq, ik, iw, u = _in_proj(xf, mix_norm_e[i], w_in_e[i], q_norm_e[i], k_norm_e[i],
                                              n_heads, idx_heads, conv_w_e.shape[2])
            attn = _attention(q, k, v, iq, ik, iw, bias_tiles, bias_stat, B, S, k_top, C)
            conv = _conv_module(u, conv_w_e[i], conv_b_e[i], conv_ln_g_e[i], conv_ln_b_e[i], B, S)
            xf = _out_proj(xf, attn, conv, w_out_e[i])
        else:
            xf = _pool_layer(xf, mix_norm_o[i], pool_w_o[i], pool_scale_o[i], B, S)
        h, comb = _router(xf, ffn_norm[l], router_group_w[l], router_group_b[l],
                          router_expert_w[l], router_expert_b[l])
        xf = _moe(xf, h, comb, w_gate[l].astype(BF16), w_up[l].astype(BF16), w_down[l].astype(BF16))
    return xf.reshape(B, S, D)
```

```python
import functools
import math

import jax
import jax.numpy as jnp
from jax import lax
from jax.experimental import pallas as pl
from jax.experimental.pallas import tpu as pltpu

F32 = jnp.float32
BF16 = jnp.bfloat16

NORM_EPS = 1e-6
HEAD_DIM = 128
IDX_DIM = 64
INDEX_TOPK = 256
REL_BUCKETS = 32
REL_MAX_DIST = 128
POOL_WINDOWS = (2, 4, 8, 16)
EXPERTS_PER_GROUP = 4
LANES = 128
VMEM_LIMIT = 56 * 1024 * 1024
NEG = -1e30
INT_MIN = -(2 ** 31)


def _dot(a, b):
    return jnp.dot(a, b, preferred_element_type=F32)


def _dot_nt(a, b):
    return lax.dot_general(a, b, (((1,), (1,)), ((), ())), preferred_element_type=F32)


def _rms(x, w):
    return x * lax.rsqrt(jnp.mean(x * x, axis=-1, keepdims=True) + NORM_EPS) * w


def _params(*sem):
    return pltpu.CompilerParams(dimension_semantics=sem, vmem_limit_bytes=VMEM_LIMIT)


def _resident(shape):
    nd = len(shape)
    return pl.BlockSpec(shape, lambda *_: (0,) * nd, pipeline_mode=pl.Buffered(1))


def _in_proj_kernel(x_ref, nw_ref, qn_ref, kn_ref, wq_ref, wkv_ref, wiq_ref, wikw_ref, wa_ref, wg_ref,
                    q_ref, k_ref, v_ref, iq_ref, ik_ref, iw_ref, u_ref, *, q_scale, iw_scale):
    h = _rms(x_ref[...], nw_ref[...]).astype(BF16)
    n_pairs = q_ref.shape[0] // 2
    qn = qn_ref[...] * q_scale
    for c in range(n_pairs):
        qq = _dot(h, wq_ref[:, c * 256:(c + 1) * 256])
        for s in range(2):
            qh = qq[:, s * HEAD_DIM:(s + 1) * HEAD_DIM]
            q_ref[2 * c + s] = _rms(qh, qn).astype(BF16)
    kv = _dot(h, wkv_ref[...])
    k_ref[...] = _rms(kv[:, :HEAD_DIM], kn_ref[...]).astype(BF16)
    v_ref[...] = kv[:, HEAD_DIM:].astype(BF16)
    for c in range(iq_ref.shape[0] // 2):
        r = _dot(h, wiq_ref[:, c * 256:(c + 1) * 256])
        iq_ref[2 * c] = r[:, :LANES].astype(BF16)
        iq_ref[2 * c + 1] = r[:, LANES:].astype(BF16)
    r = _dot(h, wikw_ref[...])
    ik_ref[...] = r[:, :LANES].astype(BF16)
    iw_ref[...] = r[:, LANES:] * iw_scale
    for c in range(u_ref.shape[1] // 256):
        cs = slice(c * 256, (c + 1) * 256)
        a = _dot(h, wa_ref[:, cs])
        g = _dot(h, wg_ref[:, cs])
        u_ref[:, cs] = (a * jax.nn.sigmoid(g)).astype(BF16)


def _in_proj(x, norm_w, w_in, q_norm, k_norm, n_heads, idx_heads, conv_ch, tm=256):
    T, D = x.shape
    attn_w = n_heads * HEAD_DIM
    iq_w = idx_heads * IDX_DIM
    o = 0
    wq = w_in[:, o:o + attn_w]; o += attn_w
    wkv = w_in[:, o:o + 2 * HEAD_DIM]; o += 2 * HEAD_DIM
    wiq = w_in[:, o:o + iq_w]; o += iq_w
    wik = w_in[:, o:o + IDX_DIM]; o += IDX_DIM
    wiw = w_in[:, o:o + idx_heads]; o += idx_heads
    wa = w_in[:, o:o + conv_ch]; o += conv_ch
    wg = w_in[:, o:o + conv_ch]; o += conv_ch
    assert o == w_in.shape[1] and 2 * IDX_DIM == LANES
    wikw = jnp.concatenate([wik, wik, wiw, jnp.zeros((D, LANES - idx_heads), w_in.dtype)], axis=1)
    ws = [w.astype(BF16) for w in (wq, wkv, wiq, wikw, wa, wg)]
    row = lambda w: pl.BlockSpec((tm, w), lambda i: (i, 0))
    heads = lambda n: pl.BlockSpec((n, tm, LANES), lambda i: (0, i, 0))
    kern = functools.partial(_in_proj_kernel, q_scale=HEAD_DIM ** -0.5,
                             iw_scale=(idx_heads ** -0.5) * (IDX_DIM ** -0.5))
    return pl.pallas_call(
        kern,
        grid=(T // tm,),
        in_specs=[row(D), _resident((1, D)), _resident((1, HEAD_DIM)), _resident((1, HEAD_DIM))]
                 + [_resident(w.shape) for w in ws],
        out_specs=[heads(n_heads), row(HEAD_DIM), row(HEAD_DIM), heads(iq_w // LANES), row(LANES), row(LANES),
                   row(conv_ch)],
        out_shape=[jax.ShapeDtypeStruct((n_heads, T, HEAD_DIM), BF16),
                   jax.ShapeDtypeStruct((T, HEAD_DIM), BF16),
                   jax.ShapeDtypeStruct((T, HEAD_DIM), BF16),
                   jax.ShapeDtypeStruct((iq_w // LANES, T, LANES), BF16),
                   jax.ShapeDtypeStruct((T, LANES), BF16),
                   jax.ShapeDtypeStruct((T, LANES), F32),
                   jax.ShapeDtypeStruct((T, conv_ch), BF16)],
        compiler_params=_params("arbitrary"),
    )(x, norm_w.reshape(1, D), q_norm.reshape(1, HEAD_DIM), k_norm.reshape(1, HEAD_DIM), *ws)


def _rel_bias_kernel(rb_ref, o_ref, stat_ref):
    _, n_heads, C, _ = o_ref.shape
    tau = lax.broadcasted_iota(jnp.int32, (C, C), 0)
    sig = lax.broadcasted_iota(jnp.int32, (C, C), 1)
    max_exact = REL_BUCKETS // 2
    for kind in range(2):
        d = tau - sig + kind * C
        n = jnp.maximum(d, 0)
        nf = jnp.maximum(n, 1).astype(F32)
        large = max_exact + (jnp.log(nf / max_exact) / math.log(REL_MAX_DIST / max_exact)
                             * (REL_BUCKETS - max_exact)).astype(jnp.int32)
        large = jnp.minimum(large, REL_BUCKETS - 1)
        bucket = jnp.where(n < max_exact, n, large)
        for h in range(n_heads):
            b = jnp.zeros((C, C), F32)
            for bk in range(REL_BUCKETS):
                b = jnp.where(bucket == bk, rb_ref[bk, h], b)
            b = b - rb_ref[REL_BUCKETS - 1, h]
            if kind == 0:
                b = jnp.where(d < 0, NEG, b)
            o_ref[kind, h] = b
    for h in range(n_heads):
        hi = rb_ref[0, h]
        lo = rb_ref[0, h]
        for bk in range(1, REL_BUCKETS):
            hi = jnp.maximum(hi, rb_ref[bk, h])
            lo = jnp.minimum(lo, rb_ref[bk, h])
        stat_ref[0, h] = hi - rb_ref[REL_BUCKETS - 1, h]
        stat_ref[1, h] = lo - rb_ref[REL_BUCKETS - 1, h]


def _rel_bias_tiles(rel_bias, C):
    n_heads = rel_bias.shape[1]
    assert C >= REL_MAX_DIST
    return pl.pallas_call(
        _rel_bias_kernel,
        in_specs=[pl.BlockSpec(memory_space=pltpu.SMEM)],
        out_specs=[pl.BlockSpec(memory_space=pltpu.VMEM), pl.BlockSpec(memory_space=pltpu.SMEM)],
        out_shape=[jax.ShapeDtypeStruct((2, n_heads, C, C), F32), jax.ShapeDtypeStruct((2, n_heads), F32)],
        compiler_params=pltpu.CompilerParams(vmem_limit_bytes=VMEM_LIMIT),
    )(rel_bias)


SHIFT_SPAN_LIMIT = 60.0


def _attn_kernel(bstat_ref, q_ref, k_ref, v_ref, iq_ref, ik_ref, iw_ref, bias_ref, o_ref,
                 key_sc, iqm_sc, vx_sc, kmax_sc, shift_sc, acc_sc, *, k_top):
    n_heads, C, _ = q_ref.shape
    idx_heads = iqm_sc.shape[0]
    qi = pl.program_id(1)
    nkv = qi + 1

    @pl.when(qi == 0)
    def _():
        vx_sc[:, :HEAD_DIM] = v_ref[...]
        vx_sc[:, HEAD_DIM:] = jnp.ones((vx_sc.shape[0], HEAD_DIM), BF16)
        kf = k_ref[...].astype(F32)
        k2 = jnp.sum(kf * kf, axis=-1, keepdims=True)
        kmax_sc[...] = jnp.broadcast_to(jnp.sqrt(jnp.max(k2, axis=0, keepdims=True)), kmax_sc.shape)

    lane = lax.broadcasted_iota(jnp.int32, (C, LANES), 1)
    for p in range(idx_heads // 2):
        qp = iq_ref[p].astype(F32)
        iqm_sc[2 * p] = jnp.where(lane < IDX_DIM, qp, 0.0).astype(BF16)
        iqm_sc[2 * p + 1] = jnp.where(lane >= IDX_DIM, qp, 0.0).astype(BF16)
    iw = iw_ref[...]
    tau = lax.broadcasted_iota(jnp.int32, (C, C), 0)
    sig = lax.broadcasted_iota(jnp.int32, (C, C), 1)

    def score_body(j, carry):
        off = pl.multiple_of(j * C, C)
        ikc = ik_ref[pl.ds(off, C), :]
        acc = jnp.zeros((C, C), F32)
        for hh in range(idx_heads):
            s = _dot_nt(iqm_sc[hh], ikc)
            acc = acc + jnp.maximum(s, 0.0) * iw[:, hh:hh + 1]
        acc = jnp.where(jnp.logical_and(j == qi, sig > tau), -jnp.inf, acc)
        bits = pltpu.bitcast(acc, jnp.int32)
        key_sc[j] = bits ^ ((bits >> 31) & 0x7FFFFFFF)
        return carry

    lax.fori_loop(0, nkv, score_body, 0)

    def bit_body(b, res):
        cand = res ^ lax.shift_left(jnp.int32(1), 31 - b)

        def cnt_body(j, cnt):
            ge = jnp.where(key_sc[j] >= cand, 1.0, 0.0)
            for t in range(C // LANES):
                cnt = cnt + ge[:, t * LANES:(t + 1) * LANES]
            return cnt

        cnt = lax.fori_loop(0, nkv, cnt_body, jnp.zeros((C, LANES), F32))
        tot = jnp.sum(cnt, axis=-1, keepdims=True)
        return jnp.where(tot >= k_top, cand, res)

    n_bits = jnp.where(nkv * C <= k_top, 0, 32)
    thr = lax.fori_loop(0, n_bits, bit_body, jnp.full((C, 1), INT_MIN, jnp.int32))

    kmax = kmax_sc[0:1, 0:1] * 1.001
    worst = jnp.zeros((C, 1), F32)
    for h in range(n_heads):
        qf = q_ref[h].astype(F32)
        bound = jnp.sqrt(jnp.sum(qf * qf, axis=-1, keepdims=True)) * kmax
        shift_sc[h] = jnp.broadcast_to(bound + bstat_ref[0, h], (C, LANES))
        worst = jnp.maximum(worst, 2.0 * bound + (bstat_ref[0, h] - bstat_ref[1, h]))
    loose = jnp.max(worst) > SHIFT_SPAN_LIMIT

    def logits(j, h, kind):
        off = pl.multiple_of(j * C, C)
        lg = _dot_nt(q_ref[h], k_ref[pl.ds(off, C), :])
        return lg if kind is None else lg + bias_ref[kind, h]

    def near_tiles(fn):
        @pl.when(qi > 0)
        def _():
            fn(qi - 1, 1)
        fn(qi, 0)

    def far_tiles(fn):
        def body(j, carry):
            fn(j, None)
            return carry
        lax.fori_loop(0, qi - 1, body, 0)

    @pl.when(loose)
    def _():
        for h in range(n_heads):
            shift_sc[h] = jnp.full((C, LANES), NEG, F32)

        def max_tile(j, kind):
            sel = key_sc[j] >= thr
            for h in range(n_heads):
                lg = jnp.where(sel, logits(j, h, kind), NEG)
                m = jnp.max(lg, axis=-1, keepdims=True)
                shift_sc[h] = jnp.maximum(shift_sc[h], jnp.broadcast_to(m, (C, LANES)))

        far_tiles(max_tile)
        near_tiles(max_tile)

    acc_sc[...] = jnp.zeros(acc_sc.shape, F32)

    def attn_tile(j, kind):
        off = pl.multiple_of(j * C, C)
        vx = vx_sc[pl.ds(off, C), :]
        sel = key_sc[j] >= thr
        for h in range(n_heads):
            sh = jnp.concatenate([shift_sc[h]] * (C // LANES), axis=1)
            p = jnp.where(sel, jnp.exp(logits(j, h, kind) - sh), 0.0)
            acc_sc[h] += _dot(p.astype(BF16), vx)

    far_tiles(attn_tile)
    near_tiles(attn_tile)
    for h in range(n_heads):
        a = acc_sc[h]
        o_ref[:, h * HEAD_DIM:(h + 1) * HEAD_DIM] = (a[:, :HEAD_DIM] / a[:, HEAD_DIM:]).astype(BF16)


def _attention(q, k, v, iq, ik, iw, bias_tiles, bias_stat, B, S, k_top, C):
    n_heads, T, _ = q.shape
    n_pairs = iq.shape[0]
    nq = S // C
    kern = functools.partial(_attn_kernel, k_top=k_top)
    heads = lambda n: pl.BlockSpec((n, C, LANES), lambda b, i: (0, b * nq + i, 0))
    seq = pl.BlockSpec((S, LANES), lambda b, i: (b, 0))
    return pl.pallas_call(
        kern,
        grid=(B, nq),
        in_specs=[pl.BlockSpec(memory_space=pltpu.SMEM),
                  heads(n_heads), seq, seq, heads(n_pairs), seq,
                  pl.BlockSpec((C, LANES), lambda b, i: (b * nq + i, 0)),
                  _resident(bias_tiles.shape)],
        out_specs=pl.BlockSpec((C, n_heads * HEAD_DIM), lambda b, i: (b * nq + i, 0)),
        out_shape=jax.ShapeDtypeStruct((T, n_heads * HEAD_DIM), BF16),
        scratch_shapes=[pltpu.VMEM((nq, C, C), jnp.int32),
                        pltpu.VMEM((2 * n_pairs, C, LANES), BF16),
                        pltpu.VMEM((S, 2 * HEAD_DIM), BF16),
                        pltpu.VMEM((8, LANES), F32),
                        pltpu.VMEM((n_heads, C, LANES), F32),
                        pltpu.VMEM((n_heads, C, 2 * HEAD_DIM), F32)],
        compiler_params=_params("arbitrary", "arbitrary"),
    )(bias_stat, q, k, v, iq, ik, iw, bias_tiles)


CONV_HALO = 32


def _conv_kernel(u_ref, halo_ref, w_ref, cb_ref, g_ref, b_ref, o_ref, buf, y_sc, *, width):
    TS, CH = u_ref.shape
    i = pl.program_id(1)
    buf[0:CONV_HALO] = jnp.where(i > 0, halo_ref[...].astype(F32), 0.0)
    buf[CONV_HALO:] = u_ref[...].astype(F32)
    base = CONV_HALO - (width - 1)
    for c in range(CH // LANES):
        cs = slice(c * LANES, (c + 1) * LANES)
        acc = jnp.zeros((TS, LANES), F32)
        for j in range(width):
            acc = acc + w_ref[j:j + 1, cs] * buf[base + j:base + j + TS, cs]
        y_sc[:, cs] = acc + cb_ref[:, cs]
    y = y_sc[...]
    mu = jnp.mean(y, axis=-1, keepdims=True)
    yc = y - mu
    var = jnp.mean(yc * yc, axis=-1, keepdims=True)
    yn = yc * lax.rsqrt(var + NORM_EPS) * g_ref[...] + b_ref[...]
    o_ref[...] = (yn * jax.nn.sigmoid(yn)).astype(BF16)


def _conv_module(u, conv_w, conv_b, ln_g, ln_b, B, S, TS=256):
    T, CH = u.shape
    width = conv_w.shape[0]
    assert width - 1 <= CONV_HALO
    ns = S // TS
    r = TS // CONV_HALO
    wpad = jnp.zeros((CONV_HALO, CH), F32).at[:width].set(conv_w)
    vec = lambda a: a.reshape(1, CH)
    kern = functools.partial(_conv_kernel, width=width)
    return pl.pallas_call(
        kern,
        grid=(B, ns),
        in_specs=[pl.BlockSpec((TS, CH), lambda b, i: (b * ns + i, 0)),
                  pl.BlockSpec((CONV_HALO, CH), lambda b, i: (jnp.maximum((b * ns + i) * r - 1, 0), 0)),
                  _resident((CONV_HALO, CH)), _resident((1, CH)), _resident((1, CH)), _resident((1, CH))],
        out_specs=pl.BlockSpec((TS, CH), lambda b, i: (b * ns + i, 0)),
        out_shape=jax.ShapeDtypeStruct((T, CH), BF16),
        scratch_shapes=[pltpu.VMEM((TS + CONV_HALO, CH), F32), pltpu.VMEM((TS, CH), F32)],
        compiler_params=_params("arbitrary", "arbitrary"),
    )(u, u, wpad, vec(conv_b), vec(ln_g), vec(ln_b))


def _out_proj_kernel(x_ref, a_ref, c_ref, wa_ref, wc_ref, o_ref):
    o_ref[...] = x_ref[...] + _dot(a_ref[...], wa_ref[...]) + _dot(c_ref[...], wc_ref[...])


def _out_proj(x, attn, conv, w_out, tm=512):
    T, D = x.shape
    aw, cw = attn.shape[1], conv.shape[1]
    wa = w_out[:aw].astype(BF16)
    wc = w_out[aw:].astype(BF16)
    row = lambda w: pl.BlockSpec((tm, w), lambda i: (i, 0))
    return pl.pallas_call(
        _out_proj_kernel,
        grid=(T // tm,),
        in_specs=[row(D), row(aw), row(cw), _resident(wa.shape), _resident(wc.shape)],
        out_specs=row(D),
        out_shape=jax.ShapeDtypeStruct((T, D), F32),
        compiler_params=_params("arbitrary"),
    )(x, attn, conv, wa, wc)


def _first_max(vals):
    m = vals[0]
    for v in vals[1:]:
        m = jnp.maximum(m, v)
    idx = jnp.full(m.shape, len(vals) - 1, jnp.int32)
    for k in range(len(vals) - 2, -1, -1):
        idx = jnp.where(vals[k] == m, k, idx)
    return m, idx


def _softmax_cols(cols):
    m = cols[0]
    for c in cols[1:]:
        m = jnp.maximum(m, c)
    e = [jnp.exp(c - m) for c in cols]
    s = e[0]
    for c in e[1:]:
        s = s + c
    return [c / s for c in e]


GROUP_ROWS = 8
MOE_CHUNK = 128


def _split3(a):
    hi = a.astype(BF16)
    r = a - hi.astype(F32)
    mid = r.astype(BF16)
    lo = (r - mid.astype(F32)).astype(BF16)
    return hi, mid, lo


def _moe_kernel(x_ref, nw_ref, wrt_ref, brt_ref, wg_ref, wu_ref, wd_ref, o_ref,
                h_sc, hs_sc, p_sc, u_sc, cs_sc, seg_sm, *, n_groups):
    TM, D = x_ref.shape
    i = pl.program_id(0)
    e = pl.program_id(1)
    n_exp = pl.num_programs(1)

    @pl.when(jnp.logical_and(i == 0, e == 0))
    def _():
        s_io = lax.broadcasted_iota(jnp.int32, (TM, TM), 0)
        t_io = lax.broadcasted_iota(jnp.int32, (TM, TM), 1)
        u_sc[...] = jnp.where(s_io < t_io, 1.0, 0.0).astype(BF16)

    @pl.when(e == 0)
    def _():
        h = _rms(x_ref[...], nw_ref[...]).astype(BF16)
        h_sc[...] = h
        lg = _dot_nt(wrt_ref[...], h) + brt_ref[...]
        row = lambda k: lg[k:k + 1, :]
        g_prob = _softmax_cols([row(g) for g in range(n_groups)])
        g_p, g_idx = _first_max(g_prob)
        e_logit = []
        for k in range(EXPERTS_PER_GROUP):
            v = row(GROUP_ROWS + k)
            for g in range(1, n_groups):
                v = jnp.where(g_idx == g, row(GROUP_ROWS + g * EXPERTS_PER_GROUP + k), v)
            e_logit.append(v)
        e_prob = _softmax_cols(e_logit)
        p1, i1 = _first_max(e_prob)
        rest = [jnp.where(i1 == k, -1.0, e_prob[k]) for k in range(EXPERTS_PER_GROUP)]
        p2, i2 = _first_max(rest)
        den = p1 + p2
        base = g_idx * EXPERTS_PER_GROUP
        e_io = lax.broadcasted_iota(jnp.int32, (LANES, TM), 0)
        comb_t = (jnp.where(e_io == base + i1, g_p * (p1 / den), 0.0)
                  + jnp.where(e_io == base + i2, g_p * (p2 / den), 0.0))

        g_io = lax.broadcasted_iota(jnp.int32, (GROUP_ROWS, TM), 0)
        onehot_t = jnp.where(g_io == g_idx, 1.0, 0.0)
        rank = _dot(onehot_t.astype(BF16), u_sc[...])
        start = jnp.int32(0)
        pos = jnp.zeros((1, TM), F32)
        for g in range(n_groups):
            seg_sm[g] = start
            pos = jnp.where(g_idx == g, start.astype(F32) + rank[g:g + 1, :], pos)
            start = start + jnp.sum(onehot_t[g:g + 1, :]).astype(jnp.int32)
        seg_sm[n_groups] = start
        pos = pos.astype(jnp.int32)

        rb = 256
        for r0 in range(0, TM, rb):
            r_io = lax.broadcasted_iota(jnp.int32, (rb, TM), 0) + r0
            p_sc[r0:r0 + rb, :] = jnp.where(r_io == pos, 1.0, 0.0).astype(BF16)
        p = p_sc[...]
        for c0 in range(0, D, 512):
            hs_sc[:, c0:c0 + 512] = _dot(p, h_sc[:, c0:c0 + 512]).astype(BF16)
        cs = jnp.zeros((TM, LANES), F32)
        for part in _split3(comb_t):
            cs = cs + _dot_nt(p, part)
        cs_sc[...] = cs
        o_ref[...] = jnp.zeros(o_ref.shape, F32)

    g = e // EXPERTS_PER_GROUP
    start = seg_sm[g]
    end = seg_sm[g + 1]
    c_lo = start // MOE_CHUNK
    c_hi = jnp.where(end > start, (end + MOE_CHUNK - 1) // MOE_CHUNK, c_lo)
    lane = lax.broadcasted_iota(jnp.int32, (MOE_CHUNK, LANES), 1)

    def chunk(c, carry):
        r0 = pl.multiple_of(c * MOE_CHUNK, MOE_CHUNK)
        rows = hs_sc[pl.ds(r0, MOE_CHUNK), :]
        w = jnp.sum(jnp.where(lane == e, cs_sc[pl.ds(r0, MOE_CHUNK), :], 0.0), axis=-1, keepdims=True)
        a = jax.nn.silu(_dot(rows, wg_ref[0])) * _dot(rows, wu_ref[0]) * w
        o_ref[pl.ds(r0, MOE_CHUNK), :] += _dot(a.astype(BF16), wd_ref[0])
        return carry

    lax.fori_loop(c_lo, c_hi, chunk, 0)

    @pl.when(e == n_exp - 1)
    def _():
        p = p_sc[...]
        for c0 in range(0, D, 512):
            ys = o_ref[:, c0:c0 + 512].astype(BF16)
            y = lax.dot_general(p, ys, (((0,), (0,)), ((), ())), preferred_element_type=F32)
            o_ref[:, c0:c0 + 512] = x_ref[:, c0:c0 + 512] + y


def _moe(x, norm_w, wg, bg, we, be, w_gate, w_up, w_down, tm=1024):
    T, D = x.shape
    n_groups, n_exp = wg.shape[1], we.shape[1]
    _, _, ff = w_gate.shape
    assert n_groups <= GROUP_ROWS and n_exp == n_groups * EXPERTS_PER_GROUP and T % tm == 0
    rows = GROUP_ROWS + n_exp
    wrt = jnp.zeros((rows, D), F32).at[:n_groups].set(wg.T).at[GROUP_ROWS:].set(we.T).astype(BF16)
    brt = jnp.zeros((rows, 1), F32).at[:n_groups, 0].set(bg).at[GROUP_ROWS:, 0].set(be)
    tile = pl.BlockSpec((tm, D), lambda i, e: (i, 0))
    return pl.pallas_call(
        functools.partial(_moe_kernel, n_groups=n_groups),
        grid=(T // tm, n_exp),
        in_specs=[pl.BlockSpec((tm, D), lambda i, e: (i, 0), pipeline_mode=pl.Buffered(1)),
                  _resident((1, D)), _resident((rows, D)), _resident((rows, 1)),
                  pl.BlockSpec((1, D, ff), lambda i, e: (e, 0, 0)),
                  pl.BlockSpec((1, D, ff), lambda i, e: (e, 0, 0)),
                  pl.BlockSpec((1, ff, D), lambda i, e: (e, 0, 0))],
        out_specs=tile,
        out_shape=jax.ShapeDtypeStruct((T, D), F32),
        scratch_shapes=[pltpu.VMEM((tm, D), BF16),
                        pltpu.VMEM((tm, D), BF16),
                        pltpu.VMEM((tm, tm), BF16),
                        pltpu.VMEM((tm, tm), BF16),
                        pltpu.VMEM((tm, LANES), F32),
                        pltpu.SMEM((GROUP_ROWS,), jnp.int32)],
        compiler_params=_params("arbitrary", "arbitrary"),
    )(x, norm_w.reshape(1, D), wrt, brt, w_gate, w_up, w_down)


POOL_HALO = 16


def _pool_kernel(x_ref, halo_ref, nw_ref, pw_ref, ps_ref, o_ref, hb):
    TS, D = x_ref.shape
    n_groups, pc, _ = pw_ref.shape
    i = pl.program_id(1)
    nw = nw_ref[...]
    hb[0:POOL_HALO] = jnp.where(i > 0, _rms(halo_ref[...], nw), 0.0)
    hb[POOL_HALO:] = _rms(x_ref[...], nw)
    t = i * TS + lax.broadcasted_iota(jnp.int32, (TS, 1), 0)
    for g, w in enumerate(POOL_WINDOWS):
        cs = slice(g * pc, (g + 1) * pc)
        cur = hb[POOL_HALO:, cs]
        s = cur
        for r in range(1, w):
            s = s + hb[POOL_HALO - r:POOL_HALO - r + TS, cs]
        count = jnp.minimum(t + 1, w).astype(F32)
        d = s / count - cur
        mixed = _dot(d.astype(BF16), pw_ref[g])
        o_ref[:, cs] = x_ref[:, cs] + ps_ref[:, cs] * mixed


def _pool_layer(x, norm_w, pool_w, pool_scale, B, S, TS=256):
    T, D = x.shape
    assert len(POOL_WINDOWS) == pool_w.shape[0] and max(POOL_WINDOWS) - 1 <= POOL_HALO
    ns = S // TS
    r = TS // POOL_HALO
    pw = pool_w.astype(BF16)
    return pl.pallas_call(
        _pool_kernel,
        grid=(B, ns),
        in_specs=[pl.BlockSpec((TS, D), lambda b, i: (b * ns + i, 0)),
                  pl.BlockSpec((POOL_HALO, D), lambda b, i: (jnp.maximum((b * ns + i) * r - 1, 0), 0)),
                  _resident((1, D)), _resident(pw.shape), _resident((1, D))],
        out_specs=pl.BlockSpec((TS, D), lambda b, i: (b * ns + i, 0)),
        out_shape=jax.ShapeDtypeStruct((T, D), F32),
        scratch_shapes=[pltpu.VMEM((TS + POOL_HALO, D), F32)],
        compiler_params=_params("arbitrary", "arbitrary"),
    )(x, x, norm_w.reshape(1, D), pw, pool_scale.reshape(1, D))


def _chunk(S):
    return 256 if S % 256 == 0 else 128


def kernel(x, rel_bias, mix_norm_e, w_in_e, q_norm_e, k_norm_e, conv_w_e, conv_b_e, conv_ln_g_e, conv_ln_b_e,
           w_out_e, mix_norm_o, pool_w_o, pool_scale_o, ffn_norm, router_group_w, router_group_b,
           router_expert_w, router_expert_b, w_gate, w_up, w_down):
    B, S, D = x.shape
    T = B * S
    depth = ffn_norm.shape[0]
    n_heads = rel_bias.shape[1]
    idx_heads = (w_in_e.shape[2] - n_heads * HEAD_DIM - 2 * HEAD_DIM - IDX_DIM - 2 * conv_w_e.shape[2]) \
        // (IDX_DIM + 1)
    k_top = min(INDEX_TOPK, S // 4)
    C = _chunk(S)
    xf = x.reshape(T, D)
    bias_tiles, bias_stat = _rel_bias_tiles(rel_bias, C)
    for l in range(depth):
        i = l // 2
        if l % 2 == 0:
            q, k, v, iq, ik, iw, u = _in_proj(xf, mix_norm_e[i], w_in_e[i], q_norm_e[i], k_norm_e[i],
                                              n_heads, idx_heads, conv_w_e.shape[2])
            attn = _attention(q, k, v, iq, ik, iw, bias_tiles, bias_stat, B, S, k_top, C)
            conv = _conv_module(u, conv_w_e[i], conv_b_e[i], conv_ln_g_e[i], conv_ln_b_e[i], B, S)
            xf = _out_proj(xf, attn, conv, w_out_e[i])
        else:
            xf = _pool_layer(xf, mix_norm_o[i], pool_w_o[i], pool_scale_o[i], B, S)
        xf = _moe(xf, ffn_norm[l], router_group_w[l], router_group_b[l], router_expert_w[l], router_expert_b[l],
                  w_gate[l].astype(BF16), w_up[l].astype(BF16), w_down[l].astype(BF16))
    return xf.reshape(B, S, D)
```

```python
import functools
import math

import jax
import jax.numpy as jnp
from jax import lax
from jax.experimental import pallas as pl
from jax.experimental.pallas import tpu as pltpu

F32 = jnp.float32
BF16 = jnp.bfloat16

NORM_EPS = 1e-6
HEAD_DIM = 128
IDX_DIM = 64
INDEX_TOPK = 256
REL_BUCKETS = 32
REL_MAX_DIST = 128
POOL_WINDOWS = (2, 4, 8, 16)
EXPERTS_PER_GROUP = 4
LANES = 128
VMEM_LIMIT = 56 * 1024 * 1024
NEG = -1e30
INT_MIN = -(2 ** 31)


def _dot(a, b):
    return jnp.dot(a, b, preferred_element_type=F32)


def _dot_nt(a, b):
    return lax.dot_general(a, b, (((1,), (1,)), ((), ())), preferred_element_type=F32)


def _rms(x, w):
    return x * lax.rsqrt(jnp.mean(x * x, axis=-1, keepdims=True) + NORM_EPS) * w


def _params(*sem):
    return pltpu.CompilerParams(dimension_semantics=sem, vmem_limit_bytes=VMEM_LIMIT)


def _resident(shape):
    nd = len(shape)
    return pl.BlockSpec(shape, lambda *_: (0,) * nd, pipeline_mode=pl.Buffered(1))


def _in_proj_kernel(x_ref, nw_ref, qn_ref, kn_ref, wq_ref, wkv_ref, wiq_ref, wikw_ref, wa_ref, wg_ref,
                    q_ref, k_ref, v_ref, iq_ref, ik_ref, iw_ref, u_ref, *, q_scale, iw_scale):
    h = _rms(x_ref[...], nw_ref[...]).astype(BF16)
    n_pairs = q_ref.shape[0] // 2
    qn = qn_ref[...] * q_scale
    for c in range(n_pairs):
        qq = _dot(h, wq_ref[:, c * 256:(c + 1) * 256])
        for s in range(2):
            qh = qq[:, s * HEAD_DIM:(s + 1) * HEAD_DIM]
            q_ref[2 * c + s] = _rms(qh, qn).astype(BF16)
    kv = _dot(h, wkv_ref[...])
    k_ref[...] = _rms(kv[:, :HEAD_DIM], kn_ref[...]).astype(BF16)
    v_ref[...] = kv[:, HEAD_DIM:].astype(BF16)
    for c in range(iq_ref.shape[0] // 2):
        r = _dot(h, wiq_ref[:, c * 256:(c + 1) * 256])
        iq_ref[2 * c] = r[:, :LANES].astype(BF16)
        iq_ref[2 * c + 1] = r[:, LANES:].astype(BF16)
    r = _dot(h, wikw_ref[...])
    ik_ref[...] = r[:, :LANES].astype(BF16)
    iw_ref[...] = r[:, LANES:] * iw_scale
    for c in range(u_ref.shape[1] // 256):
        cs = slice(c * 256, (c + 1) * 256)
        a = _dot(h, wa_ref[:, cs])
        g = _dot(h, wg_ref[:, cs])
        u_ref[:, cs] = (a * jax.nn.sigmoid(g)).astype(BF16)


def _in_proj(x, norm_w, w_in, q_norm, k_norm, n_heads, idx_heads, conv_ch, tm=256):
    T, D = x.shape
    attn_w = n_heads * HEAD_DIM
    iq_w = idx_heads * IDX_DIM
    o = 0
    wq = w_in[:, o:o + attn_w]; o += attn_w
    wkv = w_in[:, o:o + 2 * HEAD_DIM]; o += 2 * HEAD_DIM
    wiq = w_in[:, o:o + iq_w]; o += iq_w
    wik = w_in[:, o:o + IDX_DIM]; o += IDX_DIM
    wiw = w_in[:, o:o + idx_heads]; o += idx_heads
    wa = w_in[:, o:o + conv_ch]; o += conv_ch
    wg = w_in[:, o:o + conv_ch]; o += conv_ch
    assert o == w_in.shape[1] and 2 * IDX_DIM == LANES
    wikw = jnp.concatenate([wik, wik, wiw, jnp.zeros((D, LANES - idx_heads), w_in.dtype)], axis=1)
    ws = [w.astype(BF16) for w in (wq, wkv, wiq, wikw, wa, wg)]
    row = lambda w: pl.BlockSpec((tm, w), lambda i: (i, 0))
    heads = lambda n: pl.BlockSpec((n, tm, LANES), lambda i: (0, i, 0))
    kern = functools.partial(_in_proj_kernel, q_scale=HEAD_DIM ** -0.5,
                             iw_scale=(idx_heads ** -0.5) * (IDX_DIM ** -0.5))
    return pl.pallas_call(
        kern,
        grid=(T // tm,),
        in_specs=[row(D), _resident((1, D)), _resident((1, HEAD_DIM)), _resident((1, HEAD_DIM))]
                 + [_resident(w.shape) for w in ws],
        out_specs=[heads(n_heads), row(HEAD_DIM), row(HEAD_DIM), heads(iq_w // LANES), row(LANES), row(LANES),
                   row(conv_ch)],
        out_shape=[jax.ShapeDtypeStruct((n_heads, T, HEAD_DIM), BF16),
                   jax.ShapeDtypeStruct((T, HEAD_DIM), BF16),
                   jax.ShapeDtypeStruct((T, HEAD_DIM), BF16),
                   jax.ShapeDtypeStruct((iq_w // LANES, T, LANES), BF16),
                   jax.ShapeDtypeStruct((T, LANES), BF16),
                   jax.ShapeDtypeStruct((T, LANES), F32),
                   jax.ShapeDtypeStruct((T, conv_ch), BF16)],
        compiler_params=_params("arbitrary"),
    )(x, norm_w.reshape(1, D), q_norm.reshape(1, HEAD_DIM), k_norm.reshape(1, HEAD_DIM), *ws)


def _rel_bias_kernel(rb_ref, o_ref, stat_ref):
    _, n_heads, C, _ = o_ref.shape
    tau = lax.broadcasted_iota(jnp.int32, (C, C), 0)
    sig = lax.broadcasted_iota(jnp.int32, (C, C), 1)
    max_exact = REL_BUCKETS // 2
    for kind in range(2):
        d = tau - sig + kind * C
        n = jnp.maximum(d, 0)
        nf = jnp.maximum(n, 1).astype(F32)
        large = max_exact + (jnp.log(nf / max_exact) / math.log(REL_MAX_DIST / max_exact)
                             * (REL_BUCKETS - max_exact)).astype(jnp.int32)
        large = jnp.minimum(large, REL_BUCKETS - 1)
        bucket = jnp.where(n < max_exact, n, large)
        for h in range(n_heads):
            b = jnp.zeros((C, C), F32)
            for bk in range(REL_BUCKETS):
                b = jnp.where(bucket == bk, rb_ref[bk, h], b)
            b = b - rb_ref[REL_BUCKETS - 1, h]
            if kind == 0:
                b = jnp.where(d < 0, NEG, b)
            o_ref[kind, h] = b
    for h in range(n_heads):
        hi = rb_ref[0, h]
        lo = rb_ref[0, h]
        for bk in range(1, REL_BUCKETS):
            hi = jnp.maximum(hi, rb_ref[bk, h])
            lo = jnp.minimum(lo, rb_ref[bk, h])
        stat_ref[0, h] = hi - rb_ref[REL_BUCKETS - 1, h]
        stat_ref[1, h] = lo - rb_ref[REL_BUCKETS - 1, h]


def _rel_bias_tiles(rel_bias, C):
    n_heads = rel_bias.shape[1]
    assert C >= REL_MAX_DIST
    return pl.pallas_call(
        _rel_bias_kernel,
        in_specs=[pl.BlockSpec(memory_space=pltpu.SMEM)],
        out_specs=[pl.BlockSpec(memory_space=pltpu.VMEM), pl.BlockSpec(memory_space=pltpu.SMEM)],
        out_shape=[jax.ShapeDtypeStruct((2, n_heads, C, C), F32), jax.ShapeDtypeStruct((2, n_heads), F32)],
        compiler_params=pltpu.CompilerParams(vmem_limit_bytes=VMEM_LIMIT),
    )(rel_bias)


SHIFT_SPAN_LIMIT = 60.0


def _order_key(x):
    bits = pltpu.bitcast(x, jnp.int32)
    return bits ^ ((bits >> 31) & 0x7FFFFFFF)


def _row_to_col(row):
    C = row.shape[1]
    halves = []
    for part in (row >> 16, row & 0xFFFF):
        halves.append(jnp.broadcast_to(part.astype(F32), (LANES, C)).T[:, 0:1].astype(jnp.int32))
    return (halves[0] << 16) | halves[1]


def _attn_kernel(bstat_ref, q_ref, k_ref, v_ref, iq_ref, ik_ref, iw_ref, bias_ref, o_ref,
                 key_sc, keyt_sc, iqm_sc, vx_sc, kmax_sc, shift_sc, acc_sc, *, k_top):
    n_heads, C, _ = q_ref.shape
    idx_heads = iqm_sc.shape[0]
    idx_bits = max(1, (key_sc.shape[0] * C - 1).bit_length())
    qi = pl.program_id(1)
    nkv = qi + 1

    @pl.when(qi == 0)
    def _():
        vx_sc[:, :HEAD_DIM] = v_ref[...]
        vx_sc[:, HEAD_DIM:] = jnp.ones((vx_sc.shape[0], HEAD_DIM), BF16)
        kf = k_ref[...].astype(F32)
        k2 = jnp.sum(kf * kf, axis=-1, keepdims=True)
        kmax_sc[...] = jnp.broadcast_to(jnp.sqrt(jnp.max(k2, axis=0, keepdims=True)), kmax_sc.shape)

    lane = lax.broadcasted_iota(jnp.int32, (C, LANES), 1)
    for p in range(idx_heads // 2):
        qp = iq_ref[p].astype(F32)
        iqm_sc[2 * p] = jnp.where(lane < IDX_DIM, qp, 0.0).astype(BF16)
        iqm_sc[2 * p + 1] = jnp.where(lane >= IDX_DIM, qp, 0.0).astype(BF16)
    iw = iw_ref[...]
    tau = lax.broadcasted_iota(jnp.int32, (C, C), 0)
    sig = lax.broadcasted_iota(jnp.int32, (C, C), 1)

    def score_body(j, carry):
        off = pl.multiple_of(j * C, C)
        ikc = ik_ref[pl.ds(off, C), :]
        acc = jnp.zeros((C, C), F32)
        for hh in range(idx_heads):
            s = _dot_nt(iqm_sc[hh], ikc)
            acc = acc + jnp.maximum(s, 0.0) * iw[:, hh:hh + 1]
        acc = jnp.where(jnp.logical_and(j == qi, sig > tau), -jnp.inf, acc)
        key_sc[j] = _order_key(acc)
        keyt_sc[j] = _order_key(acc.T)
        return carry

    lax.fori_loop(0, nkv, score_body, 0)

    SUB = 32

    def count(hit):
        def body(j, cnt):
            for r in range(C // SUB):
                cnt = cnt + hit(j, r)
            return cnt
        cnt = lax.fori_loop(0, nkv, body, jnp.zeros((SUB, C), F32))
        return jnp.sum(cnt, axis=0, keepdims=True)

    def keyt(j, r):
        return keyt_sc[j, pl.ds(r * SUB, SUB), :]

    def bit_body(b, carry):
        res, n_res = carry
        cand = res ^ lax.shift_left(jnp.int32(1), 31 - b)
        tot = count(lambda j, r: jnp.where(keyt(j, r) >= cand, 1.0, 0.0))
        ok = tot >= k_top
        return jnp.where(ok, cand, res), jnp.where(ok, tot, n_res)

    n_bits = jnp.where(nkv * C <= k_top, 0, 32)
    thr_row, n_thr = lax.fori_loop(0, n_bits, bit_body,
                                   (jnp.full((1, C), INT_MIN, jnp.int32), jnp.zeros((1, C), F32)))
    thr = _row_to_col(thr_row)

    @pl.when(jnp.max(n_thr) > k_top)
    def _():
        need = k_top - count(lambda j, r: jnp.where(keyt(j, r) > thr_row, 1.0, 0.0))
        s_sub = lax.broadcasted_iota(jnp.int32, (SUB, C), 0)

        def idx_body(b, last):
            cand = last | lax.shift_left(jnp.int32(1), idx_bits - 1 - b)
            below = count(lambda j, r: jnp.where(
                keyt(j, r) == thr_row, jnp.where(j * C + r * SUB + s_sub < cand, 1.0, 0.0), 0.0))
            return jnp.where(below < need, cand, last)

        last = _row_to_col(lax.fori_loop(0, idx_bits, idx_body, jnp.zeros((1, C), jnp.int32)))

        def drop_body(j, carry):
            kk = key_sc[j]
            key_sc[j] = jnp.where(kk == thr, jnp.where(j * C + sig > last, kk - 1, kk), kk)
            return carry

        lax.fori_loop(0, nkv, drop_body, 0)

    kmax = kmax_sc[0:1, 0:1] * 1.001
    worst = jnp.zeros((C, 1), F32)
    for h in range(n_heads):
        qf = q_ref[h].astype(F32)
        bound = jnp.sqrt(jnp.sum(qf * qf, axis=-1, keepdims=True)) * kmax
        shift_sc[h] = jnp.broadcast_to(bound + bstat_ref[0, h], (C, LANES))
        worst = jnp.maximum(worst, 2.0 * bound + (bstat_ref[0, h] - bstat_ref[1, h]))
    loose = jnp.max(worst) > SHIFT_SPAN_LIMIT

    def logits(j, h, kind):
        off = pl.multiple_of(j * C, C)
        lg = _dot_nt(q_ref[h], k_ref[pl.ds(off, C), :])
        return lg if kind is None else lg + bias_ref[kind, h]

    def near_tiles(fn):
        @pl.when(qi > 0)
        def _():
            fn(qi - 1, 1)
        fn(qi, 0)

    def far_tiles(fn):
        def body(j, carry):
            fn(j, None)
            return carry
        lax.fori_loop(0, qi - 1, body, 0)

    @pl.when(loose)
    def _():
        for h in range(n_heads):
            shift_sc[h] = jnp.full((C, LANES), NEG, F32)

        def max_tile(j, kind):
            sel = key_sc[j] >= thr
            for h in range(n_heads):
                lg = jnp.where(sel, logits(j, h, kind), NEG)
                m = jnp.max(lg, axis=-1, keepdims=True)
                shift_sc[h] = jnp.maximum(shift_sc[h], jnp.broadcast_to(m, (C, LANES)))

        far_tiles(max_tile)
        near_tiles(max_tile)

    acc_sc[...] = jnp.zeros(acc_sc.shape, F32)

    def attn_tile(j, kind):
        off = pl.multiple_of(j * C, C)
        vx = vx_sc[pl.ds(off, C), :]
        sel = key_sc[j] >= thr
        for h in range(n_heads):
            sh = jnp.concatenate([shift_sc[h]] * (C // LANES), axis=1)
            p = jnp.where(sel, jnp.exp(logits(j, h, kind) - sh), 0.0)
            acc_sc[h] += _dot(p.astype(BF16), vx)

    far_tiles(attn_tile)
    near_tiles(attn_tile)
    for h in range(n_heads):
        a = acc_sc[h]
        o_ref[:, h * HEAD_DIM:(h + 1) * HEAD_DIM] = (a[:, :HEAD_DIM] / a[:, HEAD_DIM:]).astype(BF16)


def _attention(q, k, v, iq, ik, iw, bias_tiles, bias_stat, B, S, k_top, C):
    n_heads, T, _ = q.shape
    n_pairs = iq.shape[0]
    nq = S // C
    kern = functools.partial(_attn_kernel, k_top=k_top)
    heads = lambda n: pl.BlockSpec((n, C, LANES), lambda b, i: (0, b * nq + i, 0))
    seq = pl.BlockSpec((S, LANES), lambda b, i: (b, 0))
    return pl.pallas_call(
        kern,
        grid=(B, nq),
        in_specs=[pl.BlockSpec(memory_space=pltpu.SMEM),
                  heads(n_heads), seq, seq, heads(n_pairs), seq,
                  pl.BlockSpec((C, LANES), lambda b, i: (b * nq + i, 0)),
                  _resident(bias_tiles.shape)],
        out_specs=pl.BlockSpec((C, n_heads * HEAD_DIM), lambda b, i: (b * nq + i, 0)),
        out_shape=jax.ShapeDtypeStruct((T, n_heads * HEAD_DIM), BF16),
        scratch_shapes=[pltpu.VMEM((nq, C, C), jnp.int32),
                        pltpu.VMEM((nq, C, C), jnp.int32),
                        pltpu.VMEM((2 * n_pairs, C, LANES), BF16),
                        pltpu.VMEM((S, 2 * HEAD_DIM), BF16),
                        pltpu.VMEM((8, LANES), F32),
                        pltpu.VMEM((n_heads, C, LANES), F32),
                        pltpu.VMEM((n_heads, C, 2 * HEAD_DIM), F32)],
        compiler_params=_params("arbitrary", "arbitrary"),
    )(bias_stat, q, k, v, iq, ik, iw, bias_tiles)


CONV_HALO = 32


def _conv_kernel(u_ref, halo_ref, w_ref, cb_ref, g_ref, b_ref, o_ref, buf, y_sc, *, width):
    TS, CH = u_ref.shape
    i = pl.program_id(1)
    buf[0:CONV_HALO] = jnp.where(i > 0, halo_ref[...].astype(F32), 0.0)
    buf[CONV_HALO:] = u_ref[...].astype(F32)
    base = CONV_HALO - (width - 1)
    for c in range(CH // LANES):
        cs = slice(c * LANES, (c + 1) * LANES)
        acc = jnp.zeros((TS, LANES), F32)
        for j in range(width):
            acc = acc + w_ref[j:j + 1, cs] * buf[base + j:base + j + TS, cs]
        y_sc[:, cs] = acc + cb_ref[:, cs]
    y = y_sc[...]
    mu = jnp.mean(y, axis=-1, keepdims=True)
    yc = y - mu
    var = jnp.mean(yc * yc, axis=-1, keepdims=True)
    yn = yc * lax.rsqrt(var + NORM_EPS) * g_ref[...] + b_ref[...]
    o_ref[...] = (yn * jax.nn.sigmoid(yn)).astype(BF16)


def _conv_module(u, conv_w, conv_b, ln_g, ln_b, B, S, TS=256):
    T, CH = u.shape
    width = conv_w.shape[0]
    assert width - 1 <= CONV_HALO
    ns = S // TS
    r = TS // CONV_HALO
    wpad = jnp.zeros((CONV_HALO, CH), F32).at[:width].set(conv_w)
    vec = lambda a: a.reshape(1, CH)
    kern = functools.partial(_conv_kernel, width=width)
    return pl.pallas_call(
        kern,
        grid=(B, ns),
        in_specs=[pl.BlockSpec((TS, CH), lambda b, i: (b * ns + i, 0)),
                  pl.BlockSpec((CONV_HALO, CH), lambda b, i: (jnp.maximum((b * ns + i) * r - 1, 0), 0)),
                  _resident((CONV_HALO, CH)), _resident((1, CH)), _resident((1, CH)), _resident((1, CH))],
        out_specs=pl.BlockSpec((TS, CH), lambda b, i: (b * ns + i, 0)),
        out_shape=jax.ShapeDtypeStruct((T, CH), BF16),
        scratch_shapes=[pltpu.VMEM((TS + CONV_HALO, CH), F32), pltpu.VMEM((TS, CH), F32)],
        compiler_params=_params("arbitrary", "arbitrary"),
    )(u, u, wpad, vec(conv_b), vec(ln_g), vec(ln_b))


def _out_proj_kernel(x_ref, a_ref, c_ref, wa_ref, wc_ref, o_ref):
    o_ref[...] = x_ref[...] + _dot(a_ref[...], wa_ref[...]) + _dot(c_ref[...], wc_ref[...])


def _out_proj(x, attn, conv, w_out, tm=512):
    T, D = x.shape
    aw, cw = attn.shape[1], conv.shape[1]
    wa = w_out[:aw].astype(BF16)
    wc = w_out[aw:].astype(BF16)
    row = lambda w: pl.BlockSpec((tm, w), lambda i: (i, 0))
    return pl.pallas_call(
        _out_proj_kernel,
        grid=(T // tm,),
        in_specs=[row(D), row(aw), row(cw), _resident(wa.shape), _resident(wc.shape)],
        out_specs=row(D),
        out_shape=jax.ShapeDtypeStruct((T, D), F32),
        compiler_params=_params("arbitrary"),
    )(x, attn, conv, wa, wc)


def _first_max(vals):
    m = vals[0]
    for v in vals[1:]:
        m = jnp.maximum(m, v)
    idx = jnp.full(m.shape, len(vals) - 1, jnp.int32)
    for k in range(len(vals) - 2, -1, -1):
        idx = jnp.where(vals[k] == m, k, idx)
    return m, idx


def _softmax_cols(cols):
    m = cols[0]
    for c in cols[1:]:
        m = jnp.maximum(m, c)
    e = [jnp.exp(c - m) for c in cols]
    s = e[0]
    for c in e[1:]:
        s = s + c
    return [c / s for c in e]


GROUP_ROWS = 8
MOE_CHUNK = 128


def _split3(a):
    hi = a.astype(BF16)
    r = a - hi.astype(F32)
    mid = r.astype(BF16)
    lo = (r - mid.astype(F32)).astype(BF16)
    return hi, mid, lo


def _moe_kernel(x_ref, nw_ref, wrt_ref, brt_ref, wg_ref, wu_ref, wd_ref, o_ref,
                h_sc, hs_sc, p_sc, u_sc, cs_sc, seg_sm, *, n_groups):
    TM, D = x_ref.shape
    i = pl.program_id(0)
    e = pl.program_id(1)
    n_exp = pl.num_programs(1)

    @pl.when(jnp.logical_and(i == 0, e == 0))
    def _():
        s_io = lax.broadcasted_iota(jnp.int32, (TM, TM), 0)
        t_io = lax.broadcasted_iota(jnp.int32, (TM, TM), 1)
        u_sc[...] = jnp.where(s_io < t_io, 1.0, 0.0).astype(BF16)

    @pl.when(e == 0)
    def _():
        h = _rms(x_ref[...], nw_ref[...]).astype(BF16)
        h_sc[...] = h
        lg = _dot_nt(wrt_ref[...], h) + brt_ref[...]
        row = lambda k: lg[k:k + 1, :]
        g_prob = _softmax_cols([row(g) for g in range(n_groups)])
        g_p, g_idx = _first_max(g_prob)
        e_logit = []
        for k in range(EXPERTS_PER_GROUP):
            v = row(GROUP_ROWS + k)
            for g in range(1, n_groups):
                v = jnp.where(g_idx == g, row(GROUP_ROWS + g * EXPERTS_PER_GROUP + k), v)
            e_logit.append(v)
        e_prob = _softmax_cols(e_logit)
        p1, i1 = _first_max(e_prob)
        rest = [jnp.where(i1 == k, -1.0, e_prob[k]) for k in range(EXPERTS_PER_GROUP)]
        p2, i2 = _first_max(rest)
        den = p1 + p2
        base = g_idx * EXPERTS_PER_GROUP
        e_io = lax.broadcasted_iota(jnp.int32, (LANES, TM), 0)
        comb_t = (jnp.where(e_io == base + i1, g_p * (p1 / den), 0.0)
                  + jnp.where(e_io == base + i2, g_p * (p2 / den), 0.0))

        g_io = lax.broadcasted_iota(jnp.int32, (GROUP_ROWS, TM), 0)
        onehot_t = jnp.where(g_io == g_idx, 1.0, 0.0)
        rank = _dot(onehot_t.astype(BF16), u_sc[...])
        start = jnp.int32(0)
        pos = jnp.zeros((1, TM), F32)
        for g in range(n_groups):
            seg_sm[g] = start
            pos = jnp.where(g_idx == g, start.astype(F32) + rank[g:g + 1, :], pos)
            start = start + jnp.sum(onehot_t[g:g + 1, :]).astype(jnp.int32)
        seg_sm[n_groups] = start
        pos = pos.astype(jnp.int32)

        rb = 256
        for r0 in range(0, TM, rb):
            r_io = lax.broadcasted_iota(jnp.int32, (rb, TM), 0) + r0
            p_sc[r0:r0 + rb, :] = jnp.where(r_io == pos, 1.0, 0.0).astype(BF16)
        p = p_sc[...]
        for c0 in range(0, D, 512):
            hs_sc[:, c0:c0 + 512] = _dot(p, h_sc[:, c0:c0 + 512]).astype(BF16)
        cs = jnp.zeros((TM, LANES), F32)
        for part in _split3(comb_t):
            cs = cs + _dot_nt(p, part)
        cs_sc[...] = cs
        o_ref[...] = jnp.zeros(o_ref.shape, F32)

    g = e // EXPERTS_PER_GROUP
    start = seg_sm[g]
    end = seg_sm[g + 1]
    c_lo = start // MOE_CHUNK
    c_hi = jnp.where(end > start, (end + MOE_CHUNK - 1) // MOE_CHUNK, c_lo)
    lane = lax.broadcasted_iota(jnp.int32, (MOE_CHUNK, LANES), 1)

    def chunk(c, carry):
        r0 = pl.multiple_of(c * MOE_CHUNK, MOE_CHUNK)
        rows = hs_sc[pl.ds(r0, MOE_CHUNK), :]
        w = jnp.sum(jnp.where(lane == e, cs_sc[pl.ds(r0, MOE_CHUNK), :], 0.0), axis=-1, keepdims=True)
        a = jax.nn.silu(_dot(rows, wg_ref[0])) * _dot(rows, wu_ref[0]) * w
        o_ref[pl.ds(r0, MOE_CHUNK), :] += _dot(a.astype(BF16), wd_ref[0])
        return carry

    lax.fori_loop(c_lo, c_hi, chunk, 0)

    @pl.when(e == n_exp - 1)
    def _():
        p = p_sc[...]
        for c0 in range(0, D, 512):
            ys = o_ref[:, c0:c0 + 512].astype(BF16)
            y = lax.dot_general(p, ys, (((0,), (0,)), ((), ())), preferred_element_type=F32)
            o_ref[:, c0:c0 + 512] = x_ref[:, c0:c0 + 512] + y


def _moe(x, norm_w, wg, bg, we, be, w_gate, w_up, w_down, tm=1024):
    T, D = x.shape
    n_groups, n_exp = wg.shape[1], we.shape[1]
    _, _, ff = w_gate.shape
    assert n_groups <= GROUP_ROWS and n_exp == n_groups * EXPERTS_PER_GROUP and T % tm == 0
    rows = GROUP_ROWS + n_exp
    wrt = jnp.zeros((rows, D), F32).at[:n_groups].set(wg.T).at[GROUP_ROWS:].set(we.T).astype(BF16)
    brt = jnp.zeros((rows, 1), F32).at[:n_groups, 0].set(bg).at[GROUP_ROWS:, 0].set(be)
    tile = pl.BlockSpec((tm, D), lambda i, e: (i, 0))
    return pl.pallas_call(
        functools.partial(_moe_kernel, n_groups=n_groups),
        grid=(T // tm, n_exp),
        in_specs=[pl.BlockSpec((tm, D), lambda i, e: (i, 0), pipeline_mode=pl.Buffered(1)),
                  _resident((1, D)), _resident((rows, D)), _resident((rows, 1)),
                  pl.BlockSpec((1, D, ff), lambda i, e: (e, 0, 0)),
                  pl.BlockSpec((1, D, ff), lambda i, e: (e, 0, 0)),
                  pl.BlockSpec((1, ff, D), lambda i, e: (e, 0, 0))],
        out_specs=tile,
        out_shape=jax.ShapeDtypeStruct((T, D), F32),
        scratch_shapes=[pltpu.VMEM((tm, D), BF16),
                        pltpu.VMEM((tm, D), BF16),
                        pltpu.VMEM((tm, tm), BF16),
                        pltpu.VMEM((tm, tm), BF16),
                        pltpu.VMEM((tm, LANES), F32),
                        pltpu.SMEM((GROUP_ROWS,), jnp.int32)],
        compiler_params=_params("arbitrary", "arbitrary"),
    )(x, norm_w.reshape(1, D), wrt, brt, w_gate, w_up, w_down)


POOL_HALO = 16


def _pool_kernel(x_ref, halo_ref, nw_ref, pw_ref, ps_ref, o_ref, hb):
    TS, D = x_ref.shape
    n_groups, pc, _ = pw_ref.shape
    i = pl.program_id(1)
    nw = nw_ref[...]
    hb[0:POOL_HALO] = jnp.where(i > 0, _rms(halo_ref[...], nw), 0.0)
    hb[POOL_HALO:] = _rms(x_ref[...], nw)
    t = i * TS + lax.broadcasted_iota(jnp.int32, (TS, 1), 0)
    for g, w in enumerate(POOL_WINDOWS):
        cs = slice(g * pc, (g + 1) * pc)
        cur = hb[POOL_HALO:, cs]
        s = cur
        for r in range(1, w):
            s = s + hb[POOL_HALO - r:POOL_HALO - r + TS, cs]
        count = jnp.minimum(t + 1, w).astype(F32)
        d = s / count - cur
        mixed = _dot(d.astype(BF16), pw_ref[g])
        o_ref[:, cs] = x_ref[:, cs] + ps_ref[:, cs] * mixed


def _pool_layer(x, norm_w, pool_w, pool_scale, B, S, TS=256):
    T, D = x.shape
    assert len(POOL_WINDOWS) == pool_w.shape[0] and max(POOL_WINDOWS) - 1 <= POOL_HALO
    ns = S // TS
    r = TS // POOL_HALO
    pw = pool_w.astype(BF16)
    return pl.pallas_call(
        _pool_kernel,
        grid=(B, ns),
        in_specs=[pl.BlockSpec((TS, D), lambda b, i: (b * ns + i, 0)),
                  pl.BlockSpec((POOL_HALO, D), lambda b, i: (jnp.maximum((b * ns + i) * r - 1, 0), 0)),
                  _resident((1, D)), _resident(pw.shape), _resident((1, D))],
        out_specs=pl.BlockSpec((TS, D), lambda b, i: (b * ns + i, 0)),
        out_shape=jax.ShapeDtypeStruct((T, D), F32),
        scratch_shapes=[pltpu.VMEM((TS + POOL_HALO, D), F32)],
        compiler_params=_params("arbitrary", "arbitrary"),
    )(x, x, norm_w.reshape(1, D), pw, pool_scale.reshape(1, D))


def _chunk(S):
    return 256 if S % 256 == 0 else 128


def kernel(x, rel_bias, mix_norm_e, w_in_e, q_norm_e, k_norm_e, conv_w_e, conv_b_e, conv_ln_g_e, conv_ln_b_e,
           w_out_e, mix_norm_o, pool_w_o, pool_scale_o, ffn_norm, router_group_w, router_group_b,
           router_expert_w, router_expert_b, w_gate, w_up, w_down):
    B, S, D = x.shape
    T = B * S
    depth = ffn_norm.shape[0]
    n_heads = rel_bias.shape[1]
    idx_heads = (w_in_e.shape[2] - n_heads * HEAD_DIM - 2 * HEAD_DIM - IDX_DIM - 2 * conv_w_e.shape[2]) \
        // (IDX_DIM + 1)
    k_top = min(INDEX_TOPK, S // 4)
    C = _chunk(S)
    xf = x.reshape(T, D)
    bias_tiles, bias_stat = _rel_bias_tiles(rel_bias, C)
    for l in range(depth):
        i = l // 2
        if l % 2 == 0:
            q, k, v, iq, ik, iw, u = _in_proj(xf, mix_norm_e[i], w_in_e[i], q_norm_e[i], k_norm_e[i],
                                              n_heads, idx_heads, conv_w_e.shape[2])
            attn = _attention(q, k, v, iq, ik, iw, bias_tiles, bias_stat, B, S, k_top, C)
            conv = _conv_module(u, conv_w_e[i], conv_b_e[i], conv_ln_g_e[i], conv_ln_b_e[i], B, S)
            xf = _out_proj(xf, attn, conv, w_out_e[i])
        else:
            xf = _pool_layer(xf, mix_norm_o[i], pool_w_o[i], pool_scale_o[i], B, S)
        xf = _moe(xf, ffn_norm[l], router_group_w[l], router_group_b[l], router_expert_w[l], router_expert_b[l],
                  w_gate[l].astype(BF16), w_up[l].astype(BF16), w_down[l].astype(BF16))
    return xf.reshape(B, S, D)
```

```python
import functools
import math

import jax
import jax.numpy as jnp
from jax import lax
from jax.experimental import pallas as pl
from jax.experimental.pallas import tpu as pltpu

F32 = jnp.float32
BF16 = jnp.bfloat16

NORM_EPS = 1e-6
HEAD_DIM = 128
IDX_DIM = 64
INDEX_TOPK = 256
REL_BUCKETS = 32
REL_MAX_DIST = 128
POOL_WINDOWS = (2, 4, 8, 16)
EXPERTS_PER_GROUP = 4
LANES = 128
VMEM_LIMIT = 56 * 1024 * 1024
NEG = -1e30
INT_MIN = -(2 ** 31)


def _dot(a, b):
    return jnp.dot(a, b, preferred_element_type=F32)


def _dot_nt(a, b):
    return lax.dot_general(a, b, (((1,), (1,)), ((), ())), preferred_element_type=F32)


def _rms(x, w):
    return x * lax.rsqrt(jnp.mean(x * x, axis=-1, keepdims=True) + NORM_EPS) * w


def _params(*sem):
    return pltpu.CompilerParams(dimension_semantics=sem, vmem_limit_bytes=VMEM_LIMIT)


def _resident(shape):
    nd = len(shape)
    return pl.BlockSpec(shape, lambda *_: (0,) * nd, pipeline_mode=pl.Buffered(1))


def _in_proj_kernel(x_ref, nw_ref, qn_ref, kn_ref, wq_ref, wkv_ref, wiq_ref, wikw_ref, wa_ref, wg_ref,
                    q_ref, k_ref, v_ref, iq_ref, ik_ref, iw_ref, u_ref, *, q_scale, iw_scale):
    h = _rms(x_ref[...], nw_ref[...]).astype(BF16)
    n_pairs = q_ref.shape[0] // 2
    qn = qn_ref[...] * q_scale
    for c in range(n_pairs):
        qq = _dot(h, wq_ref[:, c * 256:(c + 1) * 256])
        for s in range(2):
            qh = qq[:, s * HEAD_DIM:(s + 1) * HEAD_DIM]
            q_ref[2 * c + s] = _rms(qh, qn).astype(BF16)
    kv = _dot(h, wkv_ref[...])
    k_ref[...] = _rms(kv[:, :HEAD_DIM], kn_ref[...]).astype(BF16)
    v_ref[...] = kv[:, HEAD_DIM:].astype(BF16)
    for c in range(iq_ref.shape[0] // 2):
        r = _dot(h, wiq_ref[:, c * 256:(c + 1) * 256])
        iq_ref[2 * c] = r[:, :LANES].astype(BF16)
        iq_ref[2 * c + 1] = r[:, LANES:].astype(BF16)
    r = _dot(h, wikw_ref[...])
    ik_ref[...] = r[:, :LANES].astype(BF16)
    iw_ref[...] = r[:, LANES:] * iw_scale
    for c in range(u_ref.shape[1] // 256):
        cs = slice(c * 256, (c + 1) * 256)
        a = _dot(h, wa_ref[:, cs])
        g = _dot(h, wg_ref[:, cs])
        u_ref[:, cs] = (a * jax.nn.sigmoid(g)).astype(BF16)


def _in_proj(x, norm_w, w_in, q_norm, k_norm, n_heads, idx_heads, conv_ch, tm=256):
    T, D = x.shape
    attn_w = n_heads * HEAD_DIM
    iq_w = idx_heads * IDX_DIM
    o = 0
    wq = w_in[:, o:o + attn_w]; o += attn_w
    wkv = w_in[:, o:o + 2 * HEAD_DIM]; o += 2 * HEAD_DIM
    wiq = w_in[:, o:o + iq_w]; o += iq_w
    wik = w_in[:, o:o + IDX_DIM]; o += IDX_DIM
    wiw = w_in[:, o:o + idx_heads]; o += idx_heads
    wa = w_in[:, o:o + conv_ch]; o += conv_ch
    wg = w_in[:, o:o + conv_ch]; o += conv_ch
    assert o == w_in.shape[1] and 2 * IDX_DIM == LANES
    wikw = jnp.concatenate([wik, wik, wiw, jnp.zeros((D, LANES - idx_heads), w_in.dtype)], axis=1)
    ws = [w.astype(BF16) for w in (wq, wkv, wiq, wikw, wa, wg)]
    row = lambda w: pl.BlockSpec((tm, w), lambda i: (i, 0))
    heads = lambda n: pl.BlockSpec((n, tm, LANES), lambda i: (0, i, 0))
    kern = functools.partial(_in_proj_kernel, q_scale=HEAD_DIM ** -0.5,
                             iw_scale=(idx_heads ** -0.5) * (IDX_DIM ** -0.5))
    return pl.pallas_call(
        kern,
        grid=(T // tm,),
        in_specs=[row(D), _resident((1, D)), _resident((1, HEAD_DIM)), _resident((1, HEAD_DIM))]
                 + [_resident(w.shape) for w in ws],
        out_specs=[heads(n_heads), row(HEAD_DIM), row(HEAD_DIM), heads(iq_w // LANES), row(LANES), row(LANES),
                   row(conv_ch)],
        out_shape=[jax.ShapeDtypeStruct((n_heads, T, HEAD_DIM), BF16),
                   jax.ShapeDtypeStruct((T, HEAD_DIM), BF16),
                   jax.ShapeDtypeStruct((T, HEAD_DIM), BF16),
                   jax.ShapeDtypeStruct((iq_w // LANES, T, LANES), BF16),
                   jax.ShapeDtypeStruct((T, LANES), BF16),
                   jax.ShapeDtypeStruct((T, LANES), F32),
                   jax.ShapeDtypeStruct((T, conv_ch), BF16)],
        compiler_params=_params("arbitrary"),
    )(x, norm_w.reshape(1, D), q_norm.reshape(1, HEAD_DIM), k_norm.reshape(1, HEAD_DIM), *ws)


def _rel_bias_kernel(rb_ref, o_ref, stat_ref):
    _, n_heads, C, _ = o_ref.shape
    tau = lax.broadcasted_iota(jnp.int32, (C, C), 0)
    sig = lax.broadcasted_iota(jnp.int32, (C, C), 1)
    max_exact = REL_BUCKETS // 2
    for kind in range(2):
        d = tau - sig + kind * C
        n = jnp.maximum(d, 0)
        nf = jnp.maximum(n, 1).astype(F32)
        large = max_exact + (jnp.log(nf / max_exact) / math.log(REL_MAX_DIST / max_exact)
                             * (REL_BUCKETS - max_exact)).astype(jnp.int32)
        large = jnp.minimum(large, REL_BUCKETS - 1)
        bucket = jnp.where(n < max_exact, n, large)
        for h in range(n_heads):
            b = jnp.zeros((C, C), F32)
            for bk in range(REL_BUCKETS):
                b = jnp.where(bucket == bk, rb_ref[bk, h], b)
            b = b - rb_ref[REL_BUCKETS - 1, h]
            if kind == 0:
                b = jnp.where(d < 0, NEG, b)
            o_ref[kind, h] = b
    for h in range(n_heads):
        hi = rb_ref[0, h]
        lo = rb_ref[0, h]
        for bk in range(1, REL_BUCKETS):
            hi = jnp.maximum(hi, rb_ref[bk, h])
            lo = jnp.minimum(lo, rb_ref[bk, h])
        stat_ref[0, h] = hi - rb_ref[REL_BUCKETS - 1, h]
        stat_ref[1, h] = lo - rb_ref[REL_BUCKETS - 1, h]


def _rel_bias_tiles(rel_bias, C):
    n_heads = rel_bias.shape[1]
    assert C >= REL_MAX_DIST
    return pl.pallas_call(
        _rel_bias_kernel,
        in_specs=[pl.BlockSpec(memory_space=pltpu.SMEM)],
        out_specs=[pl.BlockSpec(memory_space=pltpu.VMEM), pl.BlockSpec(memory_space=pltpu.SMEM)],
        out_shape=[jax.ShapeDtypeStruct((2, n_heads, C, C), F32), jax.ShapeDtypeStruct((2, n_heads), F32)],
        compiler_params=pltpu.CompilerParams(vmem_limit_bytes=VMEM_LIMIT),
    )(rel_bias)


SHIFT_SPAN_LIMIT = 60.0


def _order_key(x):
    bits = pltpu.bitcast(x, jnp.int32)
    return bits ^ ((bits >> 31) & 0x7FFFFFFF)


def _row_to_col(row):
    C = row.shape[1]
    halves = []
    for part in (row >> 16, row & 0xFFFF):
        halves.append(jnp.broadcast_to(part.astype(F32), (LANES, C)).T[:, 0:1].astype(jnp.int32))
    return (halves[0] << 16) | halves[1]


def _attn_kernel(bstat_ref, q_ref, k_ref, v_ref, iq_ref, ik_ref, iw_ref, bias_ref, o_ref,
                 key_sc, keyt_sc, iqm_sc, vx_sc, kmax_sc, shift_sc, acc_sc, *, k_top):
    n_heads, C, _ = q_ref.shape
    idx_heads = iqm_sc.shape[0]
    idx_bits = max(1, (key_sc.shape[0] * C - 1).bit_length())
    qi = pl.program_id(1)
    nkv = qi + 1

    @pl.when(qi == 0)
    def _():
        vx_sc[:, :HEAD_DIM] = v_ref[...]
        vx_sc[:, HEAD_DIM:] = jnp.ones((vx_sc.shape[0], HEAD_DIM), BF16)
        kf = k_ref[...].astype(F32)
        k2 = jnp.sum(kf * kf, axis=-1, keepdims=True)
        kmax_sc[...] = jnp.broadcast_to(jnp.sqrt(jnp.max(k2, axis=0, keepdims=True)), kmax_sc.shape)

    lane = lax.broadcasted_iota(jnp.int32, (C, LANES), 1)
    for p in range(idx_heads // 2):
        qp = iq_ref[p].astype(F32)
        iqm_sc[2 * p] = jnp.where(lane < IDX_DIM, qp, 0.0).astype(BF16)
        iqm_sc[2 * p + 1] = jnp.where(lane >= IDX_DIM, qp, 0.0).astype(BF16)
    iw = iw_ref[...]
    tau = lax.broadcasted_iota(jnp.int32, (C, C), 0)
    sig = lax.broadcasted_iota(jnp.int32, (C, C), 1)

    def score_body(j, carry):
        off = pl.multiple_of(j * C, C)
        ikc = ik_ref[pl.ds(off, C), :]
        acc = jnp.zeros((C, C), F32)
        for hh in range(idx_heads):
            s = _dot_nt(iqm_sc[hh], ikc)
            acc = acc + jnp.maximum(s, 0.0) * iw[:, hh:hh + 1]
        acc = jnp.where(jnp.logical_and(j == qi, sig > tau), -jnp.inf, acc)
        key_sc[j] = _order_key(acc)
        keyt_sc[j] = _order_key(acc.T)
        return carry

    lax.fori_loop(0, nkv, score_body, 0)

    SUB = 32

    def count(hit):
        def body(j, cnt):
            for r in range(C // SUB):
                cnt = cnt + hit(j, r)
            return cnt
        cnt = lax.fori_loop(0, nkv, body, jnp.zeros((SUB, C), F32))
        return jnp.sum(cnt, axis=0, keepdims=True)

    def keyt(j, r):
        return keyt_sc[j, pl.ds(r * SUB, SUB), :]

    def bit_body(b, carry):
        res, n_res = carry
        cand = res ^ lax.shift_left(jnp.int32(1), 31 - b)
        tot = count(lambda j, r: jnp.where(keyt(j, r) >= cand, 1.0, 0.0))
        ok = tot >= k_top
        return jnp.where(ok, cand, res), jnp.where(ok, tot, n_res)

    n_bits = jnp.where(nkv * C <= k_top, 0, 32)
    thr_row, n_thr = lax.fori_loop(0, n_bits, bit_body,
                                   (jnp.full((1, C), INT_MIN, jnp.int32), jnp.zeros((1, C), F32)))
    thr = _row_to_col(thr_row)

    @pl.when(jnp.max(n_thr) > k_top)
    def _():
        need = k_top - count(lambda j, r: jnp.where(keyt(j, r) > thr_row, 1.0, 0.0))
        s_sub = lax.broadcasted_iota(jnp.int32, (SUB, C), 0)

        def idx_body(b, last):
            cand = last | lax.shift_left(jnp.int32(1), idx_bits - 1 - b)
            below = count(lambda j, r: jnp.where(
                keyt(j, r) == thr_row, jnp.where(j * C + r * SUB + s_sub < cand, 1.0, 0.0), 0.0))
            return jnp.where(below < need, cand, last)

        last = _row_to_col(lax.fori_loop(0, idx_bits, idx_body, jnp.zeros((1, C), jnp.int32)))

        def drop_body(j, carry):
            kk = key_sc[j]
            key_sc[j] = jnp.where(kk == thr, jnp.where(j * C + sig > last, kk - 1, kk), kk)
            return carry

        lax.fori_loop(0, nkv, drop_body, 0)

    kmax = kmax_sc[0:1, 0:1] * 1.001
    worst = jnp.zeros((C, 1), F32)
    for h in range(n_heads):
        qf = q_ref[h].astype(F32)
        bound = jnp.sqrt(jnp.sum(qf * qf, axis=-1, keepdims=True)) * kmax
        shift_sc[h] = jnp.broadcast_to(bound + bstat_ref[0, h], (C, LANES))
        worst = jnp.maximum(worst, 2.0 * bound + (bstat_ref[0, h] - bstat_ref[1, h]))
    loose = jnp.max(worst) > SHIFT_SPAN_LIMIT

    def logits(j, h, kind):
        off = pl.multiple_of(j * C, C)
        lg = _dot_nt(q_ref[h], k_ref[pl.ds(off, C), :])
        return lg if kind is None else lg + bias_ref[kind, h]

    def near_tiles(fn):
        @pl.when(qi > 0)
        def _():
            fn(qi - 1, 1)
        fn(qi, 0)

    def far_tiles(fn):
        def body(j, carry):
            fn(j, None)
            return carry
        lax.fori_loop(0, qi - 1, body, 0)

    @pl.when(loose)
    def _():
        for h in range(n_heads):
            shift_sc[h] = jnp.full((C, LANES), NEG, F32)

        def max_tile(j, kind):
            sel = key_sc[j] >= thr
            for h in range(n_heads):
                lg = jnp.where(sel, logits(j, h, kind), NEG)
                m = jnp.max(lg, axis=-1, keepdims=True)
                shift_sc[h] = jnp.maximum(shift_sc[h], jnp.broadcast_to(m, (C, LANES)))

        far_tiles(max_tile)
        near_tiles(max_tile)

    acc_sc[...] = jnp.zeros(acc_sc.shape, F32)

    def attn_tile(j, kind):
        off = pl.multiple_of(j * C, C)
        vx = vx_sc[pl.ds(off, C), :]
        sel = key_sc[j] >= thr
        for h in range(n_heads):
            sh = jnp.concatenate([shift_sc[h]] * (C // LANES), axis=1)
            p = jnp.where(sel, jnp.exp(logits(j, h, kind) - sh), 0.0)
            acc_sc[h] += _dot(p.astype(BF16), vx)

    far_tiles(attn_tile)
    near_tiles(attn_tile)
    for h in range(n_heads):
        a = acc_sc[h]
        o_ref[:, h * HEAD_DIM:(h + 1) * HEAD_DIM] = (a[:, :HEAD_DIM] / a[:, HEAD_DIM:]).astype(BF16)


def _attention(q, k, v, iq, ik, iw, bias_tiles, bias_stat, B, S, k_top, C):
    n_heads, T, _ = q.shape
    n_pairs = iq.shape[0]
    nq = S // C
    kern = functools.partial(_attn_kernel, k_top=k_top)
    heads = lambda n: pl.BlockSpec((n, C, LANES), lambda b, i: (0, b * nq + i, 0))
    seq = pl.BlockSpec((S, LANES), lambda b, i: (b, 0))
    return pl.pallas_call(
        kern,
        grid=(B, nq),
        in_specs=[pl.BlockSpec(memory_space=pltpu.SMEM),
                  heads(n_heads), seq, seq, heads(n_pairs), seq,
                  pl.BlockSpec((C, LANES), lambda b, i: (b * nq + i, 0)),
                  _resident(bias_tiles.shape)],
        out_specs=pl.BlockSpec((C, n_heads * HEAD_DIM), lambda b, i: (b * nq + i, 0)),
        out_shape=jax.ShapeDtypeStruct((T, n_heads * HEAD_DIM), BF16),
        scratch_shapes=[pltpu.VMEM((nq, C, C), jnp.int32),
                        pltpu.VMEM((nq, C, C), jnp.int32),
                        pltpu.VMEM((2 * n_pairs, C, LANES), BF16),
                        pltpu.VMEM((S, 2 * HEAD_DIM), BF16),
                        pltpu.VMEM((8, LANES), F32),
                        pltpu.VMEM((n_heads, C, LANES), F32),
                        pltpu.VMEM((n_heads, C, 2 * HEAD_DIM), F32)],
        compiler_params=_params("arbitrary", "arbitrary"),
    )(bias_stat, q, k, v, iq, ik, iw, bias_tiles)


CONV_HALO = 32


def _conv_kernel(u_ref, halo_ref, w_ref, cb_ref, g_ref, b_ref, o_ref, buf, y_sc, *, width):
    TS, CH = u_ref.shape
    i = pl.program_id(1)
    buf[0:CONV_HALO] = jnp.where(i > 0, halo_ref[...].astype(F32), 0.0)
    buf[CONV_HALO:] = u_ref[...].astype(F32)
    base = CONV_HALO - (width - 1)
    for c in range(CH // LANES):
        cs = slice(c * LANES, (c + 1) * LANES)
        acc = jnp.zeros((TS, LANES), F32)
        for j in range(width):
            acc = acc + w_ref[j:j + 1, cs] * buf[base + j:base + j + TS, cs]
        y_sc[:, cs] = acc + cb_ref[:, cs]
    y = y_sc[...]
    mu = jnp.mean(y, axis=-1, keepdims=True)
    yc = y - mu
    var = jnp.mean(yc * yc, axis=-1, keepdims=True)
    yn = yc * lax.rsqrt(var + NORM_EPS) * g_ref[...] + b_ref[...]
    o_ref[...] = (yn * jax.nn.sigmoid(yn)).astype(BF16)


def _conv_module(u, conv_w, conv_b, ln_g, ln_b, B, S, TS=256):
    T, CH = u.shape
    width = conv_w.shape[0]
    assert width - 1 <= CONV_HALO
    ns = S // TS
    r = TS // CONV_HALO
    wpad = jnp.zeros((CONV_HALO, CH), F32).at[:width].set(conv_w)
    vec = lambda a: a.reshape(1, CH)
    kern = functools.partial(_conv_kernel, width=width)
    return pl.pallas_call(
        kern,
        grid=(B, ns),
        in_specs=[pl.BlockSpec((TS, CH), lambda b, i: (b * ns + i, 0)),
                  pl.BlockSpec((CONV_HALO, CH), lambda b, i: (jnp.maximum((b * ns + i) * r - 1, 0), 0)),
                  _resident((CONV_HALO, CH)), _resident((1, CH)), _resident((1, CH)), _resident((1, CH))],
        out_specs=pl.BlockSpec((TS, CH), lambda b, i: (b * ns + i, 0)),
        out_shape=jax.ShapeDtypeStruct((T, CH), BF16),
        scratch_shapes=[pltpu.VMEM((TS + CONV_HALO, CH), F32), pltpu.VMEM((TS, CH), F32)],
        compiler_params=_params("arbitrary", "arbitrary"),
    )(u, u, wpad, vec(conv_b), vec(ln_g), vec(ln_b))


def _out_proj_kernel(x_ref, a_ref, c_ref, wa_ref, wc_ref, o_ref):
    o_ref[...] = x_ref[...] + _dot(a_ref[...], wa_ref[...]) + _dot(c_ref[...], wc_ref[...])


def _out_proj(x, attn, conv, w_out, tm=512):
    T, D = x.shape
    aw, cw = attn.shape[1], conv.shape[1]
    wa = w_out[:aw].astype(BF16)
    wc = w_out[aw:].astype(BF16)
    row = lambda w: pl.BlockSpec((tm, w), lambda i: (i, 0))
    return pl.pallas_call(
        _out_proj_kernel,
        grid=(T // tm,),
        in_specs=[row(D), row(aw), row(cw), _resident(wa.shape), _resident(wc.shape)],
        out_specs=row(D),
        out_shape=jax.ShapeDtypeStruct((T, D), F32),
        compiler_params=_params("arbitrary"),
    )(x, attn, conv, wa, wc)


def _first_max(vals):
    m = vals[0]
    for v in vals[1:]:
        m = jnp.maximum(m, v)
    idx = jnp.full(m.shape, len(vals) - 1, jnp.int32)
    for k in range(len(vals) - 2, -1, -1):
        idx = jnp.where(vals[k] == m, k, idx)
    return m, idx


def _softmax_cols(cols):
    m = cols[0]
    for c in cols[1:]:
        m = jnp.maximum(m, c)
    e = [jnp.exp(c - m) for c in cols]
    s = e[0]
    for c in e[1:]:
        s = s + c
    return [c / s for c in e]


GROUP_ROWS = 8
MOE_CHUNK = 128


def _split3(a):
    hi = a.astype(BF16)
    r = a - hi.astype(F32)
    mid = r.astype(BF16)
    lo = (r - mid.astype(F32)).astype(BF16)
    return hi, mid, lo


def _moe_kernel(x_ref, nw_ref, wrt_ref, brt_ref, wg_ref, wu_ref, wd_ref, o_ref,
                h_sc, hs_sc, p_sc, u_sc, cs_sc, seg_sm, *, n_groups):
    TM, D = x_ref.shape
    i = pl.program_id(0)
    e = pl.program_id(1)
    n_exp = pl.num_programs(1)

    @pl.when(jnp.logical_and(i == 0, e == 0))
    def _():
        s_io = lax.broadcasted_iota(jnp.int32, (TM, TM), 0)
        t_io = lax.broadcasted_iota(jnp.int32, (TM, TM), 1)
        u_sc[...] = jnp.where(s_io < t_io, 1.0, 0.0).astype(BF16)

    @pl.when(e == 0)
    def _():
        h = _rms(x_ref[...], nw_ref[...]).astype(BF16)
        h_sc[...] = h
        lg = _dot_nt(wrt_ref[...], h) + brt_ref[...]
        row = lambda k: lg[k:k + 1, :]
        g_prob = _softmax_cols([row(g) for g in range(n_groups)])
        g_p, g_idx = _first_max(g_prob)
        e_logit = []
        for k in range(EXPERTS_PER_GROUP):
            v = row(GROUP_ROWS + k)
            for g in range(1, n_groups):
                v = jnp.where(g_idx == g, row(GROUP_ROWS + g * EXPERTS_PER_GROUP + k), v)
            e_logit.append(v)
        e_prob = _softmax_cols(e_logit)
        p1, i1 = _first_max(e_prob)
        rest = [jnp.where(i1 == k, -1.0, e_prob[k]) for k in range(EXPERTS_PER_GROUP)]
        p2, i2 = _first_max(rest)
        den = p1 + p2
        base = g_idx * EXPERTS_PER_GROUP
        e_io = lax.broadcasted_iota(jnp.int32, (LANES, TM), 0)
        comb_t = (jnp.where(e_io == base + i1, g_p * (p1 / den), 0.0)
                  + jnp.where(e_io == base + i2, g_p * (p2 / den), 0.0))

        g_io = lax.broadcasted_iota(jnp.int32, (GROUP_ROWS, TM), 0)
        onehot_t = jnp.where(g_io == g_idx, 1.0, 0.0)
        rank = _dot(onehot_t.astype(BF16), u_sc[...])
        start = jnp.int32(0)
        pos = jnp.zeros((1, TM), F32)
        for g in range(n_groups):
            seg_sm[g] = start
            pos = jnp.where(g_idx == g, start.astype(F32) + rank[g:g + 1, :], pos)
            start = start + jnp.sum(onehot_t[g:g + 1, :]).astype(jnp.int32)
        seg_sm[n_groups] = start
        pos = pos.astype(jnp.int32)

        rb = 256
        for r0 in range(0, TM, rb):
            r_io = lax.broadcasted_iota(jnp.int32, (rb, TM), 0) + r0
            p_sc[r0:r0 + rb, :] = jnp.where(r_io == pos, 1.0, 0.0).astype(BF16)
        p = p_sc[...]
        for c0 in range(0, D, 512):
            hs_sc[:, c0:c0 + 512] = _dot(p, h_sc[:, c0:c0 + 512]).astype(BF16)
        cs = jnp.zeros((TM, LANES), F32)
        for part in _split3(comb_t):
            cs = cs + _dot_nt(p, part)
        cs_sc[...] = cs
        o_ref[...] = jnp.zeros(o_ref.shape, F32)

    g = e // EXPERTS_PER_GROUP
    start = seg_sm[g]
    end = seg_sm[g + 1]
    c_lo = start // MOE_CHUNK
    c_hi = jnp.where(end > start, (end + MOE_CHUNK - 1) // MOE_CHUNK, c_lo)

    def expert_rows(c, n_chunks):
        m = n_chunks * MOE_CHUNK
        r0 = pl.multiple_of(c * MOE_CHUNK, MOE_CHUNK)
        rows = hs_sc[pl.ds(r0, m), :]
        lane = lax.broadcasted_iota(jnp.int32, (m, LANES), 1)
        w = jnp.sum(jnp.where(lane == e, cs_sc[pl.ds(r0, m), :], 0.0), axis=-1, keepdims=True)
        a = jax.nn.silu(_dot(rows, wg_ref[0])) * _dot(rows, wu_ref[0]) * w
        o_ref[pl.ds(r0, m), :] += _dot(a.astype(BF16), wd_ref[0])

    def pair(k, carry):
        expert_rows(c_lo + 2 * k, 2)
        return carry

    n_chunks = c_hi - c_lo
    lax.fori_loop(0, n_chunks // 2, pair, 0)

    @pl.when(n_chunks % 2 == 1)
    def _():
        expert_rows(c_hi - 1, 1)

    @pl.when(e == n_exp - 1)
    def _():
        p = p_sc[...]
        for c0 in range(0, D, 512):
            ys = o_ref[:, c0:c0 + 512].astype(BF16)
            y = lax.dot_general(p, ys, (((0,), (0,)), ((), ())), preferred_element_type=F32)
            o_ref[:, c0:c0 + 512] = x_ref[:, c0:c0 + 512] + y


def _moe(x, norm_w, wg, bg, we, be, w_gate, w_up, w_down, tm=1024):
    T, D = x.shape
    n_groups, n_exp = wg.shape[1], we.shape[1]
    _, _, ff = w_gate.shape
    assert n_groups <= GROUP_ROWS and n_exp == n_groups * EXPERTS_PER_GROUP and T % tm == 0
    rows = GROUP_ROWS + n_exp
    wrt = jnp.zeros((rows, D), F32).at[:n_groups].set(wg.T).at[GROUP_ROWS:].set(we.T).astype(BF16)
    brt = jnp.zeros((rows, 1), F32).at[:n_groups, 0].set(bg).at[GROUP_ROWS:, 0].set(be)
    tile = pl.BlockSpec((tm, D), lambda i, e: (i, 0))
    return pl.pallas_call(
        functools.partial(_moe_kernel, n_groups=n_groups),
        grid=(T // tm, n_exp),
        in_specs=[pl.BlockSpec((tm, D), lambda i, e: (i, 0), pipeline_mode=pl.Buffered(1)),
                  _resident((1, D)), _resident((rows, D)), _resident((rows, 1)),
                  pl.BlockSpec((1, D, ff), lambda i, e: (e, 0, 0)),
                  pl.BlockSpec((1, D, ff), lambda i, e: (e, 0, 0)),
                  pl.BlockSpec((1, ff, D), lambda i, e: (e, 0, 0))],
        out_specs=tile,
        out_shape=jax.ShapeDtypeStruct((T, D), F32),
        scratch_shapes=[pltpu.VMEM((tm, D), BF16),
                        pltpu.VMEM((tm, D), BF16),
                        pltpu.VMEM((tm, tm), BF16),
                        pltpu.VMEM((tm, tm), BF16),
                        pltpu.VMEM((tm, LANES), F32),
                        pltpu.SMEM((GROUP_ROWS,), jnp.int32)],
        compiler_params=_params("arbitrary", "arbitrary"),
    )(x, norm_w.reshape(1, D), wrt, brt, w_gate, w_up, w_down)


POOL_HALO = 16


def _pool_kernel(x_ref, halo_ref, nw_ref, pw_ref, ps_ref, o_ref, hb):
    TS, D = x_ref.shape
    n_groups, pc, _ = pw_ref.shape
    i = pl.program_id(1)
    nw = nw_ref[...]
    hb[0:POOL_HALO] = jnp.where(i > 0, _rms(halo_ref[...], nw), 0.0)
    hb[POOL_HALO:] = _rms(x_ref[...], nw)
    t = i * TS + lax.broadcasted_iota(jnp.int32, (TS, 1), 0)
    for g, w in enumerate(POOL_WINDOWS):
        cs = slice(g * pc, (g + 1) * pc)
        cur = hb[POOL_HALO:, cs]
        s = cur
        for r in range(1, w):
            s = s + hb[POOL_HALO - r:POOL_HALO - r + TS, cs]
        count = jnp.minimum(t + 1, w).astype(F32)
        d = s / count - cur
        mixed = _dot(d.astype(BF16), pw_ref[g])
        o_ref[:, cs] = x_ref[:, cs] + ps_ref[:, cs] * mixed


def _pool_layer(x, norm_w, pool_w, pool_scale, B, S, TS=256):
    T, D = x.shape
    assert len(POOL_WINDOWS) == pool_w.shape[0] and max(POOL_WINDOWS) - 1 <= POOL_HALO
    ns = S // TS
    r = TS // POOL_HALO
    pw = pool_w.astype(BF16)
    return pl.pallas_call(
        _pool_kernel,
        grid=(B, ns),
        in_specs=[pl.BlockSpec((TS, D), lambda b, i: (b * ns + i, 0)),
                  pl.BlockSpec((POOL_HALO, D), lambda b, i: (jnp.maximum((b * ns + i) * r - 1, 0), 0)),
                  _resident((1, D)), _resident(pw.shape), _resident((1, D))],
        out_specs=pl.BlockSpec((TS, D), lambda b, i: (b * ns + i, 0)),
        out_shape=jax.ShapeDtypeStruct((T, D), F32),
        scratch_shapes=[pltpu.VMEM((TS + POOL_HALO, D), F32)],
        compiler_params=_params("arbitrary", "arbitrary"),
    )(x, x, norm_w.reshape(1, D), pw, pool_scale.reshape(1, D))


def _chunk(S):
    return 256 if S % 256 == 0 else 128


def kernel(x, rel_bias, mix_norm_e, w_in_e, q_norm_e, k_norm_e, conv_w_e, conv_b_e, conv_ln_g_e, conv_ln_b_e,
           w_out_e, mix_norm_o, pool_w_o, pool_scale_o, ffn_norm, router_group_w, router_group_b,
           router_expert_w, router_expert_b, w_gate, w_up, w_down):
    B, S, D = x.shape
    T = B * S
    depth = ffn_norm.shape[0]
    n_heads = rel_bias.shape[1]
    idx_heads = (w_in_e.shape[2] - n_heads * HEAD_DIM - 2 * HEAD_DIM - IDX_DIM - 2 * conv_w_e.shape[2]) \
        // (IDX_DIM + 1)
    k_top = min(INDEX_TOPK, S // 4)
    C = _chunk(S)
    xf = x.reshape(T, D)
    bias_tiles, bias_stat = _rel_bias_tiles(rel_bias, C)
    for l in range(depth):
        i = l // 2
        if l % 2 == 0:
            q, k, v, iq, ik, iw, u = _in_proj(xf, mix_norm_e[i], w_in_e[i], q_norm_e[i], k_norm_e[i],
                                              n_heads, idx_heads, conv_w_e.shape[2])
            attn = _attention(q, k, v, iq, ik, iw, bias_tiles, bias_stat, B, S, k_top, C)
            conv = _conv_module(u, conv_w_e[i], conv_b_e[i], conv_ln_g_e[i], conv_ln_b_e[i], B, S)
            xf = _out_proj(xf, attn, conv, w_out_e[i])
        else:
            xf = _pool_layer(xf, mix_norm_o[i], pool_w_o[i], pool_scale_o[i], B, S)
        xf = _moe(xf, ffn_norm[l], router_group_w[l], router_group_b[l], router_expert_w[l], router_expert_b[l],
                  w_gate[l].astype(BF16), w_up[l].astype(BF16), w_down[l].astype(BF16))
    return xf.reshape(B, S, D)
```

```python
import functools
import math

import jax
import jax.numpy as jnp
from jax import lax
from jax.experimental import pallas as pl
from jax.experimental.pallas import tpu as pltpu

F32 = jnp.float32
BF16 = jnp.bfloat16

NORM_EPS = 1e-6
HEAD_DIM = 128
IDX_DIM = 64
INDEX_TOPK = 256
REL_BUCKETS = 32
REL_MAX_DIST = 128
POOL_WINDOWS = (2, 4, 8, 16)
EXPERTS_PER_GROUP = 4
LANES = 128
VMEM_LIMIT = 56 * 1024 * 1024
NEG = -1e30
INT_MIN = -(2 ** 31)


def _dot(a, b):
    return jnp.dot(a, b, preferred_element_type=F32)


def _dot_nt(a, b):
    return lax.dot_general(a, b, (((1,), (1,)), ((), ())), preferred_element_type=F32)


def _rms(x, w):
    return x * lax.rsqrt(jnp.mean(x * x, axis=-1, keepdims=True) + NORM_EPS) * w


def _params(*sem):
    return pltpu.CompilerParams(dimension_semantics=sem, vmem_limit_bytes=VMEM_LIMIT)


def _resident(shape):
    nd = len(shape)
    return pl.BlockSpec(shape, lambda *_: (0,) * nd, pipeline_mode=pl.Buffered(1))


def _in_proj_kernel(x_ref, nw_ref, qn_ref, kn_ref, wq_ref, wkv_ref, wiq_ref, wikw_ref, wa_ref, wg_ref,
                    q_ref, k_ref, v_ref, iq_ref, ik_ref, iw_ref, u_ref, *, q_scale, iw_scale):
    h = _rms(x_ref[...], nw_ref[...]).astype(BF16)
    n_pairs = q_ref.shape[0] // 2
    qn = qn_ref[...] * q_scale
    for c in range(n_pairs):
        qq = _dot(h, wq_ref[:, c * 256:(c + 1) * 256])
        for s in range(2):
            qh = qq[:, s * HEAD_DIM:(s + 1) * HEAD_DIM]
            q_ref[2 * c + s] = _rms(qh, qn).astype(BF16)
    kv = _dot(h, wkv_ref[...])
    k_ref[...] = _rms(kv[:, :HEAD_DIM], kn_ref[...]).astype(BF16)
    v_ref[...] = kv[:, HEAD_DIM:].astype(BF16)
    for c in range(iq_ref.shape[0] // 2):
        r = _dot(h, wiq_ref[:, c * 256:(c + 1) * 256])
        iq_ref[2 * c] = r[:, :LANES].astype(BF16)
        iq_ref[2 * c + 1] = r[:, LANES:].astype(BF16)
    r = _dot(h, wikw_ref[...])
    ik_ref[...] = r[:, :LANES].astype(BF16)
    iw_ref[...] = r[:, LANES:] * iw_scale
    for c in range(u_ref.shape[1] // 256):
        cs = slice(c * 256, (c + 1) * 256)
        a = _dot(h, wa_ref[:, cs])
        g = _dot(h, wg_ref[:, cs])
        u_ref[:, cs] = (a * jax.nn.sigmoid(g)).astype(BF16)


def _in_proj(x, norm_w, w_in, q_norm, k_norm, n_heads, idx_heads, conv_ch, tm=256):
    T, D = x.shape
    attn_w = n_heads * HEAD_DIM
    iq_w = idx_heads * IDX_DIM
    o = 0
    wq = w_in[:, o:o + attn_w]; o += attn_w
    wkv = w_in[:, o:o + 2 * HEAD_DIM]; o += 2 * HEAD_DIM
    wiq = w_in[:, o:o + iq_w]; o += iq_w
    wik = w_in[:, o:o + IDX_DIM]; o += IDX_DIM
    wiw = w_in[:, o:o + idx_heads]; o += idx_heads
    wa = w_in[:, o:o + conv_ch]; o += conv_ch
    wg = w_in[:, o:o + conv_ch]; o += conv_ch
    assert o == w_in.shape[1] and 2 * IDX_DIM == LANES
    wikw = jnp.concatenate([wik, wik, wiw, jnp.zeros((D, LANES - idx_heads), w_in.dtype)], axis=1)
    ws = [w.astype(BF16) for w in (wq, wkv, wiq, wikw, wa, wg)]
    row = lambda w: pl.BlockSpec((tm, w), lambda i: (i, 0))
    heads = lambda n: pl.BlockSpec((n, tm, LANES), lambda i: (0, i, 0))
    kern = functools.partial(_in_proj_kernel, q_scale=HEAD_DIM ** -0.5,
                             iw_scale=(idx_heads ** -0.5) * (IDX_DIM ** -0.5))
    return pl.pallas_call(
        kern,
        grid=(T // tm,),
        in_specs=[row(D), _resident((1, D)), _resident((1, HEAD_DIM)), _resident((1, HEAD_DIM))]
                 + [_resident(w.shape) for w in ws],
        out_specs=[heads(n_heads), row(HEAD_DIM), row(HEAD_DIM), heads(iq_w // LANES), row(LANES), row(LANES),
                   row(conv_ch)],
        out_shape=[jax.ShapeDtypeStruct((n_heads, T, HEAD_DIM), BF16),
                   jax.ShapeDtypeStruct((T, HEAD_DIM), BF16),
                   jax.ShapeDtypeStruct((T, HEAD_DIM), BF16),
                   jax.ShapeDtypeStruct((iq_w // LANES, T, LANES), BF16),
                   jax.ShapeDtypeStruct((T, LANES), BF16),
                   jax.ShapeDtypeStruct((T, LANES), F32),
                   jax.ShapeDtypeStruct((T, conv_ch), BF16)],
        compiler_params=_params("arbitrary"),
    )(x, norm_w.reshape(1, D), q_norm.reshape(1, HEAD_DIM), k_norm.reshape(1, HEAD_DIM), *ws)


def _rel_bias_kernel(rb_ref, o_ref, stat_ref):
    _, n_heads, C, _ = o_ref.shape
    tau = lax.broadcasted_iota(jnp.int32, (C, C), 0)
    sig = lax.broadcasted_iota(jnp.int32, (C, C), 1)
    max_exact = REL_BUCKETS // 2
    for kind in range(2):
        d = tau - sig + kind * C
        n = jnp.maximum(d, 0)
        nf = jnp.maximum(n, 1).astype(F32)
        large = max_exact + (jnp.log(nf / max_exact) / math.log(REL_MAX_DIST / max_exact)
                             * (REL_BUCKETS - max_exact)).astype(jnp.int32)
        large = jnp.minimum(large, REL_BUCKETS - 1)
        bucket = jnp.where(n < max_exact, n, large)
        for h in range(n_heads):
            b = jnp.zeros((C, C), F32)
            for bk in range(REL_BUCKETS):
                b = jnp.where(bucket == bk, rb_ref[bk, h], b)
            b = b - rb_ref[REL_BUCKETS - 1, h]
            if kind == 0:
                b = jnp.where(d < 0, NEG, b)
            o_ref[kind, h] = b
    for h in range(n_heads):
        hi = rb_ref[0, h]
        lo = rb_ref[0, h]
        for bk in range(1, REL_BUCKETS):
            hi = jnp.maximum(hi, rb_ref[bk, h])
            lo = jnp.minimum(lo, rb_ref[bk, h])
        stat_ref[0, h] = hi - rb_ref[REL_BUCKETS - 1, h]
        stat_ref[1, h] = lo - rb_ref[REL_BUCKETS - 1, h]


def _rel_bias_tiles(rel_bias, C):
    n_heads = rel_bias.shape[1]
    assert C >= REL_MAX_DIST
    return pl.pallas_call(
        _rel_bias_kernel,
        in_specs=[pl.BlockSpec(memory_space=pltpu.SMEM)],
        out_specs=[pl.BlockSpec(memory_space=pltpu.VMEM), pl.BlockSpec(memory_space=pltpu.SMEM)],
        out_shape=[jax.ShapeDtypeStruct((2, n_heads, C, C), F32), jax.ShapeDtypeStruct((2, n_heads), F32)],
        compiler_params=pltpu.CompilerParams(vmem_limit_bytes=VMEM_LIMIT),
    )(rel_bias)


SHIFT_SPAN_LIMIT = 60.0


def _order_key(x):
    bits = pltpu.bitcast(x, jnp.int32)
    return bits ^ ((bits >> 31) & 0x7FFFFFFF)


def _row_to_col(row):
    C = row.shape[1]
    halves = []
    for part in (row >> 16, row & 0xFFFF):
        halves.append(jnp.broadcast_to(part.astype(F32), (LANES, C)).T[:, 0:1].astype(jnp.int32))
    return (halves[0] << 16) | halves[1]


def _attn_kernel(bstat_ref, q_ref, k_ref, v_ref, iq_ref, ik_ref, iw_ref, bias_ref, o_ref,
                 key_sc, keyt_sc, keyt16_sc, iqm_sc, vx_sc, kmax_sc, shift_sc, acc_sc, *, k_top):
    n_heads, C, _ = q_ref.shape
    idx_heads = iqm_sc.shape[0]
    idx_bits = max(1, (key_sc.shape[0] * C - 1).bit_length())
    qi = pl.program_id(1)
    nkv = qi + 1

    @pl.when(qi == 0)
    def _():
        vx_sc[:, :HEAD_DIM] = v_ref[...]
        vx_sc[:, HEAD_DIM:] = jnp.ones((vx_sc.shape[0], HEAD_DIM), BF16)
        kf = k_ref[...].astype(F32)
        k2 = jnp.sum(kf * kf, axis=-1, keepdims=True)
        kmax_sc[...] = jnp.broadcast_to(jnp.sqrt(jnp.max(k2, axis=0, keepdims=True)), kmax_sc.shape)

    lane = lax.broadcasted_iota(jnp.int32, (C, LANES), 1)
    for p in range(idx_heads // 2):
        qp = iq_ref[p].astype(F32)
        iqm_sc[2 * p] = jnp.where(lane < IDX_DIM, qp, 0.0).astype(BF16)
        iqm_sc[2 * p + 1] = jnp.where(lane >= IDX_DIM, qp, 0.0).astype(BF16)
    iw = iw_ref[...]
    tau = lax.broadcasted_iota(jnp.int32, (C, C), 0)
    sig = lax.broadcasted_iota(jnp.int32, (C, C), 1)

    def score_body(j, carry):
        off = pl.multiple_of(j * C, C)
        ikc = ik_ref[pl.ds(off, C), :]
        acc = jnp.zeros((C, C), F32)
        for hh in range(idx_heads):
            s = _dot_nt(iqm_sc[hh], ikc)
            acc = acc + jnp.maximum(s, 0.0) * iw[:, hh:hh + 1]
        acc = jnp.where(jnp.logical_and(j == qi, sig > tau), -jnp.inf, acc)
        key_sc[j] = _order_key(acc)
        kt = _order_key(acc.T)
        keyt_sc[j] = kt
        keyt16_sc[j] = (kt >> 16).astype(jnp.int16)
        return carry

    lax.fori_loop(0, nkv, score_body, 0)

    SUB = 32

    def count(hit):
        def body(j, cnt):
            for r in range(C // SUB):
                cnt = cnt + hit(j, r)
            return cnt
        cnt = lax.fori_loop(0, nkv, body, jnp.zeros((SUB, C), F32))
        return jnp.sum(cnt, axis=0, keepdims=True)

    def keyt(j, r):
        return keyt_sc[j, pl.ds(r * SUB, SUB), :]

    def bit_body(b, carry):
        res, n_res = carry
        cand = res ^ lax.shift_left(jnp.int32(1), 31 - b)
        tot = count(lambda j, r: jnp.where(keyt(j, r) >= cand, 1.0, 0.0))
        ok = tot >= k_top
        return jnp.where(ok, cand, res), jnp.where(ok, tot, n_res)

    def count16(cand16):
        one = jnp.ones((SUB, C), jnp.int16)
        zero = jnp.zeros((SUB, C), jnp.int16)

        def body(j, cnt):
            for r in range(C // SUB):
                cnt = cnt + jnp.where(keyt16_sc[j, pl.ds(r * SUB, SUB), :] >= cand16, one, zero)
            return cnt
        cnt = lax.fori_loop(0, nkv, body, zero)
        return jnp.sum(cnt.astype(jnp.int32).astype(F32), axis=0, keepdims=True)

    def bit16_body(b, carry):
        res, n_res = carry
        cand = res ^ lax.shift_left(jnp.int32(1), 31 - b)
        tot = count16((cand >> 16).astype(jnp.int16))
        ok = tot >= k_top
        return jnp.where(ok, cand, res), jnp.where(ok, tot, n_res)

    searched = nkv * C > k_top
    carry = (jnp.full((1, C), INT_MIN, jnp.int32), jnp.zeros((1, C), F32))
    carry = lax.fori_loop(0, jnp.where(searched, 16, 0), bit16_body, carry)
    thr_row, n_thr = lax.fori_loop(16, jnp.where(searched, 32, 16), bit_body, carry)
    thr = _row_to_col(thr_row)

    @pl.when(jnp.max(n_thr) > k_top)
    def _():
        need = k_top - count(lambda j, r: jnp.where(keyt(j, r) > thr_row, 1.0, 0.0))
        s_sub = lax.broadcasted_iota(jnp.int32, (SUB, C), 0)

        def idx_body(b, last):
            cand = last | lax.shift_left(jnp.int32(1), idx_bits - 1 - b)
            below = count(lambda j, r: jnp.where(
                keyt(j, r) == thr_row, jnp.where(j * C + r * SUB + s_sub < cand, 1.0, 0.0), 0.0))
            return jnp.where(below < need, cand, last)

        last = _row_to_col(lax.fori_loop(0, idx_bits, idx_body, jnp.zeros((1, C), jnp.int32)))

        def drop_body(j, carry):
            kk = key_sc[j]
            key_sc[j] = jnp.where(kk == thr, jnp.where(j * C + sig > last, kk - 1, kk), kk)
            return carry

        lax.fori_loop(0, nkv, drop_body, 0)

    kmax = kmax_sc[0:1, 0:1] * 1.001
    worst = jnp.zeros((C, 1), F32)
    for h in range(n_heads):
        qf = q_ref[h].astype(F32)
        bound = jnp.sqrt(jnp.sum(qf * qf, axis=-1, keepdims=True)) * kmax
        shift_sc[h] = jnp.broadcast_to(bound + bstat_ref[0, h], (C, LANES))
        worst = jnp.maximum(worst, 2.0 * bound + (bstat_ref[0, h] - bstat_ref[1, h]))
    loose = jnp.max(worst) > SHIFT_SPAN_LIMIT

    def logits(j, h, kind):
        off = pl.multiple_of(j * C, C)
        lg = _dot_nt(q_ref[h], k_ref[pl.ds(off, C), :])
        return lg if kind is None else lg + bias_ref[kind, h]

    def near_tiles(fn):
        @pl.when(qi > 0)
        def _():
            fn(qi - 1, 1)
        fn(qi, 0)

    def far_tiles(fn):
        def body(j, carry):
            fn(j, None)
            return carry
        lax.fori_loop(0, qi - 1, body, 0)

    @pl.when(loose)
    def _():
        for h in range(n_heads):
            shift_sc[h] = jnp.full((C, LANES), NEG, F32)

        def max_tile(j, kind):
            sel = key_sc[j] >= thr
            for h in range(n_heads):
                lg = jnp.where(sel, logits(j, h, kind), NEG)
                m = jnp.max(lg, axis=-1, keepdims=True)
                shift_sc[h] = jnp.maximum(shift_sc[h], jnp.broadcast_to(m, (C, LANES)))

        far_tiles(max_tile)
        near_tiles(max_tile)

    acc_sc[...] = jnp.zeros(acc_sc.shape, F32)

    def attn_tile(j, kind):
        off = pl.multiple_of(j * C, C)
        vx = vx_sc[pl.ds(off, C), :]
        sel = key_sc[j] >= thr
        for h in range(n_heads):
            sh = jnp.concatenate([shift_sc[h]] * (C // LANES), axis=1)
            p = jnp.where(sel, jnp.exp(logits(j, h, kind) - sh), 0.0)
            acc_sc[h] += _dot(p.astype(BF16), vx)

    far_tiles(attn_tile)
    near_tiles(attn_tile)
    for h in range(n_heads):
        a = acc_sc[h]
        o_ref[:, h * HEAD_DIM:(h + 1) * HEAD_DIM] = (a[:, :HEAD_DIM] / a[:, HEAD_DIM:]).astype(BF16)


def _attention(q, k, v, iq, ik, iw, bias_tiles, bias_stat, B, S, k_top, C):
    n_heads, T, _ = q.shape
    n_pairs = iq.shape[0]
    nq = S // C
    kern = functools.partial(_attn_kernel, k_top=k_top)
    heads = lambda n: pl.BlockSpec((n, C, LANES), lambda b, i: (0, b * nq + i, 0))
    seq = pl.BlockSpec((S, LANES), lambda b, i: (b, 0))
    return pl.pallas_call(
        kern,
        grid=(B, nq),
        in_specs=[pl.BlockSpec(memory_space=pltpu.SMEM),
                  heads(n_heads), seq, seq, heads(n_pairs), seq,
                  pl.BlockSpec((C, LANES), lambda b, i: (b * nq + i, 0)),
                  _resident(bias_tiles.shape)],
        out_specs=pl.BlockSpec((C, n_heads * HEAD_DIM), lambda b, i: (b * nq + i, 0)),
        out_shape=jax.ShapeDtypeStruct((T, n_heads * HEAD_DIM), BF16),
        scratch_shapes=[pltpu.VMEM((nq, C, C), jnp.int32),
                        pltpu.VMEM((nq, C, C), jnp.int32),
                        pltpu.VMEM((nq, C, C), jnp.int16),
                        pltpu.VMEM((2 * n_pairs, C, LANES), BF16),
                        pltpu.VMEM((S, 2 * HEAD_DIM), BF16),
                        pltpu.VMEM((8, LANES), F32),
                        pltpu.VMEM((n_heads, C, LANES), F32),
                        pltpu.VMEM((n_heads, C, 2 * HEAD_DIM), F32)],
        compiler_params=_params("arbitrary", "arbitrary"),
    )(bias_stat, q, k, v, iq, ik, iw, bias_tiles)


CONV_HALO = 32


def _conv_kernel(u_ref, halo_ref, w_ref, cb_ref, g_ref, b_ref, o_ref, buf, y_sc, *, width):
    TS, CH = u_ref.shape
    i = pl.program_id(1)
    buf[0:CONV_HALO] = jnp.where(i > 0, halo_ref[...].astype(F32), 0.0)
    buf[CONV_HALO:] = u_ref[...].astype(F32)
    base = CONV_HALO - (width - 1)
    for c in range(CH // LANES):
        cs = slice(c * LANES, (c + 1) * LANES)
        acc = jnp.zeros((TS, LANES), F32)
        for j in range(width):
            acc = acc + w_ref[j:j + 1, cs] * buf[base + j:base + j + TS, cs]
        y_sc[:, cs] = acc + cb_ref[:, cs]
    y = y_sc[...]
    mu = jnp.mean(y, axis=-1, keepdims=True)
    yc = y - mu
    var = jnp.mean(yc * yc, axis=-1, keepdims=True)
    yn = yc * lax.rsqrt(var + NORM_EPS) * g_ref[...] + b_ref[...]
    o_ref[...] = (yn * jax.nn.sigmoid(yn)).astype(BF16)


def _conv_module(u, conv_w, conv_b, ln_g, ln_b, B, S, TS=256):
    T, CH = u.shape
    width = conv_w.shape[0]
    assert width - 1 <= CONV_HALO
    ns = S // TS
    r = TS // CONV_HALO
    wpad = jnp.zeros((CONV_HALO, CH), F32).at[:width].set(conv_w)
    vec = lambda a: a.reshape(1, CH)
    kern = functools.partial(_conv_kernel, width=width)
    return pl.pallas_call(
        kern,
        grid=(B, ns),
        in_specs=[pl.BlockSpec((TS, CH), lambda b, i: (b * ns + i, 0)),
                  pl.BlockSpec((CONV_HALO, CH), lambda b, i: (jnp.maximum((b * ns + i) * r - 1, 0), 0)),
                  _resident((CONV_HALO, CH)), _resident((1, CH)), _resident((1, CH)), _resident((1, CH))],
        out_specs=pl.BlockSpec((TS, CH), lambda b, i: (b * ns + i, 0)),
        out_shape=jax.ShapeDtypeStruct((T, CH), BF16),
        scratch_shapes=[pltpu.VMEM((TS + CONV_HALO, CH), F32), pltpu.VMEM((TS, CH), F32)],
        compiler_params=_params("arbitrary", "arbitrary"),
    )(u, u, wpad, vec(conv_b), vec(ln_g), vec(ln_b))


def _out_proj_kernel(x_ref, a_ref, c_ref, wa_ref, wc_ref, o_ref):
    o_ref[...] = x_ref[...] + _dot(a_ref[...], wa_ref[...]) + _dot(c_ref[...], wc_ref[...])


def _out_proj(x, attn, conv, w_out, tm=512):
    T, D = x.shape
    aw, cw = attn.shape[1], conv.shape[1]
    wa = w_out[:aw].astype(BF16)
    wc = w_out[aw:].astype(BF16)
    row = lambda w: pl.BlockSpec((tm, w), lambda i: (i, 0))
    return pl.pallas_call(
        _out_proj_kernel,
        grid=(T // tm,),
        in_specs=[row(D), row(aw), row(cw), _resident(wa.shape), _resident(wc.shape)],
        out_specs=row(D),
        out_shape=jax.ShapeDtypeStruct((T, D), F32),
        compiler_params=_params("arbitrary"),
    )(x, attn, conv, wa, wc)


def _first_max(vals):
    m = vals[0]
    for v in vals[1:]:
        m = jnp.maximum(m, v)
    idx = jnp.full(m.shape, len(vals) - 1, jnp.int32)
    for k in range(len(vals) - 2, -1, -1):
        idx = jnp.where(vals[k] == m, k, idx)
    return m, idx


def _softmax_cols(cols):
    m = cols[0]
    for c in cols[1:]:
        m = jnp.maximum(m, c)
    e = [jnp.exp(c - m) for c in cols]
    s = e[0]
    for c in e[1:]:
        s = s + c
    return [c / s for c in e]


GROUP_ROWS = 8
MOE_CHUNK = 128


def _split3(a):
    hi = a.astype(BF16)
    r = a - hi.astype(F32)
    mid = r.astype(BF16)
    lo = (r - mid.astype(F32)).astype(BF16)
    return hi, mid, lo


def _moe_kernel(x_ref, nw_ref, wrt_ref, brt_ref, wg_ref, wu_ref, wd_ref, o_ref,
                h_sc, hs_sc, p_sc, u_sc, cs_sc, seg_sm, *, n_groups):
    TM, D = x_ref.shape
    i = pl.program_id(0)
    e = pl.program_id(1)
    n_exp = pl.num_programs(1)

    @pl.when(jnp.logical_and(i == 0, e == 0))
    def _():
        s_io = lax.broadcasted_iota(jnp.int32, (TM, TM), 0)
        t_io = lax.broadcasted_iota(jnp.int32, (TM, TM), 1)
        u_sc[...] = jnp.where(s_io < t_io, 1.0, 0.0).astype(BF16)

    @pl.when(e == 0)
    def _():
        h = _rms(x_ref[...], nw_ref[...]).astype(BF16)
        h_sc[...] = h
        lg = _dot_nt(wrt_ref[...], h) + brt_ref[...]
        row = lambda k: lg[k:k + 1, :]
        g_prob = _softmax_cols([row(g) for g in range(n_groups)])
        g_p, g_idx = _first_max(g_prob)
        e_logit = []
        for k in range(EXPERTS_PER_GROUP):
            v = row(GROUP_ROWS + k)
            for g in range(1, n_groups):
                v = jnp.where(g_idx == g, row(GROUP_ROWS + g * EXPERTS_PER_GROUP + k), v)
            e_logit.append(v)
        e_prob = _softmax_cols(e_logit)
        p1, i1 = _first_max(e_prob)
        rest = [jnp.where(i1 == k, -1.0, e_prob[k]) for k in range(EXPERTS_PER_GROUP)]
        p2, i2 = _first_max(rest)
        den = p1 + p2
        base = g_idx * EXPERTS_PER_GROUP
        e_io = lax.broadcasted_iota(jnp.int32, (LANES, TM), 0)
        comb_t = (jnp.where(e_io == base + i1, g_p * (p1 / den), 0.0)
                  + jnp.where(e_io == base + i2, g_p * (p2 / den), 0.0))

        g_io = lax.broadcasted_iota(jnp.int32, (GROUP_ROWS, TM), 0)
        onehot_t = jnp.where(g_io == g_idx, 1.0, 0.0)
        rank = _dot(onehot_t.astype(BF16), u_sc[...])
        start = jnp.int32(0)
        pos = jnp.zeros((1, TM), F32)
        for g in range(n_groups):
            seg_sm[g] = start
            pos = jnp.where(g_idx == g, start.astype(F32) + rank[g:g + 1, :], pos)
            start = start + jnp.sum(onehot_t[g:g + 1, :]).astype(jnp.int32)
        seg_sm[n_groups] = start
        pos = pos.astype(jnp.int32)

        rb = 256
        for r0 in range(0, TM, rb):
            r_io = lax.broadcasted_iota(jnp.int32, (rb, TM), 0) + r0
            p_sc[r0:r0 + rb, :] = jnp.where(r_io == pos, 1.0, 0.0).astype(BF16)
        p = p_sc[...]
        for c0 in range(0, D, 512):
            hs_sc[:, c0:c0 + 512] = _dot(p, h_sc[:, c0:c0 + 512]).astype(BF16)
        cs = jnp.zeros((TM, LANES), F32)
        for part in _split3(comb_t):
            cs = cs + _dot_nt(p, part)
        cs_sc[...] = cs
        o_ref[...] = jnp.zeros(o_ref.shape, F32)

    g = e // EXPERTS_PER_GROUP
    start = seg_sm[g]
    end = seg_sm[g + 1]
    c_lo = start // MOE_CHUNK
    c_hi = jnp.where(end > start, (end + MOE_CHUNK - 1) // MOE_CHUNK, c_lo)

    def expert_rows(c, n_chunks):
        m = n_chunks * MOE_CHUNK
        r0 = pl.multiple_of(c * MOE_CHUNK, MOE_CHUNK)
        rows = hs_sc[pl.ds(r0, m), :]
        lane = lax.broadcasted_iota(jnp.int32, (m, LANES), 1)
        w = jnp.sum(jnp.where(lane == e, cs_sc[pl.ds(r0, m), :], 0.0), axis=-1, keepdims=True)
        a = jax.nn.silu(_dot(rows, wg_ref[0])) * _dot(rows, wu_ref[0]) * w
        o_ref[pl.ds(r0, m), :] += _dot(a.astype(BF16), wd_ref[0])

    def triple(k, carry):
        expert_rows(c_lo + 3 * k, 3)
        return carry

    n_chunks = c_hi - c_lo
    lax.fori_loop(0, n_chunks // 3, triple, 0)
    for rest in (1, 2):
        @pl.when(n_chunks % 3 == rest)
        def _():
            expert_rows(c_hi - rest, rest)

    @pl.when(e == n_exp - 1)
    def _():
        p = p_sc[...]
        for c0 in range(0, D, 512):
            ys = o_ref[:, c0:c0 + 512].astype(BF16)
            y = lax.dot_general(p, ys, (((0,), (0,)), ((), ())), preferred_element_type=F32)
            o_ref[:, c0:c0 + 512] = x_ref[:, c0:c0 + 512] + y


def _moe(x, norm_w, wg, bg, we, be, w_gate, w_up, w_down, tm=1024):
    T, D = x.shape
    n_groups, n_exp = wg.shape[1], we.shape[1]
    _, _, ff = w_gate.shape
    assert n_groups <= GROUP_ROWS and n_exp == n_groups * EXPERTS_PER_GROUP and T % tm == 0
    rows = GROUP_ROWS + n_exp
    wrt = jnp.zeros((rows, D), F32).at[:n_groups].set(wg.T).at[GROUP_ROWS:].set(we.T).astype(BF16)
    brt = jnp.zeros((rows, 1), F32).at[:n_groups, 0].set(bg).at[GROUP_ROWS:, 0].set(be)
    tile = pl.BlockSpec((tm, D), lambda i, e: (i, 0))
    return pl.pallas_call(
        functools.partial(_moe_kernel, n_groups=n_groups),
        grid=(T // tm, n_exp),
        in_specs=[pl.BlockSpec((tm, D), lambda i, e: (i, 0), pipeline_mode=pl.Buffered(1)),
                  _resident((1, D)), _resident((rows, D)), _resident((rows, 1)),
                  pl.BlockSpec((1, D, ff), lambda i, e: (e, 0, 0)),
                  pl.BlockSpec((1, D, ff), lambda i, e: (e, 0, 0)),
                  pl.BlockSpec((1, ff, D), lambda i, e: (e, 0, 0))],
        out_specs=tile,
        out_shape=jax.ShapeDtypeStruct((T, D), F32),
        scratch_shapes=[pltpu.VMEM((tm, D), BF16),
                        pltpu.VMEM((tm, D), BF16),
                        pltpu.VMEM((tm, tm), BF16),
                        pltpu.VMEM((tm, tm), BF16),
                        pltpu.VMEM((tm, LANES), F32),
                        pltpu.SMEM((GROUP_ROWS,), jnp.int32)],
        compiler_params=_params("arbitrary", "arbitrary"),
    )(x, norm_w.reshape(1, D), wrt, brt, w_gate, w_up, w_down)


POOL_HALO = 16


def _pool_kernel(x_ref, halo_ref, nw_ref, pw_ref, ps_ref, o_ref, hb):
    TS, D = x_ref.shape
    n_groups, pc, _ = pw_ref.shape
    i = pl.program_id(1)
    nw = nw_ref[...]
    hb[0:POOL_HALO] = jnp.where(i > 0, _rms(halo_ref[...], nw), 0.0)
    hb[POOL_HALO:] = _rms(x_ref[...], nw)
    t = i * TS + lax.broadcasted_iota(jnp.int32, (TS, 1), 0)
    for g, w in enumerate(POOL_WINDOWS):
        cs = slice(g * pc, (g + 1) * pc)
        cur = hb[POOL_HALO:, cs]
        s = cur
        for r in range(1, w):
            s = s + hb[POOL_HALO - r:POOL_HALO - r + TS, cs]
        count = jnp.minimum(t + 1, w).astype(F32)
        d = s / count - cur
        mixed = _dot(d.astype(BF16), pw_ref[g])
        o_ref[:, cs] = x_ref[:, cs] + ps_ref[:, cs] * mixed


def _pool_layer(x, norm_w, pool_w, pool_scale, B, S, TS=256):
    T, D = x.shape
    assert len(POOL_WINDOWS) == pool_w.shape[0] and max(POOL_WINDOWS) - 1 <= POOL_HALO
    ns = S // TS
    r = TS // POOL_HALO
    pw = pool_w.astype(BF16)
    return pl.pallas_call(
        _pool_kernel,
        grid=(B, ns),
        in_specs=[pl.BlockSpec((TS, D), lambda b, i: (b * ns + i, 0)),
                  pl.BlockSpec((POOL_HALO, D), lambda b, i: (jnp.maximum((b * ns + i) * r - 1, 0), 0)),
                  _resident((1, D)), _resident(pw.shape), _resident((1, D))],
        out_specs=pl.BlockSpec((TS, D), lambda b, i: (b * ns + i, 0)),
        out_shape=jax.ShapeDtypeStruct((T, D), F32),
        scratch_shapes=[pltpu.VMEM((TS + POOL_HALO, D), F32)],
        compiler_params=_params("arbitrary", "arbitrary"),
    )(x, x, norm_w.reshape(1, D), pw, pool_scale.reshape(1, D))


def _chunk(S):
    return 256 if S % 256 == 0 else 128


def kernel(x, rel_bias, mix_norm_e, w_in_e, q_norm_e, k_norm_e, conv_w_e, conv_b_e, conv_ln_g_e, conv_ln_b_e,
           w_out_e, mix_norm_o, pool_w_o, pool_scale_o, ffn_norm, router_group_w, router_group_b,
           router_expert_w, router_expert_b, w_gate, w_up, w_down):
    B, S, D = x.shape
    T = B * S
    depth = ffn_norm.shape[0]
    n_heads = rel_bias.shape[1]
    idx_heads = (w_in_e.shape[2] - n_heads * HEAD_DIM - 2 * HEAD_DIM - IDX_DIM - 2 * conv_w_e.shape[2]) \
        // (IDX_DIM + 1)
    k_top = min(INDEX_TOPK, S // 4)
    C = _chunk(S)
    xf = x.reshape(T, D)
    bias_tiles, bias_stat = _rel_bias_tiles(rel_bias, C)
    for l in range(depth):
        i = l // 2
        if l % 2 == 0:
            q, k, v, iq, ik, iw, u = _in_proj(xf, mix_norm_e[i], w_in_e[i], q_norm_e[i], k_norm_e[i],
                                              n_heads, idx_heads, conv_w_e.shape[2])
            attn = _attention(q, k, v, iq, ik, iw, bias_tiles, bias_stat, B, S, k_top, C)
            conv = _conv_module(u, conv_w_e[i], conv_b_e[i], conv_ln_g_e[i], conv_ln_b_e[i], B, S)
            xf = _out_proj(xf, attn, conv, w_out_e[i])
        else:
            xf = _pool_layer(xf, mix_norm_o[i], pool_w_o[i], pool_scale_o[i], B, S)
        xf = _moe(xf, ffn_norm[l], router_group_w[l], router_group_b[l], router_expert_w[l], router_expert_b[l],
                  w_gate[l].astype(BF16), w_up[l].astype(BF16), w_down[l].astype(BF16))
    return xf.reshape(B, S, D)
```

```python
import functools
import math

import jax
import jax.numpy as jnp
from jax import lax
from jax.experimental import pallas as pl
from jax.experimental.pallas import tpu as pltpu

F32 = jnp.float32
BF16 = jnp.bfloat16

NORM_EPS = 1e-6
HEAD_DIM = 128
IDX_DIM = 64
INDEX_TOPK = 256
REL_BUCKETS = 32
REL_MAX_DIST = 128
POOL_WINDOWS = (2, 4, 8, 16)
EXPERTS_PER_GROUP = 4
LANES = 128
VMEM_LIMIT = 56 * 1024 * 1024
NEG = -1e30
INT_MIN = -(2 ** 31)


def _dot(a, b):
    return jnp.dot(a, b, preferred_element_type=F32)


def _dot_nt(a, b):
    return lax.dot_general(a, b, (((1,), (1,)), ((), ())), preferred_element_type=F32)


def _rms(x, w):
    return x * lax.rsqrt(jnp.mean(x * x, axis=-1, keepdims=True) + NORM_EPS) * w


def _params(*sem):
    return pltpu.CompilerParams(dimension_semantics=sem, vmem_limit_bytes=VMEM_LIMIT)


def _resident(shape):
    nd = len(shape)
    return pl.BlockSpec(shape, lambda *_: (0,) * nd, pipeline_mode=pl.Buffered(1))


def _in_proj_kernel(x_ref, nw_ref, qn_ref, kn_ref, wq_ref, wkv_ref, wiq_ref, wikw_ref, wa_ref, wg_ref,
                    q_ref, k_ref, v_ref, iq_ref, ik_ref, iw_ref, u_ref, *, q_scale, iw_scale):
    h = _rms(x_ref[...], nw_ref[...]).astype(BF16)
    n_pairs = q_ref.shape[0] // 2
    qn = qn_ref[...] * q_scale
    for c in range(n_pairs):
        qq = _dot(h, wq_ref[:, c * 256:(c + 1) * 256])
        for s in range(2):
            qh = qq[:, s * HEAD_DIM:(s + 1) * HEAD_DIM]
            q_ref[2 * c + s] = _rms(qh, qn).astype(BF16)
    kv = _dot(h, wkv_ref[...])
    k_ref[...] = _rms(kv[:, :HEAD_DIM], kn_ref[...]).astype(BF16)
    v_ref[...] = kv[:, HEAD_DIM:].astype(BF16)
    for c in range(iq_ref.shape[0] // 2):
        r = _dot(h, wiq_ref[:, c * 256:(c + 1) * 256])
        iq_ref[2 * c] = r[:, :LANES].astype(BF16)
        iq_ref[2 * c + 1] = r[:, LANES:].astype(BF16)
    r = _dot(h, wikw_ref[...])
    ik_ref[...] = r[:, :LANES].astype(BF16)
    iw_ref[...] = r[:, LANES:] * iw_scale
    for c in range(u_ref.shape[1] // 256):
        cs = slice(c * 256, (c + 1) * 256)
        a = _dot(h, wa_ref[:, cs])
        g = _dot(h, wg_ref[:, cs])
        u_ref[:, cs] = (a * jax.nn.sigmoid(g)).astype(BF16)


def _split_w_in_kernel(w_ref, wq_ref, wkv_ref, wiq_ref, wikw_ref, wa_ref, wg_ref, *, idx_heads):
    o = 0
    for ref in (wq_ref, wkv_ref, wiq_ref):
        n = ref.shape[1]
        ref[...] = w_ref[:, o:o + n].astype(BF16)
        o += n
    wik = w_ref[:, o:o + IDX_DIM]
    wiw = w_ref[:, o + IDX_DIM:o + IDX_DIM + idx_heads]
    o += IDX_DIM + idx_heads
    pad = jnp.zeros((wik.shape[0], LANES - idx_heads), F32)
    wikw_ref[...] = jnp.concatenate([wik, wik, wiw, pad], axis=1).astype(BF16)
    for ref in (wa_ref, wg_ref):
        n = ref.shape[1]
        ref[...] = w_ref[:, o:o + n].astype(BF16)
        o += n


def _split_w_in(w_in, n_heads, idx_heads, conv_ch, tk=256):
    D, n_in = w_in.shape
    widths = (n_heads * HEAD_DIM, 2 * HEAD_DIM, idx_heads * IDX_DIM, 2 * LANES, conv_ch, conv_ch)
    assert n_in == sum(widths) - 2 * LANES + IDX_DIM + idx_heads and 2 * IDX_DIM == LANES and D % tk == 0
    return pl.pallas_call(
        functools.partial(_split_w_in_kernel, idx_heads=idx_heads),
        grid=(D // tk,),
        in_specs=[pl.BlockSpec((tk, n_in), lambda i: (i, 0))],
        out_specs=[pl.BlockSpec((tk, w), lambda i: (i, 0)) for w in widths],
        out_shape=[jax.ShapeDtypeStruct((D, w), BF16) for w in widths],
        compiler_params=_params("arbitrary"),
    )(w_in)


def _in_proj(x, norm_w, w_in, q_norm, k_norm, n_heads, idx_heads, conv_ch, tm=256):
    T, D = x.shape
    iq_w = idx_heads * IDX_DIM
    ws = _split_w_in(w_in, n_heads, idx_heads, conv_ch)
    row = lambda w: pl.BlockSpec((tm, w), lambda i: (i, 0))
    heads = lambda n: pl.BlockSpec((n, tm, LANES), lambda i: (0, i, 0))
    kern = functools.partial(_in_proj_kernel, q_scale=HEAD_DIM ** -0.5,
                             iw_scale=(idx_heads ** -0.5) * (IDX_DIM ** -0.5))
    return pl.pallas_call(
        kern,
        grid=(T // tm,),
        in_specs=[row(D), _resident((1, D)), _resident((1, HEAD_DIM)), _resident((1, HEAD_DIM))]
                 + [_resident(w.shape) for w in ws],
        out_specs=[heads(n_heads), row(HEAD_DIM), row(HEAD_DIM), heads(iq_w // LANES), row(LANES), row(LANES),
                   row(conv_ch)],
        out_shape=[jax.ShapeDtypeStruct((n_heads, T, HEAD_DIM), BF16),
                   jax.ShapeDtypeStruct((T, HEAD_DIM), BF16),
                   jax.ShapeDtypeStruct((T, HEAD_DIM), BF16),
                   jax.ShapeDtypeStruct((iq_w // LANES, T, LANES), BF16),
                   jax.ShapeDtypeStruct((T, LANES), BF16),
                   jax.ShapeDtypeStruct((T, LANES), F32),
                   jax.ShapeDtypeStruct((T, conv_ch), BF16)],
        compiler_params=_params("arbitrary"),
    )(x, norm_w.reshape(1, D), q_norm.reshape(1, HEAD_DIM), k_norm.reshape(1, HEAD_DIM), *ws)


def _rel_bias_kernel(rb_ref, o_ref, stat_ref):
    _, n_heads, C, _ = o_ref.shape
    tau = lax.broadcasted_iota(jnp.int32, (C, C), 0)
    sig = lax.broadcasted_iota(jnp.int32, (C, C), 1)
    max_exact = REL_BUCKETS // 2
    for kind in range(2):
        d = tau - sig + kind * C
        n = jnp.maximum(d, 0)
        nf = jnp.maximum(n, 1).astype(F32)
        large = max_exact + (jnp.log(nf / max_exact) / math.log(REL_MAX_DIST / max_exact)
                             * (REL_BUCKETS - max_exact)).astype(jnp.int32)
        large = jnp.minimum(large, REL_BUCKETS - 1)
        bucket = jnp.where(n < max_exact, n, large)
        for h in range(n_heads):
            b = jnp.zeros((C, C), F32)
            for bk in range(REL_BUCKETS):
                b = jnp.where(bucket == bk, rb_ref[bk, h], b)
            b = b - rb_ref[REL_BUCKETS - 1, h]
            if kind == 0:
                b = jnp.where(d < 0, NEG, b)
            o_ref[kind, h] = b
    for h in range(n_heads):
        hi = rb_ref[0, h]
        lo = rb_ref[0, h]
        for bk in range(1, REL_BUCKETS):
            hi = jnp.maximum(hi, rb_ref[bk, h])
            lo = jnp.minimum(lo, rb_ref[bk, h])
        stat_ref[0, h] = hi - rb_ref[REL_BUCKETS - 1, h]
        stat_ref[1, h] = lo - rb_ref[REL_BUCKETS - 1, h]


def _rel_bias_tiles(rel_bias, C):
    n_heads = rel_bias.shape[1]
    assert C >= REL_MAX_DIST
    return pl.pallas_call(
        _rel_bias_kernel,
        in_specs=[pl.BlockSpec(memory_space=pltpu.SMEM)],
        out_specs=[pl.BlockSpec(memory_space=pltpu.VMEM), pl.BlockSpec(memory_space=pltpu.SMEM)],
        out_shape=[jax.ShapeDtypeStruct((2, n_heads, C, C), F32), jax.ShapeDtypeStruct((2, n_heads), F32)],
        compiler_params=pltpu.CompilerParams(vmem_limit_bytes=VMEM_LIMIT),
    )(rel_bias)


SHIFT_SPAN_LIMIT = 60.0


def _order_key(x):
    bits = pltpu.bitcast(x, jnp.int32)
    return bits ^ ((bits >> 31) & 0x7FFFFFFF)


def _row_to_col(row):
    C = row.shape[1]
    halves = []
    for part in (row >> 16, row & 0xFFFF):
        halves.append(jnp.broadcast_to(part.astype(F32), (LANES, C)).T[:, 0:1].astype(jnp.int32))
    return (halves[0] << 16) | halves[1]


def _attn_kernel(bstat_ref, q_ref, k_ref, v_ref, iq_ref, ik_ref, iw_ref, bias_ref, o_ref,
                 key_sc, keyt_sc, keyt16_sc, iqm_sc, vx_sc, kmax_sc, shift_sc, acc_sc, *, k_top):
    n_heads, C, _ = q_ref.shape
    idx_heads = iqm_sc.shape[0]
    idx_bits = max(1, (key_sc.shape[0] * C - 1).bit_length())
    qi = pl.program_id(1)
    nkv = qi + 1

    @pl.when(qi == 0)
    def _():
        vx_sc[:, :HEAD_DIM] = v_ref[...]
        vx_sc[:, HEAD_DIM:] = jnp.ones((vx_sc.shape[0], HEAD_DIM), BF16)
        kf = k_ref[...].astype(F32)
        k2 = jnp.sum(kf * kf, axis=-1, keepdims=True)
        kmax_sc[...] = jnp.broadcast_to(jnp.sqrt(jnp.max(k2, axis=0, keepdims=True)), kmax_sc.shape)

    lane = lax.broadcasted_iota(jnp.int32, (C, LANES), 1)
    for p in range(idx_heads // 2):
        qp = iq_ref[p].astype(F32)
        iqm_sc[2 * p] = jnp.where(lane < IDX_DIM, qp, 0.0).astype(BF16)
        iqm_sc[2 * p + 1] = jnp.where(lane >= IDX_DIM, qp, 0.0).astype(BF16)
    iw = iw_ref[...]
    tau = lax.broadcasted_iota(jnp.int32, (C, C), 0)
    sig = lax.broadcasted_iota(jnp.int32, (C, C), 1)

    def score_body(j, carry):
        off = pl.multiple_of(j * C, C)
        ikc = ik_ref[pl.ds(off, C), :]
        acc = jnp.zeros((C, C), F32)
        for hh in range(idx_heads):
            s = _dot_nt(iqm_sc[hh], ikc)
            acc = acc + jnp.maximum(s, 0.0) * iw[:, hh:hh + 1]
        acc = jnp.where(jnp.logical_and(j == qi, sig > tau), -jnp.inf, acc)
        key_sc[j] = _order_key(acc)
        kt = _order_key(acc.T)
        keyt_sc[j] = kt
        keyt16_sc[j] = (kt >> 16).astype(jnp.int16)
        return carry

    lax.fori_loop(0, nkv, score_body, 0)

    SUB = 32

    def count(hit):
        def body(j, cnt):
            for r in range(C // SUB):
                cnt = cnt + hit(j, r)
            return cnt
        cnt = lax.fori_loop(0, nkv, body, jnp.zeros((SUB, C), F32))
        return jnp.sum(cnt, axis=0, keepdims=True)

    def keyt(j, r):
        return keyt_sc[j, pl.ds(r * SUB, SUB), :]

    def count16(cand16):
        one = jnp.ones((SUB, C), jnp.int16)
        zero = jnp.zeros((SUB, C), jnp.int16)

        def body(j, cnt):
            for r in range(C // SUB):
                cnt = cnt + jnp.where(keyt16_sc[j, pl.ds(r * SUB, SUB), :] >= cand16, one, zero)
            return cnt
        cnt = lax.fori_loop(0, nkv, body, zero)
        return jnp.sum(cnt.astype(jnp.int32).astype(F32), axis=0, keepdims=True)

    def bit16_body(b, carry):
        res, n_res = carry
        cand = res ^ lax.shift_left(jnp.int32(1), 31 - b)
        tot = count16((cand >> 16).astype(jnp.int16))
        ok = tot >= k_top
        return jnp.where(ok, cand, res), jnp.where(ok, tot, n_res)

    searched = nkv * C > k_top
    carry = (jnp.full((1, C), INT_MIN, jnp.int32), jnp.zeros((1, C), F32))
    top_row, n_top = lax.fori_loop(0, jnp.where(searched, 16, 0), bit16_body, carry)

    top16 = top_row >> 16

    def low_body(j, c):
        kt = keyt_sc[j]
        hi = kt >> 16
        lo = (kt & 0xFFFF) - 32768
        lo = jnp.where(hi == top16, lo, jnp.where(hi > top16, 32767, -32768))
        keyt16_sc[j] = lo.astype(jnp.int16)
        return c

    lax.fori_loop(0, jnp.where(searched, nkv, 0), low_body, 0)

    def bit16_low_body(b, carry):
        res, n_res = carry
        cand = res ^ lax.shift_left(jnp.int32(1), 31 - b)
        tot = count16(((cand & 0xFFFF) - 32768).astype(jnp.int16))
        ok = tot >= k_top
        return jnp.where(ok, cand, res), jnp.where(ok, tot, n_res)

    thr_row, n_thr = lax.fori_loop(16, jnp.where(searched, 32, 16), bit16_low_body, (top_row, n_top))
    thr = _row_to_col(thr_row)

    @pl.when(jnp.max(n_thr) > k_top)
    def _():
        need = k_top - count(lambda j, r: jnp.where(keyt(j, r) > thr_row, 1.0, 0.0))
        s_sub = lax.broadcasted_iota(jnp.int32, (SUB, C), 0)

        def idx_body(b, last):
            cand = last | lax.shift_left(jnp.int32(1), idx_bits - 1 - b)
            below = count(lambda j, r: jnp.where(
                keyt(j, r) == thr_row, jnp.where(j * C + r * SUB + s_sub < cand, 1.0, 0.0), 0.0))
            return jnp.where(below < need, cand, last)

        last = _row_to_col(lax.fori_loop(0, idx_bits, idx_body, jnp.zeros((1, C), jnp.int32)))

        def drop_body(j, carry):
            kk = key_sc[j]
            key_sc[j] = jnp.where(kk == thr, jnp.where(j * C + sig > last, kk - 1, kk), kk)
            return carry

        lax.fori_loop(0, nkv, drop_body, 0)

    kmax = kmax_sc[0:1, 0:1] * 1.001
    worst = jnp.zeros((C, 1), F32)
    for h in range(n_heads):
        qf = q_ref[h].astype(F32)
        bound = jnp.sqrt(jnp.sum(qf * qf, axis=-1, keepdims=True)) * kmax
        shift_sc[h] = jnp.broadcast_to(bound + bstat_ref[0, h], (C, LANES))
        worst = jnp.maximum(worst, 2.0 * bound + (bstat_ref[0, h] - bstat_ref[1, h]))
    loose = jnp.max(worst) > SHIFT_SPAN_LIMIT

    def logits(j, h, kind):
        off = pl.multiple_of(j * C, C)
        lg = _dot_nt(q_ref[h], k_ref[pl.ds(off, C), :])
        return lg if kind is None else lg + bias_ref[kind, h]

    def near_tiles(fn):
        @pl.when(qi > 0)
        def _():
            fn(qi - 1, 1)
        fn(qi, 0)

    def far_tiles(fn):
        def body(j, carry):
            fn(j, None)
            return carry
        lax.fori_loop(0, qi - 1, body, 0)

    @pl.when(loose)
    def _():
        for h in range(n_heads):
            shift_sc[h] = jnp.full((C, LANES), NEG, F32)

        def max_tile(j, kind):
            sel = key_sc[j] >= thr
            for h in range(n_heads):
                lg = jnp.where(sel, logits(j, h, kind), NEG)
                m = jnp.max(lg, axis=-1, keepdims=True)
                shift_sc[h] = jnp.maximum(shift_sc[h], jnp.broadcast_to(m, (C, LANES)))

        far_tiles(max_tile)
        near_tiles(max_tile)

    acc_sc[...] = jnp.zeros(acc_sc.shape, F32)

    def attn_tile(j, kind):
        off = pl.multiple_of(j * C, C)
        vx = vx_sc[pl.ds(off, C), :]
        sel = key_sc[j] >= thr
        for h in range(n_heads):
            sh = jnp.concatenate([shift_sc[h]] * (C // LANES), axis=1)
            p = jnp.where(sel, jnp.exp(logits(j, h, kind) - sh), 0.0)
            acc_sc[h] += _dot(p.astype(BF16), vx)

    far_tiles(attn_tile)
    near_tiles(attn_tile)
    for h in range(n_heads):
        a = acc_sc[h]
        o_ref[:, h * HEAD_DIM:(h + 1) * HEAD_DIM] = (a[:, :HEAD_DIM] / a[:, HEAD_DIM:]).astype(BF16)


def _attention(q, k, v, iq, ik, iw, bias_tiles, bias_stat, B, S, k_top, C):
    n_heads, T, _ = q.shape
    n_pairs = iq.shape[0]
    nq = S // C
    kern = functools.partial(_attn_kernel, k_top=k_top)
    heads = lambda n: pl.BlockSpec((n, C, LANES), lambda b, i: (0, b * nq + i, 0))
    seq = pl.BlockSpec((S, LANES), lambda b, i: (b, 0))
    return pl.pallas_call(
        kern,
        grid=(B, nq),
        in_specs=[pl.BlockSpec(memory_space=pltpu.SMEM),
                  heads(n_heads), seq, seq, heads(n_pairs), seq,
                  pl.BlockSpec((C, LANES), lambda b, i: (b * nq + i, 0)),
                  _resident(bias_tiles.shape)],
        out_specs=pl.BlockSpec((C, n_heads * HEAD_DIM), lambda b, i: (b * nq + i, 0)),
        out_shape=jax.ShapeDtypeStruct((T, n_heads * HEAD_DIM), BF16),
        scratch_shapes=[pltpu.VMEM((nq, C, C), jnp.int32),
                        pltpu.VMEM((nq, C, C), jnp.int32),
                        pltpu.VMEM((nq, C, C), jnp.int16),
                        pltpu.VMEM((2 * n_pairs, C, LANES), BF16),
                        pltpu.VMEM((S, 2 * HEAD_DIM), BF16),
                        pltpu.VMEM((8, LANES), F32),
                        pltpu.VMEM((n_heads, C, LANES), F32),
                        pltpu.VMEM((n_heads, C, 2 * HEAD_DIM), F32)],
        compiler_params=_params("arbitrary", "arbitrary"),
    )(bias_stat, q, k, v, iq, ik, iw, bias_tiles)


CONV_HALO = 32


def _conv_kernel(u_ref, halo_ref, w_ref, cb_ref, g_ref, b_ref, o_ref, buf, y_sc, *, width):
    TS, CH = u_ref.shape
    i = pl.program_id(1)
    buf[0:CONV_HALO] = jnp.where(i > 0, halo_ref[...].astype(F32), 0.0)
    buf[CONV_HALO:] = u_ref[...].astype(F32)
    base = CONV_HALO - (width - 1)
    for c in range(CH // LANES):
        cs = slice(c * LANES, (c + 1) * LANES)
        acc = jnp.zeros((TS, LANES), F32)
        for j in range(width):
            acc = acc + w_ref[j:j + 1, cs] * buf[base + j:base + j + TS, cs]
        y_sc[:, cs] = acc + cb_ref[:, cs]
    y = y_sc[...]
    mu = jnp.mean(y, axis=-1, keepdims=True)
    yc = y - mu
    var = jnp.mean(yc * yc, axis=-1, keepdims=True)
    yn = yc * lax.rsqrt(var + NORM_EPS) * g_ref[...] + b_ref[...]
    o_ref[...] = (yn * jax.nn.sigmoid(yn)).astype(BF16)


def _conv_module(u, conv_w, conv_b, ln_g, ln_b, B, S, TS=256):
    T, CH = u.shape
    width = conv_w.shape[0]
    assert width - 1 <= CONV_HALO
    ns = S // TS
    r = TS // CONV_HALO
    wpad = jnp.zeros((CONV_HALO, CH), F32).at[:width].set(conv_w)
    vec = lambda a: a.reshape(1, CH)
    kern = functools.partial(_conv_kernel, width=width)
    return pl.pallas_call(
        kern,
        grid=(B, ns),
        in_specs=[pl.BlockSpec((TS, CH), lambda b, i: (b * ns + i, 0)),
                  pl.BlockSpec((CONV_HALO, CH), lambda b, i: (jnp.maximum((b * ns + i) * r - 1, 0), 0)),
                  _resident((CONV_HALO, CH)), _resident((1, CH)), _resident((1, CH)), _resident((1, CH))],
        out_specs=pl.BlockSpec((TS, CH), lambda b, i: (b * ns + i, 0)),
        out_shape=jax.ShapeDtypeStruct((T, CH), BF16),
        scratch_shapes=[pltpu.VMEM((TS + CONV_HALO, CH), F32), pltpu.VMEM((TS, CH), F32)],
        compiler_params=_params("arbitrary", "arbitrary"),
    )(u, u, wpad, vec(conv_b), vec(ln_g), vec(ln_b))


def _out_proj_kernel(x_ref, a_ref, c_ref, wa_ref, wc_ref, o_ref):
    o_ref[...] = x_ref[...] + _dot(a_ref[...], wa_ref[...]) + _dot(c_ref[...], wc_ref[...])


def _out_proj(x, attn, conv, w_out, tm=512):
    T, D = x.shape
    aw, cw = attn.shape[1], conv.shape[1]
    wa = w_out[:aw].astype(BF16)
    wc = w_out[aw:].astype(BF16)
    row = lambda w: pl.BlockSpec((tm, w), lambda i: (i, 0))
    return pl.pallas_call(
        _out_proj_kernel,
        grid=(T // tm,),
        in_specs=[row(D), row(aw), row(cw), _resident(wa.shape), _resident(wc.shape)],
        out_specs=row(D),
        out_shape=jax.ShapeDtypeStruct((T, D), F32),
        compiler_params=_params("arbitrary"),
    )(x, attn, conv, wa, wc)


def _first_max(vals):
    m = vals[0]
    for v in vals[1:]:
        m = jnp.maximum(m, v)
    idx = jnp.full(m.shape, len(vals) - 1, jnp.int32)
    for k in range(len(vals) - 2, -1, -1):
        idx = jnp.where(vals[k] == m, k, idx)
    return m, idx


def _softmax_cols(cols):
    m = cols[0]
    for c in cols[1:]:
        m = jnp.maximum(m, c)
    e = [jnp.exp(c - m) for c in cols]
    s = e[0]
    for c in e[1:]:
        s = s + c
    return [c / s for c in e]


GROUP_ROWS = 8
MOE_CHUNK = 128


def _split3(a):
    hi = a.astype(BF16)
    r = a - hi.astype(F32)
    mid = r.astype(BF16)
    lo = (r - mid.astype(F32)).astype(BF16)
    return hi, mid, lo


def _moe_kernel(x_ref, nw_ref, wrt_ref, brt_ref, wg_ref, wu_ref, wd_ref, o_ref,
                h_sc, hs_sc, p_sc, u_sc, cs_sc, seg_sm, *, n_groups):
    TM, D = x_ref.shape
    i = pl.program_id(0)
    e = pl.program_id(1)
    n_exp = pl.num_programs(1)

    @pl.when(jnp.logical_and(i == 0, e == 0))
    def _():
        s_io = lax.broadcasted_iota(jnp.int32, (TM, TM), 0)
        t_io = lax.broadcasted_iota(jnp.int32, (TM, TM), 1)
        u_sc[...] = jnp.where(s_io < t_io, 1.0, 0.0).astype(BF16)

    @pl.when(e == 0)
    def _():
        h = _rms(x_ref[...], nw_ref[...]).astype(BF16)
        h_sc[...] = h
        lg = _dot_nt(wrt_ref[...], h) + brt_ref[...]
        row = lambda k: lg[k:k + 1, :]
        g_prob = _softmax_cols([row(g) for g in range(n_groups)])
        g_p, g_idx = _first_max(g_prob)
        e_logit = []
        for k in range(EXPERTS_PER_GROUP):
            v = row(GROUP_ROWS + k)
            for g in range(1, n_groups):
                v = jnp.where(g_idx == g, row(GROUP_ROWS + g * EXPERTS_PER_GROUP + k), v)
            e_logit.append(v)
        e_prob = _softmax_cols(e_logit)
        p1, i1 = _first_max(e_prob)
        rest = [jnp.where(i1 == k, -1.0, e_prob[k]) for k in range(EXPERTS_PER_GROUP)]
        p2, i2 = _first_max(rest)
        den = p1 + p2
        base = g_idx * EXPERTS_PER_GROUP
        e_io = lax.broadcasted_iota(jnp.int32, (LANES, TM), 0)
        comb_t = (jnp.where(e_io == base + i1, g_p * (p1 / den), 0.0)
                  + jnp.where(e_io == base + i2, g_p * (p2 / den), 0.0))

        g_io = lax.broadcasted_iota(jnp.int32, (GROUP_ROWS, TM), 0)
        onehot_t = jnp.where(g_io == g_idx, 1.0, 0.0)
        rank = _dot(onehot_t.astype(BF16), u_sc[...])
        start = jnp.int32(0)
        pos = jnp.zeros((1, TM), F32)
        for g in range(n_groups):
            seg_sm[g] = start
            pos = jnp.where(g_idx == g, start.astype(F32) + rank[g:g + 1, :], pos)
            start = start + jnp.sum(onehot_t[g:g + 1, :]).astype(jnp.int32)
        seg_sm[n_groups] = start
        pos = pos.astype(jnp.int32)

        rb = 256
        for r0 in range(0, TM, rb):
            r_io = lax.broadcasted_iota(jnp.int32, (rb, TM), 0) + r0
            p_sc[r0:r0 + rb, :] = jnp.where(r_io == pos, 1.0, 0.0).astype(BF16)
        p = p_sc[...]
        for c0 in range(0, D, 512):
            hs_sc[:, c0:c0 + 512] = _dot(p, h_sc[:, c0:c0 + 512]).astype(BF16)
        cs = jnp.zeros((TM, LANES), F32)
        for part in _split3(comb_t):
            cs = cs + _dot_nt(p, part)
        cs_sc[...] = cs
        o_ref[...] = jnp.zeros(o_ref.shape, F32)

    g = e // EXPERTS_PER_GROUP
    start = seg_sm[g]
    end = seg_sm[g + 1]
    c_lo = start // MOE_CHUNK
    c_hi = jnp.where(end > start, (end + MOE_CHUNK - 1) // MOE_CHUNK, c_lo)

    def expert_rows(c, n_chunks):
        m = n_chunks * MOE_CHUNK
        r0 = pl.multiple_of(c * MOE_CHUNK, MOE_CHUNK)
        rows = hs_sc[pl.ds(r0, m), :]
        lane = lax.broadcasted_iota(jnp.int32, (m, LANES), 1)
        w = jnp.sum(jnp.where(lane == e, cs_sc[pl.ds(r0, m), :], 0.0), axis=-1, keepdims=True)
        a = jax.nn.silu(_dot(rows, wg_ref[0])) * _dot(rows, wu_ref[0]) * w
        o_ref[pl.ds(r0, m), :] += _dot(a.astype(BF16), wd_ref[0])

    def triple(k, carry):
        expert_rows(c_lo + 3 * k, 3)
        return carry

    n_chunks = c_hi - c_lo
    lax.fori_loop(0, n_chunks // 3, triple, 0)
    for rest in (1, 2):
        @pl.when(n_chunks % 3 == rest)
        def _():
            expert_rows(c_hi - rest, rest)

    @pl.when(e == n_exp - 1)
    def _():
        p = p_sc[...]
        for c0 in range(0, D, 512):
            ys = o_ref[:, c0:c0 + 512].astype(BF16)
            y = lax.dot_general(p, ys, (((0,), (0,)), ((), ())), preferred_element_type=F32)
            o_ref[:, c0:c0 + 512] = x_ref[:, c0:c0 + 512] + y


def _moe(x, norm_w, wg, bg, we, be, w_gate, w_up, w_down, tm=1024):
    T, D = x.shape
    n_groups, n_exp = wg.shape[1], we.shape[1]
    _, _, ff = w_gate.shape
    assert n_groups <= GROUP_ROWS and n_exp == n_groups * EXPERTS_PER_GROUP and T % tm == 0
    rows = GROUP_ROWS + n_exp
    wrt = jnp.zeros((rows, D), F32).at[:n_groups].set(wg.T).at[GROUP_ROWS:].set(we.T).astype(BF16)
    brt = jnp.zeros((rows, 1), F32).at[:n_groups, 0].set(bg).at[GROUP_ROWS:, 0].set(be)
    tile = pl.BlockSpec((tm, D), lambda i, e: (i, 0))
    return pl.pallas_call(
        functools.partial(_moe_kernel, n_groups=n_groups),
        grid=(T // tm, n_exp),
        in_specs=[pl.BlockSpec((tm, D), lambda i, e: (i, 0), pipeline_mode=pl.Buffered(1)),
                  _resident((1, D)), _resident((rows, D)), _resident((rows, 1)),
                  pl.BlockSpec((1, D, ff), lambda i, e: (e, 0, 0)),
                  pl.BlockSpec((1, D, ff), lambda i, e: (e, 0, 0)),
                  pl.BlockSpec((1, ff, D), lambda i, e: (e, 0, 0))],
        out_specs=tile,
        out_shape=jax.ShapeDtypeStruct((T, D), F32),
        scratch_shapes=[pltpu.VMEM((tm, D), BF16),
                        pltpu.VMEM((tm, D), BF16),
                        pltpu.VMEM((tm, tm), BF16),
                        pltpu.VMEM((tm, tm), BF16),
                        pltpu.VMEM((tm, LANES), F32),
                        pltpu.SMEM((GROUP_ROWS,), jnp.int32)],
        compiler_params=_params("arbitrary", "arbitrary"),
    )(x, norm_w.reshape(1, D), wrt, brt, w_gate, w_up, w_down)


POOL_HALO = 16


def _pool_kernel(x_ref, halo_ref, nw_ref, pw_ref, ps_ref, o_ref, hb):
    TS, D = x_ref.shape
    n_groups, pc, _ = pw_ref.shape
    i = pl.program_id(1)
    nw = nw_ref[...]
    hb[0:POOL_HALO] = jnp.where(i > 0, _rms(halo_ref[...], nw), 0.0)
    hb[POOL_HALO:] = _rms(x_ref[...], nw)
    t = i * TS + lax.broadcasted_iota(jnp.int32, (TS, 1), 0)
    for g, w in enumerate(POOL_WINDOWS):
        cs = slice(g * pc, (g + 1) * pc)
        cur = hb[POOL_HALO:, cs]
        s = cur
        for r in range(1, w):
            s = s + hb[POOL_HALO - r:POOL_HALO - r + TS, cs]
        count = jnp.minimum(t + 1, w).astype(F32)
        d = s / count - cur
        mixed = _dot(d.astype(BF16), pw_ref[g])
        o_ref[:, cs] = x_ref[:, cs] + ps_ref[:, cs] * mixed


def _pool_layer(x, norm_w, pool_w, pool_scale, B, S, TS=256):
    T, D = x.shape
    assert len(POOL_WINDOWS) == pool_w.shape[0] and max(POOL_WINDOWS) - 1 <= POOL_HALO
    ns = S // TS
    r = TS // POOL_HALO
    pw = pool_w.astype(BF16)
    return pl.pallas_call(
        _pool_kernel,
        grid=(B, ns),
        in_specs=[pl.BlockSpec((TS, D), lambda b, i: (b * ns + i, 0)),
                  pl.BlockSpec((POOL_HALO, D), lambda b, i: (jnp.maximum((b * ns + i) * r - 1, 0), 0)),
                  _resident((1, D)), _resident(pw.shape), _resident((1, D))],
        out_specs=pl.BlockSpec((TS, D), lambda b, i: (b * ns + i, 0)),
        out_shape=jax.ShapeDtypeStruct((T, D), F32),
        scratch_shapes=[pltpu.VMEM((TS + POOL_HALO, D), F32)],
        compiler_params=_params("arbitrary", "arbitrary"),
    )(x, x, norm_w.reshape(1, D), pw, pool_scale.reshape(1, D))


def _chunk(S):
    return 256 if S % 256 == 0 else 128


def kernel(x, rel_bias, mix_norm_e, w_in_e, q_norm_e, k_norm_e, conv_w_e, conv_b_e, conv_ln_g_e, conv_ln_b_e,
           w_out_e, mix_norm_o, pool_w_o, pool_scale_o, ffn_norm, router_group_w, router_group_b,
           router_expert_w, router_expert_b, w_gate, w_up, w_down):
    B, S, D = x.shape
    T = B * S
    depth = ffn_norm.shape[0]
    n_heads = rel_bias.shape[1]
    idx_heads = (w_in_e.shape[2] - n_heads * HEAD_DIM - 2 * HEAD_DIM - IDX_DIM - 2 * conv_w_e.shape[2]) \
        // (IDX_DIM + 1)
    k_top = min(INDEX_TOPK, S // 4)
    C = _chunk(S)
    xf = x.reshape(T, D)
    bias_tiles, bias_stat = _rel_bias_tiles(rel_bias, C)
    for l in range(depth):
        i = l // 2
        if l % 2 == 0:
            q, k, v, iq, ik, iw, u = _in_proj(xf, mix_norm_e[i], w_in_e[i], q_norm_e[i], k_norm_e[i],
                                              n_heads, idx_heads, conv_w_e.shape[2])
            attn = _attention(q, k, v, iq, ik, iw, bias_tiles, bias_stat, B, S, k_top, C)
            conv = _conv_module(u, conv_w_e[i], conv_b_e[i], conv_ln_g_e[i], conv_ln_b_e[i], B, S)
            xf = _out_proj(xf, attn, conv, w_out_e[i])
        else:
            xf = _pool_layer(xf, mix_norm_o[i], pool_w_o[i], pool_scale_o[i], B, S)
        xf = _moe(xf, ffn_norm[l], router_group_w[l], router_group_b[l], router_expert_w[l], router_expert_b[l],
                  w_gate[l].astype(BF16), w_up[l].astype(BF16), w_down[l].astype(BF16))
    return xf.reshape(B, S, D)
```

```python
import functools
import math

import jax
import jax.numpy as jnp
from jax import lax
from jax.experimental import pallas as pl
from jax.experimental.pallas import tpu as pltpu

F32 = jnp.float32
BF16 = jnp.bfloat16

NORM_EPS = 1e-6
HEAD_DIM = 128
IDX_DIM = 64
INDEX_TOPK = 256
REL_BUCKETS = 32
REL_MAX_DIST = 128
POOL_WINDOWS = (2, 4, 8, 16)
EXPERTS_PER_GROUP = 4
LANES = 128
VMEM_LIMIT = 56 * 1024 * 1024
NEG = -1e30
INT_MIN = -(2 ** 31)


def _dot(a, b):
    return jnp.dot(a, b, preferred_element_type=F32)


def _dot_nt(a, b):
    return lax.dot_general(a, b, (((1,), (1,)), ((), ())), preferred_element_type=F32)


def _rms(x, w):
    return x * lax.rsqrt(jnp.mean(x * x, axis=-1, keepdims=True) + NORM_EPS) * w


def _params(*sem):
    return pltpu.CompilerParams(dimension_semantics=sem, vmem_limit_bytes=VMEM_LIMIT)


def _resident(shape):
    nd = len(shape)
    return pl.BlockSpec(shape, lambda *_: (0,) * nd, pipeline_mode=pl.Buffered(1))


def _in_proj_kernel(x_ref, nw_ref, qn_ref, kn_ref, wq_ref, wkv_ref, wiq_ref, wikw_ref, wa_ref, wg_ref,
                    q_ref, k_ref, v_ref, iq_ref, ik_ref, iw_ref, u_ref, *, q_scale, iw_scale):
    h = _rms(x_ref[...], nw_ref[...]).astype(BF16)
    n_pairs = q_ref.shape[0] // 2
    qn = qn_ref[...] * q_scale
    for c in range(n_pairs):
        qq = _dot(h, wq_ref[:, c * 256:(c + 1) * 256])
        for s in range(2):
            qh = qq[:, s * HEAD_DIM:(s + 1) * HEAD_DIM]
            q_ref[2 * c + s] = _rms(qh, qn).astype(BF16)
    kv = _dot(h, wkv_ref[...])
    k_ref[...] = _rms(kv[:, :HEAD_DIM], kn_ref[...]).astype(BF16)
    v_ref[...] = kv[:, HEAD_DIM:].astype(BF16)
    for c in range(iq_ref.shape[0] // 2):
        r = _dot(h, wiq_ref[:, c * 256:(c + 1) * 256])
        iq_ref[2 * c] = r[:, :LANES].astype(BF16)
        iq_ref[2 * c + 1] = r[:, LANES:].astype(BF16)
    r = _dot(h, wikw_ref[...])
    ik_ref[...] = r[:, :LANES].astype(BF16)
    iw_ref[...] = r[:, LANES:] * iw_scale
    for c in range(u_ref.shape[1] // 256):
        cs = slice(c * 256, (c + 1) * 256)
        a = _dot(h, wa_ref[:, cs])
        g = _dot(h, wg_ref[:, cs])
        u_ref[:, cs] = (a * jax.nn.sigmoid(g)).astype(BF16)


def _split_w_in_kernel(w_ref, wq_ref, wkv_ref, wiq_ref, wikw_ref, wa_ref, wg_ref, *, idx_heads):
    o = 0
    for ref in (wq_ref, wkv_ref, wiq_ref):
        n = ref.shape[1]
        ref[...] = w_ref[:, o:o + n].astype(BF16)
        o += n
    wik = w_ref[:, o:o + IDX_DIM]
    wiw = w_ref[:, o + IDX_DIM:o + IDX_DIM + idx_heads]
    o += IDX_DIM + idx_heads
    pad = jnp.zeros((wik.shape[0], LANES - idx_heads), F32)
    wikw_ref[...] = jnp.concatenate([wik, wik, wiw, pad], axis=1).astype(BF16)
    for ref in (wa_ref, wg_ref):
        n = ref.shape[1]
        ref[...] = w_ref[:, o:o + n].astype(BF16)
        o += n


def _split_w_in(w_in, n_heads, idx_heads, conv_ch, tk=256):
    D, n_in = w_in.shape
    widths = (n_heads * HEAD_DIM, 2 * HEAD_DIM, idx_heads * IDX_DIM, 2 * LANES, conv_ch, conv_ch)
    assert n_in == sum(widths) - 2 * LANES + IDX_DIM + idx_heads and 2 * IDX_DIM == LANES and D % tk == 0
    return pl.pallas_call(
        functools.partial(_split_w_in_kernel, idx_heads=idx_heads),
        grid=(D // tk,),
        in_specs=[pl.BlockSpec((tk, n_in), lambda i: (i, 0))],
        out_specs=[pl.BlockSpec((tk, w), lambda i: (i, 0)) for w in widths],
        out_shape=[jax.ShapeDtypeStruct((D, w), BF16) for w in widths],
        compiler_params=_params("arbitrary"),
    )(w_in)


def _in_proj(x, norm_w, w_in, q_norm, k_norm, n_heads, idx_heads, conv_ch, tm=256):
    T, D = x.shape
    iq_w = idx_heads * IDX_DIM
    ws = _split_w_in(w_in, n_heads, idx_heads, conv_ch)
    row = lambda w: pl.BlockSpec((tm, w), lambda i: (i, 0))
    heads = lambda n: pl.BlockSpec((n, tm, LANES), lambda i: (0, i, 0))
    kern = functools.partial(_in_proj_kernel, q_scale=HEAD_DIM ** -0.5,
                             iw_scale=(idx_heads ** -0.5) * (IDX_DIM ** -0.5))
    return pl.pallas_call(
        kern,
        grid=(T // tm,),
        in_specs=[row(D), _resident((1, D)), _resident((1, HEAD_DIM)), _resident((1, HEAD_DIM))]
                 + [_resident(w.shape) for w in ws],
        out_specs=[heads(n_heads), row(HEAD_DIM), row(HEAD_DIM), heads(iq_w // LANES), row(LANES), row(LANES),
                   row(conv_ch)],
        out_shape=[jax.ShapeDtypeStruct((n_heads, T, HEAD_DIM), BF16),
                   jax.ShapeDtypeStruct((T, HEAD_DIM), BF16),
                   jax.ShapeDtypeStruct((T, HEAD_DIM), BF16),
                   jax.ShapeDtypeStruct((iq_w // LANES, T, LANES), BF16),
                   jax.ShapeDtypeStruct((T, LANES), BF16),
                   jax.ShapeDtypeStruct((T, LANES), F32),
                   jax.ShapeDtypeStruct((T, conv_ch), BF16)],
        compiler_params=_params("arbitrary"),
    )(x, norm_w.reshape(1, D), q_norm.reshape(1, HEAD_DIM), k_norm.reshape(1, HEAD_DIM), *ws)


def _rel_bias_kernel(rb_ref, o_ref, stat_ref):
    _, n_heads, C, _ = o_ref.shape
    tau = lax.broadcasted_iota(jnp.int32, (C, C), 0)
    sig = lax.broadcasted_iota(jnp.int32, (C, C), 1)
    max_exact = REL_BUCKETS // 2
    for kind in range(2):
        d = tau - sig + kind * C
        n = jnp.maximum(d, 0)
        nf = jnp.maximum(n, 1).astype(F32)
        large = max_exact + (jnp.log(nf / max_exact) / math.log(REL_MAX_DIST / max_exact)
                             * (REL_BUCKETS - max_exact)).astype(jnp.int32)
        large = jnp.minimum(large, REL_BUCKETS - 1)
        bucket = jnp.where(n < max_exact, n, large)
        for h in range(n_heads):
            b = jnp.zeros((C, C), F32)
            for bk in range(REL_BUCKETS):
                b = jnp.where(bucket == bk, rb_ref[bk, h], b)
            b = b - rb_ref[REL_BUCKETS - 1, h]
            if kind == 0:
                b = jnp.where(d < 0, NEG, b)
            o_ref[kind, h] = b
    for h in range(n_heads):
        hi = rb_ref[0, h]
        lo = rb_ref[0, h]
        for bk in range(1, REL_BUCKETS):
            hi = jnp.maximum(hi, rb_ref[bk, h])
            lo = jnp.minimum(lo, rb_ref[bk, h])
        stat_ref[0, h] = hi - rb_ref[REL_BUCKETS - 1, h]
        stat_ref[1, h] = lo - rb_ref[REL_BUCKETS - 1, h]


def _rel_bias_tiles(rel_bias, C):
    n_heads = rel_bias.shape[1]
    assert C >= REL_MAX_DIST
    return pl.pallas_call(
        _rel_bias_kernel,
        in_specs=[pl.BlockSpec(memory_space=pltpu.SMEM)],
        out_specs=[pl.BlockSpec(memory_space=pltpu.VMEM), pl.BlockSpec(memory_space=pltpu.SMEM)],
        out_shape=[jax.ShapeDtypeStruct((2, n_heads, C, C), F32), jax.ShapeDtypeStruct((2, n_heads), F32)],
        compiler_params=pltpu.CompilerParams(vmem_limit_bytes=VMEM_LIMIT),
    )(rel_bias)


SHIFT_SPAN_LIMIT = 60.0


def _order_key(x):
    bits = pltpu.bitcast(x, jnp.int32)
    return bits ^ ((bits >> 31) & 0x7FFFFFFF)


def _row_to_col(row):
    C = row.shape[1]
    halves = []
    for part in (row >> 16, row & 0xFFFF):
        halves.append(jnp.broadcast_to(part.astype(F32), (LANES, C)).T[:, 0:1].astype(jnp.int32))
    return (halves[0] << 16) | halves[1]


def _attn_kernel(bstat_ref, q_ref, k_ref, v_ref, iq_ref, ik_ref, iw_ref, bias_ref, o_ref,
                 key_sc, keyt_sc, keyt16_sc, iqm_sc, vx_sc, kmax_sc, shift_sc, acc_sc, *, k_top):
    n_heads, C, _ = q_ref.shape
    idx_heads = iqm_sc.shape[0]
    idx_bits = max(1, (key_sc.shape[0] * C - 1).bit_length())
    qi = pl.program_id(1)
    nkv = qi + 1

    @pl.when(qi == 0)
    def _():
        vx_sc[:, :HEAD_DIM] = v_ref[...]
        vx_sc[:, HEAD_DIM:] = jnp.ones((vx_sc.shape[0], HEAD_DIM), BF16)
        kf = k_ref[...].astype(F32)
        k2 = jnp.sum(kf * kf, axis=-1, keepdims=True)
        kmax_sc[...] = jnp.broadcast_to(jnp.sqrt(jnp.max(k2, axis=0, keepdims=True)), kmax_sc.shape)

    lane = lax.broadcasted_iota(jnp.int32, (C, LANES), 1)
    for p in range(idx_heads // 2):
        qp = iq_ref[p].astype(F32)
        iqm_sc[2 * p] = jnp.where(lane < IDX_DIM, qp, 0.0).astype(BF16)
        iqm_sc[2 * p + 1] = jnp.where(lane >= IDX_DIM, qp, 0.0).astype(BF16)
    iw = iw_ref[...]
    tau = lax.broadcasted_iota(jnp.int32, (C, C), 0)
    sig = lax.broadcasted_iota(jnp.int32, (C, C), 1)

    def score_body(j, carry):
        off = pl.multiple_of(j * C, C)
        ikc = ik_ref[pl.ds(off, C), :]
        acc = jnp.zeros((C, C), F32)
        for hh in range(idx_heads):
            s = _dot_nt(iqm_sc[hh], ikc)
            acc = acc + jnp.maximum(s, 0.0) * iw[:, hh:hh + 1]
        acc = jnp.where(jnp.logical_and(j == qi, sig > tau), -jnp.inf, acc)
        key_sc[j] = _order_key(acc)
        kt = _order_key(acc.T)
        keyt_sc[j] = kt
        keyt16_sc[j] = (kt >> 16).astype(jnp.int16)
        return carry

    lax.fori_loop(0, nkv, score_body, 0)

    SUB = 32

    def count(hit):
        def body(j, cnt):
            for r in range(C // SUB):
                cnt = cnt + hit(j, r)
            return cnt
        cnt = lax.fori_loop(0, nkv, body, jnp.zeros((SUB, C), F32))
        return jnp.sum(cnt, axis=0, keepdims=True)

    def keyt(j, r):
        return keyt_sc[j, pl.ds(r * SUB, SUB), :]

    def count16(cand16):
        one = jnp.ones((SUB, C), jnp.int16)
        zero = jnp.zeros((SUB, C), jnp.int16)

        def body(j, cnt):
            for r in range(C // SUB):
                cnt = cnt + jnp.where(keyt16_sc[j, pl.ds(r * SUB, SUB), :] >= cand16, one, zero)
            return cnt
        cnt = lax.fori_loop(0, nkv, body, zero)
        return jnp.sum(cnt.astype(jnp.int32).astype(F32), axis=0, keepdims=True)

    def bit16_body(b, carry):
        res, n_res = carry
        cand = res ^ lax.shift_left(jnp.int32(1), 31 - b)
        tot = count16((cand >> 16).astype(jnp.int16))
        ok = tot >= k_top
        return jnp.where(ok, cand, res), jnp.where(ok, tot, n_res)

    searched = nkv * C > k_top
    carry = (jnp.full((1, C), INT_MIN, jnp.int32), jnp.zeros((1, C), F32))
    top_row, n_top = lax.fori_loop(0, jnp.where(searched, 16, 0), bit16_body, carry)

    top16 = top_row >> 16

    def low_body(j, c):
        kt = keyt_sc[j]
        hi = kt >> 16
        lo = (kt & 0xFFFF) - 32768
        lo = jnp.where(hi == top16, lo, jnp.where(hi > top16, 32767, -32768))
        keyt16_sc[j] = lo.astype(jnp.int16)
        return c

    lax.fori_loop(0, jnp.where(searched, nkv, 0), low_body, 0)

    def bit16_low_body(b, carry):
        res, n_res = carry
        cand = res ^ lax.shift_left(jnp.int32(1), 31 - b)
        tot = count16(((cand & 0xFFFF) - 32768).astype(jnp.int16))
        ok = tot >= k_top
        return jnp.where(ok, cand, res), jnp.where(ok, tot, n_res)

    thr_row, n_thr = lax.fori_loop(16, jnp.where(searched, 32, 16), bit16_low_body, (top_row, n_top))
    thr = _row_to_col(thr_row)

    @pl.when(jnp.max(n_thr) > k_top)
    def _():
        need = k_top - count(lambda j, r: jnp.where(keyt(j, r) > thr_row, 1.0, 0.0))
        s_sub = lax.broadcasted_iota(jnp.int32, (SUB, C), 0)

        def idx_body(b, last):
            cand = last | lax.shift_left(jnp.int32(1), idx_bits - 1 - b)
            below = count(lambda j, r: jnp.where(
                keyt(j, r) == thr_row, jnp.where(j * C + r * SUB + s_sub < cand, 1.0, 0.0), 0.0))
            return jnp.where(below < need, cand, last)

        last = _row_to_col(lax.fori_loop(0, idx_bits, idx_body, jnp.zeros((1, C), jnp.int32)))

        def drop_body(j, carry):
            kk = key_sc[j]
            key_sc[j] = jnp.where(kk == thr, jnp.where(j * C + sig > last, kk - 1, kk), kk)
            return carry

        lax.fori_loop(0, nkv, drop_body, 0)

    kmax = kmax_sc[0:1, 0:1] * 1.001
    worst = jnp.zeros((C, 1), F32)
    for h in range(n_heads):
        qf = q_ref[h].astype(F32)
        bound = jnp.sqrt(jnp.sum(qf * qf, axis=-1, keepdims=True)) * kmax
        shift_sc[h] = jnp.broadcast_to(bound + bstat_ref[0, h], (C, LANES))
        worst = jnp.maximum(worst, 2.0 * bound + (bstat_ref[0, h] - bstat_ref[1, h]))
    loose = jnp.max(worst) > SHIFT_SPAN_LIMIT

    def logits(j, h, kind):
        off = pl.multiple_of(j * C, C)
        lg = _dot_nt(q_ref[h], k_ref[pl.ds(off, C), :])
        return lg if kind is None else lg + bias_ref[kind, h]

    def near_tiles(fn):
        @pl.when(qi > 0)
        def _():
            fn(qi - 1, 1)
        fn(qi, 0)

    def far_tiles(fn):
        def body(j, carry):
            fn(j, None)
            return carry
        lax.fori_loop(0, qi - 1, body, 0)

    @pl.when(loose)
    def _():
        for h in range(n_heads):
            shift_sc[h] = jnp.full((C, LANES), NEG, F32)

        def max_tile(j, kind):
            sel = key_sc[j] >= thr
            for h in range(n_heads):
                lg = jnp.where(sel, logits(j, h, kind), NEG)
                m = jnp.max(lg, axis=-1, keepdims=True)
                shift_sc[h] = jnp.maximum(shift_sc[h], jnp.broadcast_to(m, (C, LANES)))

        far_tiles(max_tile)
        near_tiles(max_tile)

    acc_sc[...] = jnp.zeros(acc_sc.shape, F32)

    def attn_tile(j, kind):
        off = pl.multiple_of(j * C, C)
        vx = vx_sc[pl.ds(off, C), :]
        sel = key_sc[j] >= thr
        for h in range(n_heads):
            sh = jnp.concatenate([shift_sc[h]] * (C // LANES), axis=1)
            p = jnp.where(sel, jnp.exp(logits(j, h, kind) - sh), 0.0)
            acc_sc[h] += _dot(p.astype(BF16), vx)

    far_tiles(attn_tile)
    near_tiles(attn_tile)
    for h in range(n_heads):
        a = acc_sc[h]
        o_ref[:, h * HEAD_DIM:(h + 1) * HEAD_DIM] = (a[:, :HEAD_DIM] / a[:, HEAD_DIM:]).astype(BF16)


def _attention(q, k, v, iq, ik, iw, bias_tiles, bias_stat, B, S, k_top, C):
    n_heads, T, _ = q.shape
    n_pairs = iq.shape[0]
    nq = S // C
    kern = functools.partial(_attn_kernel, k_top=k_top)
    heads = lambda n: pl.BlockSpec((n, C, LANES), lambda b, i: (0, b * nq + i, 0))
    seq = pl.BlockSpec((S, LANES), lambda b, i: (b, 0))
    return pl.pallas_call(
        kern,
        grid=(B, nq),
        in_specs=[pl.BlockSpec(memory_space=pltpu.SMEM),
                  heads(n_heads), seq, seq, heads(n_pairs), seq,
                  pl.BlockSpec((C, LANES), lambda b, i: (b * nq + i, 0)),
                  _resident(bias_tiles.shape)],
        out_specs=pl.BlockSpec((C, n_heads * HEAD_DIM), lambda b, i: (b * nq + i, 0)),
        out_shape=jax.ShapeDtypeStruct((T, n_heads * HEAD_DIM), BF16),
        scratch_shapes=[pltpu.VMEM((nq, C, C), jnp.int32),
                        pltpu.VMEM((nq, C, C), jnp.int32),
                        pltpu.VMEM((nq, C, C), jnp.int16),
                        pltpu.VMEM((2 * n_pairs, C, LANES), BF16),
                        pltpu.VMEM((S, 2 * HEAD_DIM), BF16),
                        pltpu.VMEM((8, LANES), F32),
                        pltpu.VMEM((n_heads, C, LANES), F32),
                        pltpu.VMEM((n_heads, C, 2 * HEAD_DIM), F32)],
        compiler_params=_params("arbitrary", "arbitrary"),
    )(bias_stat, q, k, v, iq, ik, iw, bias_tiles)


CONV_HALO = 32


SUBLANES = 8
CONV_SLAB = 64


def _conv_kernel(u_ref, halo_ref, w_ref, cb_ref, g_ref, b_ref, o_ref, cp_sc, y_sc, *, width):
    TS, CH = u_ref.shape
    n_rows = CONV_HALO + TS
    i = pl.program_id(1)
    halo = jnp.where(i > 0, halo_ref[...].astype(F32), 0.0)
    pad = jnp.zeros((SUBLANES, LANES), F32)
    for c in range(CH // LANES):
        cs = slice(c * LANES, (c + 1) * LANES)
        col = jnp.concatenate([halo[:, cs], u_ref[:, cs].astype(F32), pad], axis=0)
        for r in range(SUBLANES):
            cp_sc[r, :, cs] = col[r:r + n_rows]

    base = CONV_HALO - (width - 1)
    phases = {}
    for j in range(width):
        q, r = divmod(base + j, SUBLANES)
        phases.setdefault(r, []).append((q, j))

    def slab(s, carry):
        t0 = pl.multiple_of(s * CONV_SLAB, CONV_SLAB)
        for c in range(CH // LANES):
            cs = slice(c * LANES, (c + 1) * LANES)
            acc = jnp.zeros((CONV_SLAB, LANES), F32)
            for r, taps in phases.items():
                q_lo, q_hi = taps[0][0], taps[-1][0]
                win = cp_sc[r, pl.ds(q_lo * SUBLANES + t0, CONV_SLAB + (q_hi - q_lo) * SUBLANES), cs]
                for q, j in taps:
                    off = (q - q_lo) * SUBLANES
                    acc = acc + w_ref[j:j + 1, cs] * win[off:off + CONV_SLAB]
            y_sc[pl.ds(t0, CONV_SLAB), cs] = acc + cb_ref[:, cs]
        return carry

    lax.fori_loop(0, TS // CONV_SLAB, slab, 0)
    y = y_sc[...]
    mu = jnp.mean(y, axis=-1, keepdims=True)
    yc = y - mu
    var = jnp.mean(yc * yc, axis=-1, keepdims=True)
    yn = yc * lax.rsqrt(var + NORM_EPS) * g_ref[...] + b_ref[...]
    o_ref[...] = (yn * jax.nn.sigmoid(yn)).astype(BF16)


def _conv_module(u, conv_w, conv_b, ln_g, ln_b, B, S, TS=256):
    T, CH = u.shape
    width = conv_w.shape[0]
    assert width - 1 <= CONV_HALO
    ns = S // TS
    r = TS // CONV_HALO
    wpad = jnp.zeros((CONV_HALO, CH), F32).at[:width].set(conv_w)
    vec = lambda a: a.reshape(1, CH)
    kern = functools.partial(_conv_kernel, width=width)
    return pl.pallas_call(
        kern,
        grid=(B, ns),
        in_specs=[pl.BlockSpec((TS, CH), lambda b, i: (b * ns + i, 0)),
                  pl.BlockSpec((CONV_HALO, CH), lambda b, i: (jnp.maximum((b * ns + i) * r - 1, 0), 0)),
                  _resident((CONV_HALO, CH)), _resident((1, CH)), _resident((1, CH)), _resident((1, CH))],
        out_specs=pl.BlockSpec((TS, CH), lambda b, i: (b * ns + i, 0)),
        out_shape=jax.ShapeDtypeStruct((T, CH), BF16),
        scratch_shapes=[pltpu.VMEM((SUBLANES, CONV_HALO + TS, CH), F32),
                        pltpu.VMEM((TS, CH), F32)],
        compiler_params=_params("arbitrary", "arbitrary"),
    )(u, u, wpad, vec(conv_b), vec(ln_g), vec(ln_b))


def _out_proj_kernel(x_ref, a_ref, c_ref, wa_ref, wc_ref, o_ref):
    o_ref[...] = x_ref[...] + _dot(a_ref[...], wa_ref[...]) + _dot(c_ref[...], wc_ref[...])


def _out_proj(x, attn, conv, w_out, tm=512):
    T, D = x.shape
    aw, cw = attn.shape[1], conv.shape[1]
    wa = w_out[:aw].astype(BF16)
    wc = w_out[aw:].astype(BF16)
    row = lambda w: pl.BlockSpec((tm, w), lambda i: (i, 0))
    return pl.pallas_call(
        _out_proj_kernel,
        grid=(T // tm,),
        in_specs=[row(D), row(aw), row(cw), _resident(wa.shape), _resident(wc.shape)],
        out_specs=row(D),
        out_shape=jax.ShapeDtypeStruct((T, D), F32),
        compiler_params=_params("arbitrary"),
    )(x, attn, conv, wa, wc)


def _first_max(vals):
    m = vals[0]
    for v in vals[1:]:
        m = jnp.maximum(m, v)
    idx = jnp.full(m.shape, len(vals) - 1, jnp.int32)
    for k in range(len(vals) - 2, -1, -1):
        idx = jnp.where(vals[k] == m, k, idx)
    return m, idx


def _softmax_cols(cols):
    m = cols[0]
    for c in cols[1:]:
        m = jnp.maximum(m, c)
    e = [jnp.exp(c - m) for c in cols]
    s = e[0]
    for c in e[1:]:
        s = s + c
    return [c / s for c in e]


GROUP_ROWS = 8
MOE_CHUNK = 128


def _split3(a):
    hi = a.astype(BF16)
    r = a - hi.astype(F32)
    mid = r.astype(BF16)
    lo = (r - mid.astype(F32)).astype(BF16)
    return hi, mid, lo


def _moe_kernel(x_ref, nw_ref, wrt_ref, brt_ref, wg_ref, wu_ref, wd_ref, o_ref,
                h_sc, hs_sc, p_sc, u_sc, cs_sc, seg_sm, *, n_groups):
    TM, D = x_ref.shape
    i = pl.program_id(0)
    e = pl.program_id(1)
    n_exp = pl.num_programs(1)

    @pl.when(jnp.logical_and(i == 0, e == 0))
    def _():
        s_io = lax.broadcasted_iota(jnp.int32, (TM, TM), 0)
        t_io = lax.broadcasted_iota(jnp.int32, (TM, TM), 1)
        u_sc[...] = jnp.where(s_io < t_io, 1.0, 0.0).astype(BF16)

    @pl.when(e == 0)
    def _():
        h = _rms(x_ref[...], nw_ref[...]).astype(BF16)
        h_sc[...] = h
        lg = _dot_nt(wrt_ref[...], h) + brt_ref[...]
        row = lambda k: lg[k:k + 1, :]
        g_prob = _softmax_cols([row(g) for g in range(n_groups)])
        g_p, g_idx = _first_max(g_prob)
        e_logit = []
        for k in range(EXPERTS_PER_GROUP):
            v = row(GROUP_ROWS + k)
            for g in range(1, n_groups):
                v = jnp.where(g_idx == g, row(GROUP_ROWS + g * EXPERTS_PER_GROUP + k), v)
            e_logit.append(v)
        e_prob = _softmax_cols(e_logit)
        p1, i1 = _first_max(e_prob)
        rest = [jnp.where(i1 == k, -1.0, e_prob[k]) for k in range(EXPERTS_PER_GROUP)]
        p2, i2 = _first_max(rest)
        den = p1 + p2
        base = g_idx * EXPERTS_PER_GROUP
        e_io = lax.broadcasted_iota(jnp.int32, (LANES, TM), 0)
        comb_t = (jnp.where(e_io == base + i1, g_p * (p1 / den), 0.0)
                  + jnp.where(e_io == base + i2, g_p * (p2 / den), 0.0))

        g_io = lax.broadcasted_iota(jnp.int32, (GROUP_ROWS, TM), 0)
        onehot_t = jnp.where(g_io == g_idx, 1.0, 0.0)
        rank = _dot(onehot_t.astype(BF16), u_sc[...])
        start = jnp.int32(0)
        pos = jnp.zeros((1, TM), F32)
        for g in range(n_groups):
            seg_sm[g] = start
            pos = jnp.where(g_idx == g, start.astype(F32) + rank[g:g + 1, :], pos)
            start = start + jnp.sum(onehot_t[g:g + 1, :]).astype(jnp.int32)
        seg_sm[n_groups] = start
        pos = pos.astype(jnp.int32)

        rb = 256
        for r0 in range(0, TM, rb):
            r_io = lax.broadcasted_iota(jnp.int32, (rb, TM), 0) + r0
            p_sc[r0:r0 + rb, :] = jnp.where(r_io == pos, 1.0, 0.0).astype(BF16)
        p = p_sc[...]
        for c0 in range(0, D, 512):
            hs_sc[:, c0:c0 + 512] = _dot(p, h_sc[:, c0:c0 + 512]).astype(BF16)
        cs = jnp.zeros((TM, LANES), F32)
        for part in _split3(comb_t):
            cs = cs + _dot_nt(p, part)
        cs_sc[...] = cs
        o_ref[...] = jnp.zeros(o_ref.shape, F32)

    g = e // EXPERTS_PER_GROUP
    start = seg_sm[g]
    end = seg_sm[g + 1]
    c_lo = start // MOE_CHUNK
    c_hi = jnp.where(end > start, (end + MOE_CHUNK - 1) // MOE_CHUNK, c_lo)

    def expert_rows(c, n_chunks):
        m = n_chunks * MOE_CHUNK
        r0 = pl.multiple_of(c * MOE_CHUNK, MOE_CHUNK)
        rows = hs_sc[pl.ds(r0, m), :]
        lane = lax.broadcasted_iota(jnp.int32, (m, LANES), 1)
        w = jnp.sum(jnp.where(lane == e, cs_sc[pl.ds(r0, m), :], 0.0), axis=-1, keepdims=True)
        a = jax.nn.silu(_dot(rows, wg_ref[0])) * _dot(rows, wu_ref[0]) * w
        o_ref[pl.ds(r0, m), :] += _dot(a.astype(BF16), wd_ref[0])

    def triple(k, carry):
        expert_rows(c_lo + 3 * k, 3)
        return carry

    n_chunks = c_hi - c_lo
    lax.fori_loop(0, n_chunks // 3, triple, 0)
    for rest in (1, 2):
        @pl.when(n_chunks % 3 == rest)
        def _():
            expert_rows(c_hi - rest, rest)

    @pl.when(e == n_exp - 1)
    def _():
        p = p_sc[...]
        for c0 in range(0, D, 512):
            ys = o_ref[:, c0:c0 + 512].astype(BF16)
            y = lax.dot_general(p, ys, (((0,), (0,)), ((), ())), preferred_element_type=F32)
            o_ref[:, c0:c0 + 512] = x_ref[:, c0:c0 + 512] + y


def _moe(x, norm_w, wg, bg, we, be, w_gate, w_up, w_down, layer, tm=1024):
    T, D = x.shape
    n_groups, n_exp = wg.shape[1], we.shape[1]
    ff = w_gate.shape[3]
    assert n_groups <= GROUP_ROWS and n_exp == n_groups * EXPERTS_PER_GROUP and T % tm == 0
    rows = GROUP_ROWS + n_exp
    wrt = jnp.zeros((rows, D), F32).at[:n_groups].set(wg.T).at[GROUP_ROWS:].set(we.T).astype(BF16)
    brt = jnp.zeros((rows, 1), F32).at[:n_groups, 0].set(bg).at[GROUP_ROWS:, 0].set(be)
    tile = pl.BlockSpec((tm, D), lambda i, e: (i, 0))
    return pl.pallas_call(
        functools.partial(_moe_kernel, n_groups=n_groups),
        grid=(T // tm, n_exp),
        in_specs=[pl.BlockSpec((tm, D), lambda i, e: (i, 0), pipeline_mode=pl.Buffered(1)),
                  _resident((1, D)), _resident((rows, D)), _resident((rows, 1)),
                  pl.BlockSpec((None, 1, D, ff), lambda i, e: (layer, e, 0, 0)),
                  pl.BlockSpec((None, 1, D, ff), lambda i, e: (layer, e, 0, 0)),
                  pl.BlockSpec((None, 1, ff, D), lambda i, e: (layer, e, 0, 0))],
        out_specs=tile,
        out_shape=jax.ShapeDtypeStruct((T, D), F32),
        scratch_shapes=[pltpu.VMEM((tm, D), BF16),
                        pltpu.VMEM((tm, D), BF16),
                        pltpu.VMEM((tm, tm), BF16),
                        pltpu.VMEM((tm, tm), BF16),
                        pltpu.VMEM((tm, LANES), F32),
                        pltpu.SMEM((GROUP_ROWS,), jnp.int32)],
        compiler_params=_params("arbitrary", "arbitrary"),
    )(x, norm_w.reshape(1, D), wrt, brt, w_gate, w_up, w_down)


POOL_HALO = 16


def _pool_kernel(x_ref, halo_ref, nw_ref, pw_ref, ps_ref, o_ref, hb, pa, pb):
    TS, D = x_ref.shape
    n_groups, pc, _ = pw_ref.shape
    i = pl.program_id(1)
    nw = nw_ref[...]
    top = SUBLANES + POOL_HALO
    n_rows = top + TS

    @pl.when(jnp.logical_and(pl.program_id(0) == 0, i == 0))
    def _():
        hb[0:SUBLANES] = jnp.zeros((SUBLANES, D), F32)
        pa[0:SUBLANES] = jnp.zeros((SUBLANES, pc), F32)
        pb[0:SUBLANES] = jnp.zeros((SUBLANES, pc), F32)

    hb[SUBLANES:top] = jnp.where(i > 0, _rms(halo_ref[...], nw), 0.0)
    hb[top:] = _rms(x_ref[...], nw)
    t = i * TS + lax.broadcasted_iota(jnp.int32, (TS, 1), 0)
    for g, w in enumerate(POOL_WINDOWS):
        cs = slice(g * pc, (g + 1) * pc)
        cur = hb[top:, cs]
        src, cols, step, dst = hb, cs, 1, pa
        while step < w:
            dst[SUBLANES:n_rows, :] = src[SUBLANES:n_rows, cols] + src[SUBLANES - step:n_rows - step, cols]
            src, cols, step, dst = dst, slice(None), 2 * step, (pb if dst is pa else pa)
        s = src[top:n_rows, cols]
        count = jnp.minimum(t + 1, w).astype(F32)
        d = s / count - cur
        mixed = _dot(d.astype(BF16), pw_ref[g])
        o_ref[:, cs] = x_ref[:, cs] + ps_ref[:, cs] * mixed


def _pool_layer(x, norm_w, pool_w, pool_scale, B, S, TS=256):
    T, D = x.shape
    assert len(POOL_WINDOWS) == pool_w.shape[0] and max(POOL_WINDOWS) - 1 <= POOL_HALO
    ns = S // TS
    r = TS // POOL_HALO
    pw = pool_w.astype(BF16)
    return pl.pallas_call(
        _pool_kernel,
        grid=(B, ns),
        in_specs=[pl.BlockSpec((TS, D), lambda b, i: (b * ns + i, 0)),
                  pl.BlockSpec((POOL_HALO, D), lambda b, i: (jnp.maximum((b * ns + i) * r - 1, 0), 0)),
                  _resident((1, D)), _resident(pw.shape), _resident((1, D))],
        out_specs=pl.BlockSpec((TS, D), lambda b, i: (b * ns + i, 0)),
        out_shape=jax.ShapeDtypeStruct((T, D), F32),
        scratch_shapes=[pltpu.VMEM((SUBLANES + POOL_HALO + TS, D), F32),
                        pltpu.VMEM((SUBLANES + POOL_HALO + TS, D // len(POOL_WINDOWS)), F32),
                        pltpu.VMEM((SUBLANES + POOL_HALO + TS, D // len(POOL_WINDOWS)), F32)],
        compiler_params=_params("arbitrary", "arbitrary"),
    )(x, x, norm_w.reshape(1, D), pw, pool_scale.reshape(1, D))


def _chunk(S):
    return 256 if S % 256 == 0 else 128


def kernel(x, rel_bias, mix_norm_e, w_in_e, q_norm_e, k_norm_e, conv_w_e, conv_b_e, conv_ln_g_e, conv_ln_b_e,
           w_out_e, mix_norm_o, pool_w_o, pool_scale_o, ffn_norm, router_group_w, router_group_b,
           router_expert_w, router_expert_b, w_gate, w_up, w_down):
    B, S, D = x.shape
    T = B * S
    depth = ffn_norm.shape[0]
    n_heads = rel_bias.shape[1]
    idx_heads = (w_in_e.shape[2] - n_heads * HEAD_DIM - 2 * HEAD_DIM - IDX_DIM - 2 * conv_w_e.shape[2]) \
        // (IDX_DIM + 1)
    k_top = min(INDEX_TOPK, S // 4)
    C = _chunk(S)
    xf = x.reshape(T, D)
    bias_tiles, bias_stat = _rel_bias_tiles(rel_bias, C)
    expert_w = [w.astype(BF16) for w in (w_gate, w_up, w_down)]
    for l in range(depth):
        i = l // 2
        if l % 2 == 0:
            q, k, v, iq, ik, iw, u = _in_proj(xf, mix_norm_e[i], w_in_e[i], q_norm_e[i], k_norm_e[i],
                                              n_heads, idx_heads, conv_w_e.shape[2])
            attn = _attention(q, k, v, iq, ik, iw, bias_tiles, bias_stat, B, S, k_top, C)
            conv = _conv_module(u, conv_w_e[i], conv_b_e[i], conv_ln_g_e[i], conv_ln_b_e[i], B, S)
            xf = _out_proj(xf, attn, conv, w_out_e[i])
        else:
            xf = _pool_layer(xf, mix_norm_o[i], pool_w_o[i], pool_scale_o[i], B, S)
        xf = _moe(xf, ffn_norm[l], router_group_w[l], router_group_b[l], router_expert_w[l], router_expert_b[l],
                  *expert_w, layer=l)
    return xf.reshape(B, S, D)
```

```python
import functools
import math

import jax
import jax.numpy as jnp
from jax import lax
from jax.experimental import pallas as pl
from jax.experimental.pallas import tpu as pltpu

F32 = jnp.float32
BF16 = jnp.bfloat16

NORM_EPS = 1e-6
HEAD_DIM = 128
IDX_DIM = 64
INDEX_TOPK = 256
REL_BUCKETS = 32
REL_MAX_DIST = 128
POOL_WINDOWS = (2, 4, 8, 16)
EXPERTS_PER_GROUP = 4
LANES = 128
VMEM_LIMIT = 56 * 1024 * 1024
NEG = -1e30
INT_MIN = -(2 ** 31)


def _dot(a, b):
    return jnp.dot(a, b, preferred_element_type=F32)


def _dot_nt(a, b):
    return lax.dot_general(a, b, (((1,), (1,)), ((), ())), preferred_element_type=F32)


def _rms(x, w):
    return x * lax.rsqrt(jnp.mean(x * x, axis=-1, keepdims=True) + NORM_EPS) * w


def _params(*sem):
    return pltpu.CompilerParams(dimension_semantics=sem, vmem_limit_bytes=VMEM_LIMIT)


def _resident(shape):
    nd = len(shape)
    return pl.BlockSpec(shape, lambda *_: (0,) * nd, pipeline_mode=pl.Buffered(1))


def _in_proj_kernel(x_ref, nw_ref, qn_ref, kn_ref, wq_ref, wkv_ref, wiq_ref, wikw_ref, wa_ref, wg_ref,
                    q_ref, k_ref, v_ref, iq_ref, ik_ref, iw_ref, u_ref, *, q_scale, iw_scale):
    h = _rms(x_ref[...], nw_ref[...]).astype(BF16)
    n_pairs = q_ref.shape[0] // 2
    qn = qn_ref[...] * q_scale
    for c in range(n_pairs):
        qq = _dot(h, wq_ref[:, c * 256:(c + 1) * 256])
        for s in range(2):
            qh = qq[:, s * HEAD_DIM:(s + 1) * HEAD_DIM]
            q_ref[2 * c + s] = _rms(qh, qn).astype(BF16)
    kv = _dot(h, wkv_ref[...])
    k_ref[...] = _rms(kv[:, :HEAD_DIM], kn_ref[...]).astype(BF16)
    v_ref[...] = kv[:, HEAD_DIM:].astype(BF16)
    for c in range(iq_ref.shape[0] // 2):
        r = _dot(h, wiq_ref[:, c * 256:(c + 1) * 256])
        iq_ref[2 * c] = r[:, :LANES].astype(BF16)
        iq_ref[2 * c + 1] = r[:, LANES:].astype(BF16)
    r = _dot(h, wikw_ref[...])
    ik_ref[...] = r[:, :LANES].astype(BF16)
    iw_ref[...] = r[:, LANES:] * iw_scale
    for c in range(u_ref.shape[1] // 256):
        cs = slice(c * 256, (c + 1) * 256)
        a = _dot(h, wa_ref[:, cs])
        g = _dot(h, wg_ref[:, cs])
        u_ref[:, cs] = (a * jax.nn.sigmoid(g)).astype(BF16)


def _split_w_in_kernel(w_ref, wq_ref, wkv_ref, wiq_ref, wikw_ref, wa_ref, wg_ref, *, idx_heads):
    o = 0
    for ref in (wq_ref, wkv_ref, wiq_ref):
        n = ref.shape[1]
        ref[...] = w_ref[:, o:o + n].astype(BF16)
        o += n
    wik = w_ref[:, o:o + IDX_DIM]
    wiw = w_ref[:, o + IDX_DIM:o + IDX_DIM + idx_heads]
    o += IDX_DIM + idx_heads
    pad = jnp.zeros((wik.shape[0], LANES - idx_heads), F32)
    wikw_ref[...] = jnp.concatenate([wik, wik, wiw, pad], axis=1).astype(BF16)
    for ref in (wa_ref, wg_ref):
        n = ref.shape[1]
        ref[...] = w_ref[:, o:o + n].astype(BF16)
        o += n


def _split_w_in(w_in, n_heads, idx_heads, conv_ch, tk=256):
    D, n_in = w_in.shape
    widths = (n_heads * HEAD_DIM, 2 * HEAD_DIM, idx_heads * IDX_DIM, 2 * LANES, conv_ch, conv_ch)
    assert n_in == sum(widths) - 2 * LANES + IDX_DIM + idx_heads and 2 * IDX_DIM == LANES and D % tk == 0
    return pl.pallas_call(
        functools.partial(_split_w_in_kernel, idx_heads=idx_heads),
        grid=(D // tk,),
        in_specs=[pl.BlockSpec((tk, n_in), lambda i: (i, 0))],
        out_specs=[pl.BlockSpec((tk, w), lambda i: (i, 0)) for w in widths],
        out_shape=[jax.ShapeDtypeStruct((D, w), BF16) for w in widths],
        compiler_params=_params("arbitrary"),
    )(w_in)


def _in_proj(x, norm_w, w_in, q_norm, k_norm, n_heads, idx_heads, conv_ch, tm=256):
    T, D = x.shape
    iq_w = idx_heads * IDX_DIM
    ws = _split_w_in(w_in, n_heads, idx_heads, conv_ch)
    row = lambda w: pl.BlockSpec((tm, w), lambda i: (i, 0))
    heads = lambda n: pl.BlockSpec((n, tm, LANES), lambda i: (0, i, 0))
    kern = functools.partial(_in_proj_kernel, q_scale=HEAD_DIM ** -0.5,
                             iw_scale=(idx_heads ** -0.5) * (IDX_DIM ** -0.5))
    return pl.pallas_call(
        kern,
        grid=(T // tm,),
        in_specs=[row(D), _resident((1, D)), _resident((1, HEAD_DIM)), _resident((1, HEAD_DIM))]
                 + [_resident(w.shape) for w in ws],
        out_specs=[heads(n_heads), row(HEAD_DIM), row(HEAD_DIM), heads(iq_w // LANES), row(LANES), row(LANES),
                   row(conv_ch)],
        out_shape=[jax.ShapeDtypeStruct((n_heads, T, HEAD_DIM), BF16),
                   jax.ShapeDtypeStruct((T, HEAD_DIM), BF16),
                   jax.ShapeDtypeStruct((T, HEAD_DIM), BF16),
                   jax.ShapeDtypeStruct((iq_w // LANES, T, LANES), BF16),
                   jax.ShapeDtypeStruct((T, LANES), BF16),
                   jax.ShapeDtypeStruct((T, LANES), F32),
                   jax.ShapeDtypeStruct((T, conv_ch), BF16)],
        compiler_params=_params("arbitrary"),
    )(x, norm_w.reshape(1, D), q_norm.reshape(1, HEAD_DIM), k_norm.reshape(1, HEAD_DIM), *ws)


def _rel_bias_kernel(rb_ref, o_ref, stat_ref):
    _, n_heads, C, _ = o_ref.shape
    tau = lax.broadcasted_iota(jnp.int32, (C, C), 0)
    sig = lax.broadcasted_iota(jnp.int32, (C, C), 1)
    max_exact = REL_BUCKETS // 2
    for kind in range(2):
        d = tau - sig + kind * C
        n = jnp.maximum(d, 0)
        nf = jnp.maximum(n, 1).astype(F32)
        large = max_exact + (jnp.log(nf / max_exact) / math.log(REL_MAX_DIST / max_exact)
                             * (REL_BUCKETS - max_exact)).astype(jnp.int32)
        large = jnp.minimum(large, REL_BUCKETS - 1)
        bucket = jnp.where(n < max_exact, n, large)
        for h in range(n_heads):
            b = jnp.zeros((C, C), F32)
            for bk in range(REL_BUCKETS):
                b = jnp.where(bucket == bk, rb_ref[bk, h], b)
            b = b - rb_ref[REL_BUCKETS - 1, h]
            if kind == 0:
                b = jnp.where(d < 0, NEG, b)
            o_ref[kind, h] = b
    for h in range(n_heads):
        hi = rb_ref[0, h]
        lo = rb_ref[0, h]
        for bk in range(1, REL_BUCKETS):
            hi = jnp.maximum(hi, rb_ref[bk, h])
            lo = jnp.minimum(lo, rb_ref[bk, h])
        stat_ref[0, h] = hi - rb_ref[REL_BUCKETS - 1, h]
        stat_ref[1, h] = lo - rb_ref[REL_BUCKETS - 1, h]


def _rel_bias_tiles(rel_bias, C):
    n_heads = rel_bias.shape[1]
    assert C >= REL_MAX_DIST
    return pl.pallas_call(
        _rel_bias_kernel,
        in_specs=[pl.BlockSpec(memory_space=pltpu.SMEM)],
        out_specs=[pl.BlockSpec(memory_space=pltpu.VMEM), pl.BlockSpec(memory_space=pltpu.SMEM)],
        out_shape=[jax.ShapeDtypeStruct((2, n_heads, C, C), F32), jax.ShapeDtypeStruct((2, n_heads), F32)],
        compiler_params=pltpu.CompilerParams(vmem_limit_bytes=VMEM_LIMIT),
    )(rel_bias)


SHIFT_SPAN_LIMIT = 60.0


def _order_key(x):
    bits = pltpu.bitcast(x, jnp.int32)
    return bits ^ ((bits >> 31) & 0x7FFFFFFF)


def _row_to_col(row):
    C = row.shape[1]
    halves = []
    for part in (row >> 16, row & 0xFFFF):
        halves.append(jnp.broadcast_to(part.astype(F32), (LANES, C)).T[:, 0:1].astype(jnp.int32))
    return (halves[0] << 16) | halves[1]


def _attn_kernel(bstat_ref, q_ref, k_ref, v_ref, iq_ref, ik_ref, iw_ref, bias_ref, o_ref,
                 key_sc, keyt_sc, keyt16_sc, iqm_sc, vx_sc, kmax_sc, shift_sc, acc_sc, *, k_top):
    n_heads, C, _ = q_ref.shape
    idx_heads = iqm_sc.shape[0]
    idx_bits = max(1, (key_sc.shape[0] * C - 1).bit_length())
    qi = pl.program_id(1)
    nkv = qi + 1

    @pl.when(qi == 0)
    def _():
        vx_sc[:, :HEAD_DIM] = v_ref[...]
        vx_sc[:, HEAD_DIM:] = jnp.ones((vx_sc.shape[0], HEAD_DIM), BF16)
        kf = k_ref[...].astype(F32)
        k2 = jnp.sum(kf * kf, axis=-1, keepdims=True)
        kmax_sc[...] = jnp.broadcast_to(jnp.sqrt(jnp.max(k2, axis=0, keepdims=True)), kmax_sc.shape)

    lane = lax.broadcasted_iota(jnp.int32, (C, LANES), 1)
    for p in range(idx_heads // 2):
        qp = iq_ref[p].astype(F32)
        iqm_sc[2 * p] = jnp.where(lane < IDX_DIM, qp, 0.0).astype(BF16)
        iqm_sc[2 * p + 1] = jnp.where(lane >= IDX_DIM, qp, 0.0).astype(BF16)
    iw = iw_ref[...]
    tau = lax.broadcasted_iota(jnp.int32, (C, C), 0)
    sig = lax.broadcasted_iota(jnp.int32, (C, C), 1)

    def score_tile(j):
        off = pl.multiple_of(j * C, C)
        ikc = ik_ref[pl.ds(off, C), :]
        acc = jnp.zeros((C, C), F32)
        for hh in range(idx_heads):
            s = _dot_nt(iqm_sc[hh], ikc)
            acc = acc + jnp.maximum(s, 0.0) * iw[:, hh:hh + 1]
        acc = jnp.where(jnp.logical_and(j == qi, sig > tau), -jnp.inf, acc)
        key_sc[j] = _order_key(acc)
        kt = _order_key(acc.T)
        keyt_sc[j] = kt
        keyt16_sc[j] = (kt >> 16).astype(jnp.int16)

    def score_pair(k, carry):
        score_tile(2 * k)
        score_tile(2 * k + 1)
        return carry

    lax.fori_loop(0, nkv // 2, score_pair, 0)

    @pl.when(nkv % 2 == 1)
    def _():
        score_tile(nkv - 1)

    SUB = 32

    def count(hit):
        def body(j, cnt):
            for r in range(C // SUB):
                cnt = cnt + hit(j, r)
            return cnt
        cnt = lax.fori_loop(0, nkv, body, jnp.zeros((SUB, C), F32))
        return jnp.sum(cnt, axis=0, keepdims=True)

    def keyt(j, r):
        return keyt_sc[j, pl.ds(r * SUB, SUB), :]

    def count16(cand16):
        one = jnp.ones((SUB, C), jnp.int16)
        zero = jnp.zeros((SUB, C), jnp.int16)

        def body(j, cnt):
            for r in range(C // SUB):
                cnt = cnt + jnp.where(keyt16_sc[j, pl.ds(r * SUB, SUB), :] >= cand16, one, zero)
            return cnt
        cnt = lax.fori_loop(0, nkv, body, zero)
        return jnp.sum(cnt.astype(jnp.int32).astype(F32), axis=0, keepdims=True)

    def bit16_body(b, carry):
        res, n_res = carry
        cand = res ^ lax.shift_left(jnp.int32(1), 31 - b)
        tot = count16((cand >> 16).astype(jnp.int16))
        ok = tot >= k_top
        return jnp.where(ok, cand, res), jnp.where(ok, tot, n_res)

    searched = nkv * C > k_top
    carry = (jnp.full((1, C), INT_MIN, jnp.int32), jnp.zeros((1, C), F32))
    top_row, n_top = lax.fori_loop(0, jnp.where(searched, 16, 0), bit16_body, carry)

    top16 = top_row >> 16

    def low_body(j, c):
        kt = keyt_sc[j]
        hi = kt >> 16
        lo = (kt & 0xFFFF) - 32768
        lo = jnp.where(hi == top16, lo, jnp.where(hi > top16, 32767, -32768))
        keyt16_sc[j] = lo.astype(jnp.int16)
        return c

    lax.fori_loop(0, jnp.where(searched, nkv, 0), low_body, 0)

    def bit16_low_body(b, carry):
        res, n_res = carry
        cand = res ^ lax.shift_left(jnp.int32(1), 31 - b)
        tot = count16(((cand & 0xFFFF) - 32768).astype(jnp.int16))
        ok = tot >= k_top
        return jnp.where(ok, cand, res), jnp.where(ok, tot, n_res)

    thr_row, n_thr = lax.fori_loop(16, jnp.where(searched, 32, 16), bit16_low_body, (top_row, n_top))
    thr = _row_to_col(thr_row)

    @pl.when(jnp.max(n_thr) > k_top)
    def _():
        need = k_top - count(lambda j, r: jnp.where(keyt(j, r) > thr_row, 1.0, 0.0))
        s_sub = lax.broadcasted_iota(jnp.int32, (SUB, C), 0)

        def idx_body(b, last):
            cand = last | lax.shift_left(jnp.int32(1), idx_bits - 1 - b)
            below = count(lambda j, r: jnp.where(
                keyt(j, r) == thr_row, jnp.where(j * C + r * SUB + s_sub < cand, 1.0, 0.0), 0.0))
            return jnp.where(below < need, cand, last)

        last = _row_to_col(lax.fori_loop(0, idx_bits, idx_body, jnp.zeros((1, C), jnp.int32)))

        def drop_body(j, carry):
            kk = key_sc[j]
            key_sc[j] = jnp.where(kk == thr, jnp.where(j * C + sig > last, kk - 1, kk), kk)
            return carry

        lax.fori_loop(0, nkv, drop_body, 0)

    kmax = kmax_sc[0:1, 0:1] * 1.001
    worst = jnp.zeros((C, 1), F32)
    for h in range(n_heads):
        qf = q_ref[h].astype(F32)
        bound = jnp.sqrt(jnp.sum(qf * qf, axis=-1, keepdims=True)) * kmax
        shift_sc[h] = jnp.broadcast_to(bound + bstat_ref[0, h], (C, LANES))
        worst = jnp.maximum(worst, 2.0 * bound + (bstat_ref[0, h] - bstat_ref[1, h]))
    loose = jnp.max(worst) > SHIFT_SPAN_LIMIT

    def logits(j, h, kind):
        off = pl.multiple_of(j * C, C)
        lg = _dot_nt(q_ref[h], k_ref[pl.ds(off, C), :])
        return lg if kind is None else lg + bias_ref[kind, h]

    def near_tiles(fn):
        @pl.when(qi > 0)
        def _():
            fn(qi - 1, 1)
        fn(qi, 0)

    def far_tiles(fn):
        n_far = jnp.maximum(qi - 1, 0)

        def body(k, carry):
            fn(2 * k, None)
            fn(2 * k + 1, None)
            return carry
        lax.fori_loop(0, n_far // 2, body, 0)

        @pl.when(n_far % 2 == 1)
        def _():
            fn(n_far - 1, None)

    @pl.when(loose)
    def _():
        for h in range(n_heads):
            shift_sc[h] = jnp.full((C, LANES), NEG, F32)

        def max_tile(j, kind):
            sel = key_sc[j] >= thr
            for h in range(n_heads):
                lg = jnp.where(sel, logits(j, h, kind), NEG)
                m = jnp.max(lg, axis=-1, keepdims=True)
                shift_sc[h] = jnp.maximum(shift_sc[h], jnp.broadcast_to(m, (C, LANES)))

        far_tiles(max_tile)
        near_tiles(max_tile)

    acc_sc[...] = jnp.zeros(acc_sc.shape, F32)

    def attn_tile(j, kind):
        off = pl.multiple_of(j * C, C)
        vx = vx_sc[pl.ds(off, C), :]
        sel = key_sc[j] >= thr
        for h in range(n_heads):
            sh = jnp.concatenate([shift_sc[h]] * (C // LANES), axis=1)
            p = jnp.where(sel, jnp.exp(logits(j, h, kind) - sh), 0.0)
            acc_sc[h] += _dot(p.astype(BF16), vx)

    far_tiles(attn_tile)
    near_tiles(attn_tile)
    for h in range(n_heads):
        a = acc_sc[h]
        o_ref[:, h * HEAD_DIM:(h + 1) * HEAD_DIM] = (a[:, :HEAD_DIM] / a[:, HEAD_DIM:]).astype(BF16)


def _attention(q, k, v, iq, ik, iw, bias_tiles, bias_stat, B, S, k_top, C):
    n_heads, T, _ = q.shape
    n_pairs = iq.shape[0]
    nq = S // C
    kern = functools.partial(_attn_kernel, k_top=k_top)
    heads = lambda n: pl.BlockSpec((n, C, LANES), lambda b, i: (0, b * nq + i, 0))
    seq = pl.BlockSpec((S, LANES), lambda b, i: (b, 0))
    return pl.pallas_call(
        kern,
        grid=(B, nq),
        in_specs=[pl.BlockSpec(memory_space=pltpu.SMEM),
                  heads(n_heads), seq, seq, heads(n_pairs), seq,
                  pl.BlockSpec((C, LANES), lambda b, i: (b * nq + i, 0)),
                  _resident(bias_tiles.shape)],
        out_specs=pl.BlockSpec((C, n_heads * HEAD_DIM), lambda b, i: (b * nq + i, 0)),
        out_shape=jax.ShapeDtypeStruct((T, n_heads * HEAD_DIM), BF16),
        scratch_shapes=[pltpu.VMEM((nq, C, C), jnp.int32),
                        pltpu.VMEM((nq, C, C), jnp.int32),
                        pltpu.VMEM((nq, C, C), jnp.int16),
                        pltpu.VMEM((2 * n_pairs, C, LANES), BF16),
                        pltpu.VMEM((S, 2 * HEAD_DIM), BF16),
                        pltpu.VMEM((8, LANES), F32),
                        pltpu.VMEM((n_heads, C, LANES), F32),
                        pltpu.VMEM((n_heads, C, 2 * HEAD_DIM), F32)],
        compiler_params=_params("arbitrary", "arbitrary"),
    )(bias_stat, q, k, v, iq, ik, iw, bias_tiles)


CONV_HALO = 32


SUBLANES = 8
CONV_SLAB = 64


def _conv_kernel(u_ref, halo_ref, w_ref, cb_ref, g_ref, b_ref, o_ref, cp_sc, y_sc, *, width):
    TS, CH = u_ref.shape
    n_rows = CONV_HALO + TS
    i = pl.program_id(1)
    halo = jnp.where(i > 0, halo_ref[...].astype(F32), 0.0)
    pad = jnp.zeros((SUBLANES, LANES), F32)
    for c in range(CH // LANES):
        cs = slice(c * LANES, (c + 1) * LANES)
        col = jnp.concatenate([halo[:, cs], u_ref[:, cs].astype(F32), pad], axis=0)
        for r in range(SUBLANES):
            cp_sc[r, :, cs] = col[r:r + n_rows]

    base = CONV_HALO - (width - 1)
    phases = {}
    for j in range(width):
        q, r = divmod(base + j, SUBLANES)
        phases.setdefault(r, []).append((q, j))

    def slab(s, carry):
        t0 = pl.multiple_of(s * CONV_SLAB, CONV_SLAB)
        for c in range(CH // LANES):
            cs = slice(c * LANES, (c + 1) * LANES)
            acc = jnp.zeros((CONV_SLAB, LANES), F32)
            for r, taps in phases.items():
                q_lo, q_hi = taps[0][0], taps[-1][0]
                win = cp_sc[r, pl.ds(q_lo * SUBLANES + t0, CONV_SLAB + (q_hi - q_lo) * SUBLANES), cs]
                for q, j in taps:
                    off = (q - q_lo) * SUBLANES
                    acc = acc + w_ref[j:j + 1, cs] * win[off:off + CONV_SLAB]
            y_sc[pl.ds(t0, CONV_SLAB), cs] = acc + cb_ref[:, cs]
        return carry

    lax.fori_loop(0, TS // CONV_SLAB, slab, 0)
    y = y_sc[...]
    mu = jnp.mean(y, axis=-1, keepdims=True)
    yc = y - mu
    var = jnp.mean(yc * yc, axis=-1, keepdims=True)
    yn = yc * lax.rsqrt(var + NORM_EPS) * g_ref[...] + b_ref[...]
    o_ref[...] = (yn * jax.nn.sigmoid(yn)).astype(BF16)


def _conv_module(u, conv_w, conv_b, ln_g, ln_b, B, S, TS=256):
    T, CH = u.shape
    width = conv_w.shape[0]
    assert width - 1 <= CONV_HALO
    ns = S // TS
    r = TS // CONV_HALO
    wpad = jnp.zeros((CONV_HALO, CH), F32).at[:width].set(conv_w)
    vec = lambda a: a.reshape(1, CH)
    kern = functools.partial(_conv_kernel, width=width)
    return pl.pallas_call(
        kern,
        grid=(B, ns),
        in_specs=[pl.BlockSpec((TS, CH), lambda b, i: (b * ns + i, 0)),
                  pl.BlockSpec((CONV_HALO, CH), lambda b, i: (jnp.maximum((b * ns + i) * r - 1, 0), 0)),
                  _resident((CONV_HALO, CH)), _resident((1, CH)), _resident((1, CH)), _resident((1, CH))],
        out_specs=pl.BlockSpec((TS, CH), lambda b, i: (b * ns + i, 0)),
        out_shape=jax.ShapeDtypeStruct((T, CH), BF16),
        scratch_shapes=[pltpu.VMEM((SUBLANES, CONV_HALO + TS, CH), F32),
                        pltpu.VMEM((TS, CH), F32)],
        compiler_params=_params("arbitrary", "arbitrary"),
    )(u, u, wpad, vec(conv_b), vec(ln_g), vec(ln_b))


def _out_proj_kernel(x_ref, a_ref, c_ref, wa_ref, wc_ref, o_ref):
    o_ref[...] = x_ref[...] + _dot(a_ref[...], wa_ref[...]) + _dot(c_ref[...], wc_ref[...])


def _out_proj(x, attn, conv, w_out, tm=512):
    T, D = x.shape
    aw, cw = attn.shape[1], conv.shape[1]
    wa = w_out[:aw].astype(BF16)
    wc = w_out[aw:].astype(BF16)
    row = lambda w: pl.BlockSpec((tm, w), lambda i: (i, 0))
    return pl.pallas_call(
        _out_proj_kernel,
        grid=(T // tm,),
        in_specs=[row(D), row(aw), row(cw), _resident(wa.shape), _resident(wc.shape)],
        out_specs=row(D),
        out_shape=jax.ShapeDtypeStruct((T, D), F32),
        compiler_params=_params("arbitrary"),
    )(x, attn, conv, wa, wc)


def _first_max(vals):
    m = vals[0]
    for v in vals[1:]:
        m = jnp.maximum(m, v)
    idx = jnp.full(m.shape, len(vals) - 1, jnp.int32)
    for k in range(len(vals) - 2, -1, -1):
        idx = jnp.where(vals[k] == m, k, idx)
    return m, idx


def _softmax_cols(cols):
    m = cols[0]
    for c in cols[1:]:
        m = jnp.maximum(m, c)
    e = [jnp.exp(c - m) for c in cols]
    s = e[0]
    for c in e[1:]:
        s = s + c
    return [c / s for c in e]


GROUP_ROWS = 8
MOE_CHUNK = 128


def _split3(a):
    hi = a.astype(BF16)
    r = a - hi.astype(F32)
    mid = r.astype(BF16)
    lo = (r - mid.astype(F32)).astype(BF16)
    return hi, mid, lo


def _moe_kernel(x_ref, nw_ref, wrt_ref, brt_ref, wg_ref, wu_ref, wd_ref, o_ref,
                h_sc, hs_sc, p_sc, u_sc, cs_sc, seg_sm, *, n_groups):
    TM, D = x_ref.shape
    i = pl.program_id(0)
    e = pl.program_id(1)
    n_exp = pl.num_programs(1)

    @pl.when(jnp.logical_and(i == 0, e == 0))
    def _():
        s_io = lax.broadcasted_iota(jnp.int32, (TM, TM), 0)
        t_io = lax.broadcasted_iota(jnp.int32, (TM, TM), 1)
        u_sc[...] = jnp.where(s_io < t_io, 1.0, 0.0).astype(BF16)

    @pl.when(e == 0)
    def _():
        h = _rms(x_ref[...], nw_ref[...]).astype(BF16)
        h_sc[...] = h
        lg = _dot_nt(wrt_ref[...], h) + brt_ref[...]
        row = lambda k: lg[k:k + 1, :]
        g_prob = _softmax_cols([row(g) for g in range(n_groups)])
        g_p, g_idx = _first_max(g_prob)
        e_logit = []
        for k in range(EXPERTS_PER_GROUP):
            v = row(GROUP_ROWS + k)
            for g in range(1, n_groups):
                v = jnp.where(g_idx == g, row(GROUP_ROWS + g * EXPERTS_PER_GROUP + k), v)
            e_logit.append(v)
        e_prob = _softmax_cols(e_logit)
        p1, i1 = _first_max(e_prob)
        rest = [jnp.where(i1 == k, -1.0, e_prob[k]) for k in range(EXPERTS_PER_GROUP)]
        p2, i2 = _first_max(rest)
        den = p1 + p2
        base = g_idx * EXPERTS_PER_GROUP
        e_io = lax.broadcasted_iota(jnp.int32, (LANES, TM), 0)
        comb_t = (jnp.where(e_io == base + i1, g_p * (p1 / den), 0.0)
                  + jnp.where(e_io == base + i2, g_p * (p2 / den), 0.0))

        g_io = lax.broadcasted_iota(jnp.int32, (GROUP_ROWS, TM), 0)
        onehot_t = jnp.where(g_io == g_idx, 1.0, 0.0)
        rank = _dot(onehot_t.astype(BF16), u_sc[...])
        start = jnp.int32(0)
        pos = jnp.zeros((1, TM), F32)
        for g in range(n_groups):
            seg_sm[g] = start
            pos = jnp.where(g_idx == g, start.astype(F32) + rank[g:g + 1, :], pos)
            start = start + jnp.sum(onehot_t[g:g + 1, :]).astype(jnp.int32)
        seg_sm[n_groups] = start
        pos = pos.astype(jnp.int32)

        rb = 256
        for r0 in range(0, TM, rb):
            r_io = lax.broadcasted_iota(jnp.int32, (rb, TM), 0) + r0
            p_sc[r0:r0 + rb, :] = jnp.where(r_io == pos, 1.0, 0.0).astype(BF16)
        p = p_sc[...]
        for c0 in range(0, D, 512):
            hs_sc[:, c0:c0 + 512] = _dot(p, h_sc[:, c0:c0 + 512]).astype(BF16)
        cs = jnp.zeros((TM, LANES), F32)
        for part in _split3(comb_t):
            cs = cs + _dot_nt(p, part)
        cs_sc[...] = cs
        o_ref[...] = jnp.zeros(o_ref.shape, F32)

    g = e // EXPERTS_PER_GROUP
    start = seg_sm[g]
    end = seg_sm[g + 1]
    c_lo = start // MOE_CHUNK
    c_hi = jnp.where(end > start, (end + MOE_CHUNK - 1) // MOE_CHUNK, c_lo)

    def expert_rows(c, n_chunks):
        m = n_chunks * MOE_CHUNK
        r0 = pl.multiple_of(c * MOE_CHUNK, MOE_CHUNK)
        rows = hs_sc[pl.ds(r0, m), :]
        lane = lax.broadcasted_iota(jnp.int32, (m, LANES), 1)
        w = jnp.sum(jnp.where(lane == e, cs_sc[pl.ds(r0, m), :], 0.0), axis=-1, keepdims=True)
        a = jax.nn.silu(_dot(rows, wg_ref[0])) * _dot(rows, wu_ref[0]) * w
        o_ref[pl.ds(r0, m), :] += _dot(a.astype(BF16), wd_ref[0])

    def triple(k, carry):
        expert_rows(c_lo + 3 * k, 3)
        return carry

    n_chunks = c_hi - c_lo
    lax.fori_loop(0, n_chunks // 3, triple, 0)
    for rest in (1, 2):
        @pl.when(n_chunks % 3 == rest)
        def _():
            expert_rows(c_hi - rest, rest)

    @pl.when(e == n_exp - 1)
    def _():
        p = p_sc[...]
        for c0 in range(0, D, 512):
            ys = o_ref[:, c0:c0 + 512].astype(BF16)
            y = lax.dot_general(p, ys, (((0,), (0,)), ((), ())), preferred_element_type=F32)
            o_ref[:, c0:c0 + 512] = x_ref[:, c0:c0 + 512] + y


def _moe(x, norm_w, wg, bg, we, be, w_gate, w_up, w_down, layer, tm=1024):
    T, D = x.shape
    n_groups, n_exp = wg.shape[1], we.shape[1]
    ff = w_gate.shape[3]
    assert n_groups <= GROUP_ROWS and n_exp == n_groups * EXPERTS_PER_GROUP and T % tm == 0
    rows = GROUP_ROWS + n_exp
    wrt = jnp.zeros((rows, D), F32).at[:n_groups].set(wg.T).at[GROUP_ROWS:].set(we.T).astype(BF16)
    brt = jnp.zeros((rows, 1), F32).at[:n_groups, 0].set(bg).at[GROUP_ROWS:, 0].set(be)
    tile = pl.BlockSpec((tm, D), lambda i, e: (i, 0))
    return pl.pallas_call(
        functools.partial(_moe_kernel, n_groups=n_groups),
        grid=(T // tm, n_exp),
        in_specs=[pl.BlockSpec((tm, D), lambda i, e: (i, 0), pipeline_mode=pl.Buffered(1)),
                  _resident((1, D)), _resident((rows, D)), _resident((rows, 1)),
                  pl.BlockSpec((None, 1, D, ff), lambda i, e: (layer, e, 0, 0)),
                  pl.BlockSpec((None, 1, D, ff), lambda i, e: (layer, e, 0, 0)),
                  pl.BlockSpec((None, 1, ff, D), lambda i, e: (layer, e, 0, 0))],
        out_specs=tile,
        out_shape=jax.ShapeDtypeStruct((T, D), F32),
        scratch_shapes=[pltpu.VMEM((tm, D), BF16),
                        pltpu.VMEM((tm, D), BF16),
                        pltpu.VMEM((tm, tm), BF16),
                        pltpu.VMEM((tm, tm), BF16),
                        pltpu.VMEM((tm, LANES), F32),
                        pltpu.SMEM((GROUP_ROWS,), jnp.int32)],
        compiler_params=_params("arbitrary", "arbitrary"),
    )(x, norm_w.reshape(1, D), wrt, brt, w_gate, w_up, w_down)


POOL_HALO = 16


def _pool_kernel(x_ref, halo_ref, nw_ref, pw_ref, ps_ref, o_ref, hb, pa, pb):
    TS, D = x_ref.shape
    n_groups, pc, _ = pw_ref.shape
    i = pl.program_id(1)
    nw = nw_ref[...]
    top = SUBLANES + POOL_HALO
    n_rows = top + TS

    @pl.when(jnp.logical_and(pl.program_id(0) == 0, i == 0))
    def _():
        hb[0:SUBLANES] = jnp.zeros((SUBLANES, D), F32)
        pa[0:SUBLANES] = jnp.zeros((SUBLANES, pc), F32)
        pb[0:SUBLANES] = jnp.zeros((SUBLANES, pc), F32)

    hb[SUBLANES:top] = jnp.where(i > 0, _rms(halo_ref[...], nw), 0.0)
    hb[top:] = _rms(x_ref[...], nw)
    t = i * TS + lax.broadcasted_iota(jnp.int32, (TS, 1), 0)
    for g, w in enumerate(POOL_WINDOWS):
        cs = slice(g * pc, (g + 1) * pc)
        cur = hb[top:, cs]
        src, cols, step, dst = hb, cs, 1, pa
        while step < w:
            dst[SUBLANES:n_rows, :] = src[SUBLANES:n_rows, cols] + src[SUBLANES - step:n_rows - step, cols]
            src, cols, step, dst = dst, slice(None), 2 * step, (pb if dst is pa else pa)
        s = src[top:n_rows, cols]
        count = jnp.minimum(t + 1, w).astype(F32)
        d = s / count - cur
        mixed = _dot(d.astype(BF16), pw_ref[g])
        o_ref[:, cs] = x_ref[:, cs] + ps_ref[:, cs] * mixed


def _pool_layer(x, norm_w, pool_w, pool_scale, B, S, TS=256):
    T, D = x.shape
    assert len(POOL_WINDOWS) == pool_w.shape[0] and max(POOL_WINDOWS) - 1 <= POOL_HALO
    ns = S // TS
    r = TS // POOL_HALO
    pw = pool_w.astype(BF16)
    return pl.pallas_call(
        _pool_kernel,
        grid=(B, ns),
        in_specs=[pl.BlockSpec((TS, D), lambda b, i: (b * ns + i, 0)),
                  pl.BlockSpec((POOL_HALO, D), lambda b, i: (jnp.maximum((b * ns + i) * r - 1, 0), 0)),
                  _resident((1, D)), _resident(pw.shape), _resident((1, D))],
        out_specs=pl.BlockSpec((TS, D), lambda b, i: (b * ns + i, 0)),
        out_shape=jax.ShapeDtypeStruct((T, D), F32),
        scratch_shapes=[pltpu.VMEM((SUBLANES + POOL_HALO + TS, D), F32),
                        pltpu.VMEM((SUBLANES + POOL_HALO + TS, D // len(POOL_WINDOWS)), F32),
                        pltpu.VMEM((SUBLANES + POOL_HALO + TS, D // len(POOL_WINDOWS)), F32)],
        compiler_params=_params("arbitrary", "arbitrary"),
    )(x, x, norm_w.reshape(1, D), pw, pool_scale.reshape(1, D))


def _chunk(S):
    return 256 if S % 256 == 0 else 128


def kernel(x, rel_bias, mix_norm_e, w_in_e, q_norm_e, k_norm_e, conv_w_e, conv_b_e, conv_ln_g_e, conv_ln_b_e,
           w_out_e, mix_norm_o, pool_w_o, pool_scale_o, ffn_norm, router_group_w, router_group_b,
           router_expert_w, router_expert_b, w_gate, w_up, w_down):
    B, S, D = x.shape
    T = B * S
    depth = ffn_norm.shape[0]
    n_heads = rel_bias.shape[1]
    idx_heads = (w_in_e.shape[2] - n_heads * HEAD_DIM - 2 * HEAD_DIM - IDX_DIM - 2 * conv_w_e.shape[2]) \
        // (IDX_DIM + 1)
    k_top = min(INDEX_TOPK, S // 4)
    C = _chunk(S)
    xf = x.reshape(T, D)
    bias_tiles, bias_stat = _rel_bias_tiles(rel_bias, C)
    expert_w = [w.astype(BF16) for w in (w_gate, w_up, w_down)]
    for l in range(depth):
        i = l // 2
        if l % 2 == 0:
            q, k, v, iq, ik, iw, u = _in_proj(xf, mix_norm_e[i], w_in_e[i], q_norm_e[i], k_norm_e[i],
                                              n_heads, idx_heads, conv_w_e.shape[2])
            attn = _attention(q, k, v, iq, ik, iw, bias_tiles, bias_stat, B, S, k_top, C)
            conv = _conv_module(u, conv_w_e[i], conv_b_e[i], conv_ln_g_e[i], conv_ln_b_e[i], B, S)
            xf = _out_proj(xf, attn, conv, w_out_e[i])
        else:
            xf = _pool_layer(xf, mix_norm_o[i], pool_w_o[i], pool_scale_o[i], B, S)
        xf = _moe(xf, ffn_norm[l], router_group_w[l], router_group_b[l], router_expert_w[l], router_expert_b[l],
                  *expert_w, layer=l)
    return xf.reshape(B, S, D)
```

```python
import functools
import math

import jax
import jax.numpy as jnp
from jax import lax
from jax.experimental import pallas as pl
from jax.experimental.pallas import tpu as pltpu

F32 = jnp.float32
BF16 = jnp.bfloat16

NORM_EPS = 1e-6
HEAD_DIM = 128
IDX_DIM = 64
INDEX_TOPK = 256
REL_BUCKETS = 32
REL_MAX_DIST = 128
POOL_WINDOWS = (2, 4, 8, 16)
EXPERTS_PER_GROUP = 4
LANES = 128
VMEM_LIMIT = 56 * 1024 * 1024
NEG = -1e30
INT_MIN = -(2 ** 31)


def _dot(a, b):
    return jnp.dot(a, b, preferred_element_type=F32)


def _dot_nt(a, b):
    return lax.dot_general(a, b, (((1,), (1,)), ((), ())), preferred_element_type=F32)


def _rms(x, w):
    return x * lax.rsqrt(jnp.mean(x * x, axis=-1, keepdims=True) + NORM_EPS) * w


def _params(*sem):
    return pltpu.CompilerParams(dimension_semantics=sem, vmem_limit_bytes=VMEM_LIMIT)


def _resident(shape):
    nd = len(shape)
    return pl.BlockSpec(shape, lambda *_: (0,) * nd, pipeline_mode=pl.Buffered(1))


def _in_proj_kernel(x_ref, nw_ref, qn_ref, kn_ref, wq_ref, wkv_ref, wiq_ref, wikw_ref, wa_ref, wg_ref,
                    q_ref, k_ref, v_ref, iq_ref, ik_ref, iw_ref, u_ref, *, q_scale, iw_scale):
    h = _rms(x_ref[...], nw_ref[...]).astype(BF16)
    n_pairs = q_ref.shape[0] // 2
    qn = qn_ref[...] * q_scale
    for c in range(n_pairs):
        qq = _dot(h, wq_ref[:, c * 256:(c + 1) * 256])
        for s in range(2):
            qh = qq[:, s * HEAD_DIM:(s + 1) * HEAD_DIM]
            q_ref[2 * c + s] = _rms(qh, qn).astype(BF16)
    kv = _dot(h, wkv_ref[...])
    k_ref[...] = _rms(kv[:, :HEAD_DIM], kn_ref[...]).astype(BF16)
    v_ref[...] = kv[:, HEAD_DIM:].astype(BF16)
    for c in range(iq_ref.shape[0] // 2):
        r = _dot(h, wiq_ref[:, c * 256:(c + 1) * 256])
        iq_ref[2 * c] = r[:, :LANES].astype(BF16)
        iq_ref[2 * c + 1] = r[:, LANES:].astype(BF16)
    r = _dot(h, wikw_ref[...])
    ik_ref[...] = r[:, :LANES].astype(BF16)
    iw_ref[...] = r[:, LANES:] * iw_scale
    for c in range(u_ref.shape[1] // 256):
        cs = slice(c * 256, (c + 1) * 256)
        a = _dot(h, wa_ref[:, cs])
        g = _dot(h, wg_ref[:, cs])
        u_ref[:, cs] = (a * jax.nn.sigmoid(g)).astype(BF16)


def _split_w_in_kernel(w_ref, wq_ref, wkv_ref, wiq_ref, wikw_ref, wa_ref, wg_ref, *, idx_heads):
    o = 0
    for ref in (wq_ref, wkv_ref, wiq_ref):
        n = ref.shape[1]
        ref[...] = w_ref[:, o:o + n].astype(BF16)
        o += n
    wik = w_ref[:, o:o + IDX_DIM]
    wiw = w_ref[:, o + IDX_DIM:o + IDX_DIM + idx_heads]
    o += IDX_DIM + idx_heads
    pad = jnp.zeros((wik.shape[0], LANES - idx_heads), F32)
    wikw_ref[...] = jnp.concatenate([wik, wik, wiw, pad], axis=1).astype(BF16)
    for ref in (wa_ref, wg_ref):
        n = ref.shape[1]
        ref[...] = w_ref[:, o:o + n].astype(BF16)
        o += n


def _split_w_in(w_in, n_heads, idx_heads, conv_ch, tk=256):
    D, n_in = w_in.shape
    widths = (n_heads * HEAD_DIM, 2 * HEAD_DIM, idx_heads * IDX_DIM, 2 * LANES, conv_ch, conv_ch)
    assert n_in == sum(widths) - 2 * LANES + IDX_DIM + idx_heads and 2 * IDX_DIM == LANES and D % tk == 0
    return pl.pallas_call(
        functools.partial(_split_w_in_kernel, idx_heads=idx_heads),
        grid=(D // tk,),
        in_specs=[pl.BlockSpec((tk, n_in), lambda i: (i, 0))],
        out_specs=[pl.BlockSpec((tk, w), lambda i: (i, 0)) for w in widths],
        out_shape=[jax.ShapeDtypeStruct((D, w), BF16) for w in widths],
        compiler_params=_params("arbitrary"),
    )(w_in)


def _in_proj(x, norm_w, w_in, q_norm, k_norm, n_heads, idx_heads, conv_ch, tm=256):
    T, D = x.shape
    iq_w = idx_heads * IDX_DIM
    ws = _split_w_in(w_in, n_heads, idx_heads, conv_ch)
    row = lambda w: pl.BlockSpec((tm, w), lambda i: (i, 0))
    heads = lambda n: pl.BlockSpec((n, tm, LANES), lambda i: (0, i, 0))
    kern = functools.partial(_in_proj_kernel, q_scale=HEAD_DIM ** -0.5,
                             iw_scale=(idx_heads ** -0.5) * (IDX_DIM ** -0.5))
    return pl.pallas_call(
        kern,
        grid=(T // tm,),
        in_specs=[row(D), _resident((1, D)), _resident((1, HEAD_DIM)), _resident((1, HEAD_DIM))]
                 + [_resident(w.shape) for w in ws],
        out_specs=[heads(n_heads), row(HEAD_DIM), row(HEAD_DIM), heads(iq_w // LANES), row(LANES), row(LANES),
                   row(conv_ch)],
        out_shape=[jax.ShapeDtypeStruct((n_heads, T, HEAD_DIM), BF16),
                   jax.ShapeDtypeStruct((T, HEAD_DIM), BF16),
                   jax.ShapeDtypeStruct((T, HEAD_DIM), BF16),
                   jax.ShapeDtypeStruct((iq_w // LANES, T, LANES), BF16),
                   jax.ShapeDtypeStruct((T, LANES), BF16),
                   jax.ShapeDtypeStruct((T, LANES), F32),
                   jax.ShapeDtypeStruct((T, conv_ch), BF16)],
        compiler_params=_params("arbitrary"),
    )(x, norm_w.reshape(1, D), q_norm.reshape(1, HEAD_DIM), k_norm.reshape(1, HEAD_DIM), *ws)


def _rel_bias_kernel(rb_ref, o_ref, stat_ref):
    _, n_heads, C, _ = o_ref.shape
    tau = lax.broadcasted_iota(jnp.int32, (C, C), 0)
    sig = lax.broadcasted_iota(jnp.int32, (C, C), 1)
    max_exact = REL_BUCKETS // 2
    for kind in range(2):
        d = tau - sig + kind * C
        n = jnp.maximum(d, 0)
        nf = jnp.maximum(n, 1).astype(F32)
        large = max_exact + (jnp.log(nf / max_exact) / math.log(REL_MAX_DIST / max_exact)
                             * (REL_BUCKETS - max_exact)).astype(jnp.int32)
        large = jnp.minimum(large, REL_BUCKETS - 1)
        bucket = jnp.where(n < max_exact, n, large)
        for h in range(n_heads):
            b = jnp.zeros((C, C), F32)
            for bk in range(REL_BUCKETS):
                b = jnp.where(bucket == bk, rb_ref[bk, h], b)
            b = b - rb_ref[REL_BUCKETS - 1, h]
            if kind == 0:
                b = jnp.where(d < 0, NEG, b)
            o_ref[kind, h] = b
    for h in range(n_heads):
        hi = rb_ref[0, h]
        lo = rb_ref[0, h]
        for bk in range(1, REL_BUCKETS):
            hi = jnp.maximum(hi, rb_ref[bk, h])
            lo = jnp.minimum(lo, rb_ref[bk, h])
        stat_ref[0, h] = hi - rb_ref[REL_BUCKETS - 1, h]
        stat_ref[1, h] = lo - rb_ref[REL_BUCKETS - 1, h]


def _rel_bias_tiles(rel_bias, C):
    n_heads = rel_bias.shape[1]
    assert C >= REL_MAX_DIST
    return pl.pallas_call(
        _rel_bias_kernel,
        in_specs=[pl.BlockSpec(memory_space=pltpu.SMEM)],
        out_specs=[pl.BlockSpec(memory_space=pltpu.VMEM), pl.BlockSpec(memory_space=pltpu.SMEM)],
        out_shape=[jax.ShapeDtypeStruct((2, n_heads, C, C), F32), jax.ShapeDtypeStruct((2, n_heads), F32)],
        compiler_params=pltpu.CompilerParams(vmem_limit_bytes=VMEM_LIMIT),
    )(rel_bias)


SHIFT_SPAN_LIMIT = 60.0


TILE_GROUP = 4


def _grouped_loop(n, fn, group):
    def body(k, carry):
        for u in range(group):
            fn(group * k + u)
        return carry

    lax.fori_loop(0, n // group, body, 0)
    base = (n // group) * group
    size = group // 2
    while size >= 1:
        take = ((n - base) & size) != 0

        @pl.when(take)
        def _(base=base, size=size):
            for u in range(size):
                fn(base + u)
        base = base + jnp.where(take, size, 0)
        size //= 2


def _order_key(x):
    bits = pltpu.bitcast(x, jnp.int32)
    return bits ^ ((bits >> 31) & 0x7FFFFFFF)


def _row_to_col(row):
    C = row.shape[1]
    halves = []
    for part in (row >> 16, row & 0xFFFF):
        halves.append(jnp.broadcast_to(part.astype(F32), (LANES, C)).T[:, 0:1].astype(jnp.int32))
    return (halves[0] << 16) | halves[1]


def _attn_kernel(bstat_ref, q_ref, k_ref, v_ref, iq_ref, ik_ref, iw_ref, bias_ref, o_ref,
                 key_sc, keyt_sc, keyt16_sc, iqm_sc, vx_sc, kmax_sc, shift_sc, acc_sc, *, k_top):
    n_heads, C, _ = q_ref.shape
    idx_heads = iqm_sc.shape[0]
    idx_bits = max(1, (key_sc.shape[0] * C - 1).bit_length())
    qi = pl.program_id(1)
    nkv = qi + 1

    @pl.when(qi == 0)
    def _():
        vx_sc[:, :HEAD_DIM] = v_ref[...]
        vx_sc[:, HEAD_DIM:] = jnp.ones((vx_sc.shape[0], HEAD_DIM), BF16)
        kf = k_ref[...].astype(F32)
        k2 = jnp.sum(kf * kf, axis=-1, keepdims=True)
        kmax_sc[...] = jnp.broadcast_to(jnp.sqrt(jnp.max(k2, axis=0, keepdims=True)), kmax_sc.shape)

    lane = lax.broadcasted_iota(jnp.int32, (C, LANES), 1)
    for p in range(idx_heads // 2):
        qp = iq_ref[p].astype(F32)
        iqm_sc[2 * p] = jnp.where(lane < IDX_DIM, qp, 0.0).astype(BF16)
        iqm_sc[2 * p + 1] = jnp.where(lane >= IDX_DIM, qp, 0.0).astype(BF16)
    iw = iw_ref[...]
    tau = lax.broadcasted_iota(jnp.int32, (C, C), 0)
    sig = lax.broadcasted_iota(jnp.int32, (C, C), 1)

    def score_tile(j):
        off = pl.multiple_of(j * C, C)
        ikc = ik_ref[pl.ds(off, C), :]
        acc = jnp.zeros((C, C), F32)
        for hh in range(idx_heads):
            s = _dot_nt(iqm_sc[hh], ikc)
            acc = acc + jnp.maximum(s, 0.0) * iw[:, hh:hh + 1]
        acc = jnp.where(jnp.logical_and(j == qi, sig > tau), -jnp.inf, acc)
        key_sc[j] = _order_key(acc)
        kt = _order_key(acc.T)
        keyt_sc[j] = kt
        keyt16_sc[j] = (kt >> 16).astype(jnp.int16)

    _grouped_loop(nkv, score_tile, TILE_GROUP)

    SUB = 32

    def count(hit):
        def body(j, cnt):
            for r in range(C // SUB):
                cnt = cnt + hit(j, r)
            return cnt
        cnt = lax.fori_loop(0, nkv, body, jnp.zeros((SUB, C), F32))
        return jnp.sum(cnt, axis=0, keepdims=True)

    def keyt(j, r):
        return keyt_sc[j, pl.ds(r * SUB, SUB), :]

    def count16(cand16):
        one = jnp.ones((SUB, C), jnp.int16)
        zero = jnp.zeros((SUB, C), jnp.int16)

        def body(j, cnt):
            for r in range(C // SUB):
                cnt = cnt + jnp.where(keyt16_sc[j, pl.ds(r * SUB, SUB), :] >= cand16, one, zero)
            return cnt
        cnt = lax.fori_loop(0, nkv, body, zero)
        return jnp.sum(cnt.astype(jnp.int32).astype(F32), axis=0, keepdims=True)

    def bit16_body(b, carry):
        res, n_res = carry
        cand = res ^ lax.shift_left(jnp.int32(1), 31 - b)
        tot = count16((cand >> 16).astype(jnp.int16))
        ok = tot >= k_top
        return jnp.where(ok, cand, res), jnp.where(ok, tot, n_res)

    searched = nkv * C > k_top
    carry = (jnp.full((1, C), INT_MIN, jnp.int32), jnp.zeros((1, C), F32))
    top_row, n_top = lax.fori_loop(0, jnp.where(searched, 16, 0), bit16_body, carry)

    top16 = top_row >> 16

    def low_body(j, c):
        kt = keyt_sc[j]
        hi = kt >> 16
        lo = (kt & 0xFFFF) - 32768
        lo = jnp.where(hi == top16, lo, jnp.where(hi > top16, 32767, -32768))
        keyt16_sc[j] = lo.astype(jnp.int16)
        return c

    lax.fori_loop(0, jnp.where(searched, nkv, 0), low_body, 0)

    def bit16_low_body(b, carry):
        res, n_res = carry
        cand = res ^ lax.shift_left(jnp.int32(1), 31 - b)
        tot = count16(((cand & 0xFFFF) - 32768).astype(jnp.int16))
        ok = tot >= k_top
        return jnp.where(ok, cand, res), jnp.where(ok, tot, n_res)

    thr_row, n_thr = lax.fori_loop(16, jnp.where(searched, 32, 16), bit16_low_body, (top_row, n_top))
    thr = _row_to_col(thr_row)

    @pl.when(jnp.max(n_thr) > k_top)
    def _():
        need = k_top - count(lambda j, r: jnp.where(keyt(j, r) > thr_row, 1.0, 0.0))
        s_sub = lax.broadcasted_iota(jnp.int32, (SUB, C), 0)

        def idx_body(b, last):
            cand = last | lax.shift_left(jnp.int32(1), idx_bits - 1 - b)
            below = count(lambda j, r: jnp.where(
                keyt(j, r) == thr_row, jnp.where(j * C + r * SUB + s_sub < cand, 1.0, 0.0), 0.0))
            return jnp.where(below < need, cand, last)

        last = _row_to_col(lax.fori_loop(0, idx_bits, idx_body, jnp.zeros((1, C), jnp.int32)))

        def drop_body(j, carry):
            kk = key_sc[j]
            key_sc[j] = jnp.where(kk == thr, jnp.where(j * C + sig > last, kk - 1, kk), kk)
            return carry

        lax.fori_loop(0, nkv, drop_body, 0)

    kmax = kmax_sc[0:1, 0:1] * 1.001
    worst = jnp.zeros((C, 1), F32)
    for h in range(n_heads):
        qf = q_ref[h].astype(F32)
        bound = jnp.sqrt(jnp.sum(qf * qf, axis=-1, keepdims=True)) * kmax
        shift_sc[h] = jnp.broadcast_to(bound + bstat_ref[0, h], (C, LANES))
        worst = jnp.maximum(worst, 2.0 * bound + (bstat_ref[0, h] - bstat_ref[1, h]))
    loose = jnp.max(worst) > SHIFT_SPAN_LIMIT

    def logits(j, h, kind):
        off = pl.multiple_of(j * C, C)
        lg = _dot_nt(q_ref[h], k_ref[pl.ds(off, C), :])
        return lg if kind is None else lg + bias_ref[kind, h]

    def near_tiles(fn):
        @pl.when(qi > 0)
        def _():
            fn(qi - 1, 1)
        fn(qi, 0)

    def far_tiles(fn, group=TILE_GROUP):
        _grouped_loop(jnp.maximum(qi - 1, 0), lambda j: fn(j, None), group)

    @pl.when(loose)
    def _():
        for h in range(n_heads):
            shift_sc[h] = jnp.full((C, LANES), NEG, F32)

        def max_tile(j, kind):
            sel = key_sc[j] >= thr
            for h in range(n_heads):
                lg = jnp.where(sel, logits(j, h, kind), NEG)
                m = jnp.max(lg, axis=-1, keepdims=True)
                shift_sc[h] = jnp.maximum(shift_sc[h], jnp.broadcast_to(m, (C, LANES)))

        far_tiles(max_tile, group=1)
        near_tiles(max_tile)

    acc_sc[...] = jnp.zeros(acc_sc.shape, F32)

    def attn_tile(j, kind):
        off = pl.multiple_of(j * C, C)
        vx = vx_sc[pl.ds(off, C), :]
        sel = key_sc[j] >= thr
        for h in range(n_heads):
            sh = jnp.concatenate([shift_sc[h]] * (C // LANES), axis=1)
            p = jnp.where(sel, jnp.exp(logits(j, h, kind) - sh), 0.0)
            acc_sc[h] += _dot(p.astype(BF16), vx)

    far_tiles(attn_tile)
    near_tiles(attn_tile)
    for h in range(n_heads):
        a = acc_sc[h]
        o_ref[:, h * HEAD_DIM:(h + 1) * HEAD_DIM] = (a[:, :HEAD_DIM] / a[:, HEAD_DIM:]).astype(BF16)


def _attention(q, k, v, iq, ik, iw, bias_tiles, bias_stat, B, S, k_top, C):
    n_heads, T, _ = q.shape
    n_pairs = iq.shape[0]
    nq = S // C
    kern = functools.partial(_attn_kernel, k_top=k_top)
    heads = lambda n: pl.BlockSpec((n, C, LANES), lambda b, i: (0, b * nq + i, 0))
    seq = pl.BlockSpec((S, LANES), lambda b, i: (b, 0))
    return pl.pallas_call(
        kern,
        grid=(B, nq),
        in_specs=[pl.BlockSpec(memory_space=pltpu.SMEM),
                  heads(n_heads), seq, seq, heads(n_pairs), seq,
                  pl.BlockSpec((C, LANES), lambda b, i: (b * nq + i, 0)),
                  _resident(bias_tiles.shape)],
        out_specs=pl.BlockSpec((C, n_heads * HEAD_DIM), lambda b, i: (b * nq + i, 0)),
        out_shape=jax.ShapeDtypeStruct((T, n_heads * HEAD_DIM), BF16),
        scratch_shapes=[pltpu.VMEM((nq, C, C), jnp.int32),
                        pltpu.VMEM((nq, C, C), jnp.int32),
                        pltpu.VMEM((nq, C, C), jnp.int16),
                        pltpu.VMEM((2 * n_pairs, C, LANES), BF16),
                        pltpu.VMEM((S, 2 * HEAD_DIM), BF16),
                        pltpu.VMEM((8, LANES), F32),
                        pltpu.VMEM((n_heads, C, LANES), F32),
                        pltpu.VMEM((n_heads, C, 2 * HEAD_DIM), F32)],
        compiler_params=_params("arbitrary", "arbitrary"),
    )(bias_stat, q, k, v, iq, ik, iw, bias_tiles)


CONV_HALO = 32


SUBLANES = 8
CONV_SLAB = 64


def _conv_kernel(u_ref, halo_ref, w_ref, cb_ref, g_ref, b_ref, o_ref, cp_sc, y_sc, *, width):
    TS, CH = u_ref.shape
    n_rows = CONV_HALO + TS
    i = pl.program_id(1)
    halo = jnp.where(i > 0, halo_ref[...].astype(F32), 0.0)
    pad = jnp.zeros((SUBLANES, LANES), F32)
    for c in range(CH // LANES):
        cs = slice(c * LANES, (c + 1) * LANES)
        col = jnp.concatenate([halo[:, cs], u_ref[:, cs].astype(F32), pad], axis=0)
        for r in range(SUBLANES):
            cp_sc[r, :, cs] = col[r:r + n_rows]

    base = CONV_HALO - (width - 1)
    phases = {}
    for j in range(width):
        q, r = divmod(base + j, SUBLANES)
        phases.setdefault(r, []).append((q, j))

    def slab(s, carry):
        t0 = pl.multiple_of(s * CONV_SLAB, CONV_SLAB)
        for c in range(CH // LANES):
            cs = slice(c * LANES, (c + 1) * LANES)
            acc = jnp.zeros((CONV_SLAB, LANES), F32)
            for r, taps in phases.items():
                q_lo, q_hi = taps[0][0], taps[-1][0]
                win = cp_sc[r, pl.ds(q_lo * SUBLANES + t0, CONV_SLAB + (q_hi - q_lo) * SUBLANES), cs]
                for q, j in taps:
                    off = (q - q_lo) * SUBLANES
                    acc = acc + w_ref[j:j + 1, cs] * win[off:off + CONV_SLAB]
            y_sc[pl.ds(t0, CONV_SLAB), cs] = acc + cb_ref[:, cs]
        return carry

    lax.fori_loop(0, TS // CONV_SLAB, slab, 0)
    y = y_sc[...]
    mu = jnp.mean(y, axis=-1, keepdims=True)
    yc = y - mu
    var = jnp.mean(yc * yc, axis=-1, keepdims=True)
    yn = yc * lax.rsqrt(var + NORM_EPS) * g_ref[...] + b_ref[...]
    o_ref[...] = (yn * jax.nn.sigmoid(yn)).astype(BF16)


def _conv_module(u, conv_w, conv_b, ln_g, ln_b, B, S, TS=256):
    T, CH = u.shape
    width = conv_w.shape[0]
    assert width - 1 <= CONV_HALO
    ns = S // TS
    r = TS // CONV_HALO
    wpad = jnp.zeros((CONV_HALO, CH), F32).at[:width].set(conv_w)
    vec = lambda a: a.reshape(1, CH)
    kern = functools.partial(_conv_kernel, width=width)
    return pl.pallas_call(
        kern,
        grid=(B, ns),
        in_specs=[pl.BlockSpec((TS, CH), lambda b, i: (b * ns + i, 0)),
                  pl.BlockSpec((CONV_HALO, CH), lambda b, i: (jnp.maximum((b * ns + i) * r - 1, 0), 0)),
                  _resident((CONV_HALO, CH)), _resident((1, CH)), _resident((1, CH)), _resident((1, CH))],
        out_specs=pl.BlockSpec((TS, CH), lambda b, i: (b * ns + i, 0)),
        out_shape=jax.ShapeDtypeStruct((T, CH), BF16),
        scratch_shapes=[pltpu.VMEM((SUBLANES, CONV_HALO + TS, CH), F32),
                        pltpu.VMEM((TS, CH), F32)],
        compiler_params=_params("arbitrary", "arbitrary"),
    )(u, u, wpad, vec(conv_b), vec(ln_g), vec(ln_b))


def _out_proj_kernel(x_ref, a_ref, c_ref, wa_ref, wc_ref, o_ref):
    o_ref[...] = x_ref[...] + _dot(a_ref[...], wa_ref[...]) + _dot(c_ref[...], wc_ref[...])


def _out_proj(x, attn, conv, w_out, tm=512):
    T, D = x.shape
    aw, cw = attn.shape[1], conv.shape[1]
    wa = w_out[:aw].astype(BF16)
    wc = w_out[aw:].astype(BF16)
    row = lambda w: pl.BlockSpec((tm, w), lambda i: (i, 0))
    return pl.pallas_call(
        _out_proj_kernel,
        grid=(T // tm,),
        in_specs=[row(D), row(aw), row(cw), _resident(wa.shape), _resident(wc.shape)],
        out_specs=row(D),
        out_shape=jax.ShapeDtypeStruct((T, D), F32),
        compiler_params=_params("arbitrary"),
    )(x, attn, conv, wa, wc)


def _first_max(vals):
    m = vals[0]
    for v in vals[1:]:
        m = jnp.maximum(m, v)
    idx = jnp.full(m.shape, len(vals) - 1, jnp.int32)
    for k in range(len(vals) - 2, -1, -1):
        idx = jnp.where(vals[k] == m, k, idx)
    return m, idx


def _softmax_cols(cols):
    m = cols[0]
    for c in cols[1:]:
        m = jnp.maximum(m, c)
    e = [jnp.exp(c - m) for c in cols]
    s = e[0]
    for c in e[1:]:
        s = s + c
    return [c / s for c in e]


GROUP_ROWS = 8
MOE_CHUNK = 128


def _split3(a):
    hi = a.astype(BF16)
    r = a - hi.astype(F32)
    mid = r.astype(BF16)
    lo = (r - mid.astype(F32)).astype(BF16)
    return hi, mid, lo


def _moe_kernel(x_ref, nw_ref, wrt_ref, brt_ref, wg_ref, wu_ref, wd_ref, o_ref,
                h_sc, hs_sc, p_sc, u_sc, cs_sc, seg_sm, *, n_groups):
    TM, D = x_ref.shape
    i = pl.program_id(0)
    e = pl.program_id(1)
    n_exp = pl.num_programs(1)

    @pl.when(jnp.logical_and(i == 0, e == 0))
    def _():
        s_io = lax.broadcasted_iota(jnp.int32, (TM, TM), 0)
        t_io = lax.broadcasted_iota(jnp.int32, (TM, TM), 1)
        u_sc[...] = jnp.where(s_io < t_io, 1.0, 0.0).astype(BF16)

    @pl.when(e == 0)
    def _():
        h = _rms(x_ref[...], nw_ref[...]).astype(BF16)
        h_sc[...] = h
        lg = _dot_nt(wrt_ref[...], h) + brt_ref[...]
        row = lambda k: lg[k:k + 1, :]
        g_prob = _softmax_cols([row(g) for g in range(n_groups)])
        g_p, g_idx = _first_max(g_prob)
        e_logit = []
        for k in range(EXPERTS_PER_GROUP):
            v = row(GROUP_ROWS + k)
            for g in range(1, n_groups):
                v = jnp.where(g_idx == g, row(GROUP_ROWS + g * EXPERTS_PER_GROUP + k), v)
            e_logit.append(v)
        e_prob = _softmax_cols(e_logit)
        p1, i1 = _first_max(e_prob)
        rest = [jnp.where(i1 == k, -1.0, e_prob[k]) for k in range(EXPERTS_PER_GROUP)]
        p2, i2 = _first_max(rest)
        den = p1 + p2
        base = g_idx * EXPERTS_PER_GROUP
        e_io = lax.broadcasted_iota(jnp.int32, (LANES, TM), 0)
        comb_t = (jnp.where(e_io == base + i1, g_p * (p1 / den), 0.0)
                  + jnp.where(e_io == base + i2, g_p * (p2 / den), 0.0))

        g_io = lax.broadcasted_iota(jnp.int32, (GROUP_ROWS, TM), 0)
        onehot_t = jnp.where(g_io == g_idx, 1.0, 0.0)
        rank = _dot(onehot_t.astype(BF16), u_sc[...])
        start = jnp.int32(0)
        pos = jnp.zeros((1, TM), F32)
        for g in range(n_groups):
            seg_sm[g] = start
            pos = jnp.where(g_idx == g, start.astype(F32) + rank[g:g + 1, :], pos)
            start = start + jnp.sum(onehot_t[g:g + 1, :]).astype(jnp.int32)
        seg_sm[n_groups] = start
        pos = pos.astype(jnp.int32)

        rb = 256
        for r0 in range(0, TM, rb):
            r_io = lax.broadcasted_iota(jnp.int32, (rb, TM), 0) + r0
            p_sc[r0:r0 + rb, :] = jnp.where(r_io == pos, 1.0, 0.0).astype(BF16)
        p = p_sc[...]
        for c0 in range(0, D, 512):
            hs_sc[:, c0:c0 + 512] = _dot(p, h_sc[:, c0:c0 + 512]).astype(BF16)
        cs = jnp.zeros((TM, LANES), F32)
        for part in _split3(comb_t):
            cs = cs + _dot_nt(p, part)
        cs_sc[...] = cs
        o_ref[...] = jnp.zeros(o_ref.shape, F32)

    g = e // EXPERTS_PER_GROUP
    start = seg_sm[g]
    end = seg_sm[g + 1]
    c_lo = start // MOE_CHUNK
    c_hi = jnp.where(end > start, (end + MOE_CHUNK - 1) // MOE_CHUNK, c_lo)

    def expert_rows(c, n_chunks):
        m = n_chunks * MOE_CHUNK
        r0 = pl.multiple_of(c * MOE_CHUNK, MOE_CHUNK)
        rows = hs_sc[pl.ds(r0, m), :]
        lane = lax.broadcasted_iota(jnp.int32, (m, LANES), 1)
        w = jnp.sum(jnp.where(lane == e, cs_sc[pl.ds(r0, m), :], 0.0), axis=-1, keepdims=True)
        a = jax.nn.silu(_dot(rows, wg_ref[0])) * _dot(rows, wu_ref[0]) * w
        o_ref[pl.ds(r0, m), :] += _dot(a.astype(BF16), wd_ref[0])

    def triple(k, carry):
        expert_rows(c_lo + 3 * k, 3)
        return carry

    n_chunks = c_hi - c_lo
    lax.fori_loop(0, n_chunks // 3, triple, 0)
    for rest in (1, 2):
        @pl.when(n_chunks % 3 == rest)
        def _():
            expert_rows(c_hi - rest, rest)

    @pl.when(e == n_exp - 1)
    def _():
        p = p_sc[...]
        for c0 in range(0, D, 512):
            ys = o_ref[:, c0:c0 + 512].astype(BF16)
            y = lax.dot_general(p, ys, (((0,), (0,)), ((), ())), preferred_element_type=F32)
            o_ref[:, c0:c0 + 512] = x_ref[:, c0:c0 + 512] + y


def _moe(x, norm_w, wg, bg, we, be, w_gate, w_up, w_down, layer, tm=1024):
    T, D = x.shape
    n_groups, n_exp = wg.shape[1], we.shape[1]
    ff = w_gate.shape[3]
    assert n_groups <= GROUP_ROWS and n_exp == n_groups * EXPERTS_PER_GROUP and T % tm == 0
    rows = GROUP_ROWS + n_exp
    wrt = jnp.zeros((rows, D), F32).at[:n_groups].set(wg.T).at[GROUP_ROWS:].set(we.T).astype(BF16)
    brt = jnp.zeros((rows, 1), F32).at[:n_groups, 0].set(bg).at[GROUP_ROWS:, 0].set(be)
    tile = pl.BlockSpec((tm, D), lambda i, e: (i, 0))
    return pl.pallas_call(
        functools.partial(_moe_kernel, n_groups=n_groups),
        grid=(T // tm, n_exp),
        in_specs=[pl.BlockSpec((tm, D), lambda i, e: (i, 0), pipeline_mode=pl.Buffered(1)),
                  _resident((1, D)), _resident((rows, D)), _resident((rows, 1)),
                  pl.BlockSpec((None, 1, D, ff), lambda i, e: (layer, e, 0, 0)),
                  pl.BlockSpec((None, 1, D, ff), lambda i, e: (layer, e, 0, 0)),
                  pl.BlockSpec((None, 1, ff, D), lambda i, e: (layer, e, 0, 0))],
        out_specs=tile,
        out_shape=jax.ShapeDtypeStruct((T, D), F32),
        scratch_shapes=[pltpu.VMEM((tm, D), BF16),
                        pltpu.VMEM((tm, D), BF16),
                        pltpu.VMEM((tm, tm), BF16),
                        pltpu.VMEM((tm, tm), BF16),
                        pltpu.VMEM((tm, LANES), F32),
                        pltpu.SMEM((GROUP_ROWS,), jnp.int32)],
        compiler_params=_params("arbitrary", "arbitrary"),
    )(x, norm_w.reshape(1, D), wrt, brt, w_gate, w_up, w_down)


POOL_HALO = 16


def _pool_kernel(x_ref, halo_ref, nw_ref, pw_ref, ps_ref, o_ref, hb, pa, pb):
    TS, D = x_ref.shape
    n_groups, pc, _ = pw_ref.shape
    i = pl.program_id(1)
    nw = nw_ref[...]
    top = SUBLANES + POOL_HALO
    n_rows = top + TS

    @pl.when(jnp.logical_and(pl.program_id(0) == 0, i == 0))
    def _():
        hb[0:SUBLANES] = jnp.zeros((SUBLANES, D), F32)
        pa[0:SUBLANES] = jnp.zeros((SUBLANES, pc), F32)
        pb[0:SUBLANES] = jnp.zeros((SUBLANES, pc), F32)

    hb[SUBLANES:top] = jnp.where(i > 0, _rms(halo_ref[...], nw), 0.0)
    hb[top:] = _rms(x_ref[...], nw)
    t = i * TS + lax.broadcasted_iota(jnp.int32, (TS, 1), 0)
    for g, w in enumerate(POOL_WINDOWS):
        cs = slice(g * pc, (g + 1) * pc)
        cur = hb[top:, cs]
        src, cols, step, dst = hb, cs, 1, pa
        while step < w:
            dst[SUBLANES:n_rows, :] = src[SUBLANES:n_rows, cols] + src[SUBLANES - step:n_rows - step, cols]
            src, cols, step, dst = dst, slice(None), 2 * step, (pb if dst is pa else pa)
        s = src[top:n_rows, cols]
        count = jnp.minimum(t + 1, w).astype(F32)
        d = s / count - cur
        mixed = _dot(d.astype(BF16), pw_ref[g])
        o_ref[:, cs] = x_ref[:, cs] + ps_ref[:, cs] * mixed


def _pool_layer(x, norm_w, pool_w, pool_scale, B, S, TS=256):
    T, D = x.shape
    assert len(POOL_WINDOWS) == pool_w.shape[0] and max(POOL_WINDOWS) - 1 <= POOL_HALO
    ns = S // TS
    r = TS // POOL_HALO
    pw = pool_w.astype(BF16)
    return pl.pallas_call(
        _pool_kernel,
        grid=(B, ns),
        in_specs=[pl.BlockSpec((TS, D), lambda b, i: (b * ns + i, 0)),
                  pl.BlockSpec((POOL_HALO, D), lambda b, i: (jnp.maximum((b * ns + i) * r - 1, 0), 0)),
                  _resident((1, D)), _resident(pw.shape), _resident((1, D))],
        out_specs=pl.BlockSpec((TS, D), lambda b, i: (b * ns + i, 0)),
        out_shape=jax.ShapeDtypeStruct((T, D), F32),
        scratch_shapes=[pltpu.VMEM((SUBLANES + POOL_HALO + TS, D), F32),
                        pltpu.VMEM((SUBLANES + POOL_HALO + TS, D // len(POOL_WINDOWS)), F32),
                        pltpu.VMEM((SUBLANES + POOL_HALO + TS, D // len(POOL_WINDOWS)), F32)],
        compiler_params=_params("arbitrary", "arbitrary"),
    )(x, x, norm_w.reshape(1, D), pw, pool_scale.reshape(1, D))


def _chunk(S):
    return 256 if S % 256 == 0 else 128


def kernel(x, rel_bias, mix_norm_e, w_in_e, q_norm_e, k_norm_e, conv_w_e, conv_b_e, conv_ln_g_e, conv_ln_b_e,
           w_out_e, mix_norm_o, pool_w_o, pool_scale_o, ffn_norm, router_group_w, router_group_b,
           router_expert_w, router_expert_b, w_gate, w_up, w_down):
    B, S, D = x.shape
    T = B * S
    depth = ffn_norm.shape[0]
    n_heads = rel_bias.shape[1]
    idx_heads = (w_in_e.shape[2] - n_heads * HEAD_DIM - 2 * HEAD_DIM - IDX_DIM - 2 * conv_w_e.shape[2]) \
        // (IDX_DIM + 1)
    k_top = min(INDEX_TOPK, S // 4)
    C = _chunk(S)
    xf = x.reshape(T, D)
    bias_tiles, bias_stat = _rel_bias_tiles(rel_bias, C)
    expert_w = [w.astype(BF16) for w in (w_gate, w_up, w_down)]
    for l in range(depth):
        i = l // 2
        if l % 2 == 0:
            q, k, v, iq, ik, iw, u = _in_proj(xf, mix_norm_e[i], w_in_e[i], q_norm_e[i], k_norm_e[i],
                                              n_heads, idx_heads, conv_w_e.shape[2])
            attn = _attention(q, k, v, iq, ik, iw, bias_tiles, bias_stat, B, S, k_top, C)
            conv = _conv_module(u, conv_w_e[i], conv_b_e[i], conv_ln_g_e[i], conv_ln_b_e[i], B, S)
            xf = _out_proj(xf, attn, conv, w_out_e[i])
        else:
            xf = _pool_layer(xf, mix_norm_o[i], pool_w_o[i], pool_scale_o[i], B, S)
        xf = _moe(xf, ffn_norm[l], router_group_w[l], router_group_b[l], router_expert_w[l], router_expert_b[l],
                  *expert_w, layer=l)
    return xf.reshape(B, S, D)
```

```python
import functools
import math

import jax
import jax.numpy as jnp
from jax import lax
from jax.experimental import pallas as pl
from jax.experimental.pallas import tpu as pltpu

F32 = jnp.float32
BF16 = jnp.bfloat16

NORM_EPS = 1e-6
HEAD_DIM = 128
IDX_DIM = 64
INDEX_TOPK = 256
REL_BUCKETS = 32
REL_MAX_DIST = 128
POOL_WINDOWS = (2, 4, 8, 16)
EXPERTS_PER_GROUP = 4
LANES = 128
VMEM_LIMIT = 56 * 1024 * 1024
NEG = -1e30
INT_MIN = -(2 ** 31)


def _dot(a, b):
    return jnp.dot(a, b, preferred_element_type=F32)


def _dot_nt(a, b):
    return lax.dot_general(a, b, (((1,), (1,)), ((), ())), preferred_element_type=F32)


def _rms(x, w):
    return x * lax.rsqrt(jnp.mean(x * x, axis=-1, keepdims=True) + NORM_EPS) * w


def _params(*sem):
    return pltpu.CompilerParams(dimension_semantics=sem, vmem_limit_bytes=VMEM_LIMIT)


def _resident(shape):
    nd = len(shape)
    return pl.BlockSpec(shape, lambda *_: (0,) * nd, pipeline_mode=pl.Buffered(1))


def _in_proj_kernel(x_ref, nw_ref, qn_ref, kn_ref, wq_ref, wkv_ref, wiq_ref, wikw_ref, wa_ref, wg_ref,
                    q_ref, k_ref, v_ref, iq_ref, ik_ref, iw_ref, u_ref, *, q_scale, iw_scale):
    h = _rms(x_ref[...], nw_ref[...]).astype(BF16)
    n_pairs = q_ref.shape[0] // 2
    qn = qn_ref[...] * q_scale
    for c in range(n_pairs):
        qq = _dot(h, wq_ref[:, c * 256:(c + 1) * 256])
        for s in range(2):
            qh = qq[:, s * HEAD_DIM:(s + 1) * HEAD_DIM]
            q_ref[2 * c + s] = _rms(qh, qn).astype(BF16)
    kv = _dot(h, wkv_ref[...])
    k_ref[...] = _rms(kv[:, :HEAD_DIM], kn_ref[...]).astype(BF16)
    v_ref[...] = kv[:, HEAD_DIM:].astype(BF16)
    for c in range(iq_ref.shape[0] // 2):
        r = _dot(h, wiq_ref[:, c * 256:(c + 1) * 256])
        iq_ref[2 * c] = r[:, :LANES].astype(BF16)
        iq_ref[2 * c + 1] = r[:, LANES:].astype(BF16)
    r = _dot(h, wikw_ref[...])
    ik_ref[...] = r[:, :LANES].astype(BF16)
    iw_ref[...] = r[:, LANES:] * iw_scale
    for c in range(u_ref.shape[1] // 256):
        cs = slice(c * 256, (c + 1) * 256)
        a = _dot(h, wa_ref[:, cs])
        g = _dot(h, wg_ref[:, cs])
        u_ref[:, cs] = (a * jax.nn.sigmoid(g)).astype(BF16)


def _split_w_in_kernel(w_ref, wq_ref, wkv_ref, wiq_ref, wikw_ref, wa_ref, wg_ref, *, idx_heads):
    o = 0
    for ref in (wq_ref, wkv_ref, wiq_ref):
        n = ref.shape[1]
        ref[...] = w_ref[:, o:o + n].astype(BF16)
        o += n
    wik = w_ref[:, o:o + IDX_DIM]
    wiw = w_ref[:, o + IDX_DIM:o + IDX_DIM + idx_heads]
    o += IDX_DIM + idx_heads
    pad = jnp.zeros((wik.shape[0], LANES - idx_heads), F32)
    wikw_ref[...] = jnp.concatenate([wik, wik, wiw, pad], axis=1).astype(BF16)
    for ref in (wa_ref, wg_ref):
        n = ref.shape[1]
        ref[...] = w_ref[:, o:o + n].astype(BF16)
        o += n


def _split_w_in(w_in, n_heads, idx_heads, conv_ch, tk=256):
    D, n_in = w_in.shape
    widths = (n_heads * HEAD_DIM, 2 * HEAD_DIM, idx_heads * IDX_DIM, 2 * LANES, conv_ch, conv_ch)
    assert n_in == sum(widths) - 2 * LANES + IDX_DIM + idx_heads and 2 * IDX_DIM == LANES and D % tk == 0
    return pl.pallas_call(
        functools.partial(_split_w_in_kernel, idx_heads=idx_heads),
        grid=(D // tk,),
        in_specs=[pl.BlockSpec((tk, n_in), lambda i: (i, 0))],
        out_specs=[pl.BlockSpec((tk, w), lambda i: (i, 0)) for w in widths],
        out_shape=[jax.ShapeDtypeStruct((D, w), BF16) for w in widths],
        compiler_params=_params("arbitrary"),
    )(w_in)


def _in_proj(x, norm_w, w_in, q_norm, k_norm, n_heads, idx_heads, conv_ch, tm=256):
    T, D = x.shape
    iq_w = idx_heads * IDX_DIM
    ws = _split_w_in(w_in, n_heads, idx_heads, conv_ch)
    row = lambda w: pl.BlockSpec((tm, w), lambda i: (i, 0))
    heads = lambda n: pl.BlockSpec((n, tm, LANES), lambda i: (0, i, 0))
    kern = functools.partial(_in_proj_kernel, q_scale=HEAD_DIM ** -0.5,
                             iw_scale=(idx_heads ** -0.5) * (IDX_DIM ** -0.5))
    return pl.pallas_call(
        kern,
        grid=(T // tm,),
        in_specs=[row(D), _resident((1, D)), _resident((1, HEAD_DIM)), _resident((1, HEAD_DIM))]
                 + [_resident(w.shape) for w in ws],
        out_specs=[heads(n_heads), row(HEAD_DIM), row(HEAD_DIM), heads(iq_w // LANES), row(LANES), row(LANES),
                   row(conv_ch)],
        out_shape=[jax.ShapeDtypeStruct((n_heads, T, HEAD_DIM), BF16),
                   jax.ShapeDtypeStruct((T, HEAD_DIM), BF16),
                   jax.ShapeDtypeStruct((T, HEAD_DIM), BF16),
                   jax.ShapeDtypeStruct((iq_w // LANES, T, LANES), BF16),
                   jax.ShapeDtypeStruct((T, LANES), BF16),
                   jax.ShapeDtypeStruct((T, LANES), F32),
                   jax.ShapeDtypeStruct((T, conv_ch), BF16)],
        compiler_params=_params("arbitrary"),
    )(x, norm_w.reshape(1, D), q_norm.reshape(1, HEAD_DIM), k_norm.reshape(1, HEAD_DIM), *ws)


def _rel_bias_kernel(rb_ref, o_ref, stat_ref):
    _, n_heads, C, _ = o_ref.shape
    tau = lax.broadcasted_iota(jnp.int32, (C, C), 0)
    sig = lax.broadcasted_iota(jnp.int32, (C, C), 1)
    max_exact = REL_BUCKETS // 2
    for kind in range(2):
        d = tau - sig + kind * C
        n = jnp.maximum(d, 0)
        nf = jnp.maximum(n, 1).astype(F32)
        large = max_exact + (jnp.log(nf / max_exact) / math.log(REL_MAX_DIST / max_exact)
                             * (REL_BUCKETS - max_exact)).astype(jnp.int32)
        large = jnp.minimum(large, REL_BUCKETS - 1)
        bucket = jnp.where(n < max_exact, n, large)
        for h in range(n_heads):
            b = jnp.zeros((C, C), F32)
            for bk in range(REL_BUCKETS):
                b = jnp.where(bucket == bk, rb_ref[bk, h], b)
            b = b - rb_ref[REL_BUCKETS - 1, h]
            if kind == 0:
                b = jnp.where(d < 0, NEG, b)
            o_ref[kind, h] = b
    for h in range(n_heads):
        hi = rb_ref[0, h]
        lo = rb_ref[0, h]
        for bk in range(1, REL_BUCKETS):
            hi = jnp.maximum(hi, rb_ref[bk, h])
            lo = jnp.minimum(lo, rb_ref[bk, h])
        stat_ref[0, h] = hi - rb_ref[REL_BUCKETS - 1, h]
        stat_ref[1, h] = lo - rb_ref[REL_BUCKETS - 1, h]


def _rel_bias_tiles(rel_bias, C):
    n_heads = rel_bias.shape[1]
    assert C >= REL_MAX_DIST
    return pl.pallas_call(
        _rel_bias_kernel,
        in_specs=[pl.BlockSpec(memory_space=pltpu.SMEM)],
        out_specs=[pl.BlockSpec(memory_space=pltpu.VMEM), pl.BlockSpec(memory_space=pltpu.SMEM)],
        out_shape=[jax.ShapeDtypeStruct((2, n_heads, C, C), F32), jax.ShapeDtypeStruct((2, n_heads), F32)],
        compiler_params=pltpu.CompilerParams(vmem_limit_bytes=VMEM_LIMIT),
    )(rel_bias)


SHIFT_SPAN_LIMIT = 60.0


TILE_GROUP = 4


def _grouped_loop(n, fn, group):
    def body(k, carry):
        for u in range(group):
            fn(group * k + u)
        return carry

    lax.fori_loop(0, n // group, body, 0)
    base = (n // group) * group
    size = group // 2
    while size >= 1:
        take = ((n - base) & size) != 0

        @pl.when(take)
        def _(base=base, size=size):
            for u in range(size):
                fn(base + u)
        base = base + jnp.where(take, size, 0)
        size //= 2


def _order_key(x):
    bits = pltpu.bitcast(x, jnp.int32)
    return bits ^ ((bits >> 31) & 0x7FFFFFFF)


def _row_to_col(row):
    C = row.shape[1]
    halves = []
    for part in (row >> 16, row & 0xFFFF):
        halves.append(jnp.broadcast_to(part.astype(F32), (LANES, C)).T[:, 0:1].astype(jnp.int32))
    return (halves[0] << 16) | halves[1]


def _attn_kernel(bstat_ref, q_ref, k_ref, v_ref, iq_ref, ik_ref, iw_ref, bias_ref, *rest, k_top, n_cast):
    cast_in, (o_ref, *cast_out) = rest[:n_cast], rest[n_cast:2 * n_cast + 1]
    key_sc, keyt_sc, keyt16_sc, iqm_sc, vx_sc, kmax_sc, shift_sc, acc_sc = rest[2 * n_cast + 1:]
    for src, dst in zip(cast_in, cast_out):
        dst[...] = src[...].astype(BF16)
    n_heads, C, _ = q_ref.shape
    idx_heads = iqm_sc.shape[0]
    idx_bits = max(1, (key_sc.shape[0] * C - 1).bit_length())
    qi = pl.program_id(1)
    nkv = qi + 1

    @pl.when(qi == 0)
    def _():
        vx_sc[:, :HEAD_DIM] = v_ref[...]
        vx_sc[:, HEAD_DIM:] = jnp.ones((vx_sc.shape[0], HEAD_DIM), BF16)
        kf = k_ref[...].astype(F32)
        k2 = jnp.sum(kf * kf, axis=-1, keepdims=True)
        kmax_sc[...] = jnp.broadcast_to(jnp.sqrt(jnp.max(k2, axis=0, keepdims=True)), kmax_sc.shape)

    lane = lax.broadcasted_iota(jnp.int32, (C, LANES), 1)
    for p in range(idx_heads // 2):
        qp = iq_ref[p].astype(F32)
        iqm_sc[2 * p] = jnp.where(lane < IDX_DIM, qp, 0.0).astype(BF16)
        iqm_sc[2 * p + 1] = jnp.where(lane >= IDX_DIM, qp, 0.0).astype(BF16)
    iw = iw_ref[...]
    tau = lax.broadcasted_iota(jnp.int32, (C, C), 0)
    sig = lax.broadcasted_iota(jnp.int32, (C, C), 1)

    def score_tile(j):
        off = pl.multiple_of(j * C, C)
        ikc = ik_ref[pl.ds(off, C), :]
        acc = jnp.zeros((C, C), F32)
        for hh in range(idx_heads):
            s = _dot_nt(iqm_sc[hh], ikc)
            acc = acc + jnp.maximum(s, 0.0) * iw[:, hh:hh + 1]
        acc = jnp.where(jnp.logical_and(j == qi, sig > tau), -jnp.inf, acc)
        key_sc[j] = _order_key(acc)
        kt = _order_key(acc.T)
        keyt_sc[j] = kt
        keyt16_sc[j] = (kt >> 16).astype(jnp.int16)

    _grouped_loop(nkv, score_tile, TILE_GROUP)

    SUB = 32

    def count(hit):
        def body(j, cnt):
            for r in range(C // SUB):
                cnt = cnt + hit(j, r)
            return cnt
        cnt = lax.fori_loop(0, nkv, body, jnp.zeros((SUB, C), F32))
        return jnp.sum(cnt, axis=0, keepdims=True)

    def keyt(j, r):
        return keyt_sc[j, pl.ds(r * SUB, SUB), :]

    def count16(cand16):
        one = jnp.ones((SUB, C), jnp.int16)
        zero = jnp.zeros((SUB, C), jnp.int16)

        def body(j, cnt):
            for r in range(C // SUB):
                cnt = cnt + jnp.where(keyt16_sc[j, pl.ds(r * SUB, SUB), :] >= cand16, one, zero)
            return cnt
        cnt = lax.fori_loop(0, nkv, body, zero)
        return jnp.sum(cnt.astype(jnp.int32).astype(F32), axis=0, keepdims=True)

    def bit16_body(b, carry):
        res, n_res = carry
        cand = res ^ lax.shift_left(jnp.int32(1), 31 - b)
        tot = count16((cand >> 16).astype(jnp.int16))
        ok = tot >= k_top
        return jnp.where(ok, cand, res), jnp.where(ok, tot, n_res)

    searched = nkv * C > k_top
    carry = (jnp.full((1, C), INT_MIN, jnp.int32), jnp.zeros((1, C), F32))
    top_row, n_top = lax.fori_loop(0, jnp.where(searched, 16, 0), bit16_body, carry)

    top16 = top_row >> 16

    def low_body(j, c):
        kt = keyt_sc[j]
        hi = kt >> 16
        lo = (kt & 0xFFFF) - 32768
        lo = jnp.where(hi == top16, lo, jnp.where(hi > top16, 32767, -32768))
        keyt16_sc[j] = lo.astype(jnp.int16)
        return c

    lax.fori_loop(0, jnp.where(searched, nkv, 0), low_body, 0)

    def bit16_low_body(b, carry):
        res, n_res = carry
        cand = res ^ lax.shift_left(jnp.int32(1), 31 - b)
        tot = count16(((cand & 0xFFFF) - 32768).astype(jnp.int16))
        ok = tot >= k_top
        return jnp.where(ok, cand, res), jnp.where(ok, tot, n_res)

    thr_row, n_thr = lax.fori_loop(16, jnp.where(searched, 32, 16), bit16_low_body, (top_row, n_top))
    thr = _row_to_col(thr_row)

    @pl.when(jnp.max(n_thr) > k_top)
    def _():
        need = k_top - count(lambda j, r: jnp.where(keyt(j, r) > thr_row, 1.0, 0.0))
        s_sub = lax.broadcasted_iota(jnp.int32, (SUB, C), 0)

        def idx_body(b, last):
            cand = last | lax.shift_left(jnp.int32(1), idx_bits - 1 - b)
            below = count(lambda j, r: jnp.where(
                keyt(j, r) == thr_row, jnp.where(j * C + r * SUB + s_sub < cand, 1.0, 0.0), 0.0))
            return jnp.where(below < need, cand, last)

        last = _row_to_col(lax.fori_loop(0, idx_bits, idx_body, jnp.zeros((1, C), jnp.int32)))

        def drop_body(j, carry):
            kk = key_sc[j]
            key_sc[j] = jnp.where(kk == thr, jnp.where(j * C + sig > last, kk - 1, kk), kk)
            return carry

        lax.fori_loop(0, nkv, drop_body, 0)

    kmax = kmax_sc[0:1, 0:1] * 1.001
    worst = jnp.zeros((C, 1), F32)
    for h in range(n_heads):
        qf = q_ref[h].astype(F32)
        bound = jnp.sqrt(jnp.sum(qf * qf, axis=-1, keepdims=True)) * kmax
        shift_sc[h] = jnp.broadcast_to(bound + bstat_ref[0, h], (C, LANES))
        worst = jnp.maximum(worst, 2.0 * bound + (bstat_ref[0, h] - bstat_ref[1, h]))
    loose = jnp.max(worst) > SHIFT_SPAN_LIMIT

    def logits(j, h, kind):
        off = pl.multiple_of(j * C, C)
        lg = _dot_nt(q_ref[h], k_ref[pl.ds(off, C), :])
        return lg if kind is None else lg + bias_ref[kind, h]

    def near_tiles(fn):
        @pl.when(qi > 0)
        def _():
            fn(qi - 1, 1)
        fn(qi, 0)

    def far_tiles(fn, group=TILE_GROUP):
        _grouped_loop(jnp.maximum(qi - 1, 0), lambda j: fn(j, None), group)

    @pl.when(loose)
    def _():
        for h in range(n_heads):
            shift_sc[h] = jnp.full((C, LANES), NEG, F32)

        def max_tile(j, kind):
            sel = key_sc[j] >= thr
            for h in range(n_heads):
                lg = jnp.where(sel, logits(j, h, kind), NEG)
                m = jnp.max(lg, axis=-1, keepdims=True)
                shift_sc[h] = jnp.maximum(shift_sc[h], jnp.broadcast_to(m, (C, LANES)))

        far_tiles(max_tile, group=1)
        near_tiles(max_tile)

    acc_sc[...] = jnp.zeros(acc_sc.shape, F32)

    def attn_tile(j, kind):
        off = pl.multiple_of(j * C, C)
        vx = vx_sc[pl.ds(off, C), :]
        sel = key_sc[j] >= thr
        for h in range(n_heads):
            sh = jnp.concatenate([shift_sc[h]] * (C // LANES), axis=1)
            p = jnp.where(sel, jnp.exp(logits(j, h, kind) - sh), 0.0)
            acc_sc[h] += _dot(p.astype(BF16), vx)

    far_tiles(attn_tile)
    near_tiles(attn_tile)
    for h in range(n_heads):
        a = acc_sc[h]
        o_ref[:, h * HEAD_DIM:(h + 1) * HEAD_DIM] = (a[:, :HEAD_DIM] / a[:, HEAD_DIM:]).astype(BF16)


CAST_SLAB_ELEMS = 1 << 20


def _cast_slabs(w, steps):
    rows = math.prod(w.shape[:-1])
    if rows % (steps * 16) != 0 or rows // steps * w.shape[-1] > CAST_SLAB_ELEMS:
        return None
    return w.reshape(steps, rows // steps, w.shape[-1])


def _attention(q, k, v, iq, ik, iw, bias_tiles, bias_stat, B, S, k_top, C, cast=()):
    n_heads, T, _ = q.shape
    n_pairs = iq.shape[0]
    nq = S // C
    slabs = [_cast_slabs(w, B * nq) for w in cast]
    kern = functools.partial(_attn_kernel, k_top=k_top, n_cast=len(slabs))
    heads = lambda n: pl.BlockSpec((n, C, LANES), lambda b, i: (0, b * nq + i, 0))
    seq = pl.BlockSpec((S, LANES), lambda b, i: (b, 0))
    slab_specs = [pl.BlockSpec((None,) + w.shape[1:], lambda b, i: (b * nq + i, 0, 0)) for w in slabs]
    outs = pl.pallas_call(
        kern,
        grid=(B, nq),
        in_specs=[pl.BlockSpec(memory_space=pltpu.SMEM),
                  heads(n_heads), seq, seq, heads(n_pairs), seq,
                  pl.BlockSpec((C, LANES), lambda b, i: (b * nq + i, 0)),
                  _resident(bias_tiles.shape)] + slab_specs,
        out_specs=[pl.BlockSpec((C, n_heads * HEAD_DIM), lambda b, i: (b * nq + i, 0))] + slab_specs,
        out_shape=[jax.ShapeDtypeStruct((T, n_heads * HEAD_DIM), BF16)]
                  + [jax.ShapeDtypeStruct(w.shape, BF16) for w in slabs],
        scratch_shapes=[pltpu.VMEM((nq, C, C), jnp.int32),
                        pltpu.VMEM((nq, C, C), jnp.int32),
                        pltpu.VMEM((nq, C, C), jnp.int16),
                        pltpu.VMEM((2 * n_pairs, C, LANES), BF16),
                        pltpu.VMEM((S, 2 * HEAD_DIM), BF16),
                        pltpu.VMEM((8, LANES), F32),
                        pltpu.VMEM((n_heads, C, LANES), F32),
                        pltpu.VMEM((n_heads, C, 2 * HEAD_DIM), F32)],
        compiler_params=_params("arbitrary", "arbitrary"),
    )(bias_stat, q, k, v, iq, ik, iw, bias_tiles, *slabs)
    return outs[0], [o.reshape(w.shape) for o, w in zip(outs[1:], cast)]


CONV_HALO = 32


SUBLANES = 8
CONV_SLAB = 64


def _conv_kernel(u_ref, halo_ref, w_ref, cb_ref, g_ref, b_ref, o_ref, cp_sc, y_sc, *, width):
    TS, CH = u_ref.shape
    n_rows = CONV_HALO + TS
    i = pl.program_id(1)
    halo = jnp.where(i > 0, halo_ref[...].astype(F32), 0.0)
    pad = jnp.zeros((SUBLANES, LANES), F32)
    for c in range(CH // LANES):
        cs = slice(c * LANES, (c + 1) * LANES)
        col = jnp.concatenate([halo[:, cs], u_ref[:, cs].astype(F32), pad], axis=0)
        for r in range(SUBLANES):
            cp_sc[r, :, cs] = col[r:r + n_rows]

    base = CONV_HALO - (width - 1)
    phases = {}
    for j in range(width):
        q, r = divmod(base + j, SUBLANES)
        phases.setdefault(r, []).append((q, j))

    def slab(s, carry):
        t0 = pl.multiple_of(s * CONV_SLAB, CONV_SLAB)
        for c in range(CH // LANES):
            cs = slice(c * LANES, (c + 1) * LANES)
            acc = jnp.zeros((CONV_SLAB, LANES), F32)
            for r, taps in phases.items():
                q_lo, q_hi = taps[0][0], taps[-1][0]
                win = cp_sc[r, pl.ds(q_lo * SUBLANES + t0, CONV_SLAB + (q_hi - q_lo) * SUBLANES), cs]
                for q, j in taps:
                    off = (q - q_lo) * SUBLANES
                    acc = acc + w_ref[j:j + 1, cs] * win[off:off + CONV_SLAB]
            y_sc[pl.ds(t0, CONV_SLAB), cs] = acc + cb_ref[:, cs]
        return carry

    lax.fori_loop(0, TS // CONV_SLAB, slab, 0)
    y = y_sc[...]
    mu = jnp.mean(y, axis=-1, keepdims=True)
    yc = y - mu
    var = jnp.mean(yc * yc, axis=-1, keepdims=True)
    yn = yc * lax.rsqrt(var + NORM_EPS) * g_ref[...] + b_ref[...]
    o_ref[...] = (yn * jax.nn.sigmoid(yn)).astype(BF16)


def _conv_module(u, conv_w, conv_b, ln_g, ln_b, B, S, TS=256):
    T, CH = u.shape
    width = conv_w.shape[0]
    assert width - 1 <= CONV_HALO
    ns = S // TS
    r = TS // CONV_HALO
    wpad = jnp.zeros((CONV_HALO, CH), F32).at[:width].set(conv_w)
    vec = lambda a: a.reshape(1, CH)
    kern = functools.partial(_conv_kernel, width=width)
    return pl.pallas_call(
        kern,
        grid=(B, ns),
        in_specs=[pl.BlockSpec((TS, CH), lambda b, i: (b * ns + i, 0)),
                  pl.BlockSpec((CONV_HALO, CH), lambda b, i: (jnp.maximum((b * ns + i) * r - 1, 0), 0)),
                  _resident((CONV_HALO, CH)), _resident((1, CH)), _resident((1, CH)), _resident((1, CH))],
        out_specs=pl.BlockSpec((TS, CH), lambda b, i: (b * ns + i, 0)),
        out_shape=jax.ShapeDtypeStruct((T, CH), BF16),
        scratch_shapes=[pltpu.VMEM((SUBLANES, CONV_HALO + TS, CH), F32),
                        pltpu.VMEM((TS, CH), F32)],
        compiler_params=_params("arbitrary", "arbitrary"),
    )(u, u, wpad, vec(conv_b), vec(ln_g), vec(ln_b))


def _out_proj_kernel(x_ref, a_ref, c_ref, wa_ref, wc_ref, o_ref):
    o_ref[...] = x_ref[...] + _dot(a_ref[...], wa_ref[...]) + _dot(c_ref[...], wc_ref[...])


def _out_proj(x, attn, conv, w_out, tm=512):
    T, D = x.shape
    aw, cw = attn.shape[1], conv.shape[1]
    wa = w_out[:aw].astype(BF16)
    wc = w_out[aw:].astype(BF16)
    row = lambda w: pl.BlockSpec((tm, w), lambda i: (i, 0))
    return pl.pallas_call(
        _out_proj_kernel,
        grid=(T // tm,),
        in_specs=[row(D), row(aw), row(cw), _resident(wa.shape), _resident(wc.shape)],
        out_specs=row(D),
        out_shape=jax.ShapeDtypeStruct((T, D), F32),
        compiler_params=_params("arbitrary"),
    )(x, attn, conv, wa, wc)


def _first_max(vals):
    m = vals[0]
    for v in vals[1:]:
        m = jnp.maximum(m, v)
    idx = jnp.full(m.shape, len(vals) - 1, jnp.int32)
    for k in range(len(vals) - 2, -1, -1):
        idx = jnp.where(vals[k] == m, k, idx)
    return m, idx


def _softmax_cols(cols):
    m = cols[0]
    for c in cols[1:]:
        m = jnp.maximum(m, c)
    e = [jnp.exp(c - m) for c in cols]
    s = e[0]
    for c in e[1:]:
        s = s + c
    return [c / s for c in e]


GROUP_ROWS = 8
MOE_CHUNK = 128


def _split3(a):
    hi = a.astype(BF16)
    r = a - hi.astype(F32)
    mid = r.astype(BF16)
    lo = (r - mid.astype(F32)).astype(BF16)
    return hi, mid, lo


def _moe_kernel(x_ref, nw_ref, wrt_ref, brt_ref, wg_ref, wu_ref, wd_ref, o_ref,
                h_sc, hs_sc, p_sc, u_sc, cs_sc, seg_sm, *, n_groups):
    TM, D = x_ref.shape
    i = pl.program_id(0)
    e = pl.program_id(1)
    n_exp = pl.num_programs(1)

    @pl.when(jnp.logical_and(i == 0, e == 0))
    def _():
        s_io = lax.broadcasted_iota(jnp.int32, (TM, TM), 0)
        t_io = lax.broadcasted_iota(jnp.int32, (TM, TM), 1)
        u_sc[...] = jnp.where(s_io < t_io, 1.0, 0.0).astype(BF16)

    @pl.when(e == 0)
    def _():
        h = _rms(x_ref[...], nw_ref[...]).astype(BF16)
        h_sc[...] = h
        lg = _dot_nt(wrt_ref[...], h) + brt_ref[...]
        row = lambda k: lg[k:k + 1, :]
        g_prob = _softmax_cols([row(g) for g in range(n_groups)])
        g_p, g_idx = _first_max(g_prob)
        e_logit = []
        for k in range(EXPERTS_PER_GROUP):
            v = row(GROUP_ROWS + k)
            for g in range(1, n_groups):
                v = jnp.where(g_idx == g, row(GROUP_ROWS + g * EXPERTS_PER_GROUP + k), v)
            e_logit.append(v)
        e_prob = _softmax_cols(e_logit)
        p1, i1 = _first_max(e_prob)
        rest = [jnp.where(i1 == k, -1.0, e_prob[k]) for k in range(EXPERTS_PER_GROUP)]
        p2, i2 = _first_max(rest)
        den = p1 + p2
        base = g_idx * EXPERTS_PER_GROUP
        e_io = lax.broadcasted_iota(jnp.int32, (LANES, TM), 0)
        comb_t = (jnp.where(e_io == base + i1, g_p * (p1 / den), 0.0)
                  + jnp.where(e_io == base + i2, g_p * (p2 / den), 0.0))

        g_io = lax.broadcasted_iota(jnp.int32, (GROUP_ROWS, TM), 0)
        onehot_t = jnp.where(g_io == g_idx, 1.0, 0.0)
        rank = _dot(onehot_t.astype(BF16), u_sc[...])
        start = jnp.int32(0)
        pos = jnp.zeros((1, TM), F32)
        for g in range(n_groups):
            seg_sm[g] = start
            pos = jnp.where(g_idx == g, start.astype(F32) + rank[g:g + 1, :], pos)
            start = start + jnp.sum(onehot_t[g:g + 1, :]).astype(jnp.int32)
        seg_sm[n_groups] = start
        pos = pos.astype(jnp.int32)

        rb = 256
        for r0 in range(0, TM, rb):
            r_io = lax.broadcasted_iota(jnp.int32, (rb, TM), 0) + r0
            p_sc[r0:r0 + rb, :] = jnp.where(r_io == pos, 1.0, 0.0).astype(BF16)
        p = p_sc[...]
        for c0 in range(0, D, 512):
            hs_sc[:, c0:c0 + 512] = _dot(p, h_sc[:, c0:c0 + 512]).astype(BF16)
        cs = jnp.zeros((TM, LANES), F32)
        for part in _split3(comb_t):
            cs = cs + _dot_nt(p, part)
        cs_sc[...] = cs
        o_ref[...] = jnp.zeros(o_ref.shape, F32)

    g = e // EXPERTS_PER_GROUP
    start = seg_sm[g]
    end = seg_sm[g + 1]
    c_lo = start // MOE_CHUNK
    c_hi = jnp.where(end > start, (end + MOE_CHUNK - 1) // MOE_CHUNK, c_lo)

    def expert_rows(c, n_chunks):
        m = n_chunks * MOE_CHUNK
        r0 = pl.multiple_of(c * MOE_CHUNK, MOE_CHUNK)
        rows = hs_sc[pl.ds(r0, m), :]
        lane = lax.broadcasted_iota(jnp.int32, (m, LANES), 1)
        w = jnp.sum(jnp.where(lane == e, cs_sc[pl.ds(r0, m), :], 0.0), axis=-1, keepdims=True)
        a = jax.nn.silu(_dot(rows, wg_ref[0])) * _dot(rows, wu_ref[0]) * w
        o_ref[pl.ds(r0, m), :] += _dot(a.astype(BF16), wd_ref[0])

    def triple(k, carry):
        expert_rows(c_lo + 3 * k, 3)
        return carry

    n_chunks = c_hi - c_lo
    lax.fori_loop(0, n_chunks // 3, triple, 0)
    for rest in (1, 2):
        @pl.when(n_chunks % 3 == rest)
        def _():
            expert_rows(c_hi - rest, rest)

    @pl.when(e == n_exp - 1)
    def _():
        p = p_sc[...]
        for c0 in range(0, D, 512):
            ys = o_ref[:, c0:c0 + 512].astype(BF16)
            y = lax.dot_general(p, ys, (((0,), (0,)), ((), ())), preferred_element_type=F32)
            o_ref[:, c0:c0 + 512] = x_ref[:, c0:c0 + 512] + y


def _moe(x, norm_w, wg, bg, we, be, w_gate, w_up, w_down, layer, tm=1024):
    T, D = x.shape
    n_groups, n_exp = wg.shape[1], we.shape[1]
    ff = w_gate.shape[3]
    assert n_groups <= GROUP_ROWS and n_exp == n_groups * EXPERTS_PER_GROUP and T % tm == 0
    rows = GROUP_ROWS + n_exp
    wrt = jnp.zeros((rows, D), F32).at[:n_groups].set(wg.T).at[GROUP_ROWS:].set(we.T).astype(BF16)
    brt = jnp.zeros((rows, 1), F32).at[:n_groups, 0].set(bg).at[GROUP_ROWS:, 0].set(be)
    tile = pl.BlockSpec((tm, D), lambda i, e: (i, 0))
    return pl.pallas_call(
        functools.partial(_moe_kernel, n_groups=n_groups),
        grid=(T // tm, n_exp),
        in_specs=[pl.BlockSpec((tm, D), lambda i, e: (i, 0), pipeline_mode=pl.Buffered(1)),
                  _resident((1, D)), _resident((rows, D)), _resident((rows, 1)),
                  pl.BlockSpec((None, 1, D, ff), lambda i, e: (layer, e, 0, 0)),
                  pl.BlockSpec((None, 1, D, ff), lambda i, e: (layer, e, 0, 0)),
                  pl.BlockSpec((None, 1, ff, D), lambda i, e: (layer, e, 0, 0))],
        out_specs=tile,
        out_shape=jax.ShapeDtypeStruct((T, D), F32),
        scratch_shapes=[pltpu.VMEM((tm, D), BF16),
                        pltpu.VMEM((tm, D), BF16),
                        pltpu.VMEM((tm, tm), BF16),
                        pltpu.VMEM((tm, tm), BF16),
                        pltpu.VMEM((tm, LANES), F32),
                        pltpu.SMEM((GROUP_ROWS,), jnp.int32)],
        compiler_params=_params("arbitrary", "arbitrary"),
    )(x, norm_w.reshape(1, D), wrt, brt, w_gate, w_up, w_down)


POOL_HALO = 16


def _pool_kernel(x_ref, halo_ref, nw_ref, pw_ref, ps_ref, o_ref, hb, pa, pb):
    TS, D = x_ref.shape
    n_groups, pc, _ = pw_ref.shape
    i = pl.program_id(1)
    nw = nw_ref[...]
    top = SUBLANES + POOL_HALO
    n_rows = top + TS

    @pl.when(jnp.logical_and(pl.program_id(0) == 0, i == 0))
    def _():
        hb[0:SUBLANES] = jnp.zeros((SUBLANES, D), F32)
        pa[0:SUBLANES] = jnp.zeros((SUBLANES, pc), F32)
        pb[0:SUBLANES] = jnp.zeros((SUBLANES, pc), F32)

    hb[SUBLANES:top] = jnp.where(i > 0, _rms(halo_ref[...], nw), 0.0)
    hb[top:] = _rms(x_ref[...], nw)
    t = i * TS + lax.broadcasted_iota(jnp.int32, (TS, 1), 0)
    for g, w in enumerate(POOL_WINDOWS):
        cs = slice(g * pc, (g + 1) * pc)
        cur = hb[top:, cs]
        src, cols, step, dst = hb, cs, 1, pa
        while step < w:
            dst[SUBLANES:n_rows, :] = src[SUBLANES:n_rows, cols] + src[SUBLANES - step:n_rows - step, cols]
            src, cols, step, dst = dst, slice(None), 2 * step, (pb if dst is pa else pa)
        s = src[top:n_rows, cols]
        count = jnp.minimum(t + 1, w).astype(F32)
        d = s / count - cur
        mixed = _dot(d.astype(BF16), pw_ref[g])
        o_ref[:, cs] = x_ref[:, cs] + ps_ref[:, cs] * mixed


def _pool_layer(x, norm_w, pool_w, pool_scale, B, S, TS=256):
    T, D = x.shape
    assert len(POOL_WINDOWS) == pool_w.shape[0] and max(POOL_WINDOWS) - 1 <= POOL_HALO
    ns = S // TS
    r = TS // POOL_HALO
    pw = pool_w.astype(BF16)
    return pl.pallas_call(
        _pool_kernel,
        grid=(B, ns),
        in_specs=[pl.BlockSpec((TS, D), lambda b, i: (b * ns + i, 0)),
                  pl.BlockSpec((POOL_HALO, D), lambda b, i: (jnp.maximum((b * ns + i) * r - 1, 0), 0)),
                  _resident((1, D)), _resident(pw.shape), _resident((1, D))],
        out_specs=pl.BlockSpec((TS, D), lambda b, i: (b * ns + i, 0)),
        out_shape=jax.ShapeDtypeStruct((T, D), F32),
        scratch_shapes=[pltpu.VMEM((SUBLANES + POOL_HALO + TS, D), F32),
                        pltpu.VMEM((SUBLANES + POOL_HALO + TS, D // len(POOL_WINDOWS)), F32),
                        pltpu.VMEM((SUBLANES + POOL_HALO + TS, D // len(POOL_WINDOWS)), F32)],
        compiler_params=_params("arbitrary", "arbitrary"),
    )(x, x, norm_w.reshape(1, D), pw, pool_scale.reshape(1, D))


def _chunk(S):
    return 256 if S % 256 == 0 else 128


def kernel(x, rel_bias, mix_norm_e, w_in_e, q_norm_e, k_norm_e, conv_w_e, conv_b_e, conv_ln_g_e, conv_ln_b_e,
           w_out_e, mix_norm_o, pool_w_o, pool_scale_o, ffn_norm, router_group_w, router_group_b,
           router_expert_w, router_expert_b, w_gate, w_up, w_down):
    B, S, D = x.shape
    T = B * S
    depth = ffn_norm.shape[0]
    n_heads = rel_bias.shape[1]
    idx_heads = (w_in_e.shape[2] - n_heads * HEAD_DIM - 2 * HEAD_DIM - IDX_DIM - 2 * conv_w_e.shape[2]) \
        // (IDX_DIM + 1)
    k_top = min(INDEX_TOPK, S // 4)
    C = _chunk(S)
    xf = x.reshape(T, D)
    bias_tiles, bias_stat = _rel_bias_tiles(rel_bias, C)
    expert_w = None
    for l in range(depth):
        i = l // 2
        if l % 2 == 0:
            q, k, v, iq, ik, iw, u = _in_proj(xf, mix_norm_e[i], w_in_e[i], q_norm_e[i], k_norm_e[i],
                                              n_heads, idx_heads, conv_w_e.shape[2])
            cast = (w_gate, w_up, w_down) if expert_w is None else ()
            if any(_cast_slabs(w, B * (S // C)) is None for w in cast):
                cast = ()
            attn, done = _attention(q, k, v, iq, ik, iw, bias_tiles, bias_stat, B, S, k_top, C, cast)
            expert_w = done if cast else expert_w
            conv = _conv_module(u, conv_w_e[i], conv_b_e[i], conv_ln_g_e[i], conv_ln_b_e[i], B, S)
            xf = _out_proj(xf, attn, conv, w_out_e[i])
        else:
            xf = _pool_layer(xf, mix_norm_o[i], pool_w_o[i], pool_scale_o[i], B, S)
        if expert_w is None:
            expert_w = [w.astype(BF16) for w in (w_gate, w_up, w_down)]
        xf = _moe(xf, ffn_norm[l], router_group_w[l], router_group_b[l], router_expert_w[l], router_expert_b[l],
                  *expert_w, layer=l)
    return xf.reshape(B, S, D)
```

```python
import functools
import math

import jax
import jax.numpy as jnp
from jax import lax
from jax.experimental import pallas as pl
from jax.experimental.pallas import tpu as pltpu

F32 = jnp.float32
BF16 = jnp.bfloat16

NORM_EPS = 1e-6
HEAD_DIM = 128
IDX_DIM = 64
INDEX_TOPK = 256
REL_BUCKETS = 32
REL_MAX_DIST = 128
POOL_WINDOWS = (2, 4, 8, 16)
EXPERTS_PER_GROUP = 4
LANES = 128
VMEM_LIMIT = 56 * 1024 * 1024
NEG = -1e30
INT_MIN = -(2 ** 31)


def _dot(a, b):
    return jnp.dot(a, b, preferred_element_type=F32)


def _dot_nt(a, b):
    return lax.dot_general(a, b, (((1,), (1,)), ((), ())), preferred_element_type=F32)


def _rms(x, w):
    return x * lax.rsqrt(jnp.mean(x * x, axis=-1, keepdims=True) + NORM_EPS) * w


def _params(*sem):
    return pltpu.CompilerParams(dimension_semantics=sem, vmem_limit_bytes=VMEM_LIMIT)


def _resident(shape):
    nd = len(shape)
    return pl.BlockSpec(shape, lambda *_: (0,) * nd, pipeline_mode=pl.Buffered(1))


def _in_proj_kernel(x_ref, nw_ref, qn_ref, kn_ref, wq_ref, wkv_ref, wiq_ref, wikw_ref, wa_ref, wg_ref,
                    q_ref, k_ref, v_ref, iq_ref, ik_ref, iw_ref, u_ref, *, q_scale, iw_scale):
    h = _rms(x_ref[...], nw_ref[...]).astype(BF16)
    n_pairs = q_ref.shape[0] // 2
    qn = qn_ref[...] * q_scale
    for c in range(n_pairs):
        qq = _dot(h, wq_ref[:, c * 256:(c + 1) * 256])
        for s in range(2):
            qh = qq[:, s * HEAD_DIM:(s + 1) * HEAD_DIM]
            q_ref[2 * c + s] = _rms(qh, qn).astype(BF16)
    kv = _dot(h, wkv_ref[...])
    k_ref[...] = _rms(kv[:, :HEAD_DIM], kn_ref[...]).astype(BF16)
    v_ref[...] = kv[:, HEAD_DIM:].astype(BF16)
    for c in range(iq_ref.shape[0] // 2):
        r = _dot(h, wiq_ref[:, c * 256:(c + 1) * 256])
        iq_ref[2 * c] = r[:, :LANES].astype(BF16)
        iq_ref[2 * c + 1] = r[:, LANES:].astype(BF16)
    r = _dot(h, wikw_ref[...])
    ik_ref[...] = r[:, :LANES].astype(BF16)
    iw_ref[...] = r[:, LANES:] * iw_scale
    for c in range(u_ref.shape[1] // 256):
        cs = slice(c * 256, (c + 1) * 256)
        a = _dot(h, wa_ref[:, cs])
        g = _dot(h, wg_ref[:, cs])
        u_ref[:, cs] = (a * jax.nn.sigmoid(g)).astype(BF16)


def _split_w_in_kernel(w_ref, wq_ref, wkv_ref, wiq_ref, wikw_ref, wa_ref, wg_ref, *, idx_heads):
    o = 0
    for ref in (wq_ref, wkv_ref, wiq_ref):
        n = ref.shape[1]
        ref[...] = w_ref[:, o:o + n].astype(BF16)
        o += n
    wik = w_ref[:, o:o + IDX_DIM]
    wiw = w_ref[:, o + IDX_DIM:o + IDX_DIM + idx_heads]
    o += IDX_DIM + idx_heads
    pad = jnp.zeros((wik.shape[0], LANES - idx_heads), F32)
    wikw_ref[...] = jnp.concatenate([wik, wik, wiw, pad], axis=1).astype(BF16)
    for ref in (wa_ref, wg_ref):
        n = ref.shape[1]
        ref[...] = w_ref[:, o:o + n].astype(BF16)
        o += n


def _split_w_in(w_in, n_heads, idx_heads, conv_ch, tk=256):
    D, n_in = w_in.shape
    widths = (n_heads * HEAD_DIM, 2 * HEAD_DIM, idx_heads * IDX_DIM, 2 * LANES, conv_ch, conv_ch)
    assert n_in == sum(widths) - 2 * LANES + IDX_DIM + idx_heads and 2 * IDX_DIM == LANES and D % tk == 0
    return pl.pallas_call(
        functools.partial(_split_w_in_kernel, idx_heads=idx_heads),
        grid=(D // tk,),
        in_specs=[pl.BlockSpec((tk, n_in), lambda i: (i, 0))],
        out_specs=[pl.BlockSpec((tk, w), lambda i: (i, 0)) for w in widths],
        out_shape=[jax.ShapeDtypeStruct((D, w), BF16) for w in widths],
        compiler_params=_params("arbitrary"),
    )(w_in)


def _in_proj(x, norm_w, w_in, q_norm, k_norm, n_heads, idx_heads, conv_ch, tm=256):
    T, D = x.shape
    iq_w = idx_heads * IDX_DIM
    ws = _split_w_in(w_in, n_heads, idx_heads, conv_ch)
    row = lambda w: pl.BlockSpec((tm, w), lambda i: (i, 0))
    heads = lambda n: pl.BlockSpec((n, tm, LANES), lambda i: (0, i, 0))
    kern = functools.partial(_in_proj_kernel, q_scale=HEAD_DIM ** -0.5,
                             iw_scale=(idx_heads ** -0.5) * (IDX_DIM ** -0.5))
    return pl.pallas_call(
        kern,
        grid=(T // tm,),
        in_specs=[row(D), _resident((1, D)), _resident((1, HEAD_DIM)), _resident((1, HEAD_DIM))]
                 + [_resident(w.shape) for w in ws],
        out_specs=[heads(n_heads), row(HEAD_DIM), row(HEAD_DIM), heads(iq_w // LANES), row(LANES), row(LANES),
                   row(conv_ch)],
        out_shape=[jax.ShapeDtypeStruct((n_heads, T, HEAD_DIM), BF16),
                   jax.ShapeDtypeStruct((T, HEAD_DIM), BF16),
                   jax.ShapeDtypeStruct((T, HEAD_DIM), BF16),
                   jax.ShapeDtypeStruct((iq_w // LANES, T, LANES), BF16),
                   jax.ShapeDtypeStruct((T, LANES), BF16),
                   jax.ShapeDtypeStruct((T, LANES), F32),
                   jax.ShapeDtypeStruct((T, conv_ch), BF16)],
        compiler_params=_params("arbitrary"),
    )(x, norm_w.reshape(1, D), q_norm.reshape(1, HEAD_DIM), k_norm.reshape(1, HEAD_DIM), *ws)


def _rel_bias_kernel(rb_ref, o_ref, stat_ref):
    _, n_heads, C, _ = o_ref.shape
    tau = lax.broadcasted_iota(jnp.int32, (C, C), 0)
    sig = lax.broadcasted_iota(jnp.int32, (C, C), 1)
    max_exact = REL_BUCKETS // 2
    for kind in range(2):
        d = tau - sig + kind * C
        n = jnp.maximum(d, 0)
        nf = jnp.maximum(n, 1).astype(F32)
        large = max_exact + (jnp.log(nf / max_exact) / math.log(REL_MAX_DIST / max_exact)
                             * (REL_BUCKETS - max_exact)).astype(jnp.int32)
        large = jnp.minimum(large, REL_BUCKETS - 1)
        bucket = jnp.where(n < max_exact, n, large)
        for h in range(n_heads):
            b = jnp.zeros((C, C), F32)
            for bk in range(REL_BUCKETS):
                b = jnp.where(bucket == bk, rb_ref[bk, h], b)
            b = b - rb_ref[REL_BUCKETS - 1, h]
            if kind == 0:
                b = jnp.where(d < 0, NEG, b)
            o_ref[kind, h] = b
    for h in range(n_heads):
        hi = rb_ref[0, h]
        lo = rb_ref[0, h]
        for bk in range(1, REL_BUCKETS):
            hi = jnp.maximum(hi, rb_ref[bk, h])
            lo = jnp.minimum(lo, rb_ref[bk, h])
        stat_ref[0, h] = hi - rb_ref[REL_BUCKETS - 1, h]
        stat_ref[1, h] = lo - rb_ref[REL_BUCKETS - 1, h]


def _rel_bias_tiles(rel_bias, C):
    n_heads = rel_bias.shape[1]
    assert C >= REL_MAX_DIST
    return pl.pallas_call(
        _rel_bias_kernel,
        in_specs=[pl.BlockSpec(memory_space=pltpu.SMEM)],
        out_specs=[pl.BlockSpec(memory_space=pltpu.VMEM), pl.BlockSpec(memory_space=pltpu.SMEM)],
        out_shape=[jax.ShapeDtypeStruct((2, n_heads, C, C), F32), jax.ShapeDtypeStruct((2, n_heads), F32)],
        compiler_params=pltpu.CompilerParams(vmem_limit_bytes=VMEM_LIMIT),
    )(rel_bias)


SHIFT_SPAN_LIMIT = 60.0


TILE_GROUP = 4


def _grouped_loop(n, fn, group):
    def body(k, carry):
        for u in range(group):
            fn(group * k + u)
        return carry

    lax.fori_loop(0, n // group, body, 0)
    base = (n // group) * group
    size = group // 2
    while size >= 1:
        take = ((n - base) & size) != 0

        @pl.when(take)
        def _(base=base, size=size):
            for u in range(size):
                fn(base + u)
        base = base + jnp.where(take, size, 0)
        size //= 2


def _order_key(x):
    bits = pltpu.bitcast(x, jnp.int32)
    return bits ^ ((bits >> 31) & 0x7FFFFFFF)


def _row_to_col(row):
    C = row.shape[1]
    halves = []
    for part in (row >> 16, row & 0xFFFF):
        halves.append(jnp.broadcast_to(part.astype(F32), (LANES, C)).T[:, 0:1].astype(jnp.int32))
    return (halves[0] << 16) | halves[1]


def _attn_kernel(bstat_ref, q_ref, k_ref, v_ref, iq_ref, ik_ref, iw_ref, bias_ref, *rest, k_top, n_cast):
    cast_in, (o_ref, *cast_out) = rest[:n_cast], rest[n_cast:2 * n_cast + 1]
    key_sc, keyt_sc, keyt16_sc, iqm_sc, vx_sc, kmax_sc, shift_sc, acc_sc = rest[2 * n_cast + 1:]
    for src, dst in zip(cast_in, cast_out):
        dst[...] = src[...].astype(BF16)
    n_heads, C, _ = q_ref.shape
    idx_heads = iqm_sc.shape[0]
    idx_bits = max(1, (key_sc.shape[0] * C - 1).bit_length())
    qi = pl.program_id(1)
    nkv = qi + 1

    @pl.when(qi == 0)
    def _():
        vx_sc[:, :HEAD_DIM] = v_ref[...]
        vx_sc[:, HEAD_DIM:] = jnp.ones((vx_sc.shape[0], HEAD_DIM), BF16)
        kf = k_ref[...].astype(F32)
        k2 = jnp.sum(kf * kf, axis=-1, keepdims=True)
        kmax_sc[...] = jnp.broadcast_to(jnp.sqrt(jnp.max(k2, axis=0, keepdims=True)), kmax_sc.shape)

    lane = lax.broadcasted_iota(jnp.int32, (C, LANES), 1)
    for p in range(idx_heads // 2):
        qp = iq_ref[p].astype(F32)
        iqm_sc[2 * p] = jnp.where(lane < IDX_DIM, qp, 0.0).astype(BF16)
        iqm_sc[2 * p + 1] = jnp.where(lane >= IDX_DIM, qp, 0.0).astype(BF16)
    iw = iw_ref[...]
    tau = lax.broadcasted_iota(jnp.int32, (C, C), 0)
    sig = lax.broadcasted_iota(jnp.int32, (C, C), 1)

    def score_tile(j):
        off = pl.multiple_of(j * C, C)
        ikc = ik_ref[pl.ds(off, C), :]
        acc = jnp.zeros((C, C), F32)
        for hh in range(idx_heads):
            s = _dot_nt(iqm_sc[hh], ikc)
            acc = acc + jnp.maximum(s, 0.0) * iw[:, hh:hh + 1]
        acc = jnp.where(jnp.logical_and(j == qi, sig > tau), -jnp.inf, acc)
        key_sc[j] = _order_key(acc)
        kt = _order_key(acc.T)
        keyt_sc[j] = kt
        keyt16_sc[j] = (kt >> 16).astype(jnp.int16)

    _grouped_loop(nkv, score_tile, TILE_GROUP)

    SUB = 32

    def count(hit):
        def body(j, cnt):
            for r in range(C // SUB):
                cnt = cnt + hit(j, r)
            return cnt
        cnt = lax.fori_loop(0, nkv, body, jnp.zeros((SUB, C), F32))
        return jnp.sum(cnt, axis=0, keepdims=True)

    def keyt(j, r):
        return keyt_sc[j, pl.ds(r * SUB, SUB), :]

    def count16(cand16):
        one = jnp.ones((SUB, C), jnp.int16)
        zero = jnp.zeros((SUB, C), jnp.int16)

        def body(j, cnt):
            for r in range(C // SUB):
                cnt = cnt + jnp.where(keyt16_sc[j, pl.ds(r * SUB, SUB), :] >= cand16, one, zero)
            return cnt
        cnt = lax.fori_loop(0, nkv, body, zero)
        return jnp.sum(cnt.astype(jnp.int32).astype(F32), axis=0, keepdims=True)

    def bit16_body(b, carry):
        res, n_res = carry
        cand = res ^ lax.shift_left(jnp.int32(1), 31 - b)
        tot = count16((cand >> 16).astype(jnp.int16))
        ok = tot >= k_top
        return jnp.where(ok, cand, res), jnp.where(ok, tot, n_res)

    searched = nkv * C > k_top
    carry = (jnp.full((1, C), INT_MIN, jnp.int32), jnp.zeros((1, C), F32))
    top_row, n_top = lax.fori_loop(0, jnp.where(searched, 16, 0), bit16_body, carry)

    top16 = top_row >> 16

    def low_body(j, c):
        kt = keyt_sc[j]
        hi = kt >> 16
        lo = (kt & 0xFFFF) - 32768
        lo = jnp.where(hi == top16, lo, jnp.where(hi > top16, 32767, -32768))
        keyt16_sc[j] = lo.astype(jnp.int16)
        return c

    lax.fori_loop(0, jnp.where(searched, nkv, 0), low_body, 0)

    def bit16_low_body(b, carry):
        res, n_res = carry
        cand = res ^ lax.shift_left(jnp.int32(1), 31 - b)
        tot = count16(((cand & 0xFFFF) - 32768).astype(jnp.int16))
        ok = tot >= k_top
        return jnp.where(ok, cand, res), jnp.where(ok, tot, n_res)

    thr_row, n_thr = lax.fori_loop(16, jnp.where(searched, 32, 16), bit16_low_body, (top_row, n_top))
    thr = _row_to_col(thr_row)

    @pl.when(jnp.max(n_thr) > k_top)
    def _():
        need = k_top - count(lambda j, r: jnp.where(keyt(j, r) > thr_row, 1.0, 0.0))
        s_sub = lax.broadcasted_iota(jnp.int32, (SUB, C), 0)

        def idx_body(b, last):
            cand = last | lax.shift_left(jnp.int32(1), idx_bits - 1 - b)
            below = count(lambda j, r: jnp.where(
                keyt(j, r) == thr_row, jnp.where(j * C + r * SUB + s_sub < cand, 1.0, 0.0), 0.0))
            return jnp.where(below < need, cand, last)

        last = _row_to_col(lax.fori_loop(0, idx_bits, idx_body, jnp.zeros((1, C), jnp.int32)))

        def drop_body(j, carry):
            kk = key_sc[j]
            key_sc[j] = jnp.where(kk == thr, jnp.where(j * C + sig > last, kk - 1, kk), kk)
            return carry

        lax.fori_loop(0, nkv, drop_body, 0)

    kmax = kmax_sc[0:1, 0:1] * 1.001
    worst = jnp.zeros((C, 1), F32)
    for h in range(n_heads):
        qf = q_ref[h].astype(F32)
        bound = jnp.sqrt(jnp.sum(qf * qf, axis=-1, keepdims=True)) * kmax
        shift_sc[h] = jnp.broadcast_to(bound + bstat_ref[0, h], (C, LANES))
        worst = jnp.maximum(worst, 2.0 * bound + (bstat_ref[0, h] - bstat_ref[1, h]))
    loose = jnp.max(worst) > SHIFT_SPAN_LIMIT

    def logits(j, h, kind):
        off = pl.multiple_of(j * C, C)
        lg = _dot_nt(q_ref[h], k_ref[pl.ds(off, C), :])
        return lg if kind is None else lg + bias_ref[kind, h]

    def near_tiles(fn):
        @pl.when(qi > 0)
        def _():
            fn(qi - 1, 1)
        fn(qi, 0)

    def far_tiles(fn, group=TILE_GROUP):
        _grouped_loop(jnp.maximum(qi - 1, 0), lambda j: fn(j, None), group)

    @pl.when(loose)
    def _():
        for h in range(n_heads):
            shift_sc[h] = jnp.full((C, LANES), NEG, F32)

        def max_tile(j, kind):
            sel = key_sc[j] >= thr
            for h in range(n_heads):
                lg = jnp.where(sel, logits(j, h, kind), NEG)
                m = jnp.max(lg, axis=-1, keepdims=True)
                shift_sc[h] = jnp.maximum(shift_sc[h], jnp.broadcast_to(m, (C, LANES)))

        far_tiles(max_tile, group=1)
        near_tiles(max_tile)

    acc_sc[...] = jnp.zeros(acc_sc.shape, F32)

    def attn_tile(j, kind):
        off = pl.multiple_of(j * C, C)
        vx = vx_sc[pl.ds(off, C), :]
        sel = key_sc[j] >= thr
        for h in range(n_heads):
            sh = jnp.concatenate([shift_sc[h]] * (C // LANES), axis=1)
            p = jnp.where(sel, jnp.exp(logits(j, h, kind) - sh), 0.0)
            acc_sc[h] += _dot(p.astype(BF16), vx)

    far_tiles(attn_tile)
    near_tiles(attn_tile)
    for h in range(n_heads):
        a = acc_sc[h]
        o_ref[:, h * HEAD_DIM:(h + 1) * HEAD_DIM] = (a[:, :HEAD_DIM] / a[:, HEAD_DIM:]).astype(BF16)


CAST_SLAB_ELEMS = 1 << 20


def _cast_slabs(w, steps):
    rows = math.prod(w.shape[:-1])
    if rows % (steps * 16) != 0 or rows // steps * w.shape[-1] > CAST_SLAB_ELEMS:
        return None
    return w.reshape(steps, rows // steps, w.shape[-1])


def _attention(q, k, v, iq, ik, iw, bias_tiles, bias_stat, B, S, k_top, C, cast=()):
    n_heads, T, _ = q.shape
    n_pairs = iq.shape[0]
    nq = S // C
    slabs = [_cast_slabs(w, B * nq) for w in cast]
    kern = functools.partial(_attn_kernel, k_top=k_top, n_cast=len(slabs))
    heads = lambda n: pl.BlockSpec((n, C, LANES), lambda b, i: (0, b * nq + i, 0))
    seq = pl.BlockSpec((S, LANES), lambda b, i: (b, 0))
    slab_specs = [pl.BlockSpec((None,) + w.shape[1:], lambda b, i: (b * nq + i, 0, 0)) for w in slabs]
    outs = pl.pallas_call(
        kern,
        grid=(B, nq),
        in_specs=[pl.BlockSpec(memory_space=pltpu.SMEM),
                  heads(n_heads), seq, seq, heads(n_pairs), seq,
                  pl.BlockSpec((C, LANES), lambda b, i: (b * nq + i, 0)),
                  _resident(bias_tiles.shape)] + slab_specs,
        out_specs=[pl.BlockSpec((C, n_heads * HEAD_DIM), lambda b, i: (b * nq + i, 0))] + slab_specs,
        out_shape=[jax.ShapeDtypeStruct((T, n_heads * HEAD_DIM), BF16)]
                  + [jax.ShapeDtypeStruct(w.shape, BF16) for w in slabs],
        scratch_shapes=[pltpu.VMEM((nq, C, C), jnp.int32),
                        pltpu.VMEM((nq, C, C), jnp.int32),
                        pltpu.VMEM((nq, C, C), jnp.int16),
                        pltpu.VMEM((2 * n_pairs, C, LANES), BF16),
                        pltpu.VMEM((S, 2 * HEAD_DIM), BF16),
                        pltpu.VMEM((8, LANES), F32),
                        pltpu.VMEM((n_heads, C, LANES), F32),
                        pltpu.VMEM((n_heads, C, 2 * HEAD_DIM), F32)],
        compiler_params=_params("arbitrary", "arbitrary"),
    )(bias_stat, q, k, v, iq, ik, iw, bias_tiles, *slabs)
    return outs[0], [o.reshape(w.shape) for o, w in zip(outs[1:], cast)]


CONV_HALO = 32


SUBLANES = 8
CONV_SLAB = 64


def _conv_kernel(u_ref, w_ref, cb_ref, g_ref, b_ref, o_ref, cp_sc, y_sc, halo_sc, *, width):
    TS, CH = u_ref.shape
    n_rows = CONV_HALO + TS
    i = pl.program_id(1)

    @pl.when(jnp.logical_and(pl.program_id(0) == 0, i == 0))
    def _():
        halo_sc[...] = jnp.zeros(halo_sc.shape, F32)

    halo = jnp.where(i > 0, halo_sc[...], 0.0)
    halo_sc[...] = u_ref[TS - CONV_HALO:, :].astype(F32)
    pad = jnp.zeros((SUBLANES, LANES), F32)
    for c in range(CH // LANES):
        cs = slice(c * LANES, (c + 1) * LANES)
        col = jnp.concatenate([halo[:, cs], u_ref[:, cs].astype(F32), pad], axis=0)
        for r in range(SUBLANES):
            cp_sc[r, :, cs] = col[r:r + n_rows]

    base = CONV_HALO - (width - 1)
    phases = {}
    for j in range(width):
        q, r = divmod(base + j, SUBLANES)
        phases.setdefault(r, []).append((q, j))

    def slab(s, carry):
        t0 = pl.multiple_of(s * CONV_SLAB, CONV_SLAB)
        for c in range(CH // LANES):
            cs = slice(c * LANES, (c + 1) * LANES)
            acc = jnp.zeros((CONV_SLAB, LANES), F32)
            for r, taps in phases.items():
                q_lo, q_hi = taps[0][0], taps[-1][0]
                win = cp_sc[r, pl.ds(q_lo * SUBLANES + t0, CONV_SLAB + (q_hi - q_lo) * SUBLANES), cs]
                for q, j in taps:
                    off = (q - q_lo) * SUBLANES
                    acc = acc + w_ref[j:j + 1, cs] * win[off:off + CONV_SLAB]
            y_sc[pl.ds(t0, CONV_SLAB), cs] = acc + cb_ref[:, cs]
        return carry

    lax.fori_loop(0, TS // CONV_SLAB, slab, 0)
    y = y_sc[...]
    mu = jnp.mean(y, axis=-1, keepdims=True)
    yc = y - mu
    var = jnp.mean(yc * yc, axis=-1, keepdims=True)
    yn = yc * lax.rsqrt(var + NORM_EPS) * g_ref[...] + b_ref[...]
    o_ref[...] = (yn * jax.nn.sigmoid(yn)).astype(BF16)


def _conv_module(u, conv_w, conv_b, ln_g, ln_b, B, S, TS=256):
    T, CH = u.shape
    width = conv_w.shape[0]
    assert width - 1 <= CONV_HALO
    ns = S // TS
    wpad = jnp.zeros((CONV_HALO, CH), F32).at[:width].set(conv_w)
    vec = lambda a: a.reshape(1, CH)
    kern = functools.partial(_conv_kernel, width=width)
    return pl.pallas_call(
        kern,
        grid=(B, ns),
        in_specs=[pl.BlockSpec((TS, CH), lambda b, i: (b * ns + i, 0)),
                  _resident((CONV_HALO, CH)), _resident((1, CH)), _resident((1, CH)), _resident((1, CH))],
        out_specs=pl.BlockSpec((TS, CH), lambda b, i: (b * ns + i, 0)),
        out_shape=jax.ShapeDtypeStruct((T, CH), BF16),
        scratch_shapes=[pltpu.VMEM((SUBLANES, CONV_HALO + TS, CH), F32),
                        pltpu.VMEM((TS, CH), F32),
                        pltpu.VMEM((CONV_HALO, CH), F32)],
        compiler_params=_params("arbitrary", "arbitrary"),
    )(u, wpad, vec(conv_b), vec(ln_g), vec(ln_b))


def _out_proj_kernel(x_ref, a_ref, c_ref, wa_ref, wc_ref, o_ref):
    o_ref[...] = x_ref[...] + _dot(a_ref[...], wa_ref[...]) + _dot(c_ref[...], wc_ref[...])


def _out_proj(x, attn, conv, w_out, tm=512):
    T, D = x.shape
    aw, cw = attn.shape[1], conv.shape[1]
    wa = w_out[:aw].astype(BF16)
    wc = w_out[aw:].astype(BF16)
    row = lambda w: pl.BlockSpec((tm, w), lambda i: (i, 0))
    return pl.pallas_call(
        _out_proj_kernel,
        grid=(T // tm,),
        in_specs=[row(D), row(aw), row(cw), _resident(wa.shape), _resident(wc.shape)],
        out_specs=row(D),
        out_shape=jax.ShapeDtypeStruct((T, D), F32),
        compiler_params=_params("arbitrary"),
    )(x, attn, conv, wa, wc)


def _first_max(vals):
    m = vals[0]
    for v in vals[1:]:
        m = jnp.maximum(m, v)
    idx = jnp.full(m.shape, len(vals) - 1, jnp.int32)
    for k in range(len(vals) - 2, -1, -1):
        idx = jnp.where(vals[k] == m, k, idx)
    return m, idx


def _softmax_cols(cols):
    m = cols[0]
    for c in cols[1:]:
        m = jnp.maximum(m, c)
    e = [jnp.exp(c - m) for c in cols]
    s = e[0]
    for c in e[1:]:
        s = s + c
    return [c / s for c in e]


GROUP_ROWS = 8
MOE_CHUNK = 128


def _split3(a):
    hi = a.astype(BF16)
    r = a - hi.astype(F32)
    mid = r.astype(BF16)
    lo = (r - mid.astype(F32)).astype(BF16)
    return hi, mid, lo


def _moe_kernel(x_ref, nw_ref, wrt_ref, brt_ref, wg_ref, wu_ref, wd_ref, o_ref,
                hs_sc, p_sc, cs_sc, seg_sm, *, n_groups):
    TM, D = x_ref.shape
    e = pl.program_id(1)
    n_exp = pl.num_programs(1)

    @pl.when(e == 0)
    def _():
        x = x_ref[...]
        scale = lax.rsqrt(jnp.mean(x * x, axis=-1, keepdims=True) + NORM_EPS)

        def h_cols(c0):
            cs = slice(c0, c0 + 512)
            return (x_ref[:, cs] * scale * nw_ref[:, cs]).astype(BF16)

        lg = brt_ref[...] + jnp.zeros((1, TM), F32)
        for c0 in range(0, D, 512):
            lg = lg + _dot_nt(wrt_ref[:, c0:c0 + 512], h_cols(c0))
        row = lambda k: lg[k:k + 1, :]
        g_prob = _softmax_cols([row(g) for g in range(n_groups)])
        g_p, g_idx = _first_max(g_prob)
        e_logit = []
        for k in range(EXPERTS_PER_GROUP):
            v = row(GROUP_ROWS + k)
            for g in range(1, n_groups):
                v = jnp.where(g_idx == g, row(GROUP_ROWS + g * EXPERTS_PER_GROUP + k), v)
            e_logit.append(v)
        e_prob = _softmax_cols(e_logit)
        p1, i1 = _first_max(e_prob)
        rest = [jnp.where(i1 == k, -1.0, e_prob[k]) for k in range(EXPERTS_PER_GROUP)]
        p2, i2 = _first_max(rest)
        den = p1 + p2
        base = g_idx * EXPERTS_PER_GROUP
        e_io = lax.broadcasted_iota(jnp.int32, (LANES, TM), 0)
        comb_t = (jnp.where(e_io == base + i1, g_p * (p1 / den), 0.0)
                  + jnp.where(e_io == base + i2, g_p * (p2 / den), 0.0))

        g_io = lax.broadcasted_iota(jnp.int32, (GROUP_ROWS, TM), 0)
        onehot_t = jnp.where(g_io == g_idx, 1.0, 0.0)
        upper = jnp.where(lax.broadcasted_iota(jnp.int32, (LANES, LANES), 0)
                          < lax.broadcasted_iota(jnp.int32, (LANES, LANES), 1), 1.0, 0.0).astype(BF16)
        before = jnp.zeros((GROUP_ROWS, 1), F32)
        ranks = []
        for b0 in range(0, TM, LANES):
            blk = onehot_t[:, b0:b0 + LANES]
            ranks.append(_dot(blk.astype(BF16), upper) + before)
            before = before + jnp.sum(blk, axis=1, keepdims=True)
        rank = jnp.concatenate(ranks, axis=1)
        start = jnp.int32(0)
        pos = jnp.zeros((1, TM), F32)
        for g in range(n_groups):
            seg_sm[g] = start
            pos = jnp.where(g_idx == g, start.astype(F32) + rank[g:g + 1, :], pos)
            start = start + jnp.sum(onehot_t[g:g + 1, :]).astype(jnp.int32)
        seg_sm[n_groups] = start
        pos = pos.astype(jnp.int32)

        rb = 256
        for r0 in range(0, TM, rb):
            r_io = lax.broadcasted_iota(jnp.int32, (rb, TM), 0) + r0
            p_sc[r0:r0 + rb, :] = jnp.where(r_io == pos, 1.0, 0.0).astype(BF16)
        p = p_sc[...]
        for c0 in range(0, D, 512):
            hs_sc[:, c0:c0 + 512] = _dot(p, h_cols(c0)).astype(BF16)
        cs = jnp.zeros((TM, LANES), F32)
        for part in _split3(comb_t):
            cs = cs + _dot_nt(p, part)
        cs_sc[...] = cs
        o_ref[...] = jnp.zeros(o_ref.shape, F32)

    g = e // EXPERTS_PER_GROUP
    start = seg_sm[g]
    end = seg_sm[g + 1]
    c_lo = start // MOE_CHUNK
    c_hi = jnp.where(end > start, (end + MOE_CHUNK - 1) // MOE_CHUNK, c_lo)

    def expert_rows(c, n_chunks):
        m = n_chunks * MOE_CHUNK
        r0 = pl.multiple_of(c * MOE_CHUNK, MOE_CHUNK)
        rows = hs_sc[pl.ds(r0, m), :]
        lane = lax.broadcasted_iota(jnp.int32, (m, LANES), 1)
        w = jnp.sum(jnp.where(lane == e, cs_sc[pl.ds(r0, m), :], 0.0), axis=-1, keepdims=True)
        a = jax.nn.silu(_dot(rows, wg_ref[0])) * _dot(rows, wu_ref[0]) * w
        o_ref[pl.ds(r0, m), :] += _dot(a.astype(BF16), wd_ref[0])

    def triple(k, carry):
        expert_rows(c_lo + 3 * k, 3)
        return carry

    n_chunks = c_hi - c_lo
    lax.fori_loop(0, n_chunks // 3, triple, 0)
    for rest in (1, 2):
        @pl.when(n_chunks % 3 == rest)
        def _():
            expert_rows(c_hi - rest, rest)

    @pl.when(e == n_exp - 1)
    def _():
        p = p_sc[...]
        for c0 in range(0, D, 512):
            ys = o_ref[:, c0:c0 + 512].astype(BF16)
            y = lax.dot_general(p, ys, (((0,), (0,)), ((), ())), preferred_element_type=F32)
            o_ref[:, c0:c0 + 512] = x_ref[:, c0:c0 + 512] + y


def _moe(x, norm_w, wg, bg, we, be, w_gate, w_up, w_down, layer, tm=1024):
    T, D = x.shape
    n_groups, n_exp = wg.shape[1], we.shape[1]
    ff = w_gate.shape[3]
    assert n_groups <= GROUP_ROWS and n_exp == n_groups * EXPERTS_PER_GROUP and T % tm == 0
    rows = GROUP_ROWS + n_exp
    wrt = jnp.zeros((rows, D), F32).at[:n_groups].set(wg.T).at[GROUP_ROWS:].set(we.T).astype(BF16)
    brt = jnp.zeros((rows, 1), F32).at[:n_groups, 0].set(bg).at[GROUP_ROWS:, 0].set(be)
    tile = pl.BlockSpec((tm, D), lambda i, e: (i, 0))
    return pl.pallas_call(
        functools.partial(_moe_kernel, n_groups=n_groups),
        grid=(T // tm, n_exp),
        in_specs=[tile, _resident((1, D)), _resident((rows, D)), _resident((rows, 1)),
                  pl.BlockSpec((None, 1, D, ff), lambda i, e: (layer, e, 0, 0)),
                  pl.BlockSpec((None, 1, D, ff), lambda i, e: (layer, e, 0, 0)),
                  pl.BlockSpec((None, 1, ff, D), lambda i, e: (layer, e, 0, 0))],
        out_specs=tile,
        out_shape=jax.ShapeDtypeStruct((T, D), F32),
        scratch_shapes=[pltpu.VMEM((tm, D), BF16),
                        pltpu.VMEM((tm, tm), BF16),
                        pltpu.VMEM((tm, LANES), F32),
                        pltpu.SMEM((GROUP_ROWS,), jnp.int32)],
        compiler_params=_params("arbitrary", "arbitrary"),
    )(x, norm_w.reshape(1, D), wrt, brt, w_gate, w_up, w_down)


POOL_HALO = 16


def _pool_kernel(x_ref, nw_ref, pw_ref, ps_ref, o_ref, hb, pa, pb):
    TS, D = x_ref.shape
    n_groups, pc, _ = pw_ref.shape
    i = pl.program_id(1)
    nw = nw_ref[...]
    top = SUBLANES + POOL_HALO
    n_rows = top + TS

    @pl.when(jnp.logical_and(pl.program_id(0) == 0, i == 0))
    def _():
        hb[...] = jnp.zeros(hb.shape, F32)
        pa[0:SUBLANES] = jnp.zeros((SUBLANES, pc), F32)
        pb[0:SUBLANES] = jnp.zeros((SUBLANES, pc), F32)

    hb[SUBLANES:top] = jnp.where(i > 0, hb[n_rows - POOL_HALO:n_rows], 0.0)
    hb[top:] = _rms(x_ref[...], nw)
    t = i * TS + lax.broadcasted_iota(jnp.int32, (TS, 1), 0)
    for g, w in enumerate(POOL_WINDOWS):
        cs = slice(g * pc, (g + 1) * pc)
        cur = hb[top:, cs]
        src, cols, step, dst = hb, cs, 1, pa
        while step < w:
            dst[SUBLANES:n_rows, :] = src[SUBLANES:n_rows, cols] + src[SUBLANES - step:n_rows - step, cols]
            src, cols, step, dst = dst, slice(None), 2 * step, (pb if dst is pa else pa)
        s = src[top:n_rows, cols]
        count = jnp.minimum(t + 1, w).astype(F32)
        d = s / count - cur
        mixed = _dot(d.astype(BF16), pw_ref[g])
        o_ref[:, cs] = x_ref[:, cs] + ps_ref[:, cs] * mixed


def _pool_layer(x, norm_w, pool_w, pool_scale, B, S, TS=256):
    T, D = x.shape
    assert len(POOL_WINDOWS) == pool_w.shape[0] and max(POOL_WINDOWS) - 1 <= POOL_HALO
    ns = S // TS
    pw = pool_w.astype(BF16)
    return pl.pallas_call(
        _pool_kernel,
        grid=(B, ns),
        in_specs=[pl.BlockSpec((TS, D), lambda b, i: (b * ns + i, 0)),
                  _resident((1, D)), _resident(pw.shape), _resident((1, D))],
        out_specs=pl.BlockSpec((TS, D), lambda b, i: (b * ns + i, 0)),
        out_shape=jax.ShapeDtypeStruct((T, D), F32),
        scratch_shapes=[pltpu.VMEM((SUBLANES + POOL_HALO + TS, D), F32),
                        pltpu.VMEM((SUBLANES + POOL_HALO + TS, D // len(POOL_WINDOWS)), F32),
                        pltpu.VMEM((SUBLANES + POOL_HALO + TS, D // len(POOL_WINDOWS)), F32)],
        compiler_params=_params("arbitrary", "arbitrary"),
    )(x, norm_w.reshape(1, D), pw, pool_scale.reshape(1, D))


def _chunk(S):
    return 256 if S % 256 == 0 else 128


def kernel(x, rel_bias, mix_norm_e, w_in_e, q_norm_e, k_norm_e, conv_w_e, conv_b_e, conv_ln_g_e, conv_ln_b_e,
           w_out_e, mix_norm_o, pool_w_o, pool_scale_o, ffn_norm, router_group_w, router_group_b,
           router_expert_w, router_expert_b, w_gate, w_up, w_down):
    B, S, D = x.shape
    T = B * S
    depth = ffn_norm.shape[0]
    n_heads = rel_bias.shape[1]
    idx_heads = (w_in_e.shape[2] - n_heads * HEAD_DIM - 2 * HEAD_DIM - IDX_DIM - 2 * conv_w_e.shape[2]) \
        // (IDX_DIM + 1)
    k_top = min(INDEX_TOPK, S // 4)
    C = _chunk(S)
    xf = x.reshape(T, D)
    bias_tiles, bias_stat = _rel_bias_tiles(rel_bias, C)
    expert_w = None
    for l in range(depth):
        i = l // 2
        if l % 2 == 0:
            q, k, v, iq, ik, iw, u = _in_proj(xf, mix_norm_e[i], w_in_e[i], q_norm_e[i], k_norm_e[i],
                                              n_heads, idx_heads, conv_w_e.shape[2])
            cast = (w_gate, w_up, w_down) if expert_w is None else ()
            if any(_cast_slabs(w, B * (S // C)) is None for w in cast):
                cast = ()
            attn, done = _attention(q, k, v, iq, ik, iw, bias_tiles, bias_stat, B, S, k_top, C, cast)
            expert_w = done if cast else expert_w
            conv = _conv_module(u, conv_w_e[i], conv_b_e[i], conv_ln_g_e[i], conv_ln_b_e[i], B, S)
            xf = _out_proj(xf, attn, conv, w_out_e[i])
        else:
            xf = _pool_layer(xf, mix_norm_o[i], pool_w_o[i], pool_scale_o[i], B, S)
        if expert_w is None:
            expert_w = [w.astype(BF16) for w in (w_gate, w_up, w_down)]
        xf = _moe(xf, ffn_norm[l], router_group_w[l], router_group_b[l], router_expert_w[l], router_expert_b[l],
                  *expert_w, layer=l)
    return xf.reshape(B, S, D)
```

```python
import functools
import math

import jax
import jax.numpy as jnp
from jax import lax
from jax.experimental import pallas as pl
from jax.experimental.pallas import tpu as pltpu

F32 = jnp.float32
BF16 = jnp.bfloat16

NORM_EPS = 1e-6
HEAD_DIM = 128
IDX_DIM = 64
INDEX_TOPK = 256
REL_BUCKETS = 32
REL_MAX_DIST = 128
POOL_WINDOWS = (2, 4, 8, 16)
EXPERTS_PER_GROUP = 4
LANES = 128
VMEM_LIMIT = 56 * 1024 * 1024
NEG = -1e30
INT_MIN = -(2 ** 31)


def _dot(a, b):
    return jnp.dot(a, b, preferred_element_type=F32)


def _dot_nt(a, b):
    return lax.dot_general(a, b, (((1,), (1,)), ((), ())), preferred_element_type=F32)


def _rms(x, w):
    return x * lax.rsqrt(jnp.mean(x * x, axis=-1, keepdims=True) + NORM_EPS) * w


def _params(*sem):
    return pltpu.CompilerParams(dimension_semantics=sem, vmem_limit_bytes=VMEM_LIMIT)


def _resident(shape):
    nd = len(shape)
    return pl.BlockSpec(shape, lambda *_: (0,) * nd, pipeline_mode=pl.Buffered(1))


def _in_proj_kernel(x_ref, nw_ref, qn_ref, kn_ref, wq_ref, wkv_ref, wiq_ref, wikw_ref, wa_ref, wg_ref,
                    q_ref, k_ref, v_ref, iq_ref, ik_ref, iw_ref, u_ref, *, q_scale, iw_scale):
    h = _rms(x_ref[...], nw_ref[...]).astype(BF16)
    n_pairs = q_ref.shape[0] // 2
    qn = qn_ref[...] * q_scale
    for c in range(n_pairs):
        qq = _dot(h, wq_ref[:, c * 256:(c + 1) * 256])
        for s in range(2):
            qh = qq[:, s * HEAD_DIM:(s + 1) * HEAD_DIM]
            q_ref[2 * c + s] = _rms(qh, qn).astype(BF16)
    kv = _dot(h, wkv_ref[...])
    k_ref[...] = _rms(kv[:, :HEAD_DIM], kn_ref[...]).astype(BF16)
    v_ref[...] = kv[:, HEAD_DIM:].astype(BF16)
    for c in range(iq_ref.shape[0] // 2):
        r = _dot(h, wiq_ref[:, c * 256:(c + 1) * 256])
        iq_ref[2 * c] = r[:, :LANES].astype(BF16)
        iq_ref[2 * c + 1] = r[:, LANES:].astype(BF16)
    r = _dot(h, wikw_ref[...])
    ik_ref[...] = r[:, :LANES].astype(BF16)
    iw_ref[...] = r[:, LANES:] * iw_scale
    for c in range(u_ref.shape[1] // 256):
        cs = slice(c * 256, (c + 1) * 256)
        a = _dot(h, wa_ref[:, cs])
        g = _dot(h, wg_ref[:, cs])
        u_ref[:, cs] = (a * jax.nn.sigmoid(g)).astype(BF16)


def _split_w_in_kernel(w_ref, wq_ref, wkv_ref, wiq_ref, wikw_ref, wa_ref, wg_ref, *, idx_heads):
    o = 0
    for ref in (wq_ref, wkv_ref, wiq_ref):
        n = ref.shape[1]
        ref[...] = w_ref[:, o:o + n].astype(BF16)
        o += n
    wik = w_ref[:, o:o + IDX_DIM]
    wiw = w_ref[:, o + IDX_DIM:o + IDX_DIM + idx_heads]
    o += IDX_DIM + idx_heads
    pad = jnp.zeros((wik.shape[0], LANES - idx_heads), F32)
    wikw_ref[...] = jnp.concatenate([wik, wik, wiw, pad], axis=1).astype(BF16)
    for ref in (wa_ref, wg_ref):
        n = ref.shape[1]
        ref[...] = w_ref[:, o:o + n].astype(BF16)
        o += n


def _split_w_in(w_in, n_heads, idx_heads, conv_ch, tk=256):
    D, n_in = w_in.shape
    widths = (n_heads * HEAD_DIM, 2 * HEAD_DIM, idx_heads * IDX_DIM, 2 * LANES, conv_ch, conv_ch)
    assert n_in == sum(widths) - 2 * LANES + IDX_DIM + idx_heads and 2 * IDX_DIM == LANES and D % tk == 0
    return pl.pallas_call(
        functools.partial(_split_w_in_kernel, idx_heads=idx_heads),
        grid=(D // tk,),
        in_specs=[pl.BlockSpec((tk, n_in), lambda i: (i, 0))],
        out_specs=[pl.BlockSpec((tk, w), lambda i: (i, 0)) for w in widths],
        out_shape=[jax.ShapeDtypeStruct((D, w), BF16) for w in widths],
        compiler_params=_params("arbitrary"),
    )(w_in)


def _in_proj(x, norm_w, w_in, q_norm, k_norm, n_heads, idx_heads, conv_ch, tm=256):
    T, D = x.shape
    iq_w = idx_heads * IDX_DIM
    ws = _split_w_in(w_in, n_heads, idx_heads, conv_ch)
    row = lambda w: pl.BlockSpec((tm, w), lambda i: (i, 0))
    heads = lambda n: pl.BlockSpec((n, tm, LANES), lambda i: (0, i, 0))
    kern = functools.partial(_in_proj_kernel, q_scale=HEAD_DIM ** -0.5,
                             iw_scale=(idx_heads ** -0.5) * (IDX_DIM ** -0.5))
    return pl.pallas_call(
        kern,
        grid=(T // tm,),
        in_specs=[row(D), _resident((1, D)), _resident((1, HEAD_DIM)), _resident((1, HEAD_DIM))]
                 + [_resident(w.shape) for w in ws],
        out_specs=[heads(n_heads), row(HEAD_DIM), row(HEAD_DIM), heads(iq_w // LANES), row(LANES), row(LANES),
                   row(conv_ch)],
        out_shape=[jax.ShapeDtypeStruct((n_heads, T, HEAD_DIM), BF16),
                   jax.ShapeDtypeStruct((T, HEAD_DIM), BF16),
                   jax.ShapeDtypeStruct((T, HEAD_DIM), BF16),
                   jax.ShapeDtypeStruct((iq_w // LANES, T, LANES), BF16),
                   jax.ShapeDtypeStruct((T, LANES), BF16),
                   jax.ShapeDtypeStruct((T, LANES), F32),
                   jax.ShapeDtypeStruct((T, conv_ch), BF16)],
        compiler_params=_params("arbitrary"),
    )(x, norm_w.reshape(1, D), q_norm.reshape(1, HEAD_DIM), k_norm.reshape(1, HEAD_DIM), *ws)


def _rel_bias_kernel(rb_ref, o_ref, stat_ref):
    _, n_heads, C, _ = o_ref.shape
    tau = lax.broadcasted_iota(jnp.int32, (C, C), 0)
    sig = lax.broadcasted_iota(jnp.int32, (C, C), 1)
    max_exact = REL_BUCKETS // 2
    for kind in range(2):
        d = tau - sig + kind * C
        n = jnp.maximum(d, 0)
        nf = jnp.maximum(n, 1).astype(F32)
        large = max_exact + (jnp.log(nf / max_exact) / math.log(REL_MAX_DIST / max_exact)
                             * (REL_BUCKETS - max_exact)).astype(jnp.int32)
        large = jnp.minimum(large, REL_BUCKETS - 1)
        bucket = jnp.where(n < max_exact, n, large)
        for h in range(n_heads):
            b = jnp.zeros((C, C), F32)
            for bk in range(REL_BUCKETS):
                b = jnp.where(bucket == bk, rb_ref[bk, h], b)
            b = b - rb_ref[REL_BUCKETS - 1, h]
            if kind == 0:
                b = jnp.where(d < 0, NEG, b)
            o_ref[kind, h] = b
    for h in range(n_heads):
        hi = rb_ref[0, h]
        lo = rb_ref[0, h]
        for bk in range(1, REL_BUCKETS):
            hi = jnp.maximum(hi, rb_ref[bk, h])
            lo = jnp.minimum(lo, rb_ref[bk, h])
        stat_ref[0, h] = hi - rb_ref[REL_BUCKETS - 1, h]
        stat_ref[1, h] = lo - rb_ref[REL_BUCKETS - 1, h]


def _rel_bias_tiles(rel_bias, C):
    n_heads = rel_bias.shape[1]
    assert C >= REL_MAX_DIST
    return pl.pallas_call(
        _rel_bias_kernel,
        in_specs=[pl.BlockSpec(memory_space=pltpu.SMEM)],
        out_specs=[pl.BlockSpec(memory_space=pltpu.VMEM), pl.BlockSpec(memory_space=pltpu.SMEM)],
        out_shape=[jax.ShapeDtypeStruct((2, n_heads, C, C), F32), jax.ShapeDtypeStruct((2, n_heads), F32)],
        compiler_params=pltpu.CompilerParams(vmem_limit_bytes=VMEM_LIMIT),
    )(rel_bias)


SHIFT_SPAN_LIMIT = 60.0


TILE_GROUP = 4


def _grouped_loop(n, fn, group):
    def body(k, carry):
        for u in range(group):
            fn(group * k + u)
        return carry

    lax.fori_loop(0, n // group, body, 0)
    base = (n // group) * group
    size = group // 2
    while size >= 1:
        take = ((n - base) & size) != 0

        @pl.when(take)
        def _(base=base, size=size):
            for u in range(size):
                fn(base + u)
        base = base + jnp.where(take, size, 0)
        size //= 2


def _order_key(x):
    bits = pltpu.bitcast(x, jnp.int32)
    return bits ^ ((bits >> 31) & 0x7FFFFFFF)


def _row_to_col(row):
    C = row.shape[1]
    halves = []
    for part in (row >> 16, row & 0xFFFF):
        halves.append(jnp.broadcast_to(part.astype(F32), (LANES, C)).T[:, 0:1].astype(jnp.int32))
    return (halves[0] << 16) | halves[1]


def _attn_kernel(bstat_ref, q_ref, k_ref, v_ref, iq_ref, ik_ref, iw_ref, bias_ref, *rest, k_top, n_cast):
    cast_in, (o_ref, *cast_out) = rest[:n_cast], rest[n_cast:2 * n_cast + 1]
    key_sc, keyt_sc, keyt16_sc, iqm_sc, vx_sc, kmax_sc, shift_sc, acc_sc = rest[2 * n_cast + 1:]
    for src, dst in zip(cast_in, cast_out):
        dst[...] = src[...].astype(BF16)
    n_heads, C, _ = q_ref.shape
    idx_heads = iqm_sc.shape[0]
    idx_bits = max(1, (key_sc.shape[0] * C - 1).bit_length())
    qi = pl.program_id(1)
    nkv = qi + 1

    @pl.when(qi == 0)
    def _():
        vx_sc[:, :HEAD_DIM] = v_ref[...]
        vx_sc[:, HEAD_DIM:] = jnp.ones((vx_sc.shape[0], HEAD_DIM), BF16)
        kf = k_ref[...].astype(F32)
        k2 = jnp.sum(kf * kf, axis=-1, keepdims=True)
        kmax_sc[...] = jnp.broadcast_to(jnp.sqrt(jnp.max(k2, axis=0, keepdims=True)), kmax_sc.shape)

    lane = lax.broadcasted_iota(jnp.int32, (C, LANES), 1)
    for p in range(idx_heads // 2):
        qp = iq_ref[p].astype(F32)
        iqm_sc[2 * p] = jnp.where(lane < IDX_DIM, qp, 0.0).astype(BF16)
        iqm_sc[2 * p + 1] = jnp.where(lane >= IDX_DIM, qp, 0.0).astype(BF16)
    iw = iw_ref[...]
    tau = lax.broadcasted_iota(jnp.int32, (C, C), 0)
    sig = lax.broadcasted_iota(jnp.int32, (C, C), 1)

    def score_tile(j):
        off = pl.multiple_of(j * C, C)
        ikc = ik_ref[pl.ds(off, C), :]
        acc = jnp.zeros((C, C), F32)
        for hh in range(idx_heads):
            s = _dot_nt(iqm_sc[hh], ikc)
            acc = acc + jnp.maximum(s, 0.0) * iw[:, hh:hh + 1]
        acc = jnp.where(jnp.logical_and(j == qi, sig > tau), -jnp.inf, acc)
        key_sc[j] = _order_key(acc)
        kt = _order_key(acc.T)
        keyt_sc[j] = kt
        keyt16_sc[j] = (kt >> 16).astype(jnp.int16)

    _grouped_loop(nkv, score_tile, TILE_GROUP)

    SUB = 32

    def count(hit):
        def body(j, cnt):
            for r in range(C // SUB):
                cnt = cnt + hit(j, r)
            return cnt
        cnt = lax.fori_loop(0, nkv, body, jnp.zeros((SUB, C), F32))
        return jnp.sum(cnt, axis=0, keepdims=True)

    def keyt(j, r):
        return keyt_sc[j, pl.ds(r * SUB, SUB), :]

    def count16(cand16):
        one = jnp.ones((SUB, C), jnp.int16)
        zero = jnp.zeros((SUB, C), jnp.int16)

        def body(j, cnt):
            for r in range(C // SUB):
                cnt = cnt + jnp.where(keyt16_sc[j, pl.ds(r * SUB, SUB), :] >= cand16, one, zero)
            return cnt
        cnt = lax.fori_loop(0, nkv, body, zero)
        return jnp.sum(cnt.astype(jnp.int32).astype(F32), axis=0, keepdims=True)

    def bit16_body(b, carry):
        res, n_res = carry
        cand = res ^ lax.shift_left(jnp.int32(1), 31 - b)
        tot = count16((cand >> 16).astype(jnp.int16))
        ok = tot >= k_top
        return jnp.where(ok, cand, res), jnp.where(ok, tot, n_res)

    searched = nkv * C > k_top
    carry = (jnp.full((1, C), INT_MIN, jnp.int32), jnp.zeros((1, C), F32))
    top_row, n_top = lax.fori_loop(0, jnp.where(searched, 16, 0), bit16_body, carry)

    top16 = top_row >> 16

    def low_body(j, c):
        kt = keyt_sc[j]
        hi = kt >> 16
        lo = (kt & 0xFFFF) - 32768
        lo = jnp.where(hi == top16, lo, jnp.where(hi > top16, 32767, -32768))
        keyt16_sc[j] = lo.astype(jnp.int16)
        return c

    lax.fori_loop(0, jnp.where(searched, nkv, 0), low_body, 0)

    def bit16_low_body(b, carry):
        res, n_res = carry
        cand = res ^ lax.shift_left(jnp.int32(1), 31 - b)
        tot = count16(((cand & 0xFFFF) - 32768).astype(jnp.int16))
        ok = tot >= k_top
        return jnp.where(ok, cand, res), jnp.where(ok, tot, n_res)

    thr_row, n_thr = lax.fori_loop(16, jnp.where(searched, 32, 16), bit16_low_body, (top_row, n_top))
    thr = _row_to_col(thr_row)

    @pl.when(jnp.max(n_thr) > k_top)
    def _():
        need = k_top - count(lambda j, r: jnp.where(keyt(j, r) > thr_row, 1.0, 0.0))
        s_sub = lax.broadcasted_iota(jnp.int32, (SUB, C), 0)

        def idx_body(b, last):
            cand = last | lax.shift_left(jnp.int32(1), idx_bits - 1 - b)
            below = count(lambda j, r: jnp.where(
                keyt(j, r) == thr_row, jnp.where(j * C + r * SUB + s_sub < cand, 1.0, 0.0), 0.0))
            return jnp.where(below < need, cand, last)

        last = _row_to_col(lax.fori_loop(0, idx_bits, idx_body, jnp.zeros((1, C), jnp.int32)))

        def drop_body(j, carry):
            kk = key_sc[j]
            key_sc[j] = jnp.where(kk == thr, jnp.where(j * C + sig > last, kk - 1, kk), kk)
            return carry

        lax.fori_loop(0, nkv, drop_body, 0)

    kmax = kmax_sc[0:1, 0:1] * 1.001
    worst = jnp.zeros((C, 1), F32)
    for h in range(n_heads):
        qf = q_ref[h].astype(F32)
        bound = jnp.sqrt(jnp.sum(qf * qf, axis=-1, keepdims=True)) * kmax
        shift_sc[h] = jnp.broadcast_to(bound + bstat_ref[0, h], (C, LANES))
        worst = jnp.maximum(worst, 2.0 * bound + (bstat_ref[0, h] - bstat_ref[1, h]))
    loose = jnp.max(worst) > SHIFT_SPAN_LIMIT

    def logits(j, h, kind):
        off = pl.multiple_of(j * C, C)
        lg = _dot_nt(q_ref[h], k_ref[pl.ds(off, C), :])
        return lg if kind is None else lg + bias_ref[kind, h]

    def near_tiles(fn):
        @pl.when(qi > 0)
        def _():
            fn(qi - 1, 1)
        fn(qi, 0)

    def far_tiles(fn, group=TILE_GROUP):
        _grouped_loop(jnp.maximum(qi - 1, 0), lambda j: fn(j, None), group)

    @pl.when(loose)
    def _():
        for h in range(n_heads):
            shift_sc[h] = jnp.full((C, LANES), NEG, F32)

        def max_tile(j, kind):
            sel = key_sc[j] >= thr
            for h in range(n_heads):
                lg = jnp.where(sel, logits(j, h, kind), NEG)
                m = jnp.max(lg, axis=-1, keepdims=True)
                shift_sc[h] = jnp.maximum(shift_sc[h], jnp.broadcast_to(m, (C, LANES)))

        far_tiles(max_tile, group=1)
        near_tiles(max_tile)

    acc_sc[...] = jnp.zeros(acc_sc.shape, F32)

    def attn_tile(j, kind):
        off = pl.multiple_of(j * C, C)
        vx = vx_sc[pl.ds(off, C), :]
        sel = key_sc[j] >= thr
        for h in range(n_heads):
            sh = jnp.concatenate([shift_sc[h]] * (C // LANES), axis=1)
            p = jnp.where(sel, jnp.exp(logits(j, h, kind) - sh), 0.0)
            acc_sc[h] += _dot(p.astype(BF16), vx)

    far_tiles(attn_tile)
    near_tiles(attn_tile)
    for h in range(n_heads):
        a = acc_sc[h]
        o_ref[:, h * HEAD_DIM:(h + 1) * HEAD_DIM] = (a[:, :HEAD_DIM] / a[:, HEAD_DIM:]).astype(BF16)


CAST_SLAB_ELEMS = 1 << 20


def _cast_slabs(w, steps):
    rows = math.prod(w.shape[:-1])
    if rows % (steps * 16) != 0 or rows // steps * w.shape[-1] > CAST_SLAB_ELEMS:
        return None
    return w.reshape(steps, rows // steps, w.shape[-1])


def _attention(q, k, v, iq, ik, iw, bias_tiles, bias_stat, B, S, k_top, C, cast=()):
    n_heads, T, _ = q.shape
    n_pairs = iq.shape[0]
    nq = S // C
    slabs = [_cast_slabs(w, B * nq) for w in cast]
    kern = functools.partial(_attn_kernel, k_top=k_top, n_cast=len(slabs))
    heads = lambda n: pl.BlockSpec((n, C, LANES), lambda b, i: (0, b * nq + i, 0))
    seq = pl.BlockSpec((S, LANES), lambda b, i: (b, 0))
    slab_specs = [pl.BlockSpec((None,) + w.shape[1:], lambda b, i: (b * nq + i, 0, 0)) for w in slabs]
    outs = pl.pallas_call(
        kern,
        grid=(B, nq),
        in_specs=[pl.BlockSpec(memory_space=pltpu.SMEM),
                  heads(n_heads), seq, seq, heads(n_pairs), seq,
                  pl.BlockSpec((C, LANES), lambda b, i: (b * nq + i, 0)),
                  _resident(bias_tiles.shape)] + slab_specs,
        out_specs=[pl.BlockSpec((C, n_heads * HEAD_DIM), lambda b, i: (b * nq + i, 0))] + slab_specs,
        out_shape=[jax.ShapeDtypeStruct((T, n_heads * HEAD_DIM), BF16)]
                  + [jax.ShapeDtypeStruct(w.shape, BF16) for w in slabs],
        scratch_shapes=[pltpu.VMEM((nq, C, C), jnp.int32),
                        pltpu.VMEM((nq, C, C), jnp.int32),
                        pltpu.VMEM((nq, C, C), jnp.int16),
                        pltpu.VMEM((2 * n_pairs, C, LANES), BF16),
                        pltpu.VMEM((S, 2 * HEAD_DIM), BF16),
                        pltpu.VMEM((8, LANES), F32),
                        pltpu.VMEM((n_heads, C, LANES), F32),
                        pltpu.VMEM((n_heads, C, 2 * HEAD_DIM), F32)],
        compiler_params=_params("arbitrary", "arbitrary"),
    )(bias_stat, q, k, v, iq, ik, iw, bias_tiles, *slabs)
    return outs[0], [o.reshape(w.shape) for o, w in zip(outs[1:], cast)]


CONV_HALO = 32


SUBLANES = 8
CONV_SLAB = 64


def _conv_kernel(u_ref, w_ref, cb_ref, g_ref, b_ref, o_ref, cp_sc, y_sc, halo_sc, *, width):
    TS, CH = u_ref.shape
    n_rows = CONV_HALO + TS
    i = pl.program_id(1)

    @pl.when(jnp.logical_and(pl.program_id(0) == 0, i == 0))
    def _():
        halo_sc[...] = jnp.zeros(halo_sc.shape, F32)

    halo = jnp.where(i > 0, halo_sc[...], 0.0)
    halo_sc[...] = u_ref[TS - CONV_HALO:, :].astype(F32)
    pad = jnp.zeros((SUBLANES, LANES), F32)
    for c in range(CH // LANES):
        cs = slice(c * LANES, (c + 1) * LANES)
        col = jnp.concatenate([halo[:, cs], u_ref[:, cs].astype(F32), pad], axis=0)
        for r in range(SUBLANES):
            cp_sc[r, :, cs] = col[r:r + n_rows]

    base = CONV_HALO - (width - 1)
    phases = {}
    for j in range(width):
        q, r = divmod(base + j, SUBLANES)
        phases.setdefault(r, []).append((q, j))

    def slab(s, carry):
        t0 = pl.multiple_of(s * CONV_SLAB, CONV_SLAB)
        for c in range(CH // LANES):
            cs = slice(c * LANES, (c + 1) * LANES)
            acc = jnp.zeros((CONV_SLAB, LANES), F32)
            for r, taps in phases.items():
                q_lo, q_hi = taps[0][0], taps[-1][0]
                win = cp_sc[r, pl.ds(q_lo * SUBLANES + t0, CONV_SLAB + (q_hi - q_lo) * SUBLANES), cs]
                for q, j in taps:
                    off = (q - q_lo) * SUBLANES
                    acc = acc + w_ref[j:j + 1, cs] * win[off:off + CONV_SLAB]
            y_sc[pl.ds(t0, CONV_SLAB), cs] = acc + cb_ref[:, cs]
        return carry

    lax.fori_loop(0, TS // CONV_SLAB, slab, 0)
    y = y_sc[...]
    mu = jnp.mean(y, axis=-1, keepdims=True)
    yc = y - mu
    var = jnp.mean(yc * yc, axis=-1, keepdims=True)
    yn = yc * lax.rsqrt(var + NORM_EPS) * g_ref[...] + b_ref[...]
    o_ref[...] = (yn * jax.nn.sigmoid(yn)).astype(BF16)


def _conv_module(u, conv_w, conv_b, ln_g, ln_b, B, S, TS=256):
    T, CH = u.shape
    width = conv_w.shape[0]
    assert width - 1 <= CONV_HALO
    ns = S // TS
    wpad = jnp.zeros((CONV_HALO, CH), F32).at[:width].set(conv_w)
    vec = lambda a: a.reshape(1, CH)
    kern = functools.partial(_conv_kernel, width=width)
    return pl.pallas_call(
        kern,
        grid=(B, ns),
        in_specs=[pl.BlockSpec((TS, CH), lambda b, i: (b * ns + i, 0)),
                  _resident((CONV_HALO, CH)), _resident((1, CH)), _resident((1, CH)), _resident((1, CH))],
        out_specs=pl.BlockSpec((TS, CH), lambda b, i: (b * ns + i, 0)),
        out_shape=jax.ShapeDtypeStruct((T, CH), BF16),
        scratch_shapes=[pltpu.VMEM((SUBLANES, CONV_HALO + TS, CH), F32),
                        pltpu.VMEM((TS, CH), F32),
                        pltpu.VMEM((CONV_HALO, CH), F32)],
        compiler_params=_params("arbitrary", "arbitrary"),
    )(u, wpad, vec(conv_b), vec(ln_g), vec(ln_b))


def _out_proj_kernel(x_ref, a_ref, c_ref, wa_ref, wc_ref, o_ref):
    o_ref[...] = x_ref[...] + _dot(a_ref[...], wa_ref[...]) + _dot(c_ref[...], wc_ref[...])


def _out_proj(x, attn, conv, w_out, tm=512):
    T, D = x.shape
    aw, cw = attn.shape[1], conv.shape[1]
    wa = w_out[:aw].astype(BF16)
    wc = w_out[aw:].astype(BF16)
    row = lambda w: pl.BlockSpec((tm, w), lambda i: (i, 0))
    return pl.pallas_call(
        _out_proj_kernel,
        grid=(T // tm,),
        in_specs=[row(D), row(aw), row(cw), _resident(wa.shape), _resident(wc.shape)],
        out_specs=row(D),
        out_shape=jax.ShapeDtypeStruct((T, D), F32),
        compiler_params=_params("arbitrary"),
    )(x, attn, conv, wa, wc)


def _first_max(vals):
    m = vals[0]
    for v in vals[1:]:
        m = jnp.maximum(m, v)
    idx = jnp.full(m.shape, len(vals) - 1, jnp.int32)
    for k in range(len(vals) - 2, -1, -1):
        idx = jnp.where(vals[k] == m, k, idx)
    return m, idx


def _softmax_cols(cols):
    m = cols[0]
    for c in cols[1:]:
        m = jnp.maximum(m, c)
    e = [jnp.exp(c - m) for c in cols]
    s = e[0]
    for c in e[1:]:
        s = s + c
    return [c / s for c in e]


GROUP_ROWS = 8
MOE_CHUNK = 64
MOE_MAX_CHUNKS = 8


def _split3(a):
    hi = a.astype(BF16)
    r = a - hi.astype(F32)
    mid = r.astype(BF16)
    lo = (r - mid.astype(F32)).astype(BF16)
    return hi, mid, lo


def _moe_kernel(x_ref, nw_ref, wrt_ref, brt_ref, wg_ref, wu_ref, wd_ref, o_ref,
                hs_sc, p_sc, cs_sc, seg_sm, *, n_groups):
    TM, D = x_ref.shape
    e = pl.program_id(1)
    n_exp = pl.num_programs(1)

    @pl.when(e == 0)
    def _():
        x = x_ref[...]
        scale = lax.rsqrt(jnp.mean(x * x, axis=-1, keepdims=True) + NORM_EPS)

        def h_cols(c0):
            cs = slice(c0, c0 + 512)
            return (x_ref[:, cs] * scale * nw_ref[:, cs]).astype(BF16)

        lg = brt_ref[...] + jnp.zeros((1, TM), F32)
        for c0 in range(0, D, 512):
            lg = lg + _dot_nt(wrt_ref[:, c0:c0 + 512], h_cols(c0))
        row = lambda k: lg[k:k + 1, :]
        g_prob = _softmax_cols([row(g) for g in range(n_groups)])
        g_p, g_idx = _first_max(g_prob)
        e_logit = []
        for k in range(EXPERTS_PER_GROUP):
            v = row(GROUP_ROWS + k)
            for g in range(1, n_groups):
                v = jnp.where(g_idx == g, row(GROUP_ROWS + g * EXPERTS_PER_GROUP + k), v)
            e_logit.append(v)
        e_prob = _softmax_cols(e_logit)
        p1, i1 = _first_max(e_prob)
        rest = [jnp.where(i1 == k, -1.0, e_prob[k]) for k in range(EXPERTS_PER_GROUP)]
        p2, i2 = _first_max(rest)
        den = p1 + p2
        base = g_idx * EXPERTS_PER_GROUP
        e_io = lax.broadcasted_iota(jnp.int32, (LANES, TM), 0)
        comb_t = (jnp.where(e_io == base + i1, g_p * (p1 / den), 0.0)
                  + jnp.where(e_io == base + i2, g_p * (p2 / den), 0.0))

        g_io = lax.broadcasted_iota(jnp.int32, (GROUP_ROWS, TM), 0)
        onehot_t = jnp.where(g_io == g_idx, 1.0, 0.0)
        upper = jnp.where(lax.broadcasted_iota(jnp.int32, (LANES, LANES), 0)
                          < lax.broadcasted_iota(jnp.int32, (LANES, LANES), 1), 1.0, 0.0).astype(BF16)
        before = jnp.zeros((GROUP_ROWS, 1), F32)
        ranks = []
        for b0 in range(0, TM, LANES):
            blk = onehot_t[:, b0:b0 + LANES]
            ranks.append(_dot(blk.astype(BF16), upper) + before)
            before = before + jnp.sum(blk, axis=1, keepdims=True)
        rank = jnp.concatenate(ranks, axis=1)
        start = jnp.int32(0)
        pos = jnp.zeros((1, TM), F32)
        for g in range(n_groups):
            seg_sm[g] = start
            pos = jnp.where(g_idx == g, start.astype(F32) + rank[g:g + 1, :], pos)
            start = start + jnp.sum(onehot_t[g:g + 1, :]).astype(jnp.int32)
        seg_sm[n_groups] = start
        pos = pos.astype(jnp.int32)

        rb = 256
        for r0 in range(0, TM, rb):
            r_io = lax.broadcasted_iota(jnp.int32, (rb, TM), 0) + r0
            p_sc[r0:r0 + rb, :] = jnp.where(r_io == pos, 1.0, 0.0).astype(BF16)
        p = p_sc[...]
        for c0 in range(0, D, 512):
            hs_sc[:, c0:c0 + 512] = _dot(p, h_cols(c0)).astype(BF16)
        cs = jnp.zeros((TM, LANES), F32)
        for part in _split3(comb_t):
            cs = cs + _dot_nt(p, part)
        cs_sc[...] = cs
        o_ref[...] = jnp.zeros(o_ref.shape, F32)

    g = e // EXPERTS_PER_GROUP
    start = seg_sm[g]
    end = seg_sm[g + 1]
    c_lo = start // MOE_CHUNK
    c_hi = jnp.where(end > start, (end + MOE_CHUNK - 1) // MOE_CHUNK, c_lo)

    def expert_rows(c, n_chunks):
        m = n_chunks * MOE_CHUNK
        r0 = pl.multiple_of(c * MOE_CHUNK, MOE_CHUNK)
        rows = hs_sc[pl.ds(r0, m), :]
        lane = lax.broadcasted_iota(jnp.int32, (m, LANES), 1)
        w = jnp.sum(jnp.where(lane == e, cs_sc[pl.ds(r0, m), :], 0.0), axis=-1, keepdims=True)
        a = jax.nn.silu(_dot(rows, wg_ref[0])) * _dot(rows, wu_ref[0]) * w
        o_ref[pl.ds(r0, m), :] += _dot(a.astype(BF16), wd_ref[0])

    def full(k, carry):
        expert_rows(c_lo + MOE_MAX_CHUNKS * k, MOE_MAX_CHUNKS)
        return carry

    n_chunks = c_hi - c_lo
    n_full = n_chunks // MOE_MAX_CHUNKS
    lax.fori_loop(0, n_full, full, 0)
    for rest in range(1, MOE_MAX_CHUNKS):
        @pl.when(n_chunks - n_full * MOE_MAX_CHUNKS == rest)
        def _(rest=rest):
            expert_rows(c_hi - rest, rest)

    @pl.when(e == n_exp - 1)
    def _():
        p = p_sc[...]
        for c0 in range(0, D, 512):
            ys = o_ref[:, c0:c0 + 512].astype(BF16)
            y = lax.dot_general(p, ys, (((0,), (0,)), ((), ())), preferred_element_type=F32)
            o_ref[:, c0:c0 + 512] = x_ref[:, c0:c0 + 512] + y


def _moe(x, norm_w, wg, bg, we, be, w_gate, w_up, w_down, layer, tm=1024):
    T, D = x.shape
    n_groups, n_exp = wg.shape[1], we.shape[1]
    ff = w_gate.shape[3]
    assert n_groups <= GROUP_ROWS and n_exp == n_groups * EXPERTS_PER_GROUP and T % tm == 0
    rows = GROUP_ROWS + n_exp
    wrt = jnp.zeros((rows, D), F32).at[:n_groups].set(wg.T).at[GROUP_ROWS:].set(we.T).astype(BF16)
    brt = jnp.zeros((rows, 1), F32).at[:n_groups, 0].set(bg).at[GROUP_ROWS:, 0].set(be)
    tile = pl.BlockSpec((tm, D), lambda i, e: (i, 0))
    return pl.pallas_call(
        functools.partial(_moe_kernel, n_groups=n_groups),
        grid=(T // tm, n_exp),
        in_specs=[tile, _resident((1, D)), _resident((rows, D)), _resident((rows, 1)),
                  pl.BlockSpec((None, 1, D, ff), lambda i, e: (layer, e, 0, 0)),
                  pl.BlockSpec((None, 1, D, ff), lambda i, e: (layer, e, 0, 0)),
                  pl.BlockSpec((None, 1, ff, D), lambda i, e: (layer, e, 0, 0))],
        out_specs=tile,
        out_shape=jax.ShapeDtypeStruct((T, D), F32),
        scratch_shapes=[pltpu.VMEM((tm, D), BF16),
                        pltpu.VMEM((tm, tm), BF16),
                        pltpu.VMEM((tm, LANES), F32),
                        pltpu.SMEM((GROUP_ROWS,), jnp.int32)],
        compiler_params=_params("arbitrary", "arbitrary"),
    )(x, norm_w.reshape(1, D), wrt, brt, w_gate, w_up, w_down)


POOL_HALO = 16


def _pool_kernel(x_ref, nw_ref, pw_ref, ps_ref, o_ref, hb, pa, pb):
    TS, D = x_ref.shape
    n_groups, pc, _ = pw_ref.shape
    i = pl.program_id(1)
    nw = nw_ref[...]
    top = SUBLANES + POOL_HALO
    n_rows = top + TS

    @pl.when(jnp.logical_and(pl.program_id(0) == 0, i == 0))
    def _():
        hb[...] = jnp.zeros(hb.shape, F32)
        pa[0:SUBLANES] = jnp.zeros((SUBLANES, pc), F32)
        pb[0:SUBLANES] = jnp.zeros((SUBLANES, pc), F32)

    hb[SUBLANES:top] = jnp.where(i > 0, hb[n_rows - POOL_HALO:n_rows], 0.0)
    hb[top:] = _rms(x_ref[...], nw)
    t = i * TS + lax.broadcasted_iota(jnp.int32, (TS, 1), 0)
    for g, w in enumerate(POOL_WINDOWS):
        cs = slice(g * pc, (g + 1) * pc)
        cur = hb[top:, cs]
        src, cols, step, dst = hb, cs, 1, pa
        while step < w:
            dst[SUBLANES:n_rows, :] = src[SUBLANES:n_rows, cols] + src[SUBLANES - step:n_rows - step, cols]
            src, cols, step, dst = dst, slice(None), 2 * step, (pb if dst is pa else pa)
        s = src[top:n_rows, cols]
        count = jnp.minimum(t + 1, w).astype(F32)
        d = s / count - cur
        mixed = _dot(d.astype(BF16), pw_ref[g])
        o_ref[:, cs] = x_ref[:, cs] + ps_ref[:, cs] * mixed


def _pool_layer(x, norm_w, pool_w, pool_scale, B, S, TS=256):
    T, D = x.shape
    assert len(POOL_WINDOWS) == pool_w.shape[0] and max(POOL_WINDOWS) - 1 <= POOL_HALO
    ns = S // TS
    pw = pool_w.astype(BF16)
    return pl.pallas_call(
        _pool_kernel,
        grid=(B, ns),
        in_specs=[pl.BlockSpec((TS, D), lambda b, i: (b * ns + i, 0)),
                  _resident((1, D)), _resident(pw.shape), _resident((1, D))],
        out_specs=pl.BlockSpec((TS, D), lambda b, i: (b * ns + i, 0)),
        out_shape=jax.ShapeDtypeStruct((T, D), F32),
        scratch_shapes=[pltpu.VMEM((SUBLANES + POOL_HALO + TS, D), F32),
                        pltpu.VMEM((SUBLANES + POOL_HALO + TS, D // len(POOL_WINDOWS)), F32),
                        pltpu.VMEM((SUBLANES + POOL_HALO + TS, D // len(POOL_WINDOWS)), F32)],
        compiler_params=_params("arbitrary", "arbitrary"),
    )(x, norm_w.reshape(1, D), pw, pool_scale.reshape(1, D))


def _chunk(S):
    return 256 if S % 256 == 0 else 128


def kernel(x, rel_bias, mix_norm_e, w_in_e, q_norm_e, k_norm_e, conv_w_e, conv_b_e, conv_ln_g_e, conv_ln_b_e,
           w_out_e, mix_norm_o, pool_w_o, pool_scale_o, ffn_norm, router_group_w, router_group_b,
           router_expert_w, router_expert_b, w_gate, w_up, w_down):
    B, S, D = x.shape
    T = B * S
    depth = ffn_norm.shape[0]
    n_heads = rel_bias.shape[1]
    idx_heads = (w_in_e.shape[2] - n_heads * HEAD_DIM - 2 * HEAD_DIM - IDX_DIM - 2 * conv_w_e.shape[2]) \
        // (IDX_DIM + 1)
    k_top = min(INDEX_TOPK, S // 4)
    C = _chunk(S)
    xf = x.reshape(T, D)
    bias_tiles, bias_stat = _rel_bias_tiles(rel_bias, C)
    expert_w = None
    for l in range(depth):
        i = l // 2
        if l % 2 == 0:
            q, k, v, iq, ik, iw, u = _in_proj(xf, mix_norm_e[i], w_in_e[i], q_norm_e[i], k_norm_e[i],
                                              n_heads, idx_heads, conv_w_e.shape[2])
            cast = (w_gate, w_up, w_down) if expert_w is None else ()
            if any(_cast_slabs(w, B * (S // C)) is None for w in cast):
                cast = ()
            attn, done = _attention(q, k, v, iq, ik, iw, bias_tiles, bias_stat, B, S, k_top, C, cast)
            expert_w = done if cast else expert_w
            conv = _conv_module(u, conv_w_e[i], conv_b_e[i], conv_ln_g_e[i], conv_ln_b_e[i], B, S)
            xf = _out_proj(xf, attn, conv, w_out_e[i])
        else:
            xf = _pool_layer(xf, mix_norm_o[i], pool_w_o[i], pool_scale_o[i], B, S)
        if expert_w is None:
            expert_w = [w.astype(BF16) for w in (w_gate, w_up, w_down)]
        xf = _moe(xf, ffn_norm[l], router_group_w[l], router_group_b[l], router_expert_w[l], router_expert_b[l],
                  *expert_w, layer=l)
    return xf.reshape(B, S, D)
```

```python
import functools
import math

import jax
import jax.numpy as jnp
from jax import lax
from jax.experimental import pallas as pl
from jax.experimental.pallas import tpu as pltpu

F32 = jnp.float32
BF16 = jnp.bfloat16

NORM_EPS = 1e-6
HEAD_DIM = 128
IDX_DIM = 64
INDEX_TOPK = 256
REL_BUCKETS = 32
REL_MAX_DIST = 128
POOL_WINDOWS = (2, 4, 8, 16)
EXPERTS_PER_GROUP = 4
LANES = 128
VMEM_LIMIT = 56 * 1024 * 1024
NEG = -1e30
INT_MIN = -(2 ** 31)


def _dot(a, b):
    return jnp.dot(a, b, preferred_element_type=F32)


def _dot_nt(a, b):
    return lax.dot_general(a, b, (((1,), (1,)), ((), ())), preferred_element_type=F32)


def _rms(x, w):
    return x * lax.rsqrt(jnp.mean(x * x, axis=-1, keepdims=True) + NORM_EPS) * w


def _params(*sem):
    return pltpu.CompilerParams(dimension_semantics=sem, vmem_limit_bytes=VMEM_LIMIT)


def _resident(shape):
    nd = len(shape)
    return pl.BlockSpec(shape, lambda *_: (0,) * nd, pipeline_mode=pl.Buffered(1))


def _in_proj_kernel(x_ref, nw_ref, qn_ref, kn_ref, wq_ref, wkv_ref, wiq_ref, wikw_ref, wa_ref, wg_ref,
                    q_ref, k_ref, v_ref, iq_ref, ik_ref, iw_ref, u_ref, *, q_scale, iw_scale):
    h = _rms(x_ref[...], nw_ref[...]).astype(BF16)
    n_pairs = q_ref.shape[0] // 2
    qn = qn_ref[...] * q_scale
    for c in range(n_pairs):
        qq = _dot(h, wq_ref[:, c * 256:(c + 1) * 256])
        for s in range(2):
            qh = qq[:, s * HEAD_DIM:(s + 1) * HEAD_DIM]
            q_ref[2 * c + s] = _rms(qh, qn).astype(BF16)
    kv = _dot(h, wkv_ref[...])
    k_ref[...] = _rms(kv[:, :HEAD_DIM], kn_ref[...]).astype(BF16)
    v_ref[...] = kv[:, HEAD_DIM:].astype(BF16)
    for c in range(iq_ref.shape[0] // 2):
        r = _dot(h, wiq_ref[:, c * 256:(c + 1) * 256])
        iq_ref[2 * c] = r[:, :LANES].astype(BF16)
        iq_ref[2 * c + 1] = r[:, LANES:].astype(BF16)
    r = _dot(h, wikw_ref[...])
    ik_ref[...] = r[:, :LANES].astype(BF16)
    iw_ref[...] = r[:, LANES:] * iw_scale
    for c in range(u_ref.shape[1] // 256):
        cs = slice(c * 256, (c + 1) * 256)
        a = _dot(h, wa_ref[:, cs])
        g = _dot(h, wg_ref[:, cs])
        u_ref[:, cs] = (a * jax.nn.sigmoid(g)).astype(BF16)


def _split_w_in_kernel(w_ref, wq_ref, wkv_ref, wiq_ref, wikw_ref, wa_ref, wg_ref, *, idx_heads):
    o = 0
    for ref in (wq_ref, wkv_ref, wiq_ref):
        n = ref.shape[1]
        ref[...] = w_ref[:, o:o + n].astype(BF16)
        o += n
    wik = w_ref[:, o:o + IDX_DIM]
    wiw = w_ref[:, o + IDX_DIM:o + IDX_DIM + idx_heads]
    o += IDX_DIM + idx_heads
    pad = jnp.zeros((wik.shape[0], LANES - idx_heads), F32)
    wikw_ref[...] = jnp.concatenate([wik, wik, wiw, pad], axis=1).astype(BF16)
    for ref in (wa_ref, wg_ref):
        n = ref.shape[1]
        ref[...] = w_ref[:, o:o + n].astype(BF16)
        o += n


def _split_w_in(w_in, n_heads, idx_heads, conv_ch, tk=256):
    D, n_in = w_in.shape
    widths = (n_heads * HEAD_DIM, 2 * HEAD_DIM, idx_heads * IDX_DIM, 2 * LANES, conv_ch, conv_ch)
    assert n_in == sum(widths) - 2 * LANES + IDX_DIM + idx_heads and 2 * IDX_DIM == LANES and D % tk == 0
    return pl.pallas_call(
        functools.partial(_split_w_in_kernel, idx_heads=idx_heads),
        grid=(D // tk,),
        in_specs=[pl.BlockSpec((tk, n_in), lambda i: (i, 0))],
        out_specs=[pl.BlockSpec((tk, w), lambda i: (i, 0)) for w in widths],
        out_shape=[jax.ShapeDtypeStruct((D, w), BF16) for w in widths],
        compiler_params=_params("arbitrary"),
    )(w_in)


def _in_proj(x, norm_w, w_in, q_norm, k_norm, n_heads, idx_heads, conv_ch, tm=256):
    T, D = x.shape
    iq_w = idx_heads * IDX_DIM
    ws = _split_w_in(w_in, n_heads, idx_heads, conv_ch)
    row = lambda w: pl.BlockSpec((tm, w), lambda i: (i, 0))
    heads = lambda n: pl.BlockSpec((n, tm, LANES), lambda i: (0, i, 0))
    kern = functools.partial(_in_proj_kernel, q_scale=HEAD_DIM ** -0.5,
                             iw_scale=(idx_heads ** -0.5) * (IDX_DIM ** -0.5))
    return pl.pallas_call(
        kern,
        grid=(T // tm,),
        in_specs=[row(D), _resident((1, D)), _resident((1, HEAD_DIM)), _resident((1, HEAD_DIM))]
                 + [_resident(w.shape) for w in ws],
        out_specs=[heads(n_heads), row(HEAD_DIM), row(HEAD_DIM), heads(iq_w // LANES), row(LANES), row(LANES),
                   row(conv_ch)],
        out_shape=[jax.ShapeDtypeStruct((n_heads, T, HEAD_DIM), BF16),
                   jax.ShapeDtypeStruct((T, HEAD_DIM), BF16),
                   jax.ShapeDtypeStruct((T, HEAD_DIM), BF16),
                   jax.ShapeDtypeStruct((iq_w // LANES, T, LANES), BF16),
                   jax.ShapeDtypeStruct((T, LANES), BF16),
                   jax.ShapeDtypeStruct((T, LANES), F32),
                   jax.ShapeDtypeStruct((T, conv_ch), BF16)],
        compiler_params=_params("arbitrary"),
    )(x, norm_w.reshape(1, D), q_norm.reshape(1, HEAD_DIM), k_norm.reshape(1, HEAD_DIM), *ws)


def _rel_bias_kernel(rb_ref, o_ref, stat_ref):
    _, n_heads, C, _ = o_ref.shape
    tau = lax.broadcasted_iota(jnp.int32, (C, C), 0)
    sig = lax.broadcasted_iota(jnp.int32, (C, C), 1)
    max_exact = REL_BUCKETS // 2
    for kind in range(2):
        d = tau - sig + kind * C
        n = jnp.maximum(d, 0)
        nf = jnp.maximum(n, 1).astype(F32)
        large = max_exact + (jnp.log(nf / max_exact) / math.log(REL_MAX_DIST / max_exact)
                             * (REL_BUCKETS - max_exact)).astype(jnp.int32)
        large = jnp.minimum(large, REL_BUCKETS - 1)
        bucket = jnp.where(n < max_exact, n, large)
        for h in range(n_heads):
            b = jnp.zeros((C, C), F32)
            for bk in range(REL_BUCKETS):
                b = jnp.where(bucket == bk, rb_ref[bk, h], b)
            b = b - rb_ref[REL_BUCKETS - 1, h]
            if kind == 0:
                b = jnp.where(d < 0, NEG, b)
            o_ref[kind, h] = b
    for h in range(n_heads):
        hi = rb_ref[0, h]
        lo = rb_ref[0, h]
        for bk in range(1, REL_BUCKETS):
            hi = jnp.maximum(hi, rb_ref[bk, h])
            lo = jnp.minimum(lo, rb_ref[bk, h])
        stat_ref[0, h] = hi - rb_ref[REL_BUCKETS - 1, h]
        stat_ref[1, h] = lo - rb_ref[REL_BUCKETS - 1, h]


def _rel_bias_tiles(rel_bias, C):
    n_heads = rel_bias.shape[1]
    assert C >= REL_MAX_DIST
    return pl.pallas_call(
        _rel_bias_kernel,
        in_specs=[pl.BlockSpec(memory_space=pltpu.SMEM)],
        out_specs=[pl.BlockSpec(memory_space=pltpu.VMEM), pl.BlockSpec(memory_space=pltpu.SMEM)],
        out_shape=[jax.ShapeDtypeStruct((2, n_heads, C, C), F32), jax.ShapeDtypeStruct((2, n_heads), F32)],
        compiler_params=pltpu.CompilerParams(vmem_limit_bytes=VMEM_LIMIT),
    )(rel_bias)


SHIFT_SPAN_LIMIT = 60.0


TILE_GROUP = 4


def _grouped_loop(n, fn, group):
    def body(k, carry):
        for u in range(group):
            fn(group * k + u)
        return carry

    lax.fori_loop(0, n // group, body, 0)
    base = (n // group) * group
    size = group // 2
    while size >= 1:
        take = ((n - base) & size) != 0

        @pl.when(take)
        def _(base=base, size=size):
            for u in range(size):
                fn(base + u)
        base = base + jnp.where(take, size, 0)
        size //= 2


def _order_key(x):
    bits = pltpu.bitcast(x, jnp.int32)
    return bits ^ ((bits >> 31) & 0x7FFFFFFF)


def _row_to_col(row):
    C = row.shape[1]
    halves = []
    for part in (row >> 16, row & 0xFFFF):
        halves.append(jnp.broadcast_to(part.astype(F32), (LANES, C)).T[:, 0:1].astype(jnp.int32))
    return (halves[0] << 16) | halves[1]


def _attn_kernel(bstat_ref, q_ref, k_ref, v_ref, iq_ref, ik_ref, iw_ref, bias_ref, *rest, k_top, n_cast):
    cast_in, (o_ref, *cast_out) = rest[:n_cast], rest[n_cast:2 * n_cast + 1]
    key_sc, keyt_sc, keyt16_sc, iqm_sc, vx_sc, kmax_sc, shift_sc, acc_sc = rest[2 * n_cast + 1:]
    for src, dst in zip(cast_in, cast_out):
        dst[...] = src[...].astype(BF16)
    n_heads, C, _ = q_ref.shape
    idx_heads = iqm_sc.shape[0]
    idx_bits = max(1, (key_sc.shape[0] * C - 1).bit_length())
    qi = pl.program_id(1)
    nkv = qi + 1

    @pl.when(qi == 0)
    def _():
        vx_sc[:, :HEAD_DIM] = v_ref[...]
        vx_sc[:, HEAD_DIM:] = jnp.ones((vx_sc.shape[0], HEAD_DIM), BF16)
        kf = k_ref[...].astype(F32)
        k2 = jnp.sum(kf * kf, axis=-1, keepdims=True)
        kmax_sc[...] = jnp.broadcast_to(jnp.sqrt(jnp.max(k2, axis=0, keepdims=True)), kmax_sc.shape)

    lane = lax.broadcasted_iota(jnp.int32, (C, LANES), 1)
    for p in range(idx_heads // 2):
        qp = iq_ref[p].astype(F32)
        iqm_sc[2 * p] = jnp.where(lane < IDX_DIM, qp, 0.0).astype(BF16)
        iqm_sc[2 * p + 1] = jnp.where(lane >= IDX_DIM, qp, 0.0).astype(BF16)
    iw = iw_ref[...]
    tau = lax.broadcasted_iota(jnp.int32, (C, C), 0)
    sig = lax.broadcasted_iota(jnp.int32, (C, C), 1)

    def score_tile(j):
        off = pl.multiple_of(j * C, C)
        ikc = ik_ref[pl.ds(off, C), :]
        acc = jnp.zeros((C, C), F32)
        for hh in range(idx_heads):
            s = _dot_nt(iqm_sc[hh], ikc)
            acc = acc + jnp.maximum(s, 0.0) * iw[:, hh:hh + 1]
        acc = jnp.where(jnp.logical_and(j == qi, sig > tau), -jnp.inf, acc)
        key_sc[j] = _order_key(acc)
        kt = _order_key(acc.T)
        keyt_sc[j] = kt
        keyt16_sc[j] = (kt >> 16).astype(jnp.int16)

    _grouped_loop(nkv, score_tile, TILE_GROUP)

    SUB = 32

    def count(hit):
        def body(j, cnt):
            for r in range(C // SUB):
                cnt = cnt + hit(j, r)
            return cnt
        cnt = lax.fori_loop(0, nkv, body, jnp.zeros((SUB, C), F32))
        return jnp.sum(cnt, axis=0, keepdims=True)

    def keyt(j, r):
        return keyt_sc[j, pl.ds(r * SUB, SUB), :]

    def count16(cand16):
        one = jnp.ones((SUB, C), jnp.int16)
        zero = jnp.zeros((SUB, C), jnp.int16)

        def body(j, cnt):
            for r in range(C // SUB):
                cnt = cnt + jnp.where(keyt16_sc[j, pl.ds(r * SUB, SUB), :] >= cand16, one, zero)
            return cnt
        cnt = lax.fori_loop(0, nkv, body, zero)
        return jnp.sum(cnt.astype(jnp.int32).astype(F32), axis=0, keepdims=True)

    def bit16_body(b, carry):
        res, n_res = carry
        cand = res ^ lax.shift_left(jnp.int32(1), 31 - b)
        tot = count16((cand >> 16).astype(jnp.int16))
        ok = tot >= k_top
        return jnp.where(ok, cand, res), jnp.where(ok, tot, n_res)

    searched = nkv * C > k_top
    carry = (jnp.full((1, C), INT_MIN, jnp.int32), jnp.zeros((1, C), F32))
    top_row, n_top = lax.fori_loop(0, jnp.where(searched, 16, 0), bit16_body, carry)

    top16 = top_row >> 16

    def low_body(j, c):
        kt = keyt_sc[j]
        hi = kt >> 16
        lo = (kt & 0xFFFF) - 32768
        lo = jnp.where(hi == top16, lo, jnp.where(hi > top16, 32767, -32768))
        keyt16_sc[j] = lo.astype(jnp.int16)
        return c

    lax.fori_loop(0, jnp.where(searched, nkv, 0), low_body, 0)

    def bit16_low_body(b, carry):
        res, n_res = carry
        cand = res ^ lax.shift_left(jnp.int32(1), 31 - b)
        tot = count16(((cand & 0xFFFF) - 32768).astype(jnp.int16))
        ok = tot >= k_top
        return jnp.where(ok, cand, res), jnp.where(ok, tot, n_res)

    thr_row, n_thr = lax.fori_loop(16, jnp.where(searched, 32, 16), bit16_low_body, (top_row, n_top))
    thr = _row_to_col(thr_row)

    @pl.when(jnp.max(n_thr) > k_top)
    def _():
        need = k_top - count(lambda j, r: jnp.where(keyt(j, r) > thr_row, 1.0, 0.0))
        s_sub = lax.broadcasted_iota(jnp.int32, (SUB, C), 0)

        def idx_body(b, last):
            cand = last | lax.shift_left(jnp.int32(1), idx_bits - 1 - b)
            below = count(lambda j, r: jnp.where(
                keyt(j, r) == thr_row, jnp.where(j * C + r * SUB + s_sub < cand, 1.0, 0.0), 0.0))
            return jnp.where(below < need, cand, last)

        last = _row_to_col(lax.fori_loop(0, idx_bits, idx_body, jnp.zeros((1, C), jnp.int32)))

        def drop_body(j, carry):
            kk = key_sc[j]
            key_sc[j] = jnp.where(kk == thr, jnp.where(j * C + sig > last, kk - 1, kk), kk)
            return carry

        lax.fori_loop(0, nkv, drop_body, 0)

    kmax = kmax_sc[0:1, 0:1] * 1.001
    worst = jnp.zeros((C, 1), F32)
    for h in range(n_heads):
        qf = q_ref[h].astype(F32)
        bound = jnp.sqrt(jnp.sum(qf * qf, axis=-1, keepdims=True)) * kmax
        shift_sc[h] = jnp.broadcast_to(bound + bstat_ref[0, h], (C, LANES))
        worst = jnp.maximum(worst, 2.0 * bound + (bstat_ref[0, h] - bstat_ref[1, h]))
    loose = jnp.max(worst) > SHIFT_SPAN_LIMIT

    def logits(j, h, kind):
        off = pl.multiple_of(j * C, C)
        lg = _dot_nt(q_ref[h], k_ref[pl.ds(off, C), :])
        return lg if kind is None else lg + bias_ref[kind, h]

    def near_tiles(fn):
        @pl.when(qi > 0)
        def _():
            fn(qi - 1, 1)
        fn(qi, 0)

    def far_tiles(fn, group=TILE_GROUP):
        _grouped_loop(jnp.maximum(qi - 1, 0), lambda j: fn(j, None), group)

    @pl.when(loose)
    def _():
        for h in range(n_heads):
            shift_sc[h] = jnp.full((C, LANES), NEG, F32)

        def max_tile(j, kind):
            sel = key_sc[j] >= thr
            for h in range(n_heads):
                lg = jnp.where(sel, logits(j, h, kind), NEG)
                m = jnp.max(lg, axis=-1, keepdims=True)
                shift_sc[h] = jnp.maximum(shift_sc[h], jnp.broadcast_to(m, (C, LANES)))

        far_tiles(max_tile, group=1)
        near_tiles(max_tile)

    acc_sc[...] = jnp.zeros(acc_sc.shape, F32)

    def attn_tile(j, kind):
        off = pl.multiple_of(j * C, C)
        vx = vx_sc[pl.ds(off, C), :]
        sel = key_sc[j] >= thr
        for h in range(n_heads):
            sh = jnp.concatenate([shift_sc[h]] * (C // LANES), axis=1)
            p = jnp.where(sel, jnp.exp(logits(j, h, kind) - sh), 0.0)
            acc_sc[h] += _dot(p.astype(BF16), vx)

    far_tiles(attn_tile)
    near_tiles(attn_tile)
    for h in range(n_heads):
        a = acc_sc[h]
        o_ref[:, h * HEAD_DIM:(h + 1) * HEAD_DIM] = (a[:, :HEAD_DIM] / a[:, HEAD_DIM:]).astype(BF16)


CAST_SLAB_ELEMS = 1 << 20


def _cast_slabs(w, steps):
    rows = math.prod(w.shape[:-1])
    if rows % (steps * 16) != 0 or rows // steps * w.shape[-1] > CAST_SLAB_ELEMS:
        return None
    return w.reshape(steps, rows // steps, w.shape[-1])


def _attention(q, k, v, iq, ik, iw, bias_tiles, bias_stat, B, S, k_top, C, cast=()):
    n_heads, T, _ = q.shape
    n_pairs = iq.shape[0]
    nq = S // C
    slabs = [_cast_slabs(w, B * nq) for w in cast]
    kern = functools.partial(_attn_kernel, k_top=k_top, n_cast=len(slabs))
    heads = lambda n: pl.BlockSpec((n, C, LANES), lambda b, i: (0, b * nq + i, 0))
    seq = pl.BlockSpec((S, LANES), lambda b, i: (b, 0))
    slab_specs = [pl.BlockSpec((None,) + w.shape[1:], lambda b, i: (b * nq + i, 0, 0)) for w in slabs]
    outs = pl.pallas_call(
        kern,
        grid=(B, nq),
        in_specs=[pl.BlockSpec(memory_space=pltpu.SMEM),
                  heads(n_heads), seq, seq, heads(n_pairs), seq,
                  pl.BlockSpec((C, LANES), lambda b, i: (b * nq + i, 0)),
                  _resident(bias_tiles.shape)] + slab_specs,
        out_specs=[pl.BlockSpec((C, n_heads * HEAD_DIM), lambda b, i: (b * nq + i, 0))] + slab_specs,
        out_shape=[jax.ShapeDtypeStruct((T, n_heads * HEAD_DIM), BF16)]
                  + [jax.ShapeDtypeStruct(w.shape, BF16) for w in slabs],
        scratch_shapes=[pltpu.VMEM((nq, C, C), jnp.int32),
                        pltpu.VMEM((nq, C, C), jnp.int32),
                        pltpu.VMEM((nq, C, C), jnp.int16),
                        pltpu.VMEM((2 * n_pairs, C, LANES), BF16),
                        pltpu.VMEM((S, 2 * HEAD_DIM), BF16),
                        pltpu.VMEM((8, LANES), F32),
                        pltpu.VMEM((n_heads, C, LANES), F32),
                        pltpu.VMEM((n_heads, C, 2 * HEAD_DIM), F32)],
        compiler_params=_params("arbitrary", "arbitrary"),
    )(bias_stat, q, k, v, iq, ik, iw, bias_tiles, *slabs)
    return outs[0], [o.reshape(w.shape) for o, w in zip(outs[1:], cast)]


CONV_HALO = 32


SUBLANES = 8
CONV_SLAB = 64


def _conv_kernel(u_ref, w_ref, cb_ref, g_ref, b_ref, o_ref, cp_sc, y_sc, halo_sc, *, width):
    TS, CH = u_ref.shape
    n_rows = CONV_HALO + TS
    i = pl.program_id(1)

    @pl.when(jnp.logical_and(pl.program_id(0) == 0, i == 0))
    def _():
        halo_sc[...] = jnp.zeros(halo_sc.shape, F32)

    halo = jnp.where(i > 0, halo_sc[...], 0.0)
    halo_sc[...] = u_ref[TS - CONV_HALO:, :].astype(F32)
    pad = jnp.zeros((SUBLANES, LANES), F32)
    for c in range(CH // LANES):
        cs = slice(c * LANES, (c + 1) * LANES)
        col = jnp.concatenate([halo[:, cs], u_ref[:, cs].astype(F32), pad], axis=0)
        for r in range(SUBLANES):
            cp_sc[r, :, cs] = col[r:r + n_rows]

    base = CONV_HALO - (width - 1)
    phases = {}
    for j in range(width):
        q, r = divmod(base + j, SUBLANES)
        phases.setdefault(r, []).append((q, j))

    def slab(s, carry):
        t0 = pl.multiple_of(s * CONV_SLAB, CONV_SLAB)
        for c in range(CH // LANES):
            cs = slice(c * LANES, (c + 1) * LANES)
            acc = jnp.zeros((CONV_SLAB, LANES), F32)
            for r, taps in phases.items():
                q_lo, q_hi = taps[0][0], taps[-1][0]
                win = cp_sc[r, pl.ds(q_lo * SUBLANES + t0, CONV_SLAB + (q_hi - q_lo) * SUBLANES), cs]
                for q, j in taps:
                    off = (q - q_lo) * SUBLANES
                    acc = acc + w_ref[j:j + 1, cs] * win[off:off + CONV_SLAB]
            y_sc[pl.ds(t0, CONV_SLAB), cs] = acc + cb_ref[:, cs]
        return carry

    lax.fori_loop(0, TS // CONV_SLAB, slab, 0)
    y = y_sc[...]
    mu = jnp.mean(y, axis=-1, keepdims=True)
    yc = y - mu
    var = jnp.mean(yc * yc, axis=-1, keepdims=True)
    yn = yc * lax.rsqrt(var + NORM_EPS) * g_ref[...] + b_ref[...]
    o_ref[...] = (yn * jax.nn.sigmoid(yn)).astype(BF16)


def _conv_module(u, conv_w, conv_b, ln_g, ln_b, B, S, TS=256):
    T, CH = u.shape
    width = conv_w.shape[0]
    assert width - 1 <= CONV_HALO
    ns = S // TS
    wpad = jnp.zeros((CONV_HALO, CH), F32).at[:width].set(conv_w)
    vec = lambda a: a.reshape(1, CH)
    kern = functools.partial(_conv_kernel, width=width)
    return pl.pallas_call(
        kern,
        grid=(B, ns),
        in_specs=[pl.BlockSpec((TS, CH), lambda b, i: (b * ns + i, 0)),
                  _resident((CONV_HALO, CH)), _resident((1, CH)), _resident((1, CH)), _resident((1, CH))],
        out_specs=pl.BlockSpec((TS, CH), lambda b, i: (b * ns + i, 0)),
        out_shape=jax.ShapeDtypeStruct((T, CH), BF16),
        scratch_shapes=[pltpu.VMEM((SUBLANES, CONV_HALO + TS, CH), F32),
                        pltpu.VMEM((TS, CH), F32),
                        pltpu.VMEM((CONV_HALO, CH), F32)],
        compiler_params=_params("arbitrary", "arbitrary"),
    )(u, wpad, vec(conv_b), vec(ln_g), vec(ln_b))


def _out_proj_kernel(x_ref, a_ref, c_ref, wa_ref, wc_ref, o_ref):
    o_ref[...] = x_ref[...] + _dot(a_ref[...], wa_ref[...]) + _dot(c_ref[...], wc_ref[...])


def _out_proj(x, attn, conv, w_out, tm=512):
    T, D = x.shape
    aw, cw = attn.shape[1], conv.shape[1]
    wa = w_out[:aw].astype(BF16)
    wc = w_out[aw:].astype(BF16)
    row = lambda w: pl.BlockSpec((tm, w), lambda i: (i, 0))
    return pl.pallas_call(
        _out_proj_kernel,
        grid=(T // tm,),
        in_specs=[row(D), row(aw), row(cw), _resident(wa.shape), _resident(wc.shape)],
        out_specs=row(D),
        out_shape=jax.ShapeDtypeStruct((T, D), F32),
        compiler_params=_params("arbitrary"),
    )(x, attn, conv, wa, wc)


def _first_max(vals):
    m = vals[0]
    for v in vals[1:]:
        m = jnp.maximum(m, v)
    idx = jnp.full(m.shape, len(vals) - 1, jnp.int32)
    for k in range(len(vals) - 2, -1, -1):
        idx = jnp.where(vals[k] == m, k, idx)
    return m, idx


def _softmax_cols(cols):
    m = cols[0]
    for c in cols[1:]:
        m = jnp.maximum(m, c)
    e = [jnp.exp(c - m) for c in cols]
    s = e[0]
    for c in e[1:]:
        s = s + c
    return [c / s for c in e]


GROUP_ROWS = 8
MOE_CHUNK = 32
MOE_MAX_CHUNKS = 16


def _split3(a):
    hi = a.astype(BF16)
    r = a - hi.astype(F32)
    mid = r.astype(BF16)
    lo = (r - mid.astype(F32)).astype(BF16)
    return hi, mid, lo


def _moe_kernel(x_ref, nw_ref, wrt_ref, brt_ref, wg_ref, wu_ref, wd_ref, o_ref,
                hs_sc, p_sc, cs_sc, seg_sm, *, n_groups):
    TM, D = x_ref.shape
    e = pl.program_id(1)
    n_exp = pl.num_programs(1)

    @pl.when(e == 0)
    def _():
        x = x_ref[...]
        scale = lax.rsqrt(jnp.mean(x * x, axis=-1, keepdims=True) + NORM_EPS)

        def h_cols(c0):
            cs = slice(c0, c0 + 512)
            return (x_ref[:, cs] * scale * nw_ref[:, cs]).astype(BF16)

        lg = brt_ref[...] + jnp.zeros((1, TM), F32)
        for c0 in range(0, D, 512):
            lg = lg + _dot_nt(wrt_ref[:, c0:c0 + 512], h_cols(c0))
        row = lambda k: lg[k:k + 1, :]
        g_prob = _softmax_cols([row(g) for g in range(n_groups)])
        g_p, g_idx = _first_max(g_prob)
        e_logit = []
        for k in range(EXPERTS_PER_GROUP):
            v = row(GROUP_ROWS + k)
            for g in range(1, n_groups):
                v = jnp.where(g_idx == g, row(GROUP_ROWS + g * EXPERTS_PER_GROUP + k), v)
            e_logit.append(v)
        e_prob = _softmax_cols(e_logit)
        p1, i1 = _first_max(e_prob)
        rest = [jnp.where(i1 == k, -1.0, e_prob[k]) for k in range(EXPERTS_PER_GROUP)]
        p2, i2 = _first_max(rest)
        den = p1 + p2
        base = g_idx * EXPERTS_PER_GROUP
        e_io = lax.broadcasted_iota(jnp.int32, (LANES, TM), 0)
        comb_t = (jnp.where(e_io == base + i1, g_p * (p1 / den), 0.0)
                  + jnp.where(e_io == base + i2, g_p * (p2 / den), 0.0))

        g_io = lax.broadcasted_iota(jnp.int32, (GROUP_ROWS, TM), 0)
        onehot_t = jnp.where(g_io == g_idx, 1.0, 0.0)
        upper = jnp.where(lax.broadcasted_iota(jnp.int32, (LANES, LANES), 0)
                          < lax.broadcasted_iota(jnp.int32, (LANES, LANES), 1), 1.0, 0.0).astype(BF16)
        before = jnp.zeros((GROUP_ROWS, 1), F32)
        ranks = []
        for b0 in range(0, TM, LANES):
            blk = onehot_t[:, b0:b0 + LANES]
            ranks.append(_dot(blk.astype(BF16), upper) + before)
            before = before + jnp.sum(blk, axis=1, keepdims=True)
        rank = jnp.concatenate(ranks, axis=1)
        start = jnp.int32(0)
        pos = jnp.zeros((1, TM), F32)
        for g in range(n_groups):
            seg_sm[g] = start
            pos = jnp.where(g_idx == g, start.astype(F32) + rank[g:g + 1, :], pos)
            start = start + jnp.sum(onehot_t[g:g + 1, :]).astype(jnp.int32)
        seg_sm[n_groups] = start
        pos = pos.astype(jnp.int32)

        rb = 256
        for r0 in range(0, TM, rb):
            r_io = lax.broadcasted_iota(jnp.int32, (rb, TM), 0) + r0
            p_sc[r0:r0 + rb, :] = jnp.where(r_io == pos, 1.0, 0.0).astype(BF16)
        p = p_sc[...]
        for c0 in range(0, D, 512):
            hs_sc[:, c0:c0 + 512] = _dot(p, h_cols(c0)).astype(BF16)
        cs = jnp.zeros((TM, LANES), F32)
        for part in _split3(comb_t):
            cs = cs + _dot_nt(p, part)
        cs_sc[...] = cs
        o_ref[...] = jnp.zeros(o_ref.shape, F32)

    g = e // EXPERTS_PER_GROUP
    start = seg_sm[g]
    end = seg_sm[g + 1]
    c_lo = start // MOE_CHUNK
    c_hi = jnp.where(end > start, (end + MOE_CHUNK - 1) // MOE_CHUNK, c_lo)

    def expert_rows(c, n_chunks):
        m = n_chunks * MOE_CHUNK
        r0 = pl.multiple_of(c * MOE_CHUNK, MOE_CHUNK)
        rows = hs_sc[pl.ds(r0, m), :]
        lane = lax.broadcasted_iota(jnp.int32, (m, LANES), 1)
        w = jnp.sum(jnp.where(lane == e, cs_sc[pl.ds(r0, m), :], 0.0), axis=-1, keepdims=True)
        a = jax.nn.silu(_dot(rows, wg_ref[0])) * _dot(rows, wu_ref[0]) * w
        o_ref[pl.ds(r0, m), :] += _dot(a.astype(BF16), wd_ref[0])

    def full(k, carry):
        expert_rows(c_lo + MOE_MAX_CHUNKS * k, MOE_MAX_CHUNKS)
        return carry

    n_chunks = c_hi - c_lo
    n_full = n_chunks // MOE_MAX_CHUNKS
    lax.fori_loop(0, n_full, full, 0)
    for rest in range(1, MOE_MAX_CHUNKS):
        @pl.when(n_chunks - n_full * MOE_MAX_CHUNKS == rest)
        def _(rest=rest):
            expert_rows(c_hi - rest, rest)

    @pl.when(e == n_exp - 1)
    def _():
        p = p_sc[...]
        for c0 in range(0, D, 512):
            ys = o_ref[:, c0:c0 + 512].astype(BF16)
            y = lax.dot_general(p, ys, (((0,), (0,)), ((), ())), preferred_element_type=F32)
            o_ref[:, c0:c0 + 512] = x_ref[:, c0:c0 + 512] + y


def _moe(x, norm_w, wg, bg, we, be, w_gate, w_up, w_down, layer, tm=1024):
    T, D = x.shape
    n_groups, n_exp = wg.shape[1], we.shape[1]
    ff = w_gate.shape[3]
    assert n_groups <= GROUP_ROWS and n_exp == n_groups * EXPERTS_PER_GROUP and T % tm == 0
    rows = GROUP_ROWS + n_exp
    wrt = jnp.zeros((rows, D), F32).at[:n_groups].set(wg.T).at[GROUP_ROWS:].set(we.T).astype(BF16)
    brt = jnp.zeros((rows, 1), F32).at[:n_groups, 0].set(bg).at[GROUP_ROWS:, 0].set(be)
    tile = pl.BlockSpec((tm, D), lambda i, e: (i, 0))
    return pl.pallas_call(
        functools.partial(_moe_kernel, n_groups=n_groups),
        grid=(T // tm, n_exp),
        in_specs=[tile, _resident((1, D)), _resident((rows, D)), _resident((rows, 1)),
                  pl.BlockSpec((None, 1, D, ff), lambda i, e: (layer, e, 0, 0)),
                  pl.BlockSpec((None, 1, D, ff), lambda i, e: (layer, e, 0, 0)),
                  pl.BlockSpec((None, 1, ff, D), lambda i, e: (layer, e, 0, 0))],
        out_specs=tile,
        out_shape=jax.ShapeDtypeStruct((T, D), F32),
        scratch_shapes=[pltpu.VMEM((tm, D), BF16),
                        pltpu.VMEM((tm, tm), BF16),
                        pltpu.VMEM((tm, LANES), F32),
                        pltpu.SMEM((GROUP_ROWS,), jnp.int32)],
        compiler_params=_params("arbitrary", "arbitrary"),
    )(x, norm_w.reshape(1, D), wrt, brt, w_gate, w_up, w_down)


POOL_HALO = 16


def _pool_kernel(x_ref, nw_ref, pw_ref, ps_ref, o_ref, hb, pa, pb):
    TS, D = x_ref.shape
    n_groups, pc, _ = pw_ref.shape
    i = pl.program_id(1)
    nw = nw_ref[...]
    top = SUBLANES + POOL_HALO
    n_rows = top + TS

    @pl.when(jnp.logical_and(pl.program_id(0) == 0, i == 0))
    def _():
        hb[...] = jnp.zeros(hb.shape, F32)
        pa[0:SUBLANES] = jnp.zeros((SUBLANES, pc), F32)
        pb[0:SUBLANES] = jnp.zeros((SUBLANES, pc), F32)

    hb[SUBLANES:top] = jnp.where(i > 0, hb[n_rows - POOL_HALO:n_rows], 0.0)
    hb[top:] = _rms(x_ref[...], nw)
    t = i * TS + lax.broadcasted_iota(jnp.int32, (TS, 1), 0)
    for g, w in enumerate(POOL_WINDOWS):
        cs = slice(g * pc, (g + 1) * pc)
        cur = hb[top:, cs]
        src, cols, step, dst = hb, cs, 1, pa
        while step < w:
            dst[SUBLANES:n_rows, :] = src[SUBLANES:n_rows, cols] + src[SUBLANES - step:n_rows - step, cols]
            src, cols, step, dst = dst, slice(None), 2 * step, (pb if dst is pa else pa)
        s = src[top:n_rows, cols]
        count = jnp.minimum(t + 1, w).astype(F32)
        d = s / count - cur
        mixed = _dot(d.astype(BF16), pw_ref[g])
        o_ref[:, cs] = x_ref[:, cs] + ps_ref[:, cs] * mixed


def _pool_layer(x, norm_w, pool_w, pool_scale, B, S, TS=256):
    T, D = x.shape
    assert len(POOL_WINDOWS) == pool_w.shape[0] and max(POOL_WINDOWS) - 1 <= POOL_HALO
    ns = S // TS
    pw = pool_w.astype(BF16)
    return pl.pallas_call(
        _pool_kernel,
        grid=(B, ns),
        in_specs=[pl.BlockSpec((TS, D), lambda b, i: (b * ns + i, 0)),
                  _resident((1, D)), _resident(pw.shape), _resident((1, D))],
        out_specs=pl.BlockSpec((TS, D), lambda b, i: (b * ns + i, 0)),
        out_shape=jax.ShapeDtypeStruct((T, D), F32),
        scratch_shapes=[pltpu.VMEM((SUBLANES + POOL_HALO + TS, D), F32),
                        pltpu.VMEM((SUBLANES + POOL_HALO + TS, D // len(POOL_WINDOWS)), F32),
                        pltpu.VMEM((SUBLANES + POOL_HALO + TS, D // len(POOL_WINDOWS)), F32)],
        compiler_params=_params("arbitrary", "arbitrary"),
    )(x, norm_w.reshape(1, D), pw, pool_scale.reshape(1, D))


def _chunk(S):
    return 256 if S % 256 == 0 else 128


def kernel(x, rel_bias, mix_norm_e, w_in_e, q_norm_e, k_norm_e, conv_w_e, conv_b_e, conv_ln_g_e, conv_ln_b_e,
           w_out_e, mix_norm_o, pool_w_o, pool_scale_o, ffn_norm, router_group_w, router_group_b,
           router_expert_w, router_expert_b, w_gate, w_up, w_down):
    B, S, D = x.shape
    T = B * S
    depth = ffn_norm.shape[0]
    n_heads = rel_bias.shape[1]
    idx_heads = (w_in_e.shape[2] - n_heads * HEAD_DIM - 2 * HEAD_DIM - IDX_DIM - 2 * conv_w_e.shape[2]) \
        // (IDX_DIM + 1)
    k_top = min(INDEX_TOPK, S // 4)
    C = _chunk(S)
    xf = x.reshape(T, D)
    bias_tiles, bias_stat = _rel_bias_tiles(rel_bias, C)
    expert_w = None
    for l in range(depth):
        i = l // 2
        if l % 2 == 0:
            q, k, v, iq, ik, iw, u = _in_proj(xf, mix_norm_e[i], w_in_e[i], q_norm_e[i], k_norm_e[i],
                                              n_heads, idx_heads, conv_w_e.shape[2])
            cast = (w_gate, w_up, w_down) if expert_w is None else ()
            if any(_cast_slabs(w, B * (S // C)) is None for w in cast):
                cast = ()
            attn, done = _attention(q, k, v, iq, ik, iw, bias_tiles, bias_stat, B, S, k_top, C, cast)
            expert_w = done if cast else expert_w
            conv = _conv_module(u, conv_w_e[i], conv_b_e[i], conv_ln_g_e[i], conv_ln_b_e[i], B, S)
            xf = _out_proj(xf, attn, conv, w_out_e[i])
        else:
            xf = _pool_layer(xf, mix_norm_o[i], pool_w_o[i], pool_scale_o[i], B, S)
        if expert_w is None:
            expert_w = [w.astype(BF16) for w in (w_gate, w_up, w_down)]
        xf = _moe(xf, ffn_norm[l], router_group_w[l], router_group_b[l], router_expert_w[l], router_expert_b[l],
                  *expert_w, layer=l)
    return xf.reshape(B, S, D)
```

```python
import functools
import math

import jax
import jax.numpy as jnp
from jax import lax
from jax.experimental import pallas as pl
from jax.experimental.pallas import tpu as pltpu

F32 = jnp.float32
BF16 = jnp.bfloat16

NORM_EPS = 1e-6
HEAD_DIM = 128
IDX_DIM = 64
INDEX_TOPK = 256
REL_BUCKETS = 32
REL_MAX_DIST = 128
POOL_WINDOWS = (2, 4, 8, 16)
EXPERTS_PER_GROUP = 4
LANES = 128
VMEM_LIMIT = 56 * 1024 * 1024
NEG = -1e30
INT_MIN = -(2 ** 31)


def _dot(a, b):
    return jnp.dot(a, b, preferred_element_type=F32)


def _dot_nt(a, b):
    return lax.dot_general(a, b, (((1,), (1,)), ((), ())), preferred_element_type=F32)


def _rms(x, w):
    return x * lax.rsqrt(jnp.mean(x * x, axis=-1, keepdims=True) + NORM_EPS) * w


def _params(*sem):
    return pltpu.CompilerParams(dimension_semantics=sem, vmem_limit_bytes=VMEM_LIMIT)


def _resident(shape):
    nd = len(shape)
    return pl.BlockSpec(shape, lambda *_: (0,) * nd, pipeline_mode=pl.Buffered(1))


def _in_proj_kernel(x_ref, nw_ref, qn_ref, kn_ref, wq_ref, wkv_ref, wiq_ref, wikw_ref, wa_ref, wg_ref,
                    q_ref, k_ref, v_ref, iq_ref, ik_ref, iw_ref, u_ref, *, q_scale, iw_scale):
    h = _rms(x_ref[...], nw_ref[...]).astype(BF16)
    n_pairs = q_ref.shape[0] // 2
    qn = qn_ref[...] * q_scale
    for c in range(n_pairs):
        qq = _dot(h, wq_ref[:, c * 256:(c + 1) * 256])
        for s in range(2):
            qh = qq[:, s * HEAD_DIM:(s + 1) * HEAD_DIM]
            q_ref[2 * c + s] = _rms(qh, qn).astype(BF16)
    kv = _dot(h, wkv_ref[...])
    k_ref[...] = _rms(kv[:, :HEAD_DIM], kn_ref[...]).astype(BF16)
    v_ref[...] = kv[:, HEAD_DIM:].astype(BF16)
    for c in range(iq_ref.shape[0] // 2):
        r = _dot(h, wiq_ref[:, c * 256:(c + 1) * 256])
        iq_ref[2 * c] = r[:, :LANES].astype(BF16)
        iq_ref[2 * c + 1] = r[:, LANES:].astype(BF16)
    r = _dot(h, wikw_ref[...])
    ik_ref[...] = r[:, :LANES].astype(BF16)
    iw_ref[...] = r[:, LANES:] * iw_scale
    for c in range(u_ref.shape[1] // 256):
        cs = slice(c * 256, (c + 1) * 256)
        a = _dot(h, wa_ref[:, cs])
        g = _dot(h, wg_ref[:, cs])
        u_ref[:, cs] = (a * jax.nn.sigmoid(g)).astype(BF16)


def _split_w_in_kernel(w_ref, wq_ref, wkv_ref, wiq_ref, wikw_ref, wa_ref, wg_ref, *, idx_heads):
    o = 0
    for ref in (wq_ref, wkv_ref, wiq_ref):
        n = ref.shape[1]
        ref[...] = w_ref[:, o:o + n].astype(BF16)
        o += n
    wik = w_ref[:, o:o + IDX_DIM]
    wiw = w_ref[:, o + IDX_DIM:o + IDX_DIM + idx_heads]
    o += IDX_DIM + idx_heads
    pad = jnp.zeros((wik.shape[0], LANES - idx_heads), F32)
    wikw_ref[...] = jnp.concatenate([wik, wik, wiw, pad], axis=1).astype(BF16)
    for ref in (wa_ref, wg_ref):
        n = ref.shape[1]
        ref[...] = w_ref[:, o:o + n].astype(BF16)
        o += n


def _split_w_in(w_in, n_heads, idx_heads, conv_ch, tk=256):
    D, n_in = w_in.shape
    widths = (n_heads * HEAD_DIM, 2 * HEAD_DIM, idx_heads * IDX_DIM, 2 * LANES, conv_ch, conv_ch)
    assert n_in == sum(widths) - 2 * LANES + IDX_DIM + idx_heads and 2 * IDX_DIM == LANES and D % tk == 0
    return pl.pallas_call(
        functools.partial(_split_w_in_kernel, idx_heads=idx_heads),
        grid=(D // tk,),
        in_specs=[pl.BlockSpec((tk, n_in), lambda i: (i, 0))],
        out_specs=[pl.BlockSpec((tk, w), lambda i: (i, 0)) for w in widths],
        out_shape=[jax.ShapeDtypeStruct((D, w), BF16) for w in widths],
        compiler_params=_params("arbitrary"),
    )(w_in)


def _in_proj(x, norm_w, w_in, q_norm, k_norm, n_heads, idx_heads, conv_ch, tm=256):
    T, D = x.shape
    iq_w = idx_heads * IDX_DIM
    ws = _split_w_in(w_in, n_heads, idx_heads, conv_ch)
    row = lambda w: pl.BlockSpec((tm, w), lambda i: (i, 0))
    heads = lambda n: pl.BlockSpec((n, tm, LANES), lambda i: (0, i, 0))
    kern = functools.partial(_in_proj_kernel, q_scale=HEAD_DIM ** -0.5,
                             iw_scale=(idx_heads ** -0.5) * (IDX_DIM ** -0.5))
    return pl.pallas_call(
        kern,
        grid=(T // tm,),
        in_specs=[row(D), _resident((1, D)), _resident((1, HEAD_DIM)), _resident((1, HEAD_DIM))]
                 + [_resident(w.shape) for w in ws],
        out_specs=[heads(n_heads), row(HEAD_DIM), row(HEAD_DIM), heads(iq_w // LANES), row(LANES), row(LANES),
                   row(conv_ch)],
        out_shape=[jax.ShapeDtypeStruct((n_heads, T, HEAD_DIM), BF16),
                   jax.ShapeDtypeStruct((T, HEAD_DIM), BF16),
                   jax.ShapeDtypeStruct((T, HEAD_DIM), BF16),
                   jax.ShapeDtypeStruct((iq_w // LANES, T, LANES), BF16),
                   jax.ShapeDtypeStruct((T, LANES), BF16),
                   jax.ShapeDtypeStruct((T, LANES), F32),
                   jax.ShapeDtypeStruct((T, conv_ch), BF16)],
        compiler_params=_params("arbitrary"),
    )(x, norm_w.reshape(1, D), q_norm.reshape(1, HEAD_DIM), k_norm.reshape(1, HEAD_DIM), *ws)


def _rel_bias_kernel(rb_ref, o_ref, stat_ref):
    _, n_heads, C, _ = o_ref.shape
    tau = lax.broadcasted_iota(jnp.int32, (C, C), 0)
    sig = lax.broadcasted_iota(jnp.int32, (C, C), 1)
    max_exact = REL_BUCKETS // 2
    for kind in range(2):
        d = tau - sig + kind * C
        n = jnp.maximum(d, 0)
        nf = jnp.maximum(n, 1).astype(F32)
        large = max_exact + (jnp.log(nf / max_exact) / math.log(REL_MAX_DIST / max_exact)
                             * (REL_BUCKETS - max_exact)).astype(jnp.int32)
        large = jnp.minimum(large, REL_BUCKETS - 1)
        bucket = jnp.where(n < max_exact, n, large)
        for h in range(n_heads):
            b = jnp.zeros((C, C), F32)
            for bk in range(REL_BUCKETS):
                b = jnp.where(bucket == bk, rb_ref[bk, h], b)
            b = b - rb_ref[REL_BUCKETS - 1, h]
            if kind == 0:
                b = jnp.where(d < 0, NEG, b)
            o_ref[kind, h] = b
    for h in range(n_heads):
        hi = rb_ref[0, h]
        lo = rb_ref[0, h]
        for bk in range(1, REL_BUCKETS):
            hi = jnp.maximum(hi, rb_ref[bk, h])
            lo = jnp.minimum(lo, rb_ref[bk, h])
        stat_ref[0, h] = hi - rb_ref[REL_BUCKETS - 1, h]
        stat_ref[1, h] = lo - rb_ref[REL_BUCKETS - 1, h]


def _rel_bias_tiles(rel_bias, C):
    n_heads = rel_bias.shape[1]
    assert C >= REL_MAX_DIST
    return pl.pallas_call(
        _rel_bias_kernel,
        in_specs=[pl.BlockSpec(memory_space=pltpu.SMEM)],
        out_specs=[pl.BlockSpec(memory_space=pltpu.VMEM), pl.BlockSpec(memory_space=pltpu.SMEM)],
        out_shape=[jax.ShapeDtypeStruct((2, n_heads, C, C), F32), jax.ShapeDtypeStruct((2, n_heads), F32)],
        compiler_params=pltpu.CompilerParams(vmem_limit_bytes=VMEM_LIMIT),
    )(rel_bias)


SHIFT_SPAN_LIMIT = 60.0


TILE_GROUP = 4


def _grouped_loop(n, fn, group):
    def body(k, carry):
        for u in range(group):
            fn(group * k + u)
        return carry

    lax.fori_loop(0, n // group, body, 0)
    base = (n // group) * group
    size = group // 2
    while size >= 1:
        take = ((n - base) & size) != 0

        @pl.when(take)
        def _(base=base, size=size):
            for u in range(size):
                fn(base + u)
        base = base + jnp.where(take, size, 0)
        size //= 2


def _order_key(x):
    bits = pltpu.bitcast(x, jnp.int32)
    return bits ^ ((bits >> 31) & 0x7FFFFFFF)


def _row_to_col(row):
    C = row.shape[1]
    halves = []
    for part in (row >> 16, row & 0xFFFF):
        halves.append(jnp.broadcast_to(part.astype(F32), (LANES, C)).T[:, 0:1].astype(jnp.int32))
    return (halves[0] << 16) | halves[1]


def _attn_kernel(bstat_ref, q_ref, k_ref, v_ref, iq_ref, ik_ref, iw_ref, bias_ref, *rest, k_top, n_cast):
    cast_in, (o_ref, *cast_out) = rest[:n_cast], rest[n_cast:2 * n_cast + 1]
    key_sc, keyt_sc, keyt16_sc, iqm_sc, vx_sc, kmax_sc, shift_sc, acc_sc = rest[2 * n_cast + 1:]
    for src, dst in zip(cast_in, cast_out):
        dst[...] = src[...].astype(BF16)
    n_heads, C, _ = q_ref.shape
    idx_heads = iqm_sc.shape[0]
    idx_bits = max(1, (key_sc.shape[0] * C - 1).bit_length())
    qi = pl.program_id(1)
    nkv = qi + 1

    @pl.when(qi == 0)
    def _():
        vx_sc[:, :HEAD_DIM] = v_ref[...]
        vx_sc[:, HEAD_DIM:] = jnp.ones((vx_sc.shape[0], HEAD_DIM), BF16)
        kf = k_ref[...].astype(F32)
        k2 = jnp.sum(kf * kf, axis=-1, keepdims=True)
        kmax_sc[...] = jnp.broadcast_to(jnp.sqrt(jnp.max(k2, axis=0, keepdims=True)), kmax_sc.shape)

    lane = lax.broadcasted_iota(jnp.int32, (C, LANES), 1)
    for p in range(idx_heads // 2):
        qp = iq_ref[p].astype(F32)
        iqm_sc[2 * p] = jnp.where(lane < IDX_DIM, qp, 0.0).astype(BF16)
        iqm_sc[2 * p + 1] = jnp.where(lane >= IDX_DIM, qp, 0.0).astype(BF16)
    iw = iw_ref[...]
    tau = lax.broadcasted_iota(jnp.int32, (C, C), 0)
    sig = lax.broadcasted_iota(jnp.int32, (C, C), 1)

    def score_tile(j):
        off = pl.multiple_of(j * C, C)
        ikc = ik_ref[pl.ds(off, C), :]
        acc = jnp.zeros((C, C), F32)
        for hh in range(idx_heads):
            s = _dot_nt(iqm_sc[hh], ikc)
            acc = acc + jnp.maximum(s, 0.0) * iw[:, hh:hh + 1]
        acc = jnp.where(jnp.logical_and(j == qi, sig > tau), -jnp.inf, acc)
        key_sc[j] = _order_key(acc)
        kt = _order_key(acc.T)
        keyt_sc[j] = kt
        keyt16_sc[j] = (kt >> 16).astype(jnp.int16)

    _grouped_loop(nkv, score_tile, TILE_GROUP)

    SUB = 32

    def count(hit):
        def body(j, cnt):
            for r in range(C // SUB):
                cnt = cnt + hit(j, r)
            return cnt
        cnt = lax.fori_loop(0, nkv, body, jnp.zeros((SUB, C), F32))
        return jnp.sum(cnt, axis=0, keepdims=True)

    def keyt(j, r):
        return keyt_sc[j, pl.ds(r * SUB, SUB), :]

    def count16(cand16):
        one = jnp.ones((SUB, C), jnp.int16)
        zero = jnp.zeros((SUB, C), jnp.int16)

        def body(j, cnt):
            for r in range(C // SUB):
                cnt = cnt + jnp.where(keyt16_sc[j, pl.ds(r * SUB, SUB), :] >= cand16, one, zero)
            return cnt
        cnt = lax.fori_loop(0, nkv, body, zero)
        return jnp.sum(cnt.astype(jnp.int32).astype(F32), axis=0, keepdims=True)

    def bit16_body(b, carry):
        res, n_res = carry
        cand = res ^ lax.shift_left(jnp.int32(1), 31 - b)
        tot = count16((cand >> 16).astype(jnp.int16))
        ok = tot >= k_top
        return jnp.where(ok, cand, res), jnp.where(ok, tot, n_res)

    searched = nkv * C > k_top
    carry = (jnp.full((1, C), INT_MIN, jnp.int32), jnp.zeros((1, C), F32))
    top_row, n_top = lax.fori_loop(0, jnp.where(searched, 16, 0), bit16_body, carry)

    top16 = top_row >> 16

    def low_body(j, c):
        kt = keyt_sc[j]
        hi = kt >> 16
        lo = (kt & 0xFFFF) - 32768
        lo = jnp.where(hi == top16, lo, jnp.where(hi > top16, 32767, -32768))
        keyt16_sc[j] = lo.astype(jnp.int16)
        return c

    lax.fori_loop(0, jnp.where(searched, nkv, 0), low_body, 0)

    def bit16_low_body(b, carry):
        res, n_res = carry
        cand = res ^ lax.shift_left(jnp.int32(1), 31 - b)
        tot = count16(((cand & 0xFFFF) - 32768).astype(jnp.int16))
        ok = tot >= k_top
        return jnp.where(ok, cand, res), jnp.where(ok, tot, n_res)

    thr_row, n_thr = lax.fori_loop(16, jnp.where(searched, 32, 16), bit16_low_body, (top_row, n_top))
    thr = _row_to_col(thr_row)

    @pl.when(jnp.max(n_thr) > k_top)
    def _():
        need = k_top - count(lambda j, r: jnp.where(keyt(j, r) > thr_row, 1.0, 0.0))
        s_sub = lax.broadcasted_iota(jnp.int32, (SUB, C), 0)

        def idx_body(b, last):
            cand = last | lax.shift_left(jnp.int32(1), idx_bits - 1 - b)
            below = count(lambda j, r: jnp.where(
                keyt(j, r) == thr_row, jnp.where(j * C + r * SUB + s_sub < cand, 1.0, 0.0), 0.0))
            return jnp.where(below < need, cand, last)

        last = _row_to_col(lax.fori_loop(0, idx_bits, idx_body, jnp.zeros((1, C), jnp.int32)))

        def drop_body(j, carry):
            kk = key_sc[j]
            key_sc[j] = jnp.where(kk == thr, jnp.where(j * C + sig > last, kk - 1, kk), kk)
            return carry

        lax.fori_loop(0, nkv, drop_body, 0)

    kmax = kmax_sc[0:1, 0:1] * 1.001
    worst = jnp.zeros((C, 1), F32)
    for h in range(n_heads):
        qf = q_ref[h].astype(F32)
        bound = jnp.sqrt(jnp.sum(qf * qf, axis=-1, keepdims=True)) * kmax
        shift_sc[h] = jnp.broadcast_to(bound + bstat_ref[0, h], (C, LANES))
        worst = jnp.maximum(worst, 2.0 * bound + (bstat_ref[0, h] - bstat_ref[1, h]))
    loose = jnp.max(worst) > SHIFT_SPAN_LIMIT

    def logits(j, h, kind):
        off = pl.multiple_of(j * C, C)
        lg = _dot_nt(q_ref[h], k_ref[pl.ds(off, C), :])
        return lg if kind is None else lg + bias_ref[kind, h]

    def near_tiles(fn):
        @pl.when(qi > 0)
        def _():
            fn(qi - 1, 1)
        fn(qi, 0)

    def far_tiles(fn, group=TILE_GROUP):
        _grouped_loop(jnp.maximum(qi - 1, 0), lambda j: fn(j, None), group)

    @pl.when(loose)
    def _():
        for h in range(n_heads):
            shift_sc[h] = jnp.full((C, LANES), NEG, F32)

        def max_tile(j, kind):
            sel = key_sc[j] >= thr
            for h in range(n_heads):
                lg = jnp.where(sel, logits(j, h, kind), NEG)
                m = jnp.max(lg, axis=-1, keepdims=True)
                shift_sc[h] = jnp.maximum(shift_sc[h], jnp.broadcast_to(m, (C, LANES)))

        far_tiles(max_tile, group=1)
        near_tiles(max_tile)

    acc_sc[...] = jnp.zeros(acc_sc.shape, F32)

    def attn_tile(j, kind):
        off = pl.multiple_of(j * C, C)
        vx = vx_sc[pl.ds(off, C), :]
        sel = key_sc[j] >= thr
        for h in range(n_heads):
            sh = jnp.concatenate([shift_sc[h]] * (C // LANES), axis=1)
            p = jnp.where(sel, jnp.exp(logits(j, h, kind) - sh), 0.0)
            acc_sc[h] += _dot(p.astype(BF16), vx)

    far_tiles(attn_tile)
    near_tiles(attn_tile)
    for h in range(n_heads):
        a = acc_sc[h]
        o_ref[:, h * HEAD_DIM:(h + 1) * HEAD_DIM] = (a[:, :HEAD_DIM] / a[:, HEAD_DIM:]).astype(BF16)


CAST_SLAB_ELEMS = 1 << 20


def _cast_slabs(w, steps):
    rows = math.prod(w.shape[:-1])
    if rows % (steps * 16) != 0 or rows // steps * w.shape[-1] > CAST_SLAB_ELEMS:
        return None
    return w.reshape(steps, rows // steps, w.shape[-1])


def _attention(q, k, v, iq, ik, iw, bias_tiles, bias_stat, B, S, k_top, C, cast=()):
    n_heads, T, _ = q.shape
    n_pairs = iq.shape[0]
    nq = S // C
    slabs = [_cast_slabs(w, B * nq) for w in cast]
    kern = functools.partial(_attn_kernel, k_top=k_top, n_cast=len(slabs))
    heads = lambda n: pl.BlockSpec((n, C, LANES), lambda b, i: (0, b * nq + i, 0))
    seq = pl.BlockSpec((S, LANES), lambda b, i: (b, 0))
    slab_specs = [pl.BlockSpec((None,) + w.shape[1:], lambda b, i: (b * nq + i, 0, 0)) for w in slabs]
    outs = pl.pallas_call(
        kern,
        grid=(B, nq),
        in_specs=[pl.BlockSpec(memory_space=pltpu.SMEM),
                  heads(n_heads), seq, seq, heads(n_pairs), seq,
                  pl.BlockSpec((C, LANES), lambda b, i: (b * nq + i, 0)),
                  _resident(bias_tiles.shape)] + slab_specs,
        out_specs=[pl.BlockSpec((C, n_heads * HEAD_DIM), lambda b, i: (b * nq + i, 0))] + slab_specs,
        out_shape=[jax.ShapeDtypeStruct((T, n_heads * HEAD_DIM), BF16)]
                  + [jax.ShapeDtypeStruct(w.shape, BF16) for w in slabs],
        scratch_shapes=[pltpu.VMEM((nq, C, C), jnp.int32),
                        pltpu.VMEM((nq, C, C), jnp.int32),
                        pltpu.VMEM((nq, C, C), jnp.int16),
                        pltpu.VMEM((2 * n_pairs, C, LANES), BF16),
                        pltpu.VMEM((S, 2 * HEAD_DIM), BF16),
                        pltpu.VMEM((8, LANES), F32),
                        pltpu.VMEM((n_heads, C, LANES), F32),
                        pltpu.VMEM((n_heads, C, 2 * HEAD_DIM), F32)],
        compiler_params=_params("arbitrary", "arbitrary"),
    )(bias_stat, q, k, v, iq, ik, iw, bias_tiles, *slabs)
    return outs[0], [o.reshape(w.shape) for o, w in zip(outs[1:], cast)]


CONV_HALO = 32


SUBLANES = 8
CONV_SLAB = 64


def _conv_kernel(u_ref, w_ref, cb_ref, g_ref, b_ref, o_ref, cp_sc, y_sc, halo_sc, *, width):
    TS, CH = u_ref.shape
    n_rows = CONV_HALO + TS
    i = pl.program_id(1)

    @pl.when(jnp.logical_and(pl.program_id(0) == 0, i == 0))
    def _():
        halo_sc[...] = jnp.zeros(halo_sc.shape, F32)

    halo = jnp.where(i > 0, halo_sc[...], 0.0)
    halo_sc[...] = u_ref[TS - CONV_HALO:, :].astype(F32)
    pad = jnp.zeros((SUBLANES, LANES), F32)
    for c in range(CH // LANES):
        cs = slice(c * LANES, (c + 1) * LANES)
        col = jnp.concatenate([halo[:, cs], u_ref[:, cs].astype(F32), pad], axis=0)
        for r in range(SUBLANES):
            cp_sc[r, :, cs] = col[r:r + n_rows]

    base = CONV_HALO - (width - 1)
    phases = {}
    for j in range(width):
        q, r = divmod(base + j, SUBLANES)
        phases.setdefault(r, []).append((q, j))

    def slab(s, carry):
        t0 = pl.multiple_of(s * CONV_SLAB, CONV_SLAB)
        for c in range(CH // LANES):
            cs = slice(c * LANES, (c + 1) * LANES)
            acc = jnp.zeros((CONV_SLAB, LANES), F32)
            for r, taps in phases.items():
                q_lo, q_hi = taps[0][0], taps[-1][0]
                win = cp_sc[r, pl.ds(q_lo * SUBLANES + t0, CONV_SLAB + (q_hi - q_lo) * SUBLANES), cs]
                for q, j in taps:
                    off = (q - q_lo) * SUBLANES
                    acc = acc + w_ref[j:j + 1, cs] * win[off:off + CONV_SLAB]
            y_sc[pl.ds(t0, CONV_SLAB), cs] = acc + cb_ref[:, cs]
        return carry

    lax.fori_loop(0, TS // CONV_SLAB, slab, 0)
    y = y_sc[...]
    mu = jnp.mean(y, axis=-1, keepdims=True)
    yc = y - mu
    var = jnp.mean(yc * yc, axis=-1, keepdims=True)
    yn = yc * lax.rsqrt(var + NORM_EPS) * g_ref[...] + b_ref[...]
    o_ref[...] = (yn * jax.nn.sigmoid(yn)).astype(BF16)


def _conv_module(u, conv_w, conv_b, ln_g, ln_b, B, S, TS=256):
    T, CH = u.shape
    width = conv_w.shape[0]
    assert width - 1 <= CONV_HALO
    ns = S // TS
    wpad = jnp.zeros((CONV_HALO, CH), F32).at[:width].set(conv_w)
    vec = lambda a: a.reshape(1, CH)
    kern = functools.partial(_conv_kernel, width=width)
    return pl.pallas_call(
        kern,
        grid=(B, ns),
        in_specs=[pl.BlockSpec((TS, CH), lambda b, i: (b * ns + i, 0)),
                  _resident((CONV_HALO, CH)), _resident((1, CH)), _resident((1, CH)), _resident((1, CH))],
        out_specs=pl.BlockSpec((TS, CH), lambda b, i: (b * ns + i, 0)),
        out_shape=jax.ShapeDtypeStruct((T, CH), BF16),
        scratch_shapes=[pltpu.VMEM((SUBLANES, CONV_HALO + TS, CH), F32),
                        pltpu.VMEM((TS, CH), F32),
                        pltpu.VMEM((CONV_HALO, CH), F32)],
        compiler_params=_params("arbitrary", "arbitrary"),
    )(u, wpad, vec(conv_b), vec(ln_g), vec(ln_b))


def _out_proj_kernel(x_ref, a_ref, c_ref, wa_ref, wc_ref, o_ref):
    o_ref[...] = x_ref[...] + _dot(a_ref[...], wa_ref[...]) + _dot(c_ref[...], wc_ref[...])


def _out_proj(x, attn, conv, w_out, tm=512):
    T, D = x.shape
    aw, cw = attn.shape[1], conv.shape[1]
    wa = w_out[:aw].astype(BF16)
    wc = w_out[aw:].astype(BF16)
    row = lambda w: pl.BlockSpec((tm, w), lambda i: (i, 0))
    return pl.pallas_call(
        _out_proj_kernel,
        grid=(T // tm,),
        in_specs=[row(D), row(aw), row(cw), _resident(wa.shape), _resident(wc.shape)],
        out_specs=row(D),
        out_shape=jax.ShapeDtypeStruct((T, D), F32),
        compiler_params=_params("arbitrary"),
    )(x, attn, conv, wa, wc)


def _first_max(vals):
    m = vals[0]
    for v in vals[1:]:
        m = jnp.maximum(m, v)
    idx = jnp.full(m.shape, len(vals) - 1, jnp.int32)
    for k in range(len(vals) - 2, -1, -1):
        idx = jnp.where(vals[k] == m, k, idx)
    return m, idx


def _softmax_cols(cols):
    m = cols[0]
    for c in cols[1:]:
        m = jnp.maximum(m, c)
    e = [jnp.exp(c - m) for c in cols]
    s = e[0]
    for c in e[1:]:
        s = s + c
    return [c / s for c in e]


GROUP_ROWS = 8
MOE_CHUNK = 64
MOE_MAX_CHUNKS = 8


def _split3(a):
    hi = a.astype(BF16)
    r = a - hi.astype(F32)
    mid = r.astype(BF16)
    lo = (r - mid.astype(F32)).astype(BF16)
    return hi, mid, lo


PAIR_CLASSES = [(a, b) for a in range(EXPERTS_PER_GROUP) for b in range(a + 1, EXPERTS_PER_GROUP)]
CLASS_SPAN = [(min(c for c, p in enumerate(PAIR_CLASSES) if k in p), max(c for c, p in enumerate(PAIR_CLASSES) if k in p))
              for k in range(EXPERTS_PER_GROUP)]


def _moe_kernel(x_ref, nw_ref, wrt_ref, brt_ref, wg_ref, wu_ref, wd_ref, o_ref,
                hs_sc, p_sc, cs_sc, seg_sm, *, n_groups):
    TM, D = x_ref.shape
    e = pl.program_id(1)
    n_exp = pl.num_programs(1)

    @pl.when(e == 0)
    def _():
        x = x_ref[...]
        scale = lax.rsqrt(jnp.mean(x * x, axis=-1, keepdims=True) + NORM_EPS)

        def h_cols(c0):
            cs = slice(c0, c0 + 512)
            return (x_ref[:, cs] * scale * nw_ref[:, cs]).astype(BF16)

        lg = brt_ref[...] + jnp.zeros((1, TM), F32)
        for c0 in range(0, D, 512):
            lg = lg + _dot_nt(wrt_ref[:, c0:c0 + 512], h_cols(c0))
        row = lambda k: lg[k:k + 1, :]
        g_prob = _softmax_cols([row(g) for g in range(n_groups)])
        g_p, g_idx = _first_max(g_prob)
        e_logit = []
        for k in range(EXPERTS_PER_GROUP):
            v = row(GROUP_ROWS + k)
            for g in range(1, n_groups):
                v = jnp.where(g_idx == g, row(GROUP_ROWS + g * EXPERTS_PER_GROUP + k), v)
            e_logit.append(v)
        e_prob = _softmax_cols(e_logit)
        p1, i1 = _first_max(e_prob)
        rest = [jnp.where(i1 == k, -1.0, e_prob[k]) for k in range(EXPERTS_PER_GROUP)]
        p2, i2 = _first_max(rest)
        den = p1 + p2
        base = g_idx * EXPERTS_PER_GROUP
        e_io = lax.broadcasted_iota(jnp.int32, (LANES, TM), 0)
        comb_t = (jnp.where(e_io == base + i1, g_p * (p1 / den), 0.0)
                  + jnp.where(e_io == base + i2, g_p * (p2 / den), 0.0))

        lo, hi = jnp.minimum(i1, i2), jnp.maximum(i1, i2)
        cls = jnp.zeros((1, TM), jnp.int32)
        for c, (a, b) in enumerate(PAIR_CLASSES):
            cls = jnp.where(lo * EXPERTS_PER_GROUP + hi == a * EXPERTS_PER_GROUP + b, c, cls)
        cat = g_idx * len(PAIR_CLASSES) + cls
        n_cat = n_groups * len(PAIR_CLASSES)
        cat_rows = seg_sm.shape[0]
        onehot_t = jnp.where(lax.broadcasted_iota(jnp.int32, (cat_rows, TM), 0) == cat, 1.0, 0.0)
        upper = jnp.where(lax.broadcasted_iota(jnp.int32, (LANES, LANES), 0)
                          < lax.broadcasted_iota(jnp.int32, (LANES, LANES), 1), 1.0, 0.0).astype(BF16)
        before = jnp.zeros((cat_rows, 1), F32)
        ranks = []
        for b0 in range(0, TM, LANES):
            blk = onehot_t[:, b0:b0 + LANES]
            ranks.append(_dot(blk.astype(BF16), upper) + before)
            before = before + jnp.sum(blk, axis=1, keepdims=True)
        rank = jnp.concatenate(ranks, axis=1)
        start = jnp.int32(0)
        pos = jnp.zeros((1, TM), F32)
        for c in range(n_cat):
            seg_sm[c] = start
            pos = jnp.where(cat == c, start.astype(F32) + rank[c:c + 1, :], pos)
            start = start + jnp.sum(onehot_t[c:c + 1, :]).astype(jnp.int32)
        seg_sm[n_cat] = start
        pos = pos.astype(jnp.int32)

        rb = 256
        for r0 in range(0, TM, rb):
            r_io = lax.broadcasted_iota(jnp.int32, (rb, TM), 0) + r0
            p_sc[r0:r0 + rb, :] = jnp.where(r_io == pos, 1.0, 0.0).astype(BF16)
        p = p_sc[...]
        for c0 in range(0, D, 512):
            hs_sc[:, c0:c0 + 512] = _dot(p, h_cols(c0)).astype(BF16)
        cs = jnp.zeros((TM, LANES), F32)
        for part in _split3(comb_t):
            cs = cs + _dot_nt(p, part)
        cs_sc[...] = cs
        o_ref[...] = jnp.zeros(o_ref.shape, F32)

    g = e // EXPERTS_PER_GROUP
    k = e % EXPERTS_PER_GROUP
    first, last = jnp.int32(CLASS_SPAN[0][0]), jnp.int32(CLASS_SPAN[0][1])
    for kk in range(1, EXPERTS_PER_GROUP):
        first = jnp.where(k == kk, CLASS_SPAN[kk][0], first)
        last = jnp.where(k == kk, CLASS_SPAN[kk][1], last)
    start = seg_sm[g * len(PAIR_CLASSES) + first]
    end = seg_sm[g * len(PAIR_CLASSES) + last + 1]
    c_lo = start // MOE_CHUNK
    c_hi = jnp.where(end > start, (end + MOE_CHUNK - 1) // MOE_CHUNK, c_lo)

    def expert_rows(c, n_chunks):
        m = n_chunks * MOE_CHUNK
        r0 = pl.multiple_of(c * MOE_CHUNK, MOE_CHUNK)
        rows = hs_sc[pl.ds(r0, m), :]
        lane = lax.broadcasted_iota(jnp.int32, (m, LANES), 1)
        w = jnp.sum(jnp.where(lane == e, cs_sc[pl.ds(r0, m), :], 0.0), axis=-1, keepdims=True)
        a = jax.nn.silu(_dot(rows, wg_ref[0])) * _dot(rows, wu_ref[0]) * w
        o_ref[pl.ds(r0, m), :] += _dot(a.astype(BF16), wd_ref[0])

    def full(k, carry):
        expert_rows(c_lo + MOE_MAX_CHUNKS * k, MOE_MAX_CHUNKS)
        return carry

    n_chunks = c_hi - c_lo
    n_full = n_chunks // MOE_MAX_CHUNKS
    lax.fori_loop(0, n_full, full, 0)
    for rest in range(1, MOE_MAX_CHUNKS):
        @pl.when(n_chunks - n_full * MOE_MAX_CHUNKS == rest)
        def _(rest=rest):
            expert_rows(c_hi - rest, rest)

    @pl.when(e == n_exp - 1)
    def _():
        p = p_sc[...]
        for c0 in range(0, D, 512):
            ys = o_ref[:, c0:c0 + 512].astype(BF16)
            y = lax.dot_general(p, ys, (((0,), (0,)), ((), ())), preferred_element_type=F32)
            o_ref[:, c0:c0 + 512] = x_ref[:, c0:c0 + 512] + y


def _moe(x, norm_w, wg, bg, we, be, w_gate, w_up, w_down, layer, tm=1024):
    T, D = x.shape
    n_groups, n_exp = wg.shape[1], we.shape[1]
    ff = w_gate.shape[3]
    assert n_groups <= GROUP_ROWS and n_exp == n_groups * EXPERTS_PER_GROUP and T % tm == 0
    rows = GROUP_ROWS + n_exp
    cat_rows = -(-(n_groups * len(PAIR_CLASSES) + 1) // SUBLANES) * SUBLANES
    wrt = jnp.zeros((rows, D), F32).at[:n_groups].set(wg.T).at[GROUP_ROWS:].set(we.T).astype(BF16)
    brt = jnp.zeros((rows, 1), F32).at[:n_groups, 0].set(bg).at[GROUP_ROWS:, 0].set(be)
    tile = pl.BlockSpec((tm, D), lambda i, e: (i, 0))
    return pl.pallas_call(
        functools.partial(_moe_kernel, n_groups=n_groups),
        grid=(T // tm, n_exp),
        in_specs=[tile, _resident((1, D)), _resident((rows, D)), _resident((rows, 1)),
                  pl.BlockSpec((None, 1, D, ff), lambda i, e: (layer, e, 0, 0)),
                  pl.BlockSpec((None, 1, D, ff), lambda i, e: (layer, e, 0, 0)),
                  pl.BlockSpec((None, 1, ff, D), lambda i, e: (layer, e, 0, 0))],
        out_specs=tile,
        out_shape=jax.ShapeDtypeStruct((T, D), F32),
        scratch_shapes=[pltpu.VMEM((tm, D), BF16),
                        pltpu.VMEM((tm, tm), BF16),
                        pltpu.VMEM((tm, LANES), F32),
                        pltpu.SMEM((cat_rows,), jnp.int32)],
        compiler_params=_params("arbitrary", "arbitrary"),
    )(x, norm_w.reshape(1, D), wrt, brt, w_gate, w_up, w_down)


POOL_HALO = 16


def _pool_kernel(x_ref, nw_ref, pw_ref, ps_ref, o_ref, hb, pa, pb):
    TS, D = x_ref.shape
    n_groups, pc, _ = pw_ref.shape
    i = pl.program_id(1)
    nw = nw_ref[...]
    top = SUBLANES + POOL_HALO
    n_rows = top + TS

    @pl.when(jnp.logical_and(pl.program_id(0) == 0, i == 0))
    def _():
        hb[...] = jnp.zeros(hb.shape, F32)
        pa[0:SUBLANES] = jnp.zeros((SUBLANES, pc), F32)
        pb[0:SUBLANES] = jnp.zeros((SUBLANES, pc), F32)

    hb[SUBLANES:top] = jnp.where(i > 0, hb[n_rows - POOL_HALO:n_rows], 0.0)
    hb[top:] = _rms(x_ref[...], nw)
    t = i * TS + lax.broadcasted_iota(jnp.int32, (TS, 1), 0)
    for g, w in enumerate(POOL_WINDOWS):
        cs = slice(g * pc, (g + 1) * pc)
        cur = hb[top:, cs]
        src, cols, step, dst = hb, cs, 1, pa
        while step < w:
            dst[SUBLANES:n_rows, :] = src[SUBLANES:n_rows, cols] + src[SUBLANES - step:n_rows - step, cols]
            src, cols, step, dst = dst, slice(None), 2 * step, (pb if dst is pa else pa)
        s = src[top:n_rows, cols]
        count = jnp.minimum(t + 1, w).astype(F32)
        d = s / count - cur
        mixed = _dot(d.astype(BF16), pw_ref[g])
        o_ref[:, cs] = x_ref[:, cs] + ps_ref[:, cs] * mixed


def _pool_layer(x, norm_w, pool_w, pool_scale, B, S, TS=256):
    T, D = x.shape
    assert len(POOL_WINDOWS) == pool_w.shape[0] and max(POOL_WINDOWS) - 1 <= POOL_HALO
    ns = S // TS
    pw = pool_w.astype(BF16)
    return pl.pallas_call(
        _pool_kernel,
        grid=(B, ns),
        in_specs=[pl.BlockSpec((TS, D), lambda b, i: (b * ns + i, 0)),
                  _resident((1, D)), _resident(pw.shape), _resident((1, D))],
        out_specs=pl.BlockSpec((TS, D), lambda b, i: (b * ns + i, 0)),
        out_shape=jax.ShapeDtypeStruct((T, D), F32),
        scratch_shapes=[pltpu.VMEM((SUBLANES + POOL_HALO + TS, D), F32),
                        pltpu.VMEM((SUBLANES + POOL_HALO + TS, D // len(POOL_WINDOWS)), F32),
                        pltpu.VMEM((SUBLANES + POOL_HALO + TS, D // len(POOL_WINDOWS)), F32)],
        compiler_params=_params("arbitrary", "arbitrary"),
    )(x, norm_w.reshape(1, D), pw, pool_scale.reshape(1, D))


def _chunk(S):
    return 256 if S % 256 == 0 else 128


def kernel(x, rel_bias, mix_norm_e, w_in_e, q_norm_e, k_norm_e, conv_w_e, conv_b_e, conv_ln_g_e, conv_ln_b_e,
           w_out_e, mix_norm_o, pool_w_o, pool_scale_o, ffn_norm, router_group_w, router_group_b,
           router_expert_w, router_expert_b, w_gate, w_up, w_down):
    B, S, D = x.shape
    T = B * S
    depth = ffn_norm.shape[0]
    n_heads = rel_bias.shape[1]
    idx_heads = (w_in_e.shape[2] - n_heads * HEAD_DIM - 2 * HEAD_DIM - IDX_DIM - 2 * conv_w_e.shape[2]) \
        // (IDX_DIM + 1)
    k_top = min(INDEX_TOPK, S // 4)
    C = _chunk(S)
    xf = x.reshape(T, D)
    bias_tiles, bias_stat = _rel_bias_tiles(rel_bias, C)
    expert_w = None
    for l in range(depth):
        i = l // 2
        if l % 2 == 0:
            q, k, v, iq, ik, iw, u = _in_proj(xf, mix_norm_e[i], w_in_e[i], q_norm_e[i], k_norm_e[i],
                                              n_heads, idx_heads, conv_w_e.shape[2])
            cast = (w_gate, w_up, w_down) if expert_w is None else ()
            if any(_cast_slabs(w, B * (S // C)) is None for w in cast):
                cast = ()
            attn, done = _attention(q, k, v, iq, ik, iw, bias_tiles, bias_stat, B, S, k_top, C, cast)
            expert_w = done if cast else expert_w
            conv = _conv_module(u, conv_w_e[i], conv_b_e[i], conv_ln_g_e[i], conv_ln_b_e[i], B, S)
            xf = _out_proj(xf, attn, conv, w_out_e[i])
        else:
            xf = _pool_layer(xf, mix_norm_o[i], pool_w_o[i], pool_scale_o[i], B, S)
        if expert_w is None:
            expert_w = [w.astype(BF16) for w in (w_gate, w_up, w_down)]
        xf = _moe(xf, ffn_norm[l], router_group_w[l], router_group_b[l], router_expert_w[l], router_expert_b[l],
                  *expert_w, layer=l)
    return xf.reshape(B, S, D)
```

```python
import functools
import math

import jax
import jax.numpy as jnp
from jax import lax
from jax.experimental import pallas as pl
from jax.experimental.pallas import tpu as pltpu

F32 = jnp.float32
BF16 = jnp.bfloat16

NORM_EPS = 1e-6
HEAD_DIM = 128
IDX_DIM = 64
INDEX_TOPK = 256
REL_BUCKETS = 32
REL_MAX_DIST = 128
POOL_WINDOWS = (2, 4, 8, 16)
EXPERTS_PER_GROUP = 4
LANES = 128
VMEM_LIMIT = 56 * 1024 * 1024
NEG = -1e30
INT_MIN = -(2 ** 31)


def _dot(a, b):
    return jnp.dot(a, b, preferred_element_type=F32)


def _dot_nt(a, b):
    return lax.dot_general(a, b, (((1,), (1,)), ((), ())), preferred_element_type=F32)


def _rms(x, w):
    return x * lax.rsqrt(jnp.mean(x * x, axis=-1, keepdims=True) + NORM_EPS) * w


def _params(*sem):
    return pltpu.CompilerParams(dimension_semantics=sem, vmem_limit_bytes=VMEM_LIMIT)


def _resident(shape):
    nd = len(shape)
    return pl.BlockSpec(shape, lambda *_: (0,) * nd, pipeline_mode=pl.Buffered(1))


def _in_proj_kernel(x_ref, nw_ref, qn_ref, kn_ref, wq_ref, wkv_ref, wiq_ref, wikw_ref, wa_ref, wg_ref,
                    q_ref, k_ref, v_ref, iq_ref, ik_ref, iw_ref, u_ref, *, q_scale, iw_scale):
    h = _rms(x_ref[...], nw_ref[...]).astype(BF16)
    n_pairs = q_ref.shape[0] // 2
    qn = qn_ref[...] * q_scale
    for c in range(n_pairs):
        qq = _dot(h, wq_ref[:, c * 256:(c + 1) * 256])
        for s in range(2):
            qh = qq[:, s * HEAD_DIM:(s + 1) * HEAD_DIM]
            q_ref[2 * c + s] = _rms(qh, qn).astype(BF16)
    kv = _dot(h, wkv_ref[...])
    k_ref[...] = _rms(kv[:, :HEAD_DIM], kn_ref[...]).astype(BF16)
    v_ref[...] = kv[:, HEAD_DIM:].astype(BF16)
    for c in range(iq_ref.shape[0] // 2):
        r = _dot(h, wiq_ref[:, c * 256:(c + 1) * 256])
        iq_ref[2 * c] = r[:, :LANES].astype(BF16)
        iq_ref[2 * c + 1] = r[:, LANES:].astype(BF16)
    r = _dot(h, wikw_ref[...])
    ik_ref[...] = r[:, :LANES].astype(BF16)
    iw_ref[...] = r[:, LANES:] * iw_scale
    for c in range(u_ref.shape[1] // 256):
        cs = slice(c * 256, (c + 1) * 256)
        a = _dot(h, wa_ref[:, cs])
        g = _dot(h, wg_ref[:, cs])
        u_ref[:, cs] = (a * jax.nn.sigmoid(g)).astype(BF16)


def _split_w_in_kernel(w_ref, wq_ref, wkv_ref, wiq_ref, wikw_ref, wa_ref, wg_ref, *, idx_heads):
    o = 0
    for ref in (wq_ref, wkv_ref, wiq_ref):
        n = ref.shape[1]
        ref[...] = w_ref[:, o:o + n].astype(BF16)
        o += n
    wik = w_ref[:, o:o + IDX_DIM]
    wiw = w_ref[:, o + IDX_DIM:o + IDX_DIM + idx_heads]
    o += IDX_DIM + idx_heads
    pad = jnp.zeros((wik.shape[0], LANES - idx_heads), F32)
    wikw_ref[...] = jnp.concatenate([wik, wik, wiw, pad], axis=1).astype(BF16)
    for ref in (wa_ref, wg_ref):
        n = ref.shape[1]
        ref[...] = w_ref[:, o:o + n].astype(BF16)
        o += n


def _split_w_in(w_in, n_heads, idx_heads, conv_ch, tk=256):
    D, n_in = w_in.shape
    widths = (n_heads * HEAD_DIM, 2 * HEAD_DIM, idx_heads * IDX_DIM, 2 * LANES, conv_ch, conv_ch)
    assert n_in == sum(widths) - 2 * LANES + IDX_DIM + idx_heads and 2 * IDX_DIM == LANES and D % tk == 0
    return pl.pallas_call(
        functools.partial(_split_w_in_kernel, idx_heads=idx_heads),
        grid=(D // tk,),
        in_specs=[pl.BlockSpec((tk, n_in), lambda i: (i, 0))],
        out_specs=[pl.BlockSpec((tk, w), lambda i: (i, 0)) for w in widths],
        out_shape=[jax.ShapeDtypeStruct((D, w), BF16) for w in widths],
        compiler_params=_params("arbitrary"),
    )(w_in)


def _in_proj(x, norm_w, w_in, q_norm, k_norm, n_heads, idx_heads, conv_ch, tm=256):
    T, D = x.shape
    iq_w = idx_heads * IDX_DIM
    ws = _split_w_in(w_in, n_heads, idx_heads, conv_ch)
    row = lambda w: pl.BlockSpec((tm, w), lambda i: (i, 0))
    heads = lambda n: pl.BlockSpec((n, tm, LANES), lambda i: (0, i, 0))
    kern = functools.partial(_in_proj_kernel, q_scale=HEAD_DIM ** -0.5,
                             iw_scale=(idx_heads ** -0.5) * (IDX_DIM ** -0.5))
    return pl.pallas_call(
        kern,
        grid=(T // tm,),
        in_specs=[row(D), _resident((1, D)), _resident((1, HEAD_DIM)), _resident((1, HEAD_DIM))]
                 + [_resident(w.shape) for w in ws],
        out_specs=[heads(n_heads), row(HEAD_DIM), row(HEAD_DIM), heads(iq_w // LANES), row(LANES), row(LANES),
                   row(conv_ch)],
        out_shape=[jax.ShapeDtypeStruct((n_heads, T, HEAD_DIM), BF16),
                   jax.ShapeDtypeStruct((T, HEAD_DIM), BF16),
                   jax.ShapeDtypeStruct((T, HEAD_DIM), BF16),
                   jax.ShapeDtypeStruct((iq_w // LANES, T, LANES), BF16),
                   jax.ShapeDtypeStruct((T, LANES), BF16),
                   jax.ShapeDtypeStruct((T, LANES), F32),
                   jax.ShapeDtypeStruct((T, conv_ch), BF16)],
        compiler_params=_params("arbitrary"),
    )(x, norm_w.reshape(1, D), q_norm.reshape(1, HEAD_DIM), k_norm.reshape(1, HEAD_DIM), *ws)


def _rel_bias_kernel(rb_ref, o_ref, stat_ref):
    _, n_heads, C, _ = o_ref.shape
    tau = lax.broadcasted_iota(jnp.int32, (C, C), 0)
    sig = lax.broadcasted_iota(jnp.int32, (C, C), 1)
    max_exact = REL_BUCKETS // 2
    for kind in range(2):
        d = tau - sig + kind * C
        n = jnp.maximum(d, 0)
        nf = jnp.maximum(n, 1).astype(F32)
        large = max_exact + (jnp.log(nf / max_exact) / math.log(REL_MAX_DIST / max_exact)
                             * (REL_BUCKETS - max_exact)).astype(jnp.int32)
        large = jnp.minimum(large, REL_BUCKETS - 1)
        bucket = jnp.where(n < max_exact, n, large)
        for h in range(n_heads):
            b = jnp.zeros((C, C), F32)
            for bk in range(REL_BUCKETS):
                b = jnp.where(bucket == bk, rb_ref[bk, h], b)
            b = b - rb_ref[REL_BUCKETS - 1, h]
            if kind == 0:
                b = jnp.where(d < 0, NEG, b)
            o_ref[kind, h] = b
    for h in range(n_heads):
        hi = rb_ref[0, h]
        lo = rb_ref[0, h]
        for bk in range(1, REL_BUCKETS):
            hi = jnp.maximum(hi, rb_ref[bk, h])
            lo = jnp.minimum(lo, rb_ref[bk, h])
        stat_ref[0, h] = hi - rb_ref[REL_BUCKETS - 1, h]
        stat_ref[1, h] = lo - rb_ref[REL_BUCKETS - 1, h]


def _rel_bias_tiles(rel_bias, C):
    n_heads = rel_bias.shape[1]
    assert C >= REL_MAX_DIST
    return pl.pallas_call(
        _rel_bias_kernel,
        in_specs=[pl.BlockSpec(memory_space=pltpu.SMEM)],
        out_specs=[pl.BlockSpec(memory_space=pltpu.VMEM), pl.BlockSpec(memory_space=pltpu.SMEM)],
        out_shape=[jax.ShapeDtypeStruct((2, n_heads, C, C), F32), jax.ShapeDtypeStruct((2, n_heads), F32)],
        compiler_params=pltpu.CompilerParams(vmem_limit_bytes=VMEM_LIMIT),
    )(rel_bias)


SHIFT_SPAN_LIMIT = 60.0


TILE_GROUP = 8


def _grouped_loop(n, fn, group):
    def body(k, carry):
        for u in range(group):
            fn(group * k + u)
        return carry

    lax.fori_loop(0, n // group, body, 0)
    base = (n // group) * group
    size = group // 2
    while size >= 1:
        take = ((n - base) & size) != 0

        @pl.when(take)
        def _(base=base, size=size):
            for u in range(size):
                fn(base + u)
        base = base + jnp.where(take, size, 0)
        size //= 2


def _order_key(x):
    bits = pltpu.bitcast(x, jnp.int32)
    return bits ^ ((bits >> 31) & 0x7FFFFFFF)


def _row_to_col(row):
    C = row.shape[1]
    halves = []
    for part in (row >> 16, row & 0xFFFF):
        halves.append(jnp.broadcast_to(part.astype(F32), (LANES, C)).T[:, 0:1].astype(jnp.int32))
    return (halves[0] << 16) | halves[1]


def _attn_kernel(bstat_ref, q_ref, k_ref, v_ref, iq_ref, ik_ref, iw_ref, bias_ref, *rest, k_top, n_cast):
    cast_in, (o_ref, *cast_out) = rest[:n_cast], rest[n_cast:2 * n_cast + 1]
    key_sc, keyt_sc, keyt16_sc, iqm_sc, vx_sc, kmax_sc, shift_sc, acc_sc = rest[2 * n_cast + 1:]
    for src, dst in zip(cast_in, cast_out):
        dst[...] = src[...].astype(BF16)
    n_heads, C, _ = q_ref.shape
    idx_heads = iqm_sc.shape[0]
    idx_bits = max(1, (key_sc.shape[0] * C - 1).bit_length())
    qi = pl.program_id(1)
    nkv = qi + 1

    @pl.when(qi == 0)
    def _():
        vx_sc[:, :HEAD_DIM] = v_ref[...]
        vx_sc[:, HEAD_DIM:] = jnp.ones((vx_sc.shape[0], HEAD_DIM), BF16)
        kf = k_ref[...].astype(F32)
        k2 = jnp.sum(kf * kf, axis=-1, keepdims=True)
        kmax_sc[...] = jnp.broadcast_to(jnp.sqrt(jnp.max(k2, axis=0, keepdims=True)), kmax_sc.shape)

    lane = lax.broadcasted_iota(jnp.int32, (C, LANES), 1)
    for p in range(idx_heads // 2):
        qp = iq_ref[p].astype(F32)
        iqm_sc[2 * p] = jnp.where(lane < IDX_DIM, qp, 0.0).astype(BF16)
        iqm_sc[2 * p + 1] = jnp.where(lane >= IDX_DIM, qp, 0.0).astype(BF16)
    iw = iw_ref[...]
    tau = lax.broadcasted_iota(jnp.int32, (C, C), 0)
    sig = lax.broadcasted_iota(jnp.int32, (C, C), 1)

    def score_tile(j):
        off = pl.multiple_of(j * C, C)
        ikc = ik_ref[pl.ds(off, C), :]
        acc = jnp.zeros((C, C), F32)
        for hh in range(idx_heads):
            s = _dot_nt(iqm_sc[hh], ikc)
            acc = acc + jnp.maximum(s, 0.0) * iw[:, hh:hh + 1]
        acc = jnp.where(jnp.logical_and(j == qi, sig > tau), -jnp.inf, acc)
        key_sc[j] = _order_key(acc)
        kt = _order_key(acc.T)
        keyt_sc[j] = kt
        keyt16_sc[j] = (kt >> 16).astype(jnp.int16)

    _grouped_loop(nkv, score_tile, TILE_GROUP)

    SUB = 32

    def count(hit):
        def body(j, cnt):
            for r in range(C // SUB):
                cnt = cnt + hit(j, r)
            return cnt
        cnt = lax.fori_loop(0, nkv, body, jnp.zeros((SUB, C), F32))
        return jnp.sum(cnt, axis=0, keepdims=True)

    def keyt(j, r):
        return keyt_sc[j, pl.ds(r * SUB, SUB), :]

    def count16(cand16):
        one = jnp.ones((SUB, C), jnp.int16)
        zero = jnp.zeros((SUB, C), jnp.int16)

        def body(j, cnt):
            for r in range(C // SUB):
                cnt = cnt + jnp.where(keyt16_sc[j, pl.ds(r * SUB, SUB), :] >= cand16, one, zero)
            return cnt
        cnt = lax.fori_loop(0, nkv, body, zero)
        return jnp.sum(cnt.astype(jnp.int32).astype(F32), axis=0, keepdims=True)

    def bit16_body(b, carry):
        res, n_res = carry
        cand = res ^ lax.shift_left(jnp.int32(1), 31 - b)
        tot = count16((cand >> 16).astype(jnp.int16))
        ok = tot >= k_top
        return jnp.where(ok, cand, res), jnp.where(ok, tot, n_res)

    searched = nkv * C > k_top
    carry = (jnp.full((1, C), INT_MIN, jnp.int32), jnp.zeros((1, C), F32))
    top_row, n_top = lax.fori_loop(0, jnp.where(searched, 16, 0), bit16_body, carry)

    top16 = top_row >> 16

    def low_body(j, c):
        kt = keyt_sc[j]
        hi = kt >> 16
        lo = (kt & 0xFFFF) - 32768
        lo = jnp.where(hi == top16, lo, jnp.where(hi > top16, 32767, -32768))
        keyt16_sc[j] = lo.astype(jnp.int16)
        return c

    lax.fori_loop(0, jnp.where(searched, nkv, 0), low_body, 0)

    def bit16_low_body(b, carry):
        res, n_res = carry
        cand = res ^ lax.shift_left(jnp.int32(1), 31 - b)
        tot = count16(((cand & 0xFFFF) - 32768).astype(jnp.int16))
        ok = tot >= k_top
        return jnp.where(ok, cand, res), jnp.where(ok, tot, n_res)

    thr_row, n_thr = lax.fori_loop(16, jnp.where(searched, 32, 16), bit16_low_body, (top_row, n_top))
    thr = _row_to_col(thr_row)

    @pl.when(jnp.max(n_thr) > k_top)
    def _():
        need = k_top - count(lambda j, r: jnp.where(keyt(j, r) > thr_row, 1.0, 0.0))
        s_sub = lax.broadcasted_iota(jnp.int32, (SUB, C), 0)

        def idx_body(b, last):
            cand = last | lax.shift_left(jnp.int32(1), idx_bits - 1 - b)
            below = count(lambda j, r: jnp.where(
                keyt(j, r) == thr_row, jnp.where(j * C + r * SUB + s_sub < cand, 1.0, 0.0), 0.0))
            return jnp.where(below < need, cand, last)

        last = _row_to_col(lax.fori_loop(0, idx_bits, idx_body, jnp.zeros((1, C), jnp.int32)))

        def drop_body(j, carry):
            kk = key_sc[j]
            key_sc[j] = jnp.where(kk == thr, jnp.where(j * C + sig > last, kk - 1, kk), kk)
            return carry

        lax.fori_loop(0, nkv, drop_body, 0)

    kmax = kmax_sc[0:1, 0:1] * 1.001
    worst = jnp.zeros((C, 1), F32)
    for h in range(n_heads):
        qf = q_ref[h].astype(F32)
        bound = jnp.sqrt(jnp.sum(qf * qf, axis=-1, keepdims=True)) * kmax
        shift_sc[h] = jnp.broadcast_to(bound + bstat_ref[0, h], (C, LANES))
        worst = jnp.maximum(worst, 2.0 * bound + (bstat_ref[0, h] - bstat_ref[1, h]))
    loose = jnp.max(worst) > SHIFT_SPAN_LIMIT

    def logits(j, h, kind):
        off = pl.multiple_of(j * C, C)
        lg = _dot_nt(q_ref[h], k_ref[pl.ds(off, C), :])
        return lg if kind is None else lg + bias_ref[kind, h]

    def near_tiles(fn):
        @pl.when(qi > 0)
        def _():
            fn(qi - 1, 1)
            fn(qi, 0)

        @pl.when(qi == 0)
        def _():
            fn(qi, 0)

    def far_tiles(fn, group=TILE_GROUP):
        _grouped_loop(jnp.maximum(qi - 1, 0), lambda j: fn(j, None), group)

    @pl.when(loose)
    def _():
        for h in range(n_heads):
            shift_sc[h] = jnp.full((C, LANES), NEG, F32)

        def max_tile(j, kind):
            sel = key_sc[j] >= thr
            for h in range(n_heads):
                lg = jnp.where(sel, logits(j, h, kind), NEG)
                m = jnp.max(lg, axis=-1, keepdims=True)
                shift_sc[h] = jnp.maximum(shift_sc[h], jnp.broadcast_to(m, (C, LANES)))

        far_tiles(max_tile, group=1)
        near_tiles(max_tile)

    acc_sc[...] = jnp.zeros(acc_sc.shape, F32)

    def attn_tile(j, kind):
        off = pl.multiple_of(j * C, C)
        vx = vx_sc[pl.ds(off, C), :]
        sel = key_sc[j] >= thr
        for h in range(n_heads):
            sh = jnp.concatenate([shift_sc[h]] * (C // LANES), axis=1)
            p = jnp.where(sel, jnp.exp(logits(j, h, kind) - sh), 0.0)
            acc_sc[h] += _dot(p.astype(BF16), vx)

    far_tiles(attn_tile)
    near_tiles(attn_tile)
    for h in range(n_heads):
        a = acc_sc[h]
        o_ref[:, h * HEAD_DIM:(h + 1) * HEAD_DIM] = (a[:, :HEAD_DIM] / a[:, HEAD_DIM:]).astype(BF16)


CAST_SLAB_ELEMS = 1 << 20


def _cast_slabs(w, steps):
    rows = math.prod(w.shape[:-1])
    if rows % (steps * 16) != 0 or rows // steps * w.shape[-1] > CAST_SLAB_ELEMS:
        return None
    return w.reshape(steps, rows // steps, w.shape[-1])


def _attention(q, k, v, iq, ik, iw, bias_tiles, bias_stat, B, S, k_top, C, cast=()):
    n_heads, T, _ = q.shape
    n_pairs = iq.shape[0]
    nq = S // C
    slabs = [_cast_slabs(w, B * nq) for w in cast]
    kern = functools.partial(_attn_kernel, k_top=k_top, n_cast=len(slabs))
    heads = lambda n: pl.BlockSpec((n, C, LANES), lambda b, i: (0, b * nq + i, 0))
    seq = pl.BlockSpec((S, LANES), lambda b, i: (b, 0))
    slab_specs = [pl.BlockSpec((None,) + w.shape[1:], lambda b, i: (b * nq + i, 0, 0)) for w in slabs]
    outs = pl.pallas_call(
        kern,
        grid=(B, nq),
        in_specs=[pl.BlockSpec(memory_space=pltpu.SMEM),
                  heads(n_heads), seq, seq, heads(n_pairs), seq,
                  pl.BlockSpec((C, LANES), lambda b, i: (b * nq + i, 0)),
                  _resident(bias_tiles.shape)] + slab_specs,
        out_specs=[pl.BlockSpec((C, n_heads * HEAD_DIM), lambda b, i: (b * nq + i, 0))] + slab_specs,
        out_shape=[jax.ShapeDtypeStruct((T, n_heads * HEAD_DIM), BF16)]
                  + [jax.ShapeDtypeStruct(w.shape, BF16) for w in slabs],
        scratch_shapes=[pltpu.VMEM((nq, C, C), jnp.int32),
                        pltpu.VMEM((nq, C, C), jnp.int32),
                        pltpu.VMEM((nq, C, C), jnp.int16),
                        pltpu.VMEM((2 * n_pairs, C, LANES), BF16),
                        pltpu.VMEM((S, 2 * HEAD_DIM), BF16),
                        pltpu.VMEM((8, LANES), F32),
                        pltpu.VMEM((n_heads, C, LANES), F32),
                        pltpu.VMEM((n_heads, C, 2 * HEAD_DIM), F32)],
        compiler_params=_params("arbitrary", "arbitrary"),
    )(bias_stat, q, k, v, iq, ik, iw, bias_tiles, *slabs)
    return outs[0], [o.reshape(w.shape) for o, w in zip(outs[1:], cast)]


CONV_HALO = 32


SUBLANES = 8
CONV_SLAB = 64


def _conv_kernel(u_ref, w_ref, cb_ref, g_ref, b_ref, o_ref, cp_sc, y_sc, halo_sc, *, width):
    TS, CH = u_ref.shape
    n_rows = CONV_HALO + TS
    i = pl.program_id(1)

    @pl.when(jnp.logical_and(pl.program_id(0) == 0, i == 0))
    def _():
        halo_sc[...] = jnp.zeros(halo_sc.shape, F32)

    halo = jnp.where(i > 0, halo_sc[...], 0.0)
    halo_sc[...] = u_ref[TS - CONV_HALO:, :].astype(F32)
    pad = jnp.zeros((SUBLANES, LANES), F32)
    for c in range(CH // LANES):
        cs = slice(c * LANES, (c + 1) * LANES)
        col = jnp.concatenate([halo[:, cs], u_ref[:, cs].astype(F32), pad], axis=0)
        for r in range(SUBLANES):
            cp_sc[r, :, cs] = col[r:r + n_rows]

    base = CONV_HALO - (width - 1)
    phases = {}
    for j in range(width):
        q, r = divmod(base + j, SUBLANES)
        phases.setdefault(r, []).append((q, j))

    def slab(s, carry):
        t0 = pl.multiple_of(s * CONV_SLAB, CONV_SLAB)
        for c in range(CH // LANES):
            cs = slice(c * LANES, (c + 1) * LANES)
            acc = jnp.zeros((CONV_SLAB, LANES), F32)
            for r, taps in phases.items():
                q_lo, q_hi = taps[0][0], taps[-1][0]
                win = cp_sc[r, pl.ds(q_lo * SUBLANES + t0, CONV_SLAB + (q_hi - q_lo) * SUBLANES), cs]
                for q, j in taps:
                    off = (q - q_lo) * SUBLANES
                    acc = acc + w_ref[j:j + 1, cs] * win[off:off + CONV_SLAB]
            y_sc[pl.ds(t0, CONV_SLAB), cs] = acc + cb_ref[:, cs]
        return carry

    lax.fori_loop(0, TS // CONV_SLAB, slab, 0)
    y = y_sc[...]
    mu = jnp.mean(y, axis=-1, keepdims=True)
    yc = y - mu
    var = jnp.mean(yc * yc, axis=-1, keepdims=True)
    yn = yc * lax.rsqrt(var + NORM_EPS) * g_ref[...] + b_ref[...]
    o_ref[...] = (yn * jax.nn.sigmoid(yn)).astype(BF16)


def _conv_module(u, conv_w, conv_b, ln_g, ln_b, B, S, TS=256):
    T, CH = u.shape
    width = conv_w.shape[0]
    assert width - 1 <= CONV_HALO
    ns = S // TS
    wpad = jnp.zeros((CONV_HALO, CH), F32).at[:width].set(conv_w)
    vec = lambda a: a.reshape(1, CH)
    kern = functools.partial(_conv_kernel, width=width)
    return pl.pallas_call(
        kern,
        grid=(B, ns),
        in_specs=[pl.BlockSpec((TS, CH), lambda b, i: (b * ns + i, 0)),
                  _resident((CONV_HALO, CH)), _resident((1, CH)), _resident((1, CH)), _resident((1, CH))],
        out_specs=pl.BlockSpec((TS, CH), lambda b, i: (b * ns + i, 0)),
        out_shape=jax.ShapeDtypeStruct((T, CH), BF16),
        scratch_shapes=[pltpu.VMEM((SUBLANES, CONV_HALO + TS, CH), F32),
                        pltpu.VMEM((TS, CH), F32),
                        pltpu.VMEM((CONV_HALO, CH), F32)],
        compiler_params=_params("arbitrary", "arbitrary"),
    )(u, wpad, vec(conv_b), vec(ln_g), vec(ln_b))


def _out_proj_kernel(x_ref, a_ref, c_ref, wa_ref, wc_ref, o_ref):
    o_ref[...] = x_ref[...] + _dot(a_ref[...], wa_ref[...]) + _dot(c_ref[...], wc_ref[...])


def _out_proj(x, attn, conv, w_out, tm=512):
    T, D = x.shape
    aw, cw = attn.shape[1], conv.shape[1]
    wa = w_out[:aw].astype(BF16)
    wc = w_out[aw:].astype(BF16)
    row = lambda w: pl.BlockSpec((tm, w), lambda i: (i, 0))
    return pl.pallas_call(
        _out_proj_kernel,
        grid=(T // tm,),
        in_specs=[row(D), row(aw), row(cw), _resident(wa.shape), _resident(wc.shape)],
        out_specs=row(D),
        out_shape=jax.ShapeDtypeStruct((T, D), F32),
        compiler_params=_params("arbitrary"),
    )(x, attn, conv, wa, wc)


def _first_max(vals):
    m = vals[0]
    for v in vals[1:]:
        m = jnp.maximum(m, v)
    idx = jnp.full(m.shape, len(vals) - 1, jnp.int32)
    for k in range(len(vals) - 2, -1, -1):
        idx = jnp.where(vals[k] == m, k, idx)
    return m, idx


def _softmax_cols(cols):
    m = cols[0]
    for c in cols[1:]:
        m = jnp.maximum(m, c)
    e = [jnp.exp(c - m) for c in cols]
    s = e[0]
    for c in e[1:]:
        s = s + c
    return [c / s for c in e]


GROUP_ROWS = 8
MOE_CHUNK = 64
MOE_MAX_CHUNKS = 8


def _split3(a):
    hi = a.astype(BF16)
    r = a - hi.astype(F32)
    mid = r.astype(BF16)
    lo = (r - mid.astype(F32)).astype(BF16)
    return hi, mid, lo


PAIR_CLASSES = [(a, b) for a in range(EXPERTS_PER_GROUP) for b in range(a + 1, EXPERTS_PER_GROUP)]
CLASS_SPAN = [(min(c for c, p in enumerate(PAIR_CLASSES) if k in p), max(c for c, p in enumerate(PAIR_CLASSES) if k in p))
              for k in range(EXPERTS_PER_GROUP)]


def _moe_kernel(x_ref, nw_ref, wrt_ref, brt_ref, wg_ref, wu_ref, wd_ref, o_ref,
                hs_sc, p_sc, cs_sc, seg_sm, *, n_groups):
    TM, D = x_ref.shape
    e = pl.program_id(1)
    n_exp = pl.num_programs(1)

    @pl.when(e == 0)
    def _():
        x = x_ref[...]
        scale = lax.rsqrt(jnp.mean(x * x, axis=-1, keepdims=True) + NORM_EPS)

        def h_cols(c0):
            cs = slice(c0, c0 + 512)
            return (x_ref[:, cs] * scale * nw_ref[:, cs]).astype(BF16)

        lg = brt_ref[...] + jnp.zeros((1, TM), F32)
        for c0 in range(0, D, 512):
            lg = lg + _dot_nt(wrt_ref[:, c0:c0 + 512], h_cols(c0))
        row = lambda k: lg[k:k + 1, :]
        g_prob = _softmax_cols([row(g) for g in range(n_groups)])
        g_p, g_idx = _first_max(g_prob)
        e_logit = []
        for k in range(EXPERTS_PER_GROUP):
            v = row(GROUP_ROWS + k)
            for g in range(1, n_groups):
                v = jnp.where(g_idx == g, row(GROUP_ROWS + g * EXPERTS_PER_GROUP + k), v)
            e_logit.append(v)
        e_prob = _softmax_cols(e_logit)
        p1, i1 = _first_max(e_prob)
        rest = [jnp.where(i1 == k, -1.0, e_prob[k]) for k in range(EXPERTS_PER_GROUP)]
        p2, i2 = _first_max(rest)
        den = p1 + p2
        base = g_idx * EXPERTS_PER_GROUP
        e_io = lax.broadcasted_iota(jnp.int32, (LANES, TM), 0)
        comb_t = (jnp.where(e_io == base + i1, g_p * (p1 / den), 0.0)
                  + jnp.where(e_io == base + i2, g_p * (p2 / den), 0.0))

        lo, hi = jnp.minimum(i1, i2), jnp.maximum(i1, i2)
        cls = jnp.zeros((1, TM), jnp.int32)
        for c, (a, b) in enumerate(PAIR_CLASSES):
            cls = jnp.where(lo * EXPERTS_PER_GROUP + hi == a * EXPERTS_PER_GROUP + b, c, cls)
        cat = g_idx * len(PAIR_CLASSES) + cls
        n_cat = n_groups * len(PAIR_CLASSES)
        cat_rows = seg_sm.shape[0]
        onehot_t = jnp.where(lax.broadcasted_iota(jnp.int32, (cat_rows, TM), 0) == cat, 1.0, 0.0)
        upper = jnp.where(lax.broadcasted_iota(jnp.int32, (LANES, LANES), 0)
                          < lax.broadcasted_iota(jnp.int32, (LANES, LANES), 1), 1.0, 0.0).astype(BF16)
        before = jnp.zeros((cat_rows, 1), F32)
        ranks = []
        for b0 in range(0, TM, LANES):
            blk = onehot_t[:, b0:b0 + LANES]
            ranks.append(_dot(blk.astype(BF16), upper) + before)
            before = before + jnp.sum(blk, axis=1, keepdims=True)
        rank = jnp.concatenate(ranks, axis=1)
        start = jnp.int32(0)
        pos = jnp.zeros((1, TM), F32)
        for c in range(n_cat):
            seg_sm[c] = start
            pos = jnp.where(cat == c, start.astype(F32) + rank[c:c + 1, :], pos)
            start = start + jnp.sum(onehot_t[c:c + 1, :]).astype(jnp.int32)
        seg_sm[n_cat] = start
        pos = pos.astype(jnp.int32)

        rb = 256
        for r0 in range(0, TM, rb):
            r_io = lax.broadcasted_iota(jnp.int32, (rb, TM), 0) + r0
            p_sc[r0:r0 + rb, :] = jnp.where(r_io == pos, 1.0, 0.0).astype(BF16)
        p = p_sc[...]
        for c0 in range(0, D, 512):
            hs_sc[:, c0:c0 + 512] = _dot(p, h_cols(c0)).astype(BF16)
        cs = jnp.zeros((TM, LANES), F32)
        for part in _split3(comb_t):
            cs = cs + _dot_nt(p, part)
        cs_sc[...] = cs
        o_ref[...] = jnp.zeros(o_ref.shape, F32)

    g = e // EXPERTS_PER_GROUP
    k = e % EXPERTS_PER_GROUP
    first, last = jnp.int32(CLASS_SPAN[0][0]), jnp.int32(CLASS_SPAN[0][1])
    for kk in range(1, EXPERTS_PER_GROUP):
        first = jnp.where(k == kk, CLASS_SPAN[kk][0], first)
        last = jnp.where(k == kk, CLASS_SPAN[kk][1], last)
    start = seg_sm[g * len(PAIR_CLASSES) + first]
    end = seg_sm[g * len(PAIR_CLASSES) + last + 1]
    c_lo = start // MOE_CHUNK
    c_hi = jnp.where(end > start, (end + MOE_CHUNK - 1) // MOE_CHUNK, c_lo)

    def expert_rows(c, n_chunks):
        m = n_chunks * MOE_CHUNK
        r0 = pl.multiple_of(c * MOE_CHUNK, MOE_CHUNK)
        rows = hs_sc[pl.ds(r0, m), :]
        lane = lax.broadcasted_iota(jnp.int32, (m, LANES), 1)
        w = jnp.sum(jnp.where(lane == e, cs_sc[pl.ds(r0, m), :], 0.0), axis=-1, keepdims=True)
        a = jax.nn.silu(_dot(rows, wg_ref[0])) * _dot(rows, wu_ref[0]) * w
        o_ref[pl.ds(r0, m), :] += _dot(a.astype(BF16), wd_ref[0])

    def full(k, carry):
        expert_rows(c_lo + MOE_MAX_CHUNKS * k, MOE_MAX_CHUNKS)
        return carry

    n_chunks = c_hi - c_lo
    n_full = n_chunks // MOE_MAX_CHUNKS
    lax.fori_loop(0, n_full, full, 0)
    for rest in range(1, MOE_MAX_CHUNKS):
        @pl.when(n_chunks - n_full * MOE_MAX_CHUNKS == rest)
        def _(rest=rest):
            expert_rows(c_hi - rest, rest)

    @pl.when(e == n_exp - 1)
    def _():
        p = p_sc[...]
        for c0 in range(0, D, 512):
            ys = o_ref[:, c0:c0 + 512].astype(BF16)
            y = lax.dot_general(p, ys, (((0,), (0,)), ((), ())), preferred_element_type=F32)
            o_ref[:, c0:c0 + 512] = x_ref[:, c0:c0 + 512] + y


def _moe(x, norm_w, wg, bg, we, be, w_gate, w_up, w_down, layer, tm=1024):
    T, D = x.shape
    n_groups, n_exp = wg.shape[1], we.shape[1]
    ff = w_gate.shape[3]
    assert n_groups <= GROUP_ROWS and n_exp == n_groups * EXPERTS_PER_GROUP and T % tm == 0
    rows = GROUP_ROWS + n_exp
    cat_rows = -(-(n_groups * len(PAIR_CLASSES) + 1) // SUBLANES) * SUBLANES
    wrt = jnp.zeros((rows, D), F32).at[:n_groups].set(wg.T).at[GROUP_ROWS:].set(we.T).astype(BF16)
    brt = jnp.zeros((rows, 1), F32).at[:n_groups, 0].set(bg).at[GROUP_ROWS:, 0].set(be)
    tile = pl.BlockSpec((tm, D), lambda i, e: (i, 0))
    return pl.pallas_call(
        functools.partial(_moe_kernel, n_groups=n_groups),
        grid=(T // tm, n_exp),
        in_specs=[tile, _resident((1, D)), _resident((rows, D)), _resident((rows, 1)),
                  pl.BlockSpec((None, 1, D, ff), lambda i, e: (layer, e, 0, 0)),
                  pl.BlockSpec((None, 1, D, ff), lambda i, e: (layer, e, 0, 0)),
                  pl.BlockSpec((None, 1, ff, D), lambda i, e: (layer, e, 0, 0))],
        out_specs=tile,
        out_shape=jax.ShapeDtypeStruct((T, D), F32),
        scratch_shapes=[pltpu.VMEM((tm, D), BF16),
                        pltpu.VMEM((tm, tm), BF16),
                        pltpu.VMEM((tm, LANES), F32),
                        pltpu.SMEM((cat_rows,), jnp.int32)],
        compiler_params=_params("arbitrary", "arbitrary"),
    )(x, norm_w.reshape(1, D), wrt, brt, w_gate, w_up, w_down)


POOL_HALO = 16


def _pool_kernel(x_ref, nw_ref, pw_ref, ps_ref, o_ref, hb, pa, pb):
    TS, D = x_ref.shape
    n_groups, pc, _ = pw_ref.shape
    i = pl.program_id(1)
    nw = nw_ref[...]
    top = SUBLANES + POOL_HALO
    n_rows = top + TS

    @pl.when(jnp.logical_and(pl.program_id(0) == 0, i == 0))
    def _():
        hb[...] = jnp.zeros(hb.shape, F32)
        pa[0:SUBLANES] = jnp.zeros((SUBLANES, pc), F32)
        pb[0:SUBLANES] = jnp.zeros((SUBLANES, pc), F32)

    hb[SUBLANES:top] = jnp.where(i > 0, hb[n_rows - POOL_HALO:n_rows], 0.0)
    hb[top:] = _rms(x_ref[...], nw)
    t = i * TS + lax.broadcasted_iota(jnp.int32, (TS, 1), 0)
    for g, w in enumerate(POOL_WINDOWS):
        cs = slice(g * pc, (g + 1) * pc)
        cur = hb[top:, cs]
        src, cols, step, dst = hb, cs, 1, pa
        while step < w:
            dst[SUBLANES:n_rows, :] = src[SUBLANES:n_rows, cols] + src[SUBLANES - step:n_rows - step, cols]
            src, cols, step, dst = dst, slice(None), 2 * step, (pb if dst is pa else pa)
        s = src[top:n_rows, cols]
        count = jnp.minimum(t + 1, w).astype(F32)
        d = s / count - cur
        mixed = _dot(d.astype(BF16), pw_ref[g])
        o_ref[:, cs] = x_ref[:, cs] + ps_ref[:, cs] * mixed


def _pool_layer(x, norm_w, pool_w, pool_scale, B, S, TS=256):
    T, D = x.shape
    assert len(POOL_WINDOWS) == pool_w.shape[0] and max(POOL_WINDOWS) - 1 <= POOL_HALO
    ns = S // TS
    pw = pool_w.astype(BF16)
    return pl.pallas_call(
        _pool_kernel,
        grid=(B, ns),
        in_specs=[pl.BlockSpec((TS, D), lambda b, i: (b * ns + i, 0)),
                  _resident((1, D)), _resident(pw.shape), _resident((1, D))],
        out_specs=pl.BlockSpec((TS, D), lambda b, i: (b * ns + i, 0)),
        out_shape=jax.ShapeDtypeStruct((T, D), F32),
        scratch_shapes=[pltpu.VMEM((SUBLANES + POOL_HALO + TS, D), F32),
                        pltpu.VMEM((SUBLANES + POOL_HALO + TS, D // len(POOL_WINDOWS)), F32),
                        pltpu.VMEM((SUBLANES + POOL_HALO + TS, D // len(POOL_WINDOWS)), F32)],
        compiler_params=_params("arbitrary", "arbitrary"),
    )(x, norm_w.reshape(1, D), pw, pool_scale.reshape(1, D))


def _chunk(S):
    return 256 if S % 256 == 0 else 128


def kernel(x, rel_bias, mix_norm_e, w_in_e, q_norm_e, k_norm_e, conv_w_e, conv_b_e, conv_ln_g_e, conv_ln_b_e,
           w_out_e, mix_norm_o, pool_w_o, pool_scale_o, ffn_norm, router_group_w, router_group_b,
           router_expert_w, router_expert_b, w_gate, w_up, w_down):
    B, S, D = x.shape
    T = B * S
    depth = ffn_norm.shape[0]
    n_heads = rel_bias.shape[1]
    idx_heads = (w_in_e.shape[2] - n_heads * HEAD_DIM - 2 * HEAD_DIM - IDX_DIM - 2 * conv_w_e.shape[2]) \
        // (IDX_DIM + 1)
    k_top = min(INDEX_TOPK, S // 4)
    C = _chunk(S)
    xf = x.reshape(T, D)
    bias_tiles, bias_stat = _rel_bias_tiles(rel_bias, C)
    expert_w = None
    for l in range(depth):
        i = l // 2
        if l % 2 == 0:
            q, k, v, iq, ik, iw, u = _in_proj(xf, mix_norm_e[i], w_in_e[i], q_norm_e[i], k_norm_e[i],
                                              n_heads, idx_heads, conv_w_e.shape[2])
            cast = (w_gate, w_up, w_down) if expert_w is None else ()
            if any(_cast_slabs(w, B * (S // C)) is None for w in cast):
                cast = ()
            attn, done = _attention(q, k, v, iq, ik, iw, bias_tiles, bias_stat, B, S, k_top, C, cast)
            expert_w = done if cast else expert_w
            conv = _conv_module(u, conv_w_e[i], conv_b_e[i], conv_ln_g_e[i], conv_ln_b_e[i], B, S)
            xf = _out_proj(xf, attn, conv, w_out_e[i])
        else:
            xf = _pool_layer(xf, mix_norm_o[i], pool_w_o[i], pool_scale_o[i], B, S)
        if expert_w is None:
            expert_w = [w.astype(BF16) for w in (w_gate, w_up, w_down)]
        xf = _moe(xf, ffn_norm[l], router_group_w[l], router_group_b[l], router_expert_w[l], router_expert_b[l],
                  *expert_w, layer=l)
    return xf.reshape(B, S, D)
```

```python
import functools
import math

import jax
import jax.numpy as jnp
from jax import lax
from jax.experimental import pallas as pl
from jax.experimental.pallas import tpu as pltpu

F32 = jnp.float32
BF16 = jnp.bfloat16

NORM_EPS = 1e-6
HEAD_DIM = 128
IDX_DIM = 64
INDEX_TOPK = 256
REL_BUCKETS = 32
REL_MAX_DIST = 128
POOL_WINDOWS = (2, 4, 8, 16)
EXPERTS_PER_GROUP = 4
LANES = 128
VMEM_LIMIT = 56 * 1024 * 1024
NEG = -1e30
INT_MIN = -(2 ** 31)


def _dot(a, b):
    return jnp.dot(a, b, preferred_element_type=F32)


def _dot_nt(a, b):
    return lax.dot_general(a, b, (((1,), (1,)), ((), ())), preferred_element_type=F32)


def _rms(x, w):
    return x * lax.rsqrt(jnp.mean(x * x, axis=-1, keepdims=True) + NORM_EPS) * w


def _params(*sem):
    return pltpu.CompilerParams(dimension_semantics=sem, vmem_limit_bytes=VMEM_LIMIT)


def _resident(shape):
    nd = len(shape)
    return pl.BlockSpec(shape, lambda *_: (0,) * nd, pipeline_mode=pl.Buffered(1))


def _in_proj_kernel(x_ref, nw_ref, qn_ref, kn_ref, wq_ref, wkv_ref, wiq_ref, wikw_ref, wa_ref, wg_ref,
                    q_ref, k_ref, v_ref, iq_ref, ik_ref, iw_ref, u_ref, *, q_scale, iw_scale):
    h = _rms(x_ref[...], nw_ref[...]).astype(BF16)
    n_pairs = q_ref.shape[0] // 2
    qn = qn_ref[...] * q_scale
    for c in range(n_pairs):
        qq = _dot(h, wq_ref[:, c * 256:(c + 1) * 256])
        for s in range(2):
            qh = qq[:, s * HEAD_DIM:(s + 1) * HEAD_DIM]
            q_ref[2 * c + s] = _rms(qh, qn).astype(BF16)
    kv = _dot(h, wkv_ref[...])
    k_ref[...] = _rms(kv[:, :HEAD_DIM], kn_ref[...]).astype(BF16)
    v_ref[...] = kv[:, HEAD_DIM:].astype(BF16)
    for c in range(iq_ref.shape[0] // 2):
        r = _dot(h, wiq_ref[:, c * 256:(c + 1) * 256])
        iq_ref[2 * c] = r[:, :LANES].astype(BF16)
        iq_ref[2 * c + 1] = r[:, LANES:].astype(BF16)
    r = _dot(h, wikw_ref[...])
    ik_ref[...] = r[:, :LANES].astype(BF16)
    iw_ref[...] = r[:, LANES:] * iw_scale
    for c in range(u_ref.shape[1] // 256):
        cs = slice(c * 256, (c + 1) * 256)
        a = _dot(h, wa_ref[:, cs])
        g = _dot(h, wg_ref[:, cs])
        u_ref[:, cs] = (a * jax.nn.sigmoid(g)).astype(BF16)


def _split_w_in_kernel(w_ref, wq_ref, wkv_ref, wiq_ref, wikw_ref, wa_ref, wg_ref, *, idx_heads):
    o = 0
    for ref in (wq_ref, wkv_ref, wiq_ref):
        n = ref.shape[1]
        ref[...] = w_ref[:, o:o + n].astype(BF16)
        o += n
    wik = w_ref[:, o:o + IDX_DIM]
    wiw = w_ref[:, o + IDX_DIM:o + IDX_DIM + idx_heads]
    o += IDX_DIM + idx_heads
    pad = jnp.zeros((wik.shape[0], LANES - idx_heads), F32)
    wikw_ref[...] = jnp.concatenate([wik, wik, wiw, pad], axis=1).astype(BF16)
    for ref in (wa_ref, wg_ref):
        n = ref.shape[1]
        ref[...] = w_ref[:, o:o + n].astype(BF16)
        o += n


def _split_w_in(w_in, n_heads, idx_heads, conv_ch, tk=256):
    D, n_in = w_in.shape
    widths = (n_heads * HEAD_DIM, 2 * HEAD_DIM, idx_heads * IDX_DIM, 2 * LANES, conv_ch, conv_ch)
    assert n_in == sum(widths) - 2 * LANES + IDX_DIM + idx_heads and 2 * IDX_DIM == LANES and D % tk == 0
    return pl.pallas_call(
        functools.partial(_split_w_in_kernel, idx_heads=idx_heads),
        grid=(D // tk,),
        in_specs=[pl.BlockSpec((tk, n_in), lambda i: (i, 0))],
        out_specs=[pl.BlockSpec((tk, w), lambda i: (i, 0)) for w in widths],
        out_shape=[jax.ShapeDtypeStruct((D, w), BF16) for w in widths],
        compiler_params=_params("arbitrary"),
    )(w_in)


def _in_proj(x, norm_w, w_in, q_norm, k_norm, n_heads, idx_heads, conv_ch, tm=256):
    T, D = x.shape
    iq_w = idx_heads * IDX_DIM
    ws = _split_w_in(w_in, n_heads, idx_heads, conv_ch)
    row = lambda w: pl.BlockSpec((tm, w), lambda i: (i, 0))
    heads = lambda n: pl.BlockSpec((n, tm, LANES), lambda i: (0, i, 0))
    kern = functools.partial(_in_proj_kernel, q_scale=HEAD_DIM ** -0.5,
                             iw_scale=(idx_heads ** -0.5) * (IDX_DIM ** -0.5))
    return pl.pallas_call(
        kern,
        grid=(T // tm,),
        in_specs=[row(D), _resident((1, D)), _resident((1, HEAD_DIM)), _resident((1, HEAD_DIM))]
                 + [_resident(w.shape) for w in ws],
        out_specs=[heads(n_heads), row(HEAD_DIM), row(HEAD_DIM), heads(iq_w // LANES), row(LANES), row(LANES),
                   row(conv_ch)],
        out_shape=[jax.ShapeDtypeStruct((n_heads, T, HEAD_DIM), BF16),
                   jax.ShapeDtypeStruct((T, HEAD_DIM), BF16),
                   jax.ShapeDtypeStruct((T, HEAD_DIM), BF16),
                   jax.ShapeDtypeStruct((iq_w // LANES, T, LANES), BF16),
                   jax.ShapeDtypeStruct((T, LANES), BF16),
                   jax.ShapeDtypeStruct((T, LANES), F32),
                   jax.ShapeDtypeStruct((T, conv_ch), BF16)],
        compiler_params=_params("arbitrary"),
    )(x, norm_w.reshape(1, D), q_norm.reshape(1, HEAD_DIM), k_norm.reshape(1, HEAD_DIM), *ws)


def _rel_bias_kernel(rb_ref, o_ref, stat_ref):
    _, n_heads, C, _ = o_ref.shape
    tau = lax.broadcasted_iota(jnp.int32, (C, C), 0)
    sig = lax.broadcasted_iota(jnp.int32, (C, C), 1)
    max_exact = REL_BUCKETS // 2
    for kind in range(2):
        d = tau - sig + kind * C
        n = jnp.maximum(d, 0)
        nf = jnp.maximum(n, 1).astype(F32)
        large = max_exact + (jnp.log(nf / max_exact) / math.log(REL_MAX_DIST / max_exact)
                             * (REL_BUCKETS - max_exact)).astype(jnp.int32)
        large = jnp.minimum(large, REL_BUCKETS - 1)
        bucket = jnp.where(n < max_exact, n, large)
        for h in range(n_heads):
            b = jnp.zeros((C, C), F32)
            for bk in range(REL_BUCKETS):
                b = jnp.where(bucket == bk, rb_ref[bk, h], b)
            b = b - rb_ref[REL_BUCKETS - 1, h]
            if kind == 0:
                b = jnp.where(d < 0, NEG, b)
            o_ref[kind, h] = b
    for h in range(n_heads):
        hi = rb_ref[0, h]
        lo = rb_ref[0, h]
        for bk in range(1, REL_BUCKETS):
            hi = jnp.maximum(hi, rb_ref[bk, h])
            lo = jnp.minimum(lo, rb_ref[bk, h])
        stat_ref[0, h] = hi - rb_ref[REL_BUCKETS - 1, h]
        stat_ref[1, h] = lo - rb_ref[REL_BUCKETS - 1, h]


def _rel_bias_tiles(rel_bias, C):
    n_heads = rel_bias.shape[1]
    assert C >= REL_MAX_DIST
    return pl.pallas_call(
        _rel_bias_kernel,
        in_specs=[pl.BlockSpec(memory_space=pltpu.SMEM)],
        out_specs=[pl.BlockSpec(memory_space=pltpu.VMEM), pl.BlockSpec(memory_space=pltpu.SMEM)],
        out_shape=[jax.ShapeDtypeStruct((2, n_heads, C, C), F32), jax.ShapeDtypeStruct((2, n_heads), F32)],
        compiler_params=pltpu.CompilerParams(vmem_limit_bytes=VMEM_LIMIT),
    )(rel_bias)


SHIFT_SPAN_LIMIT = 60.0


TILE_GROUP = 8


def _grouped_loop(n, fn, group):
    def body(k, carry):
        for u in range(group):
            fn(group * k + u)
        return carry

    lax.fori_loop(0, n // group, body, 0)
    base = (n // group) * group
    size = group // 2
    while size >= 1:
        take = ((n - base) & size) != 0

        @pl.when(take)
        def _(base=base, size=size):
            for u in range(size):
                fn(base + u)
        base = base + jnp.where(take, size, 0)
        size //= 2


def _order_key(x):
    bits = pltpu.bitcast(x, jnp.int32)
    return bits ^ ((bits >> 31) & 0x7FFFFFFF)


def _row_to_col(row):
    C = row.shape[1]
    halves = []
    for part in (row >> 16, row & 0xFFFF):
        halves.append(jnp.broadcast_to(part.astype(F32), (LANES, C)).T[:, 0:1].astype(jnp.int32))
    return (halves[0] << 16) | halves[1]


def _attn_kernel(bstat_ref, q_ref, k_ref, v_ref, iq_ref, ik_ref, iw_ref, bias_ref, *rest, k_top, n_cast):
    cast_in, (o_ref, *cast_out) = rest[:n_cast], rest[n_cast:2 * n_cast + 1]
    key_sc, keyt_sc, keyt16_sc, iqm_sc, vx_sc, kmax_sc, shift_sc, acc_sc = rest[2 * n_cast + 1:]
    for src, dst in zip(cast_in, cast_out):
        dst[...] = src[...].astype(BF16)
    n_heads, C, _ = q_ref.shape
    idx_heads = iqm_sc.shape[0]
    idx_bits = max(1, (key_sc.shape[0] * C - 1).bit_length())
    qi = pl.program_id(1)
    nkv = qi + 1

    @pl.when(qi == 0)
    def _():
        vx_sc[:, :HEAD_DIM] = v_ref[...]
        vx_sc[:, HEAD_DIM:] = jnp.ones((vx_sc.shape[0], HEAD_DIM), BF16)
        kf = k_ref[...].astype(F32)
        k2 = jnp.sum(kf * kf, axis=-1, keepdims=True)
        kmax_sc[...] = jnp.broadcast_to(jnp.sqrt(jnp.max(k2, axis=0, keepdims=True)), kmax_sc.shape)

    lane = lax.broadcasted_iota(jnp.int32, (C, LANES), 1)
    for p in range(idx_heads // 2):
        qp = iq_ref[p].astype(F32)
        iqm_sc[2 * p] = jnp.where(lane < IDX_DIM, qp, 0.0).astype(BF16)
        iqm_sc[2 * p + 1] = jnp.where(lane >= IDX_DIM, qp, 0.0).astype(BF16)
    iw = iw_ref[...]
    tau = lax.broadcasted_iota(jnp.int32, (C, C), 0)
    sig = lax.broadcasted_iota(jnp.int32, (C, C), 1)

    def score_tile(j):
        off = pl.multiple_of(j * C, C)
        ikc = ik_ref[pl.ds(off, C), :]
        acc = jnp.zeros((C, C), F32)
        for hh in range(idx_heads):
            s = _dot_nt(iqm_sc[hh], ikc)
            acc = acc + jnp.maximum(s, 0.0) * iw[:, hh:hh + 1]
        acc = jnp.where(jnp.logical_and(j == qi, sig > tau), -jnp.inf, acc)
        key_sc[j] = _order_key(acc)
        kt = _order_key(acc.T)
        keyt_sc[j] = kt
        keyt16_sc[j] = (kt >> 16).astype(jnp.int16)

    _grouped_loop(nkv, score_tile, TILE_GROUP)

    SUB = 32

    def count(hit):
        def body(j, cnt):
            for r in range(C // SUB):
                cnt = cnt + hit(j, r)
            return cnt
        cnt = lax.fori_loop(0, nkv, body, jnp.zeros((SUB, C), F32))
        return jnp.sum(cnt, axis=0, keepdims=True)

    def keyt(j, r):
        return keyt_sc[j, pl.ds(r * SUB, SUB), :]

    def count16(cand16):
        one = jnp.ones((SUB, C), jnp.int16)
        zero = jnp.zeros((SUB, C), jnp.int16)

        def body(j, cnt):
            for r in range(C // SUB):
                cnt = cnt + jnp.where(keyt16_sc[j, pl.ds(r * SUB, SUB), :] >= cand16, one, zero)
            return cnt
        cnt = lax.fori_loop(0, nkv, body, zero)
        return jnp.sum(cnt.astype(jnp.int32).astype(F32), axis=0, keepdims=True)

    def bit16_body(b, carry):
        res, n_res = carry
        cand = res ^ lax.shift_left(jnp.int32(1), 31 - b)
        tot = count16((cand >> 16).astype(jnp.int16))
        ok = tot >= k_top
        return jnp.where(ok, cand, res), jnp.where(ok, tot, n_res)

    searched = nkv * C > k_top
    carry = (jnp.full((1, C), INT_MIN, jnp.int32), jnp.zeros((1, C), F32))
    top_row, n_top = lax.fori_loop(0, jnp.where(searched, 16, 0), bit16_body, carry)

    top16 = top_row >> 16

    def low_body(j, c):
        kt = keyt_sc[j]
        hi = kt >> 16
        lo = (kt & 0xFFFF) - 32768
        lo = jnp.where(hi == top16, lo, jnp.where(hi > top16, 32767, -32768))
        keyt16_sc[j] = lo.astype(jnp.int16)
        return c

    lax.fori_loop(0, jnp.where(searched, nkv, 0), low_body, 0)

    def bit16_low_body(b, carry):
        res, n_res = carry
        cand = res ^ lax.shift_left(jnp.int32(1), 31 - b)
        tot = count16(((cand & 0xFFFF) - 32768).astype(jnp.int16))
        ok = tot >= k_top
        return jnp.where(ok, cand, res), jnp.where(ok, tot, n_res)

    thr_row, n_thr = lax.fori_loop(16, jnp.where(searched, 32, 16), bit16_low_body, (top_row, n_top))
    thr = _row_to_col(thr_row)

    @pl.when(jnp.max(n_thr) > k_top)
    def _():
        need = k_top - count(lambda j, r: jnp.where(keyt(j, r) > thr_row, 1.0, 0.0))
        s_sub = lax.broadcasted_iota(jnp.int32, (SUB, C), 0)

        def idx_body(b, last):
            cand = last | lax.shift_left(jnp.int32(1), idx_bits - 1 - b)
            below = count(lambda j, r: jnp.where(
                keyt(j, r) == thr_row, jnp.where(j * C + r * SUB + s_sub < cand, 1.0, 0.0), 0.0))
            return jnp.where(below < need, cand, last)

        last = _row_to_col(lax.fori_loop(0, idx_bits, idx_body, jnp.zeros((1, C), jnp.int32)))

        def drop_body(j, carry):
            kk = key_sc[j]
            key_sc[j] = jnp.where(kk == thr, jnp.where(j * C + sig > last, kk - 1, kk), kk)
            return carry

        lax.fori_loop(0, nkv, drop_body, 0)

    kmax = kmax_sc[0:1, 0:1] * 1.001
    worst = jnp.zeros((C, 1), F32)
    for h in range(n_heads):
        qf = q_ref[h].astype(F32)
        bound = jnp.sqrt(jnp.sum(qf * qf, axis=-1, keepdims=True)) * kmax
        shift_sc[h] = jnp.broadcast_to(bound + bstat_ref[0, h], (C, LANES))
        worst = jnp.maximum(worst, 2.0 * bound + (bstat_ref[0, h] - bstat_ref[1, h]))
    loose = jnp.max(worst) > SHIFT_SPAN_LIMIT

    def logits(j, h, kind):
        off = pl.multiple_of(j * C, C)
        lg = _dot_nt(q_ref[h], k_ref[pl.ds(off, C), :])
        return lg if kind is None else lg + bias_ref[kind, h]

    def near_tiles(fn):
        @pl.when(qi > 0)
        def _():
            fn(qi - 1, 1)
            fn(qi, 0)

        @pl.when(qi == 0)
        def _():
            fn(qi, 0)

    def far_tiles(fn, group=TILE_GROUP):
        _grouped_loop(jnp.maximum(qi - 1, 0), lambda j: fn(j, None), group)

    @pl.when(loose)
    def _():
        for h in range(n_heads):
            shift_sc[h] = jnp.full((C, LANES), NEG, F32)

        def max_tile(j, kind):
            sel = key_sc[j] >= thr
            for h in range(n_heads):
                lg = jnp.where(sel, logits(j, h, kind), NEG)
                m = jnp.max(lg, axis=-1, keepdims=True)
                shift_sc[h] = jnp.maximum(shift_sc[h], jnp.broadcast_to(m, (C, LANES)))

        far_tiles(max_tile, group=1)
        near_tiles(max_tile)

    acc_sc[...] = jnp.zeros(acc_sc.shape, F32)

    def attn_tile(j, kind):
        off = pl.multiple_of(j * C, C)
        vx = vx_sc[pl.ds(off, C), :]
        sel = key_sc[j] >= thr
        for h in range(n_heads):
            sh = jnp.concatenate([shift_sc[h]] * (C // LANES), axis=1)
            p = jnp.where(sel, jnp.exp(logits(j, h, kind) - sh), 0.0)
            acc_sc[h] += _dot(p.astype(BF16), vx)

    far_tiles(attn_tile)
    near_tiles(attn_tile)
    for h in range(n_heads):
        a = acc_sc[h]
        o_ref[:, h * HEAD_DIM:(h + 1) * HEAD_DIM] = (a[:, :HEAD_DIM] / a[:, HEAD_DIM:]).astype(BF16)


CAST_SLAB_ELEMS = 1 << 20


def _cast_slabs(w, steps):
    rows = math.prod(w.shape[:-1])
    if rows % (steps * 16) != 0 or rows // steps * w.shape[-1] > CAST_SLAB_ELEMS:
        return None
    return w.reshape(steps, rows // steps, w.shape[-1])


def _attention(q, k, v, iq, ik, iw, bias_tiles, bias_stat, B, S, k_top, C, cast=()):
    n_heads, T, _ = q.shape
    n_pairs = iq.shape[0]
    nq = S // C
    slabs = [_cast_slabs(w, B * nq) for w in cast]
    kern = functools.partial(_attn_kernel, k_top=k_top, n_cast=len(slabs))
    heads = lambda n: pl.BlockSpec((n, C, LANES), lambda b, i: (0, b * nq + i, 0))
    seq = pl.BlockSpec((S, LANES), lambda b, i: (b, 0))
    slab_specs = [pl.BlockSpec((None,) + w.shape[1:], lambda b, i: (b * nq + i, 0, 0)) for w in slabs]
    outs = pl.pallas_call(
        kern,
        grid=(B, nq),
        in_specs=[pl.BlockSpec(memory_space=pltpu.SMEM),
                  heads(n_heads), seq, seq, heads(n_pairs), seq,
                  pl.BlockSpec((C, LANES), lambda b, i: (b * nq + i, 0)),
                  _resident(bias_tiles.shape)] + slab_specs,
        out_specs=[pl.BlockSpec((C, n_heads * HEAD_DIM), lambda b, i: (b * nq + i, 0))] + slab_specs,
        out_shape=[jax.ShapeDtypeStruct((T, n_heads * HEAD_DIM), BF16)]
                  + [jax.ShapeDtypeStruct(w.shape, BF16) for w in slabs],
        scratch_shapes=[pltpu.VMEM((nq, C, C), jnp.int32),
                        pltpu.VMEM((nq, C, C), jnp.int32),
                        pltpu.VMEM((nq, C, C), jnp.int16),
                        pltpu.VMEM((2 * n_pairs, C, LANES), BF16),
                        pltpu.VMEM((S, 2 * HEAD_DIM), BF16),
                        pltpu.VMEM((8, LANES), F32),
                        pltpu.VMEM((n_heads, C, LANES), F32),
                        pltpu.VMEM((n_heads, C, 2 * HEAD_DIM), F32)],
        compiler_params=_params("arbitrary", "arbitrary"),
    )(bias_stat, q, k, v, iq, ik, iw, bias_tiles, *slabs)
    return outs[0], [o.reshape(w.shape) for o, w in zip(outs[1:], cast)]


CONV_HALO = 32


SUBLANES = 8
CONV_SLAB = 64


def _conv_kernel(u_ref, w_ref, cb_ref, g_ref, b_ref, o_ref, cp_sc, y_sc, halo_sc, *, width):
    TS, CH = u_ref.shape
    n_rows = CONV_HALO + TS
    i = pl.program_id(1)

    @pl.when(jnp.logical_and(pl.program_id(0) == 0, i == 0))
    def _():
        halo_sc[...] = jnp.zeros(halo_sc.shape, F32)

    halo = jnp.where(i > 0, halo_sc[...], 0.0)
    halo_sc[...] = u_ref[TS - CONV_HALO:, :].astype(F32)
    pad = jnp.zeros((SUBLANES, LANES), F32)
    for c in range(CH // LANES):
        cs = slice(c * LANES, (c + 1) * LANES)
        col = jnp.concatenate([halo[:, cs], u_ref[:, cs].astype(F32), pad], axis=0)
        for r in range(SUBLANES):
            cp_sc[r, :, cs] = col[r:r + n_rows]

    base = CONV_HALO - (width - 1)
    phases = {}
    for j in range(width):
        q, r = divmod(base + j, SUBLANES)
        phases.setdefault(r, []).append((q, j))

    def slab(s, carry):
        t0 = pl.multiple_of(s * CONV_SLAB, CONV_SLAB)
        for c in range(CH // LANES):
            cs = slice(c * LANES, (c + 1) * LANES)
            acc = jnp.zeros((CONV_SLAB, LANES), F32)
            for r, taps in phases.items():
                q_lo, q_hi = taps[0][0], taps[-1][0]
                win = cp_sc[r, pl.ds(q_lo * SUBLANES + t0, CONV_SLAB + (q_hi - q_lo) * SUBLANES), cs]
                for q, j in taps:
                    off = (q - q_lo) * SUBLANES
                    acc = acc + w_ref[j:j + 1, cs] * win[off:off + CONV_SLAB]
            y_sc[pl.ds(t0, CONV_SLAB), cs] = acc + cb_ref[:, cs]
        return carry

    lax.fori_loop(0, TS // CONV_SLAB, slab, 0)
    y = y_sc[...]
    mu = jnp.mean(y, axis=-1, keepdims=True)
    yc = y - mu
    var = jnp.mean(yc * yc, axis=-1, keepdims=True)
    yn = yc * lax.rsqrt(var + NORM_EPS) * g_ref[...] + b_ref[...]
    o_ref[...] = (yn * jax.nn.sigmoid(yn)).astype(BF16)


def _conv_module(u, conv_w, conv_b, ln_g, ln_b, B, S, TS=256):
    T, CH = u.shape
    width = conv_w.shape[0]
    assert width - 1 <= CONV_HALO
    ns = S // TS
    wpad = jnp.zeros((CONV_HALO, CH), F32).at[:width].set(conv_w)
    vec = lambda a: a.reshape(1, CH)
    kern = functools.partial(_conv_kernel, width=width)
    return pl.pallas_call(
        kern,
        grid=(B, ns),
        in_specs=[pl.BlockSpec((TS, CH), lambda b, i: (b * ns + i, 0)),
                  _resident((CONV_HALO, CH)), _resident((1, CH)), _resident((1, CH)), _resident((1, CH))],
        out_specs=pl.BlockSpec((TS, CH), lambda b, i: (b * ns + i, 0)),
        out_shape=jax.ShapeDtypeStruct((T, CH), BF16),
        scratch_shapes=[pltpu.VMEM((SUBLANES, CONV_HALO + TS, CH), F32),
                        pltpu.VMEM((TS, CH), F32),
                        pltpu.VMEM((CONV_HALO, CH), F32)],
        compiler_params=_params("arbitrary", "arbitrary"),
    )(u, wpad, vec(conv_b), vec(ln_g), vec(ln_b))


def _out_proj_kernel(x_ref, a_ref, c_ref, wa_ref, wc_ref, o_ref):
    o_ref[...] = x_ref[...] + _dot(a_ref[...], wa_ref[...]) + _dot(c_ref[...], wc_ref[...])


def _out_proj(x, attn, conv, w_out, tm=512):
    T, D = x.shape
    aw, cw = attn.shape[1], conv.shape[1]
    wa = w_out[:aw].astype(BF16)
    wc = w_out[aw:].astype(BF16)
    row = lambda w: pl.BlockSpec((tm, w), lambda i: (i, 0))
    return pl.pallas_call(
        _out_proj_kernel,
        grid=(T // tm,),
        in_specs=[row(D), row(aw), row(cw), _resident(wa.shape), _resident(wc.shape)],
        out_specs=row(D),
        out_shape=jax.ShapeDtypeStruct((T, D), F32),
        compiler_params=_params("arbitrary"),
    )(x, attn, conv, wa, wc)


def _first_max(vals):
    m = vals[0]
    for v in vals[1:]:
        m = jnp.maximum(m, v)
    idx = jnp.full(m.shape, len(vals) - 1, jnp.int32)
    for k in range(len(vals) - 2, -1, -1):
        idx = jnp.where(vals[k] == m, k, idx)
    return m, idx


def _softmax_cols(cols):
    m = cols[0]
    for c in cols[1:]:
        m = jnp.maximum(m, c)
    e = [jnp.exp(c - m) for c in cols]
    s = e[0]
    for c in e[1:]:
        s = s + c
    return [c / s for c in e]


GROUP_ROWS = 8
MOE_CHUNK = 64
MOE_MAX_CHUNKS = 8


def _split3(a):
    hi = a.astype(BF16)
    r = a - hi.astype(F32)
    mid = r.astype(BF16)
    lo = (r - mid.astype(F32)).astype(BF16)
    return hi, mid, lo


PAIR_CLASSES = [(a, b) for a in range(EXPERTS_PER_GROUP) for b in range(a + 1, EXPERTS_PER_GROUP)]
CLASS_SPAN = [(min(c for c, p in enumerate(PAIR_CLASSES) if k in p), max(c for c, p in enumerate(PAIR_CLASSES) if k in p))
              for k in range(EXPERTS_PER_GROUP)]


W_SLOTS = 3
MOE_VMEM_LIMIT = 62 * 1024 * 1024


def _moe_kernel(x_ref, nw_ref, wrt_ref, brt_ref, wg_hbm, wu_hbm, wd_hbm, o_ref,
                hs_sc, p_sc, cs_sc, seg_sm, wg_sc, wu_sc, wd_sc, w_sem, *, n_groups, layer):
    TM, D = x_ref.shape
    e = pl.program_id(1)
    n_exp = pl.num_programs(1)
    step = pl.program_id(0) * n_exp + e
    n_steps = pl.num_programs(0) * n_exp

    def weight_copies(s):
        slot, expert = s % W_SLOTS, s % n_exp
        return [pltpu.make_async_copy(src.at[layer, expert], dst.at[slot], w_sem.at[k, slot])
                for k, (src, dst) in enumerate(((wg_hbm, wg_sc), (wu_hbm, wu_sc), (wd_hbm, wd_sc)))]

    @pl.when(step == 0)
    def _():
        for s in range(W_SLOTS - 1):
            for cp in weight_copies(s):
                cp.start()

    @pl.when(step + W_SLOTS - 1 < n_steps)
    def _():
        for cp in weight_copies(step + W_SLOTS - 1):
            cp.start()

    @pl.when(e == 0)
    def _():
        x = x_ref[...]
        scale = lax.rsqrt(jnp.mean(x * x, axis=-1, keepdims=True) + NORM_EPS)

        def h_cols(c0):
            cs = slice(c0, c0 + 512)
            return (x_ref[:, cs] * scale * nw_ref[:, cs]).astype(BF16)

        lg = brt_ref[...] + jnp.zeros((1, TM), F32)
        for c0 in range(0, D, 512):
            lg = lg + _dot_nt(wrt_ref[:, c0:c0 + 512], h_cols(c0))
        row = lambda k: lg[k:k + 1, :]
        g_prob = _softmax_cols([row(g) for g in range(n_groups)])
        g_p, g_idx = _first_max(g_prob)
        e_logit = []
        for k in range(EXPERTS_PER_GROUP):
            v = row(GROUP_ROWS + k)
            for g in range(1, n_groups):
                v = jnp.where(g_idx == g, row(GROUP_ROWS + g * EXPERTS_PER_GROUP + k), v)
            e_logit.append(v)
        e_prob = _softmax_cols(e_logit)
        p1, i1 = _first_max(e_prob)
        rest = [jnp.where(i1 == k, -1.0, e_prob[k]) for k in range(EXPERTS_PER_GROUP)]
        p2, i2 = _first_max(rest)
        den = p1 + p2
        base = g_idx * EXPERTS_PER_GROUP
        e_io = lax.broadcasted_iota(jnp.int32, (LANES, TM), 0)
        comb_t = (jnp.where(e_io == base + i1, g_p * (p1 / den), 0.0)
                  + jnp.where(e_io == base + i2, g_p * (p2 / den), 0.0))

        lo, hi = jnp.minimum(i1, i2), jnp.maximum(i1, i2)
        cls = jnp.zeros((1, TM), jnp.int32)
        for c, (a, b) in enumerate(PAIR_CLASSES):
            cls = jnp.where(lo * EXPERTS_PER_GROUP + hi == a * EXPERTS_PER_GROUP + b, c, cls)
        cat = g_idx * len(PAIR_CLASSES) + cls
        n_cat = n_groups * len(PAIR_CLASSES)
        cat_rows = seg_sm.shape[0]
        onehot_t = jnp.where(lax.broadcasted_iota(jnp.int32, (cat_rows, TM), 0) == cat, 1.0, 0.0)
        upper = jnp.where(lax.broadcasted_iota(jnp.int32, (LANES, LANES), 0)
                          < lax.broadcasted_iota(jnp.int32, (LANES, LANES), 1), 1.0, 0.0).astype(BF16)
        before = jnp.zeros((cat_rows, 1), F32)
        ranks = []
        for b0 in range(0, TM, LANES):
            blk = onehot_t[:, b0:b0 + LANES]
            ranks.append(_dot(blk.astype(BF16), upper) + before)
            before = before + jnp.sum(blk, axis=1, keepdims=True)
        rank = jnp.concatenate(ranks, axis=1)
        start = jnp.int32(0)
        pos = jnp.zeros((1, TM), F32)
        for c in range(n_cat):
            seg_sm[c] = start
            pos = jnp.where(cat == c, start.astype(F32) + rank[c:c + 1, :], pos)
            start = start + jnp.sum(onehot_t[c:c + 1, :]).astype(jnp.int32)
        seg_sm[n_cat] = start
        pos = pos.astype(jnp.int32)

        rb = 256
        for r0 in range(0, TM, rb):
            r_io = lax.broadcasted_iota(jnp.int32, (rb, TM), 0) + r0
            p_sc[r0:r0 + rb, :] = jnp.where(r_io == pos, 1.0, 0.0).astype(BF16)
        p = p_sc[...]
        for c0 in range(0, D, 512):
            hs_sc[:, c0:c0 + 512] = _dot(p, h_cols(c0)).astype(BF16)
        cs = jnp.zeros((TM, LANES), F32)
        for part in _split3(comb_t):
            cs = cs + _dot_nt(p, part)
        cs_sc[...] = cs
        o_ref[...] = jnp.zeros(o_ref.shape, F32)

    g = e // EXPERTS_PER_GROUP
    k = e % EXPERTS_PER_GROUP
    first, last = jnp.int32(CLASS_SPAN[0][0]), jnp.int32(CLASS_SPAN[0][1])
    for kk in range(1, EXPERTS_PER_GROUP):
        first = jnp.where(k == kk, CLASS_SPAN[kk][0], first)
        last = jnp.where(k == kk, CLASS_SPAN[kk][1], last)
    start = seg_sm[g * len(PAIR_CLASSES) + first]
    end = seg_sm[g * len(PAIR_CLASSES) + last + 1]
    c_lo = start // MOE_CHUNK
    c_hi = jnp.where(end > start, (end + MOE_CHUNK - 1) // MOE_CHUNK, c_lo)

    for cp in weight_copies(step):
        cp.wait()
    slot = step % W_SLOTS

    def expert_rows(c, n_chunks):
        m = n_chunks * MOE_CHUNK
        r0 = pl.multiple_of(c * MOE_CHUNK, MOE_CHUNK)
        rows = hs_sc[pl.ds(r0, m), :]
        lane = lax.broadcasted_iota(jnp.int32, (m, LANES), 1)
        w = jnp.sum(jnp.where(lane == e, cs_sc[pl.ds(r0, m), :], 0.0), axis=-1, keepdims=True)
        a = jax.nn.silu(_dot(rows, wg_sc[slot])) * _dot(rows, wu_sc[slot]) * w
        o_ref[pl.ds(r0, m), :] += _dot(a.astype(BF16), wd_sc[slot])

    def full(k, carry):
        expert_rows(c_lo + MOE_MAX_CHUNKS * k, MOE_MAX_CHUNKS)
        return carry

    n_chunks = c_hi - c_lo
    n_full = n_chunks // MOE_MAX_CHUNKS
    lax.fori_loop(0, n_full, full, 0)
    for rest in range(1, MOE_MAX_CHUNKS):
        @pl.when(n_chunks - n_full * MOE_MAX_CHUNKS == rest)
        def _(rest=rest):
            expert_rows(c_hi - rest, rest)

    @pl.when(e == n_exp - 1)
    def _():
        p = p_sc[...]
        for c0 in range(0, D, 512):
            ys = o_ref[:, c0:c0 + 512].astype(BF16)
            y = lax.dot_general(p, ys, (((0,), (0,)), ((), ())), preferred_element_type=F32)
            o_ref[:, c0:c0 + 512] = x_ref[:, c0:c0 + 512] + y


def _moe(x, norm_w, wg, bg, we, be, w_gate, w_up, w_down, layer, tm=1024):
    T, D = x.shape
    n_groups, n_exp = wg.shape[1], we.shape[1]
    ff = w_gate.shape[3]
    assert n_groups <= GROUP_ROWS and n_exp == n_groups * EXPERTS_PER_GROUP and T % tm == 0
    rows = GROUP_ROWS + n_exp
    cat_rows = -(-(n_groups * len(PAIR_CLASSES) + 1) // SUBLANES) * SUBLANES
    wrt = jnp.zeros((rows, D), F32).at[:n_groups].set(wg.T).at[GROUP_ROWS:].set(we.T).astype(BF16)
    brt = jnp.zeros((rows, 1), F32).at[:n_groups, 0].set(bg).at[GROUP_ROWS:, 0].set(be)
    tile = pl.BlockSpec((tm, D), lambda i, e: (i, 0))
    return pl.pallas_call(
        functools.partial(_moe_kernel, n_groups=n_groups, layer=layer),
        grid=(T // tm, n_exp),
        in_specs=[tile, _resident((1, D)), _resident((rows, D)), _resident((rows, 1)),
                  pl.BlockSpec(memory_space=pl.ANY), pl.BlockSpec(memory_space=pl.ANY),
                  pl.BlockSpec(memory_space=pl.ANY)],
        out_specs=tile,
        out_shape=jax.ShapeDtypeStruct((T, D), F32),
        scratch_shapes=[pltpu.VMEM((tm, D), BF16),
                        pltpu.VMEM((tm, tm), BF16),
                        pltpu.VMEM((tm, LANES), F32),
                        pltpu.SMEM((cat_rows,), jnp.int32),
                        pltpu.VMEM((W_SLOTS, D, ff), BF16),
                        pltpu.VMEM((W_SLOTS, D, ff), BF16),
                        pltpu.VMEM((W_SLOTS, ff, D), BF16),
                        pltpu.SemaphoreType.DMA((3, W_SLOTS))],
        compiler_params=pltpu.CompilerParams(dimension_semantics=("arbitrary", "arbitrary"),
                                             vmem_limit_bytes=MOE_VMEM_LIMIT),
    )(x, norm_w.reshape(1, D), wrt, brt, w_gate, w_up, w_down)


POOL_HALO = 16


def _pool_kernel(x_ref, nw_ref, pw_ref, ps_ref, o_ref, hb, pa, pb):
    TS, D = x_ref.shape
    n_groups, pc, _ = pw_ref.shape
    i = pl.program_id(1)
    nw = nw_ref[...]
    top = SUBLANES + POOL_HALO
    n_rows = top + TS

    @pl.when(jnp.logical_and(pl.program_id(0) == 0, i == 0))
    def _():
        hb[...] = jnp.zeros(hb.shape, F32)
        pa[0:SUBLANES] = jnp.zeros((SUBLANES, pc), F32)
        pb[0:SUBLANES] = jnp.zeros((SUBLANES, pc), F32)

    hb[SUBLANES:top] = jnp.where(i > 0, hb[n_rows - POOL_HALO:n_rows], 0.0)
    hb[top:] = _rms(x_ref[...], nw)
    t = i * TS + lax.broadcasted_iota(jnp.int32, (TS, 1), 0)
    for g, w in enumerate(POOL_WINDOWS):
        cs = slice(g * pc, (g + 1) * pc)
        cur = hb[top:, cs]
        src, cols, step, dst = hb, cs, 1, pa
        while step < w:
            dst[SUBLANES:n_rows, :] = src[SUBLANES:n_rows, cols] + src[SUBLANES - step:n_rows - step, cols]
            src, cols, step, dst = dst, slice(None), 2 * step, (pb if dst is pa else pa)
        s = src[top:n_rows, cols]
        count = jnp.minimum(t + 1, w).astype(F32)
        d = s / count - cur
        mixed = _dot(d.astype(BF16), pw_ref[g])
        o_ref[:, cs] = x_ref[:, cs] + ps_ref[:, cs] * mixed


def _pool_layer(x, norm_w, pool_w, pool_scale, B, S, TS=256):
    T, D = x.shape
    assert len(POOL_WINDOWS) == pool_w.shape[0] and max(POOL_WINDOWS) - 1 <= POOL_HALO
    ns = S // TS
    pw = pool_w.astype(BF16)
    return pl.pallas_call(
        _pool_kernel,
        grid=(B, ns),
        in_specs=[pl.BlockSpec((TS, D), lambda b, i: (b * ns + i, 0)),
                  _resident((1, D)), _resident(pw.shape), _resident((1, D))],
        out_specs=pl.BlockSpec((TS, D), lambda b, i: (b * ns + i, 0)),
        out_shape=jax.ShapeDtypeStruct((T, D), F32),
        scratch_shapes=[pltpu.VMEM((SUBLANES + POOL_HALO + TS, D), F32),
                        pltpu.VMEM((SUBLANES + POOL_HALO + TS, D // len(POOL_WINDOWS)), F32),
                        pltpu.VMEM((SUBLANES + POOL_HALO + TS, D // len(POOL_WINDOWS)), F32)],
        compiler_params=_params("arbitrary", "arbitrary"),
    )(x, norm_w.reshape(1, D), pw, pool_scale.reshape(1, D))


def _chunk(S):
    return 256 if S % 256 == 0 else 128


def kernel(x, rel_bias, mix_norm_e, w_in_e, q_norm_e, k_norm_e, conv_w_e, conv_b_e, conv_ln_g_e, conv_ln_b_e,
           w_out_e, mix_norm_o, pool_w_o, pool_scale_o, ffn_norm, router_group_w, router_group_b,
           router_expert_w, router_expert_b, w_gate, w_up, w_down):
    B, S, D = x.shape
    T = B * S
    depth = ffn_norm.shape[0]
    n_heads = rel_bias.shape[1]
    idx_heads = (w_in_e.shape[2] - n_heads * HEAD_DIM - 2 * HEAD_DIM - IDX_DIM - 2 * conv_w_e.shape[2]) \
        // (IDX_DIM + 1)
    k_top = min(INDEX_TOPK, S // 4)
    C = _chunk(S)
    xf = x.reshape(T, D)
    bias_tiles, bias_stat = _rel_bias_tiles(rel_bias, C)
    expert_w = None
    for l in range(depth):
        i = l // 2
        if l % 2 == 0:
            q, k, v, iq, ik, iw, u = _in_proj(xf, mix_norm_e[i], w_in_e[i], q_norm_e[i], k_norm_e[i],
                                              n_heads, idx_heads, conv_w_e.shape[2])
            cast = (w_gate, w_up, w_down) if expert_w is None else ()
            if any(_cast_slabs(w, B * (S // C)) is None for w in cast):
                cast = ()
            attn, done = _attention(q, k, v, iq, ik, iw, bias_tiles, bias_stat, B, S, k_top, C, cast)
            expert_w = done if cast else expert_w
            conv = _conv_module(u, conv_w_e[i], conv_b_e[i], conv_ln_g_e[i], conv_ln_b_e[i], B, S)
            xf = _out_proj(xf, attn, conv, w_out_e[i])
        else:
            xf = _pool_layer(xf, mix_norm_o[i], pool_w_o[i], pool_scale_o[i], B, S)
        if expert_w is None:
            expert_w = [w.astype(BF16) for w in (w_gate, w_up, w_down)]
        xf = _moe(xf, ffn_norm[l], router_group_w[l], router_group_b[l], router_expert_w[l], router_expert_b[l],
                  *expert_w, layer=l)
    return xf.reshape(B, S, D)
```

```python
import functools
import math

import jax
import jax.numpy as jnp
from jax import lax
from jax.experimental import pallas as pl
from jax.experimental.pallas import tpu as pltpu

F32 = jnp.float32
BF16 = jnp.bfloat16

NORM_EPS = 1e-6
HEAD_DIM = 128
IDX_DIM = 64
INDEX_TOPK = 256
REL_BUCKETS = 32
REL_MAX_DIST = 128
POOL_WINDOWS = (2, 4, 8, 16)
EXPERTS_PER_GROUP = 4
LANES = 128
VMEM_LIMIT = 56 * 1024 * 1024
NEG = -1e30
INT_MIN = -(2 ** 31)


def _dot(a, b):
    return jnp.dot(a, b, preferred_element_type=F32)


def _dot_nt(a, b):
    return lax.dot_general(a, b, (((1,), (1,)), ((), ())), preferred_element_type=F32)


def _rms(x, w):
    return x * lax.rsqrt(jnp.mean(x * x, axis=-1, keepdims=True) + NORM_EPS) * w


def _params(*sem):
    return pltpu.CompilerParams(dimension_semantics=sem, vmem_limit_bytes=VMEM_LIMIT)


def _resident(shape):
    nd = len(shape)
    return pl.BlockSpec(shape, lambda *_: (0,) * nd, pipeline_mode=pl.Buffered(1))


def _in_proj_kernel(x_ref, nw_ref, qn_ref, kn_ref, wq_ref, wkv_ref, wiq_ref, wikw_ref, wa_ref, wg_ref,
                    q_ref, k_ref, v_ref, iq_ref, ik_ref, iw_ref, u_ref, *, q_scale, iw_scale):
    h = _rms(x_ref[...], nw_ref[...]).astype(BF16)
    n_pairs = q_ref.shape[0] // 2
    qn = qn_ref[...] * q_scale
    for c in range(n_pairs):
        qq = _dot(h, wq_ref[:, c * 256:(c + 1) * 256])
        for s in range(2):
            qh = qq[:, s * HEAD_DIM:(s + 1) * HEAD_DIM]
            q_ref[2 * c + s] = _rms(qh, qn).astype(BF16)
    kv = _dot(h, wkv_ref[...])
    k_ref[...] = _rms(kv[:, :HEAD_DIM], kn_ref[...]).astype(BF16)
    v_ref[...] = kv[:, HEAD_DIM:].astype(BF16)
    for c in range(iq_ref.shape[0] // 2):
        r = _dot(h, wiq_ref[:, c * 256:(c + 1) * 256])
        iq_ref[2 * c] = r[:, :LANES].astype(BF16)
        iq_ref[2 * c + 1] = r[:, LANES:].astype(BF16)
    r = _dot(h, wikw_ref[...])
    ik_ref[...] = r[:, :LANES].astype(BF16)
    iw_ref[...] = r[:, LANES:] * iw_scale
    for c in range(u_ref.shape[1] // 256):
        cs = slice(c * 256, (c + 1) * 256)
        a = _dot(h, wa_ref[:, cs])
        g = _dot(h, wg_ref[:, cs])
        u_ref[:, cs] = (a * jax.nn.sigmoid(g)).astype(BF16)


def _split_w_in_kernel(w_ref, wq_ref, wkv_ref, wiq_ref, wikw_ref, wa_ref, wg_ref, *, idx_heads):
    o = 0
    for ref in (wq_ref, wkv_ref, wiq_ref):
        n = ref.shape[1]
        ref[...] = w_ref[:, o:o + n].astype(BF16)
        o += n
    wik = w_ref[:, o:o + IDX_DIM]
    wiw = w_ref[:, o + IDX_DIM:o + IDX_DIM + idx_heads]
    o += IDX_DIM + idx_heads
    pad = jnp.zeros((wik.shape[0], LANES - idx_heads), F32)
    wikw_ref[...] = jnp.concatenate([wik, wik, wiw, pad], axis=1).astype(BF16)
    for ref in (wa_ref, wg_ref):
        n = ref.shape[1]
        ref[...] = w_ref[:, o:o + n].astype(BF16)
        o += n


def _split_w_in(w_in, n_heads, idx_heads, conv_ch, tk=256):
    D, n_in = w_in.shape
    widths = (n_heads * HEAD_DIM, 2 * HEAD_DIM, idx_heads * IDX_DIM, 2 * LANES, conv_ch, conv_ch)
    assert n_in == sum(widths) - 2 * LANES + IDX_DIM + idx_heads and 2 * IDX_DIM == LANES and D % tk == 0
    return pl.pallas_call(
        functools.partial(_split_w_in_kernel, idx_heads=idx_heads),
        grid=(D // tk,),
        in_specs=[pl.BlockSpec((tk, n_in), lambda i: (i, 0))],
        out_specs=[pl.BlockSpec((tk, w), lambda i: (i, 0)) for w in widths],
        out_shape=[jax.ShapeDtypeStruct((D, w), BF16) for w in widths],
        compiler_params=_params("arbitrary"),
    )(w_in)


def _in_proj(x, norm_w, w_in, q_norm, k_norm, n_heads, idx_heads, conv_ch, tm=256):
    T, D = x.shape
    iq_w = idx_heads * IDX_DIM
    ws = _split_w_in(w_in, n_heads, idx_heads, conv_ch)
    row = lambda w: pl.BlockSpec((tm, w), lambda i: (i, 0))
    heads = lambda n: pl.BlockSpec((n, tm, LANES), lambda i: (0, i, 0))
    kern = functools.partial(_in_proj_kernel, q_scale=HEAD_DIM ** -0.5,
                             iw_scale=(idx_heads ** -0.5) * (IDX_DIM ** -0.5))
    return pl.pallas_call(
        kern,
        grid=(T // tm,),
        in_specs=[row(D), _resident((1, D)), _resident((1, HEAD_DIM)), _resident((1, HEAD_DIM))]
                 + [_resident(w.shape) for w in ws],
        out_specs=[heads(n_heads), row(HEAD_DIM), row(HEAD_DIM), heads(iq_w // LANES), row(LANES), row(LANES),
                   row(conv_ch)],
        out_shape=[jax.ShapeDtypeStruct((n_heads, T, HEAD_DIM), BF16),
                   jax.ShapeDtypeStruct((T, HEAD_DIM), BF16),
                   jax.ShapeDtypeStruct((T, HEAD_DIM), BF16),
                   jax.ShapeDtypeStruct((iq_w // LANES, T, LANES), BF16),
                   jax.ShapeDtypeStruct((T, LANES), BF16),
                   jax.ShapeDtypeStruct((T, LANES), F32),
                   jax.ShapeDtypeStruct((T, conv_ch), BF16)],
        compiler_params=_params("arbitrary"),
    )(x, norm_w.reshape(1, D), q_norm.reshape(1, HEAD_DIM), k_norm.reshape(1, HEAD_DIM), *ws)


def _rel_bias_kernel(rb_ref, o_ref, stat_ref):
    _, n_heads, C, _ = o_ref.shape
    tau = lax.broadcasted_iota(jnp.int32, (C, C), 0)
    sig = lax.broadcasted_iota(jnp.int32, (C, C), 1)
    max_exact = REL_BUCKETS // 2
    for kind in range(2):
        d = tau - sig + kind * C
        n = jnp.maximum(d, 0)
        nf = jnp.maximum(n, 1).astype(F32)
        large = max_exact + (jnp.log(nf / max_exact) / math.log(REL_MAX_DIST / max_exact)
                             * (REL_BUCKETS - max_exact)).astype(jnp.int32)
        large = jnp.minimum(large, REL_BUCKETS - 1)
        bucket = jnp.where(n < max_exact, n, large)
        for h in range(n_heads):
            b = jnp.zeros((C, C), F32)
            for bk in range(REL_BUCKETS):
                b = jnp.where(bucket == bk, rb_ref[bk, h], b)
            b = b - rb_ref[REL_BUCKETS - 1, h]
            if kind == 0:
                b = jnp.where(d < 0, NEG, b)
            o_ref[kind, h] = b
    for h in range(n_heads):
        hi = rb_ref[0, h]
        lo = rb_ref[0, h]
        for bk in range(1, REL_BUCKETS):
            hi = jnp.maximum(hi, rb_ref[bk, h])
            lo = jnp.minimum(lo, rb_ref[bk, h])
        stat_ref[0, h] = hi - rb_ref[REL_BUCKETS - 1, h]
        stat_ref[1, h] = lo - rb_ref[REL_BUCKETS - 1, h]


def _rel_bias_tiles(rel_bias, C):
    n_heads = rel_bias.shape[1]
    assert C >= REL_MAX_DIST
    return pl.pallas_call(
        _rel_bias_kernel,
        in_specs=[pl.BlockSpec(memory_space=pltpu.SMEM)],
        out_specs=[pl.BlockSpec(memory_space=pltpu.VMEM), pl.BlockSpec(memory_space=pltpu.SMEM)],
        out_shape=[jax.ShapeDtypeStruct((2, n_heads, C, C), F32), jax.ShapeDtypeStruct((2, n_heads), F32)],
        compiler_params=pltpu.CompilerParams(vmem_limit_bytes=VMEM_LIMIT),
    )(rel_bias)


SHIFT_SPAN_LIMIT = 60.0


TILE_GROUP = 8


def _grouped_loop(n, fn, group):
    def body(k, carry):
        for u in range(group):
            fn(group * k + u)
        return carry

    lax.fori_loop(0, n // group, body, 0)
    base = (n // group) * group
    size = group // 2
    while size >= 1:
        take = ((n - base) & size) != 0

        @pl.when(take)
        def _(base=base, size=size):
            for u in range(size):
                fn(base + u)
        base = base + jnp.where(take, size, 0)
        size //= 2


def _order_key(x):
    bits = pltpu.bitcast(x, jnp.int32)
    return bits ^ ((bits >> 31) & 0x7FFFFFFF)


def _row_to_col(row):
    C = row.shape[1]
    halves = []
    for part in (row >> 16, row & 0xFFFF):
        halves.append(jnp.broadcast_to(part.astype(F32), (LANES, C)).T[:, 0:1].astype(jnp.int32))
    return (halves[0] << 16) | halves[1]


def _attn_kernel(bstat_ref, q_ref, k_ref, v_ref, iq_ref, ik_ref, iw_ref, bias_ref, *rest, k_top, n_cast):
    cast_in, (o_ref, *cast_out) = rest[:n_cast], rest[n_cast:2 * n_cast + 1]
    key_sc, keyt_sc, keyt16_sc, iqm_sc, vx_sc, kmax_sc, shift_sc, acc_sc = rest[2 * n_cast + 1:]
    for src, dst in zip(cast_in, cast_out):
        dst[...] = src[...].astype(BF16)
    n_heads, C, _ = q_ref.shape
    idx_heads = iqm_sc.shape[0]
    idx_bits = max(1, (key_sc.shape[0] * C - 1).bit_length())
    qi = pl.program_id(1)
    nkv = qi + 1

    @pl.when(qi == 0)
    def _():
        vx_sc[:, :HEAD_DIM] = v_ref[...]
        vx_sc[:, HEAD_DIM:] = jnp.ones((vx_sc.shape[0], HEAD_DIM), BF16)
        kf = k_ref[...].astype(F32)
        k2 = jnp.sum(kf * kf, axis=-1, keepdims=True)
        kmax_sc[...] = jnp.broadcast_to(jnp.sqrt(jnp.max(k2, axis=0, keepdims=True)), kmax_sc.shape)

    lane = lax.broadcasted_iota(jnp.int32, (C, LANES), 1)
    for p in range(idx_heads // 2):
        qp = iq_ref[p].astype(F32)
        iqm_sc[2 * p] = jnp.where(lane < IDX_DIM, qp, 0.0).astype(BF16)
        iqm_sc[2 * p + 1] = jnp.where(lane >= IDX_DIM, qp, 0.0).astype(BF16)
    iw = iw_ref[...]
    tau = lax.broadcasted_iota(jnp.int32, (C, C), 0)
    sig = lax.broadcasted_iota(jnp.int32, (C, C), 1)

    def score_tile(j):
        off = pl.multiple_of(j * C, C)
        ikc = ik_ref[pl.ds(off, C), :]
        acc = jnp.zeros((C, C), F32)
        for hh in range(idx_heads):
            s = _dot_nt(iqm_sc[hh], ikc)
            acc = acc + jnp.maximum(s, 0.0) * iw[:, hh:hh + 1]
        acc = jnp.where(jnp.logical_and(j == qi, sig > tau), -jnp.inf, acc)
        key_sc[j] = _order_key(acc)
        kt = _order_key(acc.T)
        keyt_sc[j] = kt
        keyt16_sc[j] = (kt >> 16).astype(jnp.int16)

    _grouped_loop(nkv, score_tile, TILE_GROUP)

    SUB = 32

    def count(hit):
        def body(j, cnt):
            for r in range(C // SUB):
                cnt = cnt + hit(j, r)
            return cnt
        cnt = lax.fori_loop(0, nkv, body, jnp.zeros((SUB, C), F32))
        return jnp.sum(cnt, axis=0, keepdims=True)

    def keyt(j, r):
        return keyt_sc[j, pl.ds(r * SUB, SUB), :]

    def count16(cand16):
        one = jnp.ones((SUB, C), jnp.int16)
        zero = jnp.zeros((SUB, C), jnp.int16)

        def body(j, cnt):
            for r in range(C // SUB):
                cnt = cnt + jnp.where(keyt16_sc[j, pl.ds(r * SUB, SUB), :] >= cand16, one, zero)
            return cnt
        cnt = lax.fori_loop(0, nkv, body, zero)
        return jnp.sum(cnt.astype(jnp.int32).astype(F32), axis=0, keepdims=True)

    def bit16_body(b, carry):
        res, n_res = carry
        cand = res ^ lax.shift_left(jnp.int32(1), 31 - b)
        tot = count16((cand >> 16).astype(jnp.int16))
        ok = tot >= k_top
        return jnp.where(ok, cand, res), jnp.where(ok, tot, n_res)

    searched = nkv * C > k_top
    carry = (jnp.full((1, C), INT_MIN, jnp.int32), jnp.zeros((1, C), F32))
    top_row, n_top = lax.fori_loop(0, jnp.where(searched, 16, 0), bit16_body, carry)

    top16 = top_row >> 16

    def low_body(j, c):
        kt = keyt_sc[j]
        hi = kt >> 16
        lo = (kt & 0xFFFF) - 32768
        lo = jnp.where(hi == top16, lo, jnp.where(hi > top16, 32767, -32768))
        keyt16_sc[j] = lo.astype(jnp.int16)
        return c

    lax.fori_loop(0, jnp.where(searched, nkv, 0), low_body, 0)

    def bit16_low_body(b, carry):
        res, n_res = carry
        cand = res ^ lax.shift_left(jnp.int32(1), 31 - b)
        tot = count16(((cand & 0xFFFF) - 32768).astype(jnp.int16))
        ok = tot >= k_top
        return jnp.where(ok, cand, res), jnp.where(ok, tot, n_res)

    thr_row, n_thr = lax.fori_loop(16, jnp.where(searched, 32, 16), bit16_low_body, (top_row, n_top))
    thr = _row_to_col(thr_row)

    @pl.when(jnp.max(n_thr) > k_top)
    def _():
        need = k_top - count(lambda j, r: jnp.where(keyt(j, r) > thr_row, 1.0, 0.0))
        s_sub = lax.broadcasted_iota(jnp.int32, (SUB, C), 0)

        def idx_body(b, last):
            cand = last | lax.shift_left(jnp.int32(1), idx_bits - 1 - b)
            below = count(lambda j, r: jnp.where(
                keyt(j, r) == thr_row, jnp.where(j * C + r * SUB + s_sub < cand, 1.0, 0.0), 0.0))
            return jnp.where(below < need, cand, last)

        last = _row_to_col(lax.fori_loop(0, idx_bits, idx_body, jnp.zeros((1, C), jnp.int32)))

        def drop_body(j, carry):
            kk = key_sc[j]
            key_sc[j] = jnp.where(kk == thr, jnp.where(j * C + sig > last, kk - 1, kk), kk)
            return carry

        lax.fori_loop(0, nkv, drop_body, 0)

    kmax = kmax_sc[0:1, 0:1] * 1.001
    worst = jnp.zeros((C, 1), F32)
    for h in range(n_heads):
        qf = q_ref[h].astype(F32)
        bound = jnp.sqrt(jnp.sum(qf * qf, axis=-1, keepdims=True)) * kmax
        shift_sc[h] = jnp.broadcast_to(bound + bstat_ref[0, h], (C, LANES))
        worst = jnp.maximum(worst, 2.0 * bound + (bstat_ref[0, h] - bstat_ref[1, h]))
    loose = jnp.max(worst) > SHIFT_SPAN_LIMIT

    def logits(j, h, kind):
        off = pl.multiple_of(j * C, C)
        lg = _dot_nt(q_ref[h], k_ref[pl.ds(off, C), :])
        return lg if kind is None else lg + bias_ref[kind, h]

    def near_tiles(fn):
        @pl.when(qi > 0)
        def _():
            fn(qi - 1, 1)
            fn(qi, 0)

        @pl.when(qi == 0)
        def _():
            fn(qi, 0)

    def far_tiles(fn, group=TILE_GROUP):
        _grouped_loop(jnp.maximum(qi - 1, 0), lambda j: fn(j, None), group)

    @pl.when(loose)
    def _():
        for h in range(n_heads):
            shift_sc[h] = jnp.full((C, LANES), NEG, F32)

        def max_tile(j, kind):
            sel = key_sc[j] >= thr
            for h in range(n_heads):
                lg = jnp.where(sel, logits(j, h, kind), NEG)
                m = jnp.max(lg, axis=-1, keepdims=True)
                shift_sc[h] = jnp.maximum(shift_sc[h], jnp.broadcast_to(m, (C, LANES)))

        far_tiles(max_tile, group=1)
        near_tiles(max_tile)

    acc_sc[...] = jnp.zeros(acc_sc.shape, F32)

    def attn_tile(j, kind):
        off = pl.multiple_of(j * C, C)
        vx = vx_sc[pl.ds(off, C), :]
        sel = key_sc[j] >= thr
        for h in range(n_heads):
            sh = jnp.concatenate([shift_sc[h]] * (C // LANES), axis=1)
            p = jnp.where(sel, jnp.exp(logits(j, h, kind) - sh), 0.0)
            acc_sc[h] += _dot(p.astype(BF16), vx)

    far_tiles(attn_tile)
    near_tiles(attn_tile)
    for h in range(n_heads):
        a = acc_sc[h]
        o_ref[:, h * HEAD_DIM:(h + 1) * HEAD_DIM] = (a[:, :HEAD_DIM] / a[:, HEAD_DIM:]).astype(BF16)


CAST_SLAB_ELEMS = 1 << 20


def _cast_slabs(w, steps):
    rows = math.prod(w.shape[:-1])
    if rows % (steps * 16) != 0 or rows // steps * w.shape[-1] > CAST_SLAB_ELEMS:
        return None
    return w.reshape(steps, rows // steps, w.shape[-1])


def _attention(q, k, v, iq, ik, iw, bias_tiles, bias_stat, B, S, k_top, C, cast=()):
    n_heads, T, _ = q.shape
    n_pairs = iq.shape[0]
    nq = S // C
    slabs = [_cast_slabs(w, B * nq) for w in cast]
    kern = functools.partial(_attn_kernel, k_top=k_top, n_cast=len(slabs))
    heads = lambda n: pl.BlockSpec((n, C, LANES), lambda b, i: (0, b * nq + i, 0))
    seq = pl.BlockSpec((S, LANES), lambda b, i: (b, 0))
    slab_specs = [pl.BlockSpec((None,) + w.shape[1:], lambda b, i: (b * nq + i, 0, 0)) for w in slabs]
    outs = pl.pallas_call(
        kern,
        grid=(B, nq),
        in_specs=[pl.BlockSpec(memory_space=pltpu.SMEM),
                  heads(n_heads), seq, seq, heads(n_pairs), seq,
                  pl.BlockSpec((C, LANES), lambda b, i: (b * nq + i, 0)),
                  _resident(bias_tiles.shape)] + slab_specs,
        out_specs=[pl.BlockSpec((C, n_heads * HEAD_DIM), lambda b, i: (b * nq + i, 0))] + slab_specs,
        out_shape=[jax.ShapeDtypeStruct((T, n_heads * HEAD_DIM), BF16)]
                  + [jax.ShapeDtypeStruct(w.shape, BF16) for w in slabs],
        scratch_shapes=[pltpu.VMEM((nq, C, C), jnp.int32),
                        pltpu.VMEM((nq, C, C), jnp.int32),
                        pltpu.VMEM((nq, C, C), jnp.int16),
                        pltpu.VMEM((2 * n_pairs, C, LANES), BF16),
                        pltpu.VMEM((S, 2 * HEAD_DIM), BF16),
                        pltpu.VMEM((8, LANES), F32),
                        pltpu.VMEM((n_heads, C, LANES), F32),
                        pltpu.VMEM((n_heads, C, 2 * HEAD_DIM), F32)],
        compiler_params=_params("arbitrary", "arbitrary"),
    )(bias_stat, q, k, v, iq, ik, iw, bias_tiles, *slabs)
    return outs[0], [o.reshape(w.shape) for o, w in zip(outs[1:], cast)]


CONV_HALO = 32


SUBLANES = 8
CONV_SLAB = 64


def _conv_kernel(u_ref, w_ref, cb_ref, g_ref, b_ref, o_ref, cp_sc, y_sc, halo_sc, *, width):
    TS, CH = u_ref.shape
    n_rows = CONV_HALO + TS
    i = pl.program_id(1)

    @pl.when(jnp.logical_and(pl.program_id(0) == 0, i == 0))
    def _():
        halo_sc[...] = jnp.zeros(halo_sc.shape, F32)

    halo = jnp.where(i > 0, halo_sc[...], 0.0)
    halo_sc[...] = u_ref[TS - CONV_HALO:, :].astype(F32)
    pad = jnp.zeros((SUBLANES, LANES), F32)
    for c in range(CH // LANES):
        cs = slice(c * LANES, (c + 1) * LANES)
        col = jnp.concatenate([halo[:, cs], u_ref[:, cs].astype(F32), pad], axis=0)
        for r in range(SUBLANES):
            cp_sc[r, :, cs] = col[r:r + n_rows]

    base = CONV_HALO - (width - 1)
    phases = {}
    for j in range(width):
        q, r = divmod(base + j, SUBLANES)
        phases.setdefault(r, []).append((q, j))

    def slab(s, carry):
        t0 = pl.multiple_of(s * CONV_SLAB, CONV_SLAB)
        for c in range(CH // LANES):
            cs = slice(c * LANES, (c + 1) * LANES)
            acc = jnp.zeros((CONV_SLAB, LANES), F32)
            for r, taps in phases.items():
                q_lo, q_hi = taps[0][0], taps[-1][0]
                win = cp_sc[r, pl.ds(q_lo * SUBLANES + t0, CONV_SLAB + (q_hi - q_lo) * SUBLANES), cs]
                for q, j in taps:
                    off = (q - q_lo) * SUBLANES
                    acc = acc + w_ref[j:j + 1, cs] * win[off:off + CONV_SLAB]
            y_sc[pl.ds(t0, CONV_SLAB), cs] = acc + cb_ref[:, cs]
        return carry

    lax.fori_loop(0, TS // CONV_SLAB, slab, 0)
    y = y_sc[...]
    mu = jnp.mean(y, axis=-1, keepdims=True)
    yc = y - mu
    var = jnp.mean(yc * yc, axis=-1, keepdims=True)
    yn = yc * lax.rsqrt(var + NORM_EPS) * g_ref[...] + b_ref[...]
    o_ref[...] = (yn * jax.nn.sigmoid(yn)).astype(BF16)


def _conv_module(u, conv_w, conv_b, ln_g, ln_b, B, S, TS=256):
    T, CH = u.shape
    width = conv_w.shape[0]
    assert width - 1 <= CONV_HALO
    ns = S // TS
    wpad = jnp.zeros((CONV_HALO, CH), F32).at[:width].set(conv_w)
    vec = lambda a: a.reshape(1, CH)
    kern = functools.partial(_conv_kernel, width=width)
    return pl.pallas_call(
        kern,
        grid=(B, ns),
        in_specs=[pl.BlockSpec((TS, CH), lambda b, i: (b * ns + i, 0)),
                  _resident((CONV_HALO, CH)), _resident((1, CH)), _resident((1, CH)), _resident((1, CH))],
        out_specs=pl.BlockSpec((TS, CH), lambda b, i: (b * ns + i, 0)),
        out_shape=jax.ShapeDtypeStruct((T, CH), BF16),
        scratch_shapes=[pltpu.VMEM((SUBLANES, CONV_HALO + TS, CH), F32),
                        pltpu.VMEM((TS, CH), F32),
                        pltpu.VMEM((CONV_HALO, CH), F32)],
        compiler_params=_params("arbitrary", "arbitrary"),
    )(u, wpad, vec(conv_b), vec(ln_g), vec(ln_b))


def _out_proj_kernel(x_ref, a_ref, c_ref, wa_ref, wc_ref, o_ref):
    o_ref[...] = x_ref[...] + _dot(a_ref[...], wa_ref[...]) + _dot(c_ref[...], wc_ref[...])


def _out_proj(x, attn, conv, w_out, tm=512):
    T, D = x.shape
    aw, cw = attn.shape[1], conv.shape[1]
    wa = w_out[:aw].astype(BF16)
    wc = w_out[aw:].astype(BF16)
    row = lambda w: pl.BlockSpec((tm, w), lambda i: (i, 0))
    return pl.pallas_call(
        _out_proj_kernel,
        grid=(T // tm,),
        in_specs=[row(D), row(aw), row(cw), _resident(wa.shape), _resident(wc.shape)],
        out_specs=row(D),
        out_shape=jax.ShapeDtypeStruct((T, D), F32),
        compiler_params=_params("arbitrary"),
    )(x, attn, conv, wa, wc)


def _first_max(vals):
    m = vals[0]
    for v in vals[1:]:
        m = jnp.maximum(m, v)
    idx = jnp.full(m.shape, len(vals) - 1, jnp.int32)
    for k in range(len(vals) - 2, -1, -1):
        idx = jnp.where(vals[k] == m, k, idx)
    return m, idx


def _softmax_cols(cols):
    m = cols[0]
    for c in cols[1:]:
        m = jnp.maximum(m, c)
    e = [jnp.exp(c - m) for c in cols]
    s = e[0]
    for c in e[1:]:
        s = s + c
    return [c / s for c in e]


GROUP_ROWS = 8
MOE_CHUNK = 32
MOE_MAX_CHUNKS = 16


def _split3(a):
    hi = a.astype(BF16)
    r = a - hi.astype(F32)
    mid = r.astype(BF16)
    lo = (r - mid.astype(F32)).astype(BF16)
    return hi, mid, lo


PAIR_CLASSES = [(a, b) for a in range(EXPERTS_PER_GROUP) for b in range(a + 1, EXPERTS_PER_GROUP)]
CLASS_SPAN = [(min(c for c, p in enumerate(PAIR_CLASSES) if k in p), max(c for c, p in enumerate(PAIR_CLASSES) if k in p))
              for k in range(EXPERTS_PER_GROUP)]


W_SLOTS = 3
MOE_VMEM_LIMIT = 62 * 1024 * 1024


def _moe_kernel(x_ref, nw_ref, wrt_ref, brt_ref, wg_hbm, wu_hbm, wd_hbm, o_ref,
                hs_sc, p_sc, cs_sc, seg_sm, wg_sc, wu_sc, wd_sc, w_sem, *, n_groups, layer):
    TM, D = x_ref.shape
    e = pl.program_id(1)
    n_exp = pl.num_programs(1)
    step = pl.program_id(0) * n_exp + e
    n_steps = pl.num_programs(0) * n_exp

    def weight_copies(s):
        slot, expert = s % W_SLOTS, s % n_exp
        return [pltpu.make_async_copy(src.at[layer, expert], dst.at[slot], w_sem.at[k, slot])
                for k, (src, dst) in enumerate(((wg_hbm, wg_sc), (wu_hbm, wu_sc), (wd_hbm, wd_sc)))]

    @pl.when(step == 0)
    def _():
        for s in range(W_SLOTS - 1):
            for cp in weight_copies(s):
                cp.start()

    @pl.when(step + W_SLOTS - 1 < n_steps)
    def _():
        for cp in weight_copies(step + W_SLOTS - 1):
            cp.start()

    @pl.when(e == 0)
    def _():
        x = x_ref[...]
        scale = lax.rsqrt(jnp.mean(x * x, axis=-1, keepdims=True) + NORM_EPS)

        def h_cols(c0):
            cs = slice(c0, c0 + 512)
            return (x_ref[:, cs] * scale * nw_ref[:, cs]).astype(BF16)

        lg = brt_ref[...] + jnp.zeros((1, TM), F32)
        for c0 in range(0, D, 512):
            lg = lg + _dot_nt(wrt_ref[:, c0:c0 + 512], h_cols(c0))
        row = lambda k: lg[k:k + 1, :]
        g_prob = _softmax_cols([row(g) for g in range(n_groups)])
        g_p, g_idx = _first_max(g_prob)
        e_logit = []
        for k in range(EXPERTS_PER_GROUP):
            v = row(GROUP_ROWS + k)
            for g in range(1, n_groups):
                v = jnp.where(g_idx == g, row(GROUP_ROWS + g * EXPERTS_PER_GROUP + k), v)
            e_logit.append(v)
        e_prob = _softmax_cols(e_logit)
        p1, i1 = _first_max(e_prob)
        rest = [jnp.where(i1 == k, -1.0, e_prob[k]) for k in range(EXPERTS_PER_GROUP)]
        p2, i2 = _first_max(rest)
        den = p1 + p2
        base = g_idx * EXPERTS_PER_GROUP
        e_io = lax.broadcasted_iota(jnp.int32, (LANES, TM), 0)
        comb_t = (jnp.where(e_io == base + i1, g_p * (p1 / den), 0.0)
                  + jnp.where(e_io == base + i2, g_p * (p2 / den), 0.0))

        lo, hi = jnp.minimum(i1, i2), jnp.maximum(i1, i2)
        cls = jnp.zeros((1, TM), jnp.int32)
        for c, (a, b) in enumerate(PAIR_CLASSES):
            cls = jnp.where(lo * EXPERTS_PER_GROUP + hi == a * EXPERTS_PER_GROUP + b, c, cls)
        cat = g_idx * len(PAIR_CLASSES) + cls
        n_cat = n_groups * len(PAIR_CLASSES)
        cat_rows = seg_sm.shape[0]
        onehot_t = jnp.where(lax.broadcasted_iota(jnp.int32, (cat_rows, TM), 0) == cat, 1.0, 0.0)
        upper = jnp.where(lax.broadcasted_iota(jnp.int32, (LANES, LANES), 0)
                          < lax.broadcasted_iota(jnp.int32, (LANES, LANES), 1), 1.0, 0.0).astype(BF16)
        before = jnp.zeros((cat_rows, 1), F32)
        ranks = []
        for b0 in range(0, TM, LANES):
            blk = onehot_t[:, b0:b0 + LANES]
            ranks.append(_dot(blk.astype(BF16), upper) + before)
            before = before + jnp.sum(blk, axis=1, keepdims=True)
        rank = jnp.concatenate(ranks, axis=1)
        start = jnp.int32(0)
        pos = jnp.zeros((1, TM), F32)
        for c in range(n_cat):
            seg_sm[c] = start
            pos = jnp.where(cat == c, start.astype(F32) + rank[c:c + 1, :], pos)
            start = start + jnp.sum(onehot_t[c:c + 1, :]).astype(jnp.int32)
        seg_sm[n_cat] = start
        pos = pos.astype(jnp.int32)

        rb = 256
        for r0 in range(0, TM, rb):
            r_io = lax.broadcasted_iota(jnp.int32, (rb, TM), 0) + r0
            p_sc[r0:r0 + rb, :] = jnp.where(r_io == pos, 1.0, 0.0).astype(BF16)
        p = p_sc[...]
        for c0 in range(0, D, 512):
            hs_sc[:, c0:c0 + 512] = _dot(p, h_cols(c0)).astype(BF16)
        cs = jnp.zeros((TM, LANES), F32)
        for part in _split3(comb_t):
            cs = cs + _dot_nt(p, part)
        cs_sc[...] = cs
        o_ref[...] = jnp.zeros(o_ref.shape, F32)

    g = e // EXPERTS_PER_GROUP
    k = e % EXPERTS_PER_GROUP
    first, last = jnp.int32(CLASS_SPAN[0][0]), jnp.int32(CLASS_SPAN[0][1])
    for kk in range(1, EXPERTS_PER_GROUP):
        first = jnp.where(k == kk, CLASS_SPAN[kk][0], first)
        last = jnp.where(k == kk, CLASS_SPAN[kk][1], last)
    start = seg_sm[g * len(PAIR_CLASSES) + first]
    end = seg_sm[g * len(PAIR_CLASSES) + last + 1]
    c_lo = start // MOE_CHUNK
    c_hi = jnp.where(end > start, (end + MOE_CHUNK - 1) // MOE_CHUNK, c_lo)

    for cp in weight_copies(step):
        cp.wait()
    slot = step % W_SLOTS

    def expert_rows(c, n_chunks):
        m = n_chunks * MOE_CHUNK
        r0 = pl.multiple_of(c * MOE_CHUNK, MOE_CHUNK)
        rows = hs_sc[pl.ds(r0, m), :]
        lane = lax.broadcasted_iota(jnp.int32, (m, LANES), 1)
        w = jnp.sum(jnp.where(lane == e, cs_sc[pl.ds(r0, m), :], 0.0), axis=-1, keepdims=True)
        a = jax.nn.silu(_dot(rows, wg_sc[slot])) * _dot(rows, wu_sc[slot]) * w
        o_ref[pl.ds(r0, m), :] += _dot(a.astype(BF16), wd_sc[slot])

    def full(k, carry):
        expert_rows(c_lo + MOE_MAX_CHUNKS * k, MOE_MAX_CHUNKS)
        return carry

    n_chunks = c_hi - c_lo
    n_full = n_chunks // MOE_MAX_CHUNKS
    lax.fori_loop(0, n_full, full, 0)
    for rest in range(1, MOE_MAX_CHUNKS):
        @pl.when(n_chunks - n_full * MOE_MAX_CHUNKS == rest)
        def _(rest=rest):
            expert_rows(c_hi - rest, rest)

    @pl.when(e == n_exp - 1)
    def _():
        p = p_sc[...]
        for c0 in range(0, D, 512):
            ys = o_ref[:, c0:c0 + 512].astype(BF16)
            y = lax.dot_general(p, ys, (((0,), (0,)), ((), ())), preferred_element_type=F32)
            o_ref[:, c0:c0 + 512] = x_ref[:, c0:c0 + 512] + y


def _moe(x, norm_w, wg, bg, we, be, w_gate, w_up, w_down, layer, tm=1024):
    T, D = x.shape
    n_groups, n_exp = wg.shape[1], we.shape[1]
    ff = w_gate.shape[3]
    assert n_groups <= GROUP_ROWS and n_exp == n_groups * EXPERTS_PER_GROUP and T % tm == 0
    rows = GROUP_ROWS + n_exp
    cat_rows = -(-(n_groups * len(PAIR_CLASSES) + 1) // SUBLANES) * SUBLANES
    wrt = jnp.zeros((rows, D), F32).at[:n_groups].set(wg.T).at[GROUP_ROWS:].set(we.T).astype(BF16)
    brt = jnp.zeros((rows, 1), F32).at[:n_groups, 0].set(bg).at[GROUP_ROWS:, 0].set(be)
    tile = pl.BlockSpec((tm, D), lambda i, e: (i, 0))
    return pl.pallas_call(
        functools.partial(_moe_kernel, n_groups=n_groups, layer=layer),
        grid=(T // tm, n_exp),
        in_specs=[tile, _resident((1, D)), _resident((rows, D)), _resident((rows, 1)),
                  pl.BlockSpec(memory_space=pl.ANY), pl.BlockSpec(memory_space=pl.ANY),
                  pl.BlockSpec(memory_space=pl.ANY)],
        out_specs=tile,
        out_shape=jax.ShapeDtypeStruct((T, D), F32),
        scratch_shapes=[pltpu.VMEM((tm, D), BF16),
                        pltpu.VMEM((tm, tm), BF16),
                        pltpu.VMEM((tm, LANES), F32),
                        pltpu.SMEM((cat_rows,), jnp.int32),
                        pltpu.VMEM((W_SLOTS, D, ff), BF16),
                        pltpu.VMEM((W_SLOTS, D, ff), BF16),
                        pltpu.VMEM((W_SLOTS, ff, D), BF16),
                        pltpu.SemaphoreType.DMA((3, W_SLOTS))],
        compiler_params=pltpu.CompilerParams(dimension_semantics=("arbitrary", "arbitrary"),
                                             vmem_limit_bytes=MOE_VMEM_LIMIT),
    )(x, norm_w.reshape(1, D), wrt, brt, w_gate, w_up, w_down)


POOL_HALO = 16


def _pool_kernel(x_ref, nw_ref, pw_ref, ps_ref, o_ref, hb, pa, pb):
    TS, D = x_ref.shape
    n_groups, pc, _ = pw_ref.shape
    i = pl.program_id(1)
    nw = nw_ref[...]
    top = SUBLANES + POOL_HALO
    n_rows = top + TS

    @pl.when(jnp.logical_and(pl.program_id(0) == 0, i == 0))
    def _():
        hb[...] = jnp.zeros(hb.shape, F32)
        pa[0:SUBLANES] = jnp.zeros((SUBLANES, pc), F32)
        pb[0:SUBLANES] = jnp.zeros((SUBLANES, pc), F32)

    hb[SUBLANES:top] = jnp.where(i > 0, hb[n_rows - POOL_HALO:n_rows], 0.0)
    hb[top:] = _rms(x_ref[...], nw)
    t = i * TS + lax.broadcasted_iota(jnp.int32, (TS, 1), 0)
    for g, w in enumerate(POOL_WINDOWS):
        cs = slice(g * pc, (g + 1) * pc)
        cur = hb[top:, cs]
        src, cols, step, dst = hb, cs, 1, pa
        while step < w:
            dst[SUBLANES:n_rows, :] = src[SUBLANES:n_rows, cols] + src[SUBLANES - step:n_rows - step, cols]
            src, cols, step, dst = dst, slice(None), 2 * step, (pb if dst is pa else pa)
        s = src[top:n_rows, cols]
        count = jnp.minimum(t + 1, w).astype(F32)
        d = s / count - cur
        mixed = _dot(d.astype(BF16), pw_ref[g])
        o_ref[:, cs] = x_ref[:, cs] + ps_ref[:, cs] * mixed


def _pool_layer(x, norm_w, pool_w, pool_scale, B, S, TS=256):
    T, D = x.shape
    assert len(POOL_WINDOWS) == pool_w.shape[0] and max(POOL_WINDOWS) - 1 <= POOL_HALO
    ns = S // TS
    pw = pool_w.astype(BF16)
    return pl.pallas_call(
        _pool_kernel,
        grid=(B, ns),
        in_specs=[pl.BlockSpec((TS, D), lambda b, i: (b * ns + i, 0)),
                  _resident((1, D)), _resident(pw.shape), _resident((1, D))],
        out_specs=pl.BlockSpec((TS, D), lambda b, i: (b * ns + i, 0)),
        out_shape=jax.ShapeDtypeStruct((T, D), F32),
        scratch_shapes=[pltpu.VMEM((SUBLANES + POOL_HALO + TS, D), F32),
                        pltpu.VMEM((SUBLANES + POOL_HALO + TS, D // len(POOL_WINDOWS)), F32),
                        pltpu.VMEM((SUBLANES + POOL_HALO + TS, D // len(POOL_WINDOWS)), F32)],
        compiler_params=_params("arbitrary", "arbitrary"),
    )(x, norm_w.reshape(1, D), pw, pool_scale.reshape(1, D))


def _chunk(S):
    return 256 if S % 256 == 0 else 128


def kernel(x, rel_bias, mix_norm_e, w_in_e, q_norm_e, k_norm_e, conv_w_e, conv_b_e, conv_ln_g_e, conv_ln_b_e,
           w_out_e, mix_norm_o, pool_w_o, pool_scale_o, ffn_norm, router_group_w, router_group_b,
           router_expert_w, router_expert_b, w_gate, w_up, w_down):
    B, S, D = x.shape
    T = B * S
    depth = ffn_norm.shape[0]
    n_heads = rel_bias.shape[1]
    idx_heads = (w_in_e.shape[2] - n_heads * HEAD_DIM - 2 * HEAD_DIM - IDX_DIM - 2 * conv_w_e.shape[2]) \
        // (IDX_DIM + 1)
    k_top = min(INDEX_TOPK, S // 4)
    C = _chunk(S)
    xf = x.reshape(T, D)
    bias_tiles, bias_stat = _rel_bias_tiles(rel_bias, C)
    expert_w = None
    for l in range(depth):
        i = l // 2
        if l % 2 == 0:
            q, k, v, iq, ik, iw, u = _in_proj(xf, mix_norm_e[i], w_in_e[i], q_norm_e[i], k_norm_e[i],
                                              n_heads, idx_heads, conv_w_e.shape[2])
            cast = (w_gate, w_up, w_down) if expert_w is None else ()
            if any(_cast_slabs(w, B * (S // C)) is None for w in cast):
                cast = ()
            attn, done = _attention(q, k, v, iq, ik, iw, bias_tiles, bias_stat, B, S, k_top, C, cast)
            expert_w = done if cast else expert_w
            conv = _conv_module(u, conv_w_e[i], conv_b_e[i], conv_ln_g_e[i], conv_ln_b_e[i], B, S)
            xf = _out_proj(xf, attn, conv, w_out_e[i])
        else:
            xf = _pool_layer(xf, mix_norm_o[i], pool_w_o[i], pool_scale_o[i], B, S)
        if expert_w is None:
            expert_w = [w.astype(BF16) for w in (w_gate, w_up, w_down)]
        xf = _moe(xf, ffn_norm[l], router_group_w[l], router_group_b[l], router_expert_w[l], router_expert_b[l],
                  *expert_w, layer=l)
    return xf.reshape(B, S, D)
```

```python
import functools
import math

import jax
import jax.numpy as jnp
from jax import lax
from jax.experimental import pallas as pl
from jax.experimental.pallas import tpu as pltpu

F32 = jnp.float32
BF16 = jnp.bfloat16

NORM_EPS = 1e-6
HEAD_DIM = 128
IDX_DIM = 64
INDEX_TOPK = 256
REL_BUCKETS = 32
REL_MAX_DIST = 128
POOL_WINDOWS = (2, 4, 8, 16)
EXPERTS_PER_GROUP = 4
LANES = 128
VMEM_LIMIT = 56 * 1024 * 1024
NEG = -1e30
INT_MIN = -(2 ** 31)


def _dot(a, b):
    return jnp.dot(a, b, preferred_element_type=F32)


def _dot_nt(a, b):
    return lax.dot_general(a, b, (((1,), (1,)), ((), ())), preferred_element_type=F32)


def _rms(x, w):
    return x * lax.rsqrt(jnp.mean(x * x, axis=-1, keepdims=True) + NORM_EPS) * w


def _params(*sem):
    return pltpu.CompilerParams(dimension_semantics=sem, vmem_limit_bytes=VMEM_LIMIT)


def _resident(shape):
    nd = len(shape)
    return pl.BlockSpec(shape, lambda *_: (0,) * nd, pipeline_mode=pl.Buffered(1))


def _in_proj_kernel(x_ref, nw_ref, qn_ref, kn_ref, wq_ref, wkv_ref, wiq_ref, wikw_ref, wa_ref, wg_ref,
                    q_ref, k_ref, v_ref, iq_ref, ik_ref, iw_ref, u_ref, *, q_scale, iw_scale):
    h = _rms(x_ref[...], nw_ref[...]).astype(BF16)
    n_pairs = q_ref.shape[0] // 2
    qn = qn_ref[...] * q_scale
    for c in range(n_pairs):
        qq = _dot(h, wq_ref[:, c * 256:(c + 1) * 256])
        for s in range(2):
            qh = qq[:, s * HEAD_DIM:(s + 1) * HEAD_DIM]
            q_ref[2 * c + s] = _rms(qh, qn).astype(BF16)
    kv = _dot(h, wkv_ref[...])
    k_ref[...] = _rms(kv[:, :HEAD_DIM], kn_ref[...]).astype(BF16)
    v_ref[...] = kv[:, HEAD_DIM:].astype(BF16)
    for c in range(iq_ref.shape[0] // 2):
        r = _dot(h, wiq_ref[:, c * 256:(c + 1) * 256])
        iq_ref[2 * c] = r[:, :LANES].astype(BF16)
        iq_ref[2 * c + 1] = r[:, LANES:].astype(BF16)
    r = _dot(h, wikw_ref[...])
    ik_ref[...] = r[:, :LANES].astype(BF16)
    iw_ref[...] = r[:, LANES:] * iw_scale
    for c in range(u_ref.shape[1] // 256):
        cs = slice(c * 256, (c + 1) * 256)
        a = _dot(h, wa_ref[:, cs])
        g = _dot(h, wg_ref[:, cs])
        u_ref[:, cs] = (a * jax.nn.sigmoid(g)).astype(BF16)


def _split_w_in_kernel(w_ref, wq_ref, wkv_ref, wiq_ref, wikw_ref, wa_ref, wg_ref, *, idx_heads):
    o = 0
    for ref in (wq_ref, wkv_ref, wiq_ref):
        n = ref.shape[1]
        ref[...] = w_ref[:, o:o + n].astype(BF16)
        o += n
    wik = w_ref[:, o:o + IDX_DIM]
    wiw = w_ref[:, o + IDX_DIM:o + IDX_DIM + idx_heads]
    o += IDX_DIM + idx_heads
    pad = jnp.zeros((wik.shape[0], LANES - idx_heads), F32)
    wikw_ref[...] = jnp.concatenate([wik, wik, wiw, pad], axis=1).astype(BF16)
    for ref in (wa_ref, wg_ref):
        n = ref.shape[1]
        ref[...] = w_ref[:, o:o + n].astype(BF16)
        o += n


def _split_w_in(w_in, n_heads, idx_heads, conv_ch, tk=256):
    D, n_in = w_in.shape
    widths = (n_heads * HEAD_DIM, 2 * HEAD_DIM, idx_heads * IDX_DIM, 2 * LANES, conv_ch, conv_ch)
    assert n_in == sum(widths) - 2 * LANES + IDX_DIM + idx_heads and 2 * IDX_DIM == LANES and D % tk == 0
    return pl.pallas_call(
        functools.partial(_split_w_in_kernel, idx_heads=idx_heads),
        grid=(D // tk,),
        in_specs=[pl.BlockSpec((tk, n_in), lambda i: (i, 0))],
        out_specs=[pl.BlockSpec((tk, w), lambda i: (i, 0)) for w in widths],
        out_shape=[jax.ShapeDtypeStruct((D, w), BF16) for w in widths],
        compiler_params=_params("arbitrary"),
    )(w_in)


def _in_proj(x, norm_w, w_in, q_norm, k_norm, n_heads, idx_heads, conv_ch, tm=256):
    T, D = x.shape
    iq_w = idx_heads * IDX_DIM
    ws = _split_w_in(w_in, n_heads, idx_heads, conv_ch)
    row = lambda w: pl.BlockSpec((tm, w), lambda i: (i, 0))
    heads = lambda n: pl.BlockSpec((n, tm, LANES), lambda i: (0, i, 0))
    kern = functools.partial(_in_proj_kernel, q_scale=HEAD_DIM ** -0.5,
                             iw_scale=(idx_heads ** -0.5) * (IDX_DIM ** -0.5))
    return pl.pallas_call(
        kern,
        grid=(T // tm,),
        in_specs=[row(D), _resident((1, D)), _resident((1, HEAD_DIM)), _resident((1, HEAD_DIM))]
                 + [_resident(w.shape) for w in ws],
        out_specs=[heads(n_heads), row(HEAD_DIM), row(HEAD_DIM), heads(iq_w // LANES), row(LANES), row(LANES),
                   row(conv_ch)],
        out_shape=[jax.ShapeDtypeStruct((n_heads, T, HEAD_DIM), BF16),
                   jax.ShapeDtypeStruct((T, HEAD_DIM), BF16),
                   jax.ShapeDtypeStruct((T, HEAD_DIM), BF16),
                   jax.ShapeDtypeStruct((iq_w // LANES, T, LANES), BF16),
                   jax.ShapeDtypeStruct((T, LANES), BF16),
                   jax.ShapeDtypeStruct((T, LANES), F32),
                   jax.ShapeDtypeStruct((T, conv_ch), BF16)],
        compiler_params=_params("arbitrary"),
    )(x, norm_w.reshape(1, D), q_norm.reshape(1, HEAD_DIM), k_norm.reshape(1, HEAD_DIM), *ws)


def _rel_bias_kernel(rb_ref, o_ref, stat_ref):
    _, n_heads, C, _ = o_ref.shape
    tau = lax.broadcasted_iota(jnp.int32, (C, C), 0)
    sig = lax.broadcasted_iota(jnp.int32, (C, C), 1)
    max_exact = REL_BUCKETS // 2
    for kind in range(2):
        d = tau - sig + kind * C
        n = jnp.maximum(d, 0)
        nf = jnp.maximum(n, 1).astype(F32)
        large = max_exact + (jnp.log(nf / max_exact) / math.log(REL_MAX_DIST / max_exact)
                             * (REL_BUCKETS - max_exact)).astype(jnp.int32)
        large = jnp.minimum(large, REL_BUCKETS - 1)
        bucket = jnp.where(n < max_exact, n, large)
        for h in range(n_heads):
            b = jnp.zeros((C, C), F32)
            for bk in range(REL_BUCKETS):
                b = jnp.where(bucket == bk, rb_ref[bk, h], b)
            b = b - rb_ref[REL_BUCKETS - 1, h]
            if kind == 0:
                b = jnp.where(d < 0, NEG, b)
            o_ref[kind, h] = b
    for h in range(n_heads):
        hi = rb_ref[0, h]
        lo = rb_ref[0, h]
        for bk in range(1, REL_BUCKETS):
            hi = jnp.maximum(hi, rb_ref[bk, h])
            lo = jnp.minimum(lo, rb_ref[bk, h])
        stat_ref[0, h] = hi - rb_ref[REL_BUCKETS - 1, h]
        stat_ref[1, h] = lo - rb_ref[REL_BUCKETS - 1, h]


def _rel_bias_tiles(rel_bias, C):
    n_heads = rel_bias.shape[1]
    assert C >= REL_MAX_DIST
    return pl.pallas_call(
        _rel_bias_kernel,
        in_specs=[pl.BlockSpec(memory_space=pltpu.SMEM)],
        out_specs=[pl.BlockSpec(memory_space=pltpu.VMEM), pl.BlockSpec(memory_space=pltpu.SMEM)],
        out_shape=[jax.ShapeDtypeStruct((2, n_heads, C, C), F32), jax.ShapeDtypeStruct((2, n_heads), F32)],
        compiler_params=pltpu.CompilerParams(vmem_limit_bytes=VMEM_LIMIT),
    )(rel_bias)


SHIFT_SPAN_LIMIT = 60.0


TILE_GROUP = 8


def _grouped_loop(n, fn, group):
    def body(k, carry):
        for u in range(group):
            fn(group * k + u)
        return carry

    lax.fori_loop(0, n // group, body, 0)
    base = (n // group) * group
    size = group // 2
    while size >= 1:
        take = ((n - base) & size) != 0

        @pl.when(take)
        def _(base=base, size=size):
            for u in range(size):
                fn(base + u)
        base = base + jnp.where(take, size, 0)
        size //= 2


def _order_key(x):
    bits = pltpu.bitcast(x, jnp.int32)
    return bits ^ ((bits >> 31) & 0x7FFFFFFF)


def _row_to_col(row):
    C = row.shape[1]
    halves = []
    for part in (row >> 16, row & 0xFFFF):
        halves.append(jnp.broadcast_to(part.astype(F32), (LANES, C)).T[:, 0:1].astype(jnp.int32))
    return (halves[0] << 16) | halves[1]


def _attn_kernel(bstat_ref, q_ref, k_ref, v_ref, iq_ref, ik_ref, iw_ref, bias_ref, *rest, k_top, n_cast):
    cast_in, (o_ref, *cast_out) = rest[:n_cast], rest[n_cast:2 * n_cast + 1]
    key_sc, keyt_sc, keyt16_sc, iqm_sc, vx_sc, kmax_sc, shift_sc, acc_sc = rest[2 * n_cast + 1:]
    for src, dst in zip(cast_in, cast_out):
        dst[...] = src[...].astype(BF16)
    n_heads, C, _ = q_ref.shape
    idx_heads = iqm_sc.shape[0]
    idx_bits = max(1, (key_sc.shape[0] * C - 1).bit_length())
    qi = pl.program_id(1)
    nkv = qi + 1

    @pl.when(qi == 0)
    def _():
        vx_sc[:, :HEAD_DIM] = v_ref[...]
        vx_sc[:, HEAD_DIM:] = jnp.ones((vx_sc.shape[0], HEAD_DIM), BF16)
        kf = k_ref[...].astype(F32)
        k2 = jnp.sum(kf * kf, axis=-1, keepdims=True)
        kmax_sc[...] = jnp.broadcast_to(jnp.sqrt(jnp.max(k2, axis=0, keepdims=True)), kmax_sc.shape)

    lane = lax.broadcasted_iota(jnp.int32, (C, LANES), 1)
    for p in range(idx_heads // 2):
        qp = iq_ref[p].astype(F32)
        iqm_sc[2 * p] = jnp.where(lane < IDX_DIM, qp, 0.0).astype(BF16)
        iqm_sc[2 * p + 1] = jnp.where(lane >= IDX_DIM, qp, 0.0).astype(BF16)
    iw = iw_ref[...]
    tau = lax.broadcasted_iota(jnp.int32, (C, C), 0)
    sig = lax.broadcasted_iota(jnp.int32, (C, C), 1)

    def score_tile(j):
        off = pl.multiple_of(j * C, C)
        ikc = ik_ref[pl.ds(off, C), :]
        acc = jnp.zeros((C, C), F32)
        for hh in range(idx_heads):
            s = _dot_nt(iqm_sc[hh], ikc)
            acc = acc + jnp.maximum(s, 0.0) * iw[:, hh:hh + 1]
        acc = jnp.where(jnp.logical_and(j == qi, sig > tau), -jnp.inf, acc)
        key_sc[j] = _order_key(acc)
        kt = _order_key(acc.T)
        keyt_sc[j] = kt
        keyt16_sc[j] = (kt >> 16).astype(jnp.int16)

    _grouped_loop(nkv, score_tile, TILE_GROUP)

    SUB = 32

    def count(hit):
        def body(j, cnt):
            for r in range(C // SUB):
                cnt = cnt + hit(j, r)
            return cnt
        cnt = lax.fori_loop(0, nkv, body, jnp.zeros((SUB, C), F32))
        return jnp.sum(cnt, axis=0, keepdims=True)

    def keyt(j, r):
        return keyt_sc[j, pl.ds(r * SUB, SUB), :]

    def count16(cand16):
        one = jnp.ones((SUB, C), jnp.int16)
        zero = jnp.zeros((SUB, C), jnp.int16)

        def body(j, cnt):
            for r in range(C // SUB):
                cnt = cnt + jnp.where(keyt16_sc[j, pl.ds(r * SUB, SUB), :] >= cand16, one, zero)
            return cnt
        cnt = lax.fori_loop(0, nkv, body, zero)
        return jnp.sum(cnt.astype(jnp.int32).astype(F32), axis=0, keepdims=True)

    def bit16_body(b, carry):
        res, n_res = carry
        cand = res ^ lax.shift_left(jnp.int32(1), 31 - b)
        tot = count16((cand >> 16).astype(jnp.int16))
        ok = tot >= k_top
        return jnp.where(ok, cand, res), jnp.where(ok, tot, n_res)

    searched = nkv * C > k_top
    carry = (jnp.full((1, C), INT_MIN, jnp.int32), jnp.zeros((1, C), F32))
    top_row, n_top = lax.fori_loop(0, jnp.where(searched, 16, 0), bit16_body, carry)

    top16 = top_row >> 16

    def low_body(j, c):
        kt = keyt_sc[j]
        hi = kt >> 16
        lo = (kt & 0xFFFF) - 32768
        lo = jnp.where(hi == top16, lo, jnp.where(hi > top16, 32767, -32768))
        keyt16_sc[j] = lo.astype(jnp.int16)
        return c

    lax.fori_loop(0, jnp.where(searched, nkv, 0), low_body, 0)

    def bit16_low_body(b, carry):
        res, n_res = carry
        cand = res ^ lax.shift_left(jnp.int32(1), 31 - b)
        tot = count16(((cand & 0xFFFF) - 32768).astype(jnp.int16))
        ok = tot >= k_top
        return jnp.where(ok, cand, res), jnp.where(ok, tot, n_res)

    thr_row, n_thr = lax.fori_loop(16, jnp.where(searched, 32, 16), bit16_low_body, (top_row, n_top))
    thr = _row_to_col(thr_row)

    @pl.when(jnp.max(n_thr) > k_top)
    def _():
        need = k_top - count(lambda j, r: jnp.where(keyt(j, r) > thr_row, 1.0, 0.0))
        s_sub = lax.broadcasted_iota(jnp.int32, (SUB, C), 0)

        def idx_body(b, last):
            cand = last | lax.shift_left(jnp.int32(1), idx_bits - 1 - b)
            below = count(lambda j, r: jnp.where(
                keyt(j, r) == thr_row, jnp.where(j * C + r * SUB + s_sub < cand, 1.0, 0.0), 0.0))
            return jnp.where(below < need, cand, last)

        last = _row_to_col(lax.fori_loop(0, idx_bits, idx_body, jnp.zeros((1, C), jnp.int32)))

        def drop_body(j, carry):
            kk = key_sc[j]
            key_sc[j] = jnp.where(kk == thr, jnp.where(j * C + sig > last, kk - 1, kk), kk)
            return carry

        lax.fori_loop(0, nkv, drop_body, 0)

    kmax = kmax_sc[0:1, 0:1] * 1.001
    worst = jnp.zeros((C, 1), F32)
    for h in range(n_heads):
        qf = q_ref[h].astype(F32)
        bound = jnp.sqrt(jnp.sum(qf * qf, axis=-1, keepdims=True)) * kmax
        shift_sc[h] = jnp.broadcast_to(bound + bstat_ref[0, h], (C, LANES))
        worst = jnp.maximum(worst, 2.0 * bound + (bstat_ref[0, h] - bstat_ref[1, h]))
    loose = jnp.max(worst) > SHIFT_SPAN_LIMIT

    def logits(j, h, kind):
        off = pl.multiple_of(j * C, C)
        lg = _dot_nt(q_ref[h], k_ref[pl.ds(off, C), :])
        return lg if kind is None else lg + bias_ref[kind, h]

    def near_tiles(fn):
        @pl.when(qi > 0)
        def _():
            fn(qi - 1, 1)
            fn(qi, 0)

        @pl.when(qi == 0)
        def _():
            fn(qi, 0)

    def far_tiles(fn, group=TILE_GROUP):
        _grouped_loop(jnp.maximum(qi - 1, 0), lambda j: fn(j, None), group)

    @pl.when(loose)
    def _():
        for h in range(n_heads):
            shift_sc[h] = jnp.full((C, LANES), NEG, F32)

        def max_tile(j, kind):
            sel = key_sc[j] >= thr
            for h in range(n_heads):
                lg = jnp.where(sel, logits(j, h, kind), NEG)
                m = jnp.max(lg, axis=-1, keepdims=True)
                shift_sc[h] = jnp.maximum(shift_sc[h], jnp.broadcast_to(m, (C, LANES)))

        far_tiles(max_tile, group=1)
        near_tiles(max_tile)

    acc_sc[...] = jnp.zeros(acc_sc.shape, F32)

    def attn_tile(j, kind):
        off = pl.multiple_of(j * C, C)
        vx = vx_sc[pl.ds(off, C), :]
        sel = key_sc[j] >= thr
        for h in range(n_heads):
            sh = jnp.concatenate([shift_sc[h]] * (C // LANES), axis=1)
            p = jnp.where(sel, jnp.exp(logits(j, h, kind) - sh), 0.0)
            acc_sc[h] += _dot(p.astype(BF16), vx)

    far_tiles(attn_tile)
    near_tiles(attn_tile)
    for h in range(n_heads):
        a = acc_sc[h]
        o_ref[:, h * HEAD_DIM:(h + 1) * HEAD_DIM] = (a[:, :HEAD_DIM] / a[:, HEAD_DIM:]).astype(BF16)


CAST_SLAB_ELEMS = 1 << 20


def _cast_slabs(w, steps):
    rows = math.prod(w.shape[:-1])
    if rows % (steps * 16) != 0 or rows // steps * w.shape[-1] > CAST_SLAB_ELEMS:
        return None
    return w.reshape(steps, rows // steps, w.shape[-1])


def _attention(q, k, v, iq, ik, iw, bias_tiles, bias_stat, B, S, k_top, C, cast=()):
    n_heads, T, _ = q.shape
    n_pairs = iq.shape[0]
    nq = S // C
    slabs = [_cast_slabs(w, B * nq) for w in cast]
    kern = functools.partial(_attn_kernel, k_top=k_top, n_cast=len(slabs))
    heads = lambda n: pl.BlockSpec((n, C, LANES), lambda b, i: (0, b * nq + i, 0))
    seq = pl.BlockSpec((S, LANES), lambda b, i: (b, 0))
    slab_specs = [pl.BlockSpec((None,) + w.shape[1:], lambda b, i: (b * nq + i, 0, 0)) for w in slabs]
    outs = pl.pallas_call(
        kern,
        grid=(B, nq),
        in_specs=[pl.BlockSpec(memory_space=pltpu.SMEM),
                  heads(n_heads), seq, seq, heads(n_pairs), seq,
                  pl.BlockSpec((C, LANES), lambda b, i: (b * nq + i, 0)),
                  _resident(bias_tiles.shape)] + slab_specs,
        out_specs=[pl.BlockSpec((C, n_heads * HEAD_DIM), lambda b, i: (b * nq + i, 0))] + slab_specs,
        out_shape=[jax.ShapeDtypeStruct((T, n_heads * HEAD_DIM), BF16)]
                  + [jax.ShapeDtypeStruct(w.shape, BF16) for w in slabs],
        scratch_shapes=[pltpu.VMEM((nq, C, C), jnp.int32),
                        pltpu.VMEM((nq, C, C), jnp.int32),
                        pltpu.VMEM((nq, C, C), jnp.int16),
                        pltpu.VMEM((2 * n_pairs, C, LANES), BF16),
                        pltpu.VMEM((S, 2 * HEAD_DIM), BF16),
                        pltpu.VMEM((8, LANES), F32),
                        pltpu.VMEM((n_heads, C, LANES), F32),
                        pltpu.VMEM((n_heads, C, 2 * HEAD_DIM), F32)],
        compiler_params=_params("arbitrary", "arbitrary"),
    )(bias_stat, q, k, v, iq, ik, iw, bias_tiles, *slabs)
    return outs[0], [o.reshape(w.shape) for o, w in zip(outs[1:], cast)]


CONV_HALO = 32


SUBLANES = 8
CONV_SLAB = 64


def _conv_kernel(u_ref, w_ref, cb_ref, g_ref, b_ref, o_ref, cp_sc, y_sc, halo_sc, *, width):
    TS, CH = u_ref.shape
    n_rows = CONV_HALO + TS
    i = pl.program_id(1)

    @pl.when(jnp.logical_and(pl.program_id(0) == 0, i == 0))
    def _():
        halo_sc[...] = jnp.zeros(halo_sc.shape, F32)

    halo = jnp.where(i > 0, halo_sc[...], 0.0)
    halo_sc[...] = u_ref[TS - CONV_HALO:, :].astype(F32)
    pad = jnp.zeros((SUBLANES, LANES), F32)
    for c in range(CH // LANES):
        cs = slice(c * LANES, (c + 1) * LANES)
        col = jnp.concatenate([halo[:, cs], u_ref[:, cs].astype(F32), pad], axis=0)
        for r in range(SUBLANES):
            cp_sc[r, :, cs] = col[r:r + n_rows]

    base = CONV_HALO - (width - 1)
    phases = {}
    for j in range(width):
        q, r = divmod(base + j, SUBLANES)
        phases.setdefault(r, []).append((q, j))

    def slab(s, carry):
        t0 = pl.multiple_of(s * CONV_SLAB, CONV_SLAB)
        for c in range(CH // LANES):
            cs = slice(c * LANES, (c + 1) * LANES)
            acc = jnp.zeros((CONV_SLAB, LANES), F32)
            for r, taps in phases.items():
                q_lo, q_hi = taps[0][0], taps[-1][0]
                win = cp_sc[r, pl.ds(q_lo * SUBLANES + t0, CONV_SLAB + (q_hi - q_lo) * SUBLANES), cs]
                for q, j in taps:
                    off = (q - q_lo) * SUBLANES
                    acc = acc + w_ref[j:j + 1, cs] * win[off:off + CONV_SLAB]
            y_sc[pl.ds(t0, CONV_SLAB), cs] = acc + cb_ref[:, cs]
        return carry

    lax.fori_loop(0, TS // CONV_SLAB, slab, 0)
    y = y_sc[...]
    mu = jnp.mean(y, axis=-1, keepdims=True)
    yc = y - mu
    var = jnp.mean(yc * yc, axis=-1, keepdims=True)
    yn = yc * lax.rsqrt(var + NORM_EPS) * g_ref[...] + b_ref[...]
    o_ref[...] = (yn * jax.nn.sigmoid(yn)).astype(BF16)


def _conv_module(u, conv_w, conv_b, ln_g, ln_b, B, S, TS=256):
    T, CH = u.shape
    width = conv_w.shape[0]
    assert width - 1 <= CONV_HALO
    ns = S // TS
    wpad = jnp.zeros((CONV_HALO, CH), F32).at[:width].set(conv_w)
    vec = lambda a: a.reshape(1, CH)
    kern = functools.partial(_conv_kernel, width=width)
    return pl.pallas_call(
        kern,
        grid=(B, ns),
        in_specs=[pl.BlockSpec((TS, CH), lambda b, i: (b * ns + i, 0)),
                  _resident((CONV_HALO, CH)), _resident((1, CH)), _resident((1, CH)), _resident((1, CH))],
        out_specs=pl.BlockSpec((TS, CH), lambda b, i: (b * ns + i, 0)),
        out_shape=jax.ShapeDtypeStruct((T, CH), BF16),
        scratch_shapes=[pltpu.VMEM((SUBLANES, CONV_HALO + TS, CH), F32),
                        pltpu.VMEM((TS, CH), F32),
                        pltpu.VMEM((CONV_HALO, CH), F32)],
        compiler_params=_params("arbitrary", "arbitrary"),
    )(u, wpad, vec(conv_b), vec(ln_g), vec(ln_b))


def _out_proj_kernel(x_ref, a_ref, c_ref, wa_ref, wc_ref, o_ref):
    o_ref[...] = x_ref[...] + _dot(a_ref[...], wa_ref[...]) + _dot(c_ref[...], wc_ref[...])


def _out_proj(x, attn, conv, w_out, tm=512):
    T, D = x.shape
    aw, cw = attn.shape[1], conv.shape[1]
    wa = w_out[:aw].astype(BF16)
    wc = w_out[aw:].astype(BF16)
    row = lambda w: pl.BlockSpec((tm, w), lambda i: (i, 0))
    return pl.pallas_call(
        _out_proj_kernel,
        grid=(T // tm,),
        in_specs=[row(D), row(aw), row(cw), _resident(wa.shape), _resident(wc.shape)],
        out_specs=row(D),
        out_shape=jax.ShapeDtypeStruct((T, D), F32),
        compiler_params=_params("arbitrary"),
    )(x, attn, conv, wa, wc)


def _first_max(vals):
    m = vals[0]
    for v in vals[1:]:
        m = jnp.maximum(m, v)
    idx = jnp.full(m.shape, len(vals) - 1, jnp.int32)
    for k in range(len(vals) - 2, -1, -1):
        idx = jnp.where(vals[k] == m, k, idx)
    return m, idx


def _softmax_cols(cols):
    m = cols[0]
    for c in cols[1:]:
        m = jnp.maximum(m, c)
    e = [jnp.exp(c - m) for c in cols]
    s = e[0]
    for c in e[1:]:
        s = s + c
    return [c / s for c in e]


GROUP_ROWS = 8
MOE_CHUNK = 64
MOE_MAX_CHUNKS = 8


def _split3(a):
    hi = a.astype(BF16)
    r = a - hi.astype(F32)
    mid = r.astype(BF16)
    lo = (r - mid.astype(F32)).astype(BF16)
    return hi, mid, lo


PAIR_CLASSES = [(a, b) for a in range(EXPERTS_PER_GROUP) for b in range(a + 1, EXPERTS_PER_GROUP)]
CLASS_SPAN = [(min(c for c, p in enumerate(PAIR_CLASSES) if k in p), max(c for c, p in enumerate(PAIR_CLASSES) if k in p))
              for k in range(EXPERTS_PER_GROUP)]


ROUTE_ROWS = 16
W_SLOTS = 3
MOE_VMEM_LIMIT = 62 * 1024 * 1024


def _moe_kernel(x_ref, nw_ref, wrt_ref, brt_ref, wg_hbm, wu_hbm, wd_hbm, o_ref,
                hs_sc, p_sc, cs_sc, seg_sm, wg_sc, wu_sc, wd_sc, w_sem, *, n_groups, layer):
    TM, D = x_ref.shape
    e = pl.program_id(1)
    n_exp = pl.num_programs(1)
    step = pl.program_id(0) * n_exp + e
    n_steps = pl.num_programs(0) * n_exp

    def weight_copies(s):
        slot, expert = s % W_SLOTS, s % n_exp
        return [pltpu.make_async_copy(src.at[layer, expert], dst.at[slot], w_sem.at[k, slot])
                for k, (src, dst) in enumerate(((wg_hbm, wg_sc), (wu_hbm, wu_sc), (wd_hbm, wd_sc)))]

    @pl.when(step == 0)
    def _():
        for s in range(W_SLOTS - 1):
            for cp in weight_copies(s):
                cp.start()

    @pl.when(step + W_SLOTS - 1 < n_steps)
    def _():
        for cp in weight_copies(step + W_SLOTS - 1):
            cp.start()

    @pl.when(e == 0)
    def _():
        x = x_ref[...]
        scale = lax.rsqrt(jnp.mean(x * x, axis=-1, keepdims=True) + NORM_EPS)

        def h_cols(c0):
            cs = slice(c0, c0 + 512)
            return (x_ref[:, cs] * scale * nw_ref[:, cs]).astype(BF16)

        lg = brt_ref[...] + jnp.zeros((1, TM), F32)
        for c0 in range(0, D, 512):
            lg = lg + _dot_nt(wrt_ref[:, c0:c0 + 512], h_cols(c0))
        row = lambda k: lg[k:k + 1, :]
        g_prob = _softmax_cols([row(g) for g in range(n_groups)])
        g_p, g_idx = _first_max(g_prob)
        e_logit = []
        for k in range(EXPERTS_PER_GROUP):
            v = row(GROUP_ROWS + k)
            for g in range(1, n_groups):
                v = jnp.where(g_idx == g, row(GROUP_ROWS + g * EXPERTS_PER_GROUP + k), v)
            e_logit.append(v)
        e_prob = _softmax_cols(e_logit)
        p1, i1 = _first_max(e_prob)
        rest = [jnp.where(i1 == k, -1.0, e_prob[k]) for k in range(EXPERTS_PER_GROUP)]
        p2, i2 = _first_max(rest)
        den = p1 + p2
        base = g_idx * EXPERTS_PER_GROUP
        pieces = [t.astype(F32) for t in (*_split3(g_p * (p1 / den)), *_split3(g_p * (p2 / den)))]
        pieces += [(base + i1).astype(F32), (base + i2).astype(F32)]
        r_io = lax.broadcasted_iota(jnp.int32, (ROUTE_ROWS, TM), 0)
        route_t = jnp.zeros((ROUTE_ROWS, TM), F32)
        for r, piece in enumerate(pieces):
            route_t = jnp.where(r_io == r, piece, route_t)
        route_t = route_t.astype(BF16)

        lo, hi = jnp.minimum(i1, i2), jnp.maximum(i1, i2)
        cls = jnp.zeros((1, TM), jnp.int32)
        for c, (a, b) in enumerate(PAIR_CLASSES):
            cls = jnp.where(lo * EXPERTS_PER_GROUP + hi == a * EXPERTS_PER_GROUP + b, c, cls)
        cat = g_idx * len(PAIR_CLASSES) + cls
        n_cat = n_groups * len(PAIR_CLASSES)
        cat_rows = seg_sm.shape[0]
        onehot_t = jnp.where(lax.broadcasted_iota(jnp.int32, (cat_rows, TM), 0) == cat, 1.0, 0.0)
        upper = jnp.where(lax.broadcasted_iota(jnp.int32, (LANES, LANES), 0)
                          < lax.broadcasted_iota(jnp.int32, (LANES, LANES), 1), 1.0, 0.0).astype(BF16)
        before = jnp.zeros((cat_rows, 1), F32)
        ranks = []
        for b0 in range(0, TM, LANES):
            blk = onehot_t[:, b0:b0 + LANES]
            ranks.append(_dot(blk.astype(BF16), upper) + before)
            before = before + jnp.sum(blk, axis=1, keepdims=True)
        rank = jnp.concatenate(ranks, axis=1)
        start = jnp.zeros((1, 1), F32)
        pos = jnp.zeros((1, TM), F32)
        for c in range(n_cat):
            seg_sm[c] = jnp.sum(start).astype(jnp.int32)
            pos = jnp.where(cat == c, start + rank[c:c + 1, :], pos)
            start = start + before[c:c + 1, :]
        seg_sm[n_cat] = jnp.sum(start).astype(jnp.int32)
        pos = pos.astype(jnp.int32)

        rb = 256
        for r0 in range(0, TM, rb):
            r_io = lax.broadcasted_iota(jnp.int32, (rb, TM), 0) + r0
            p_sc[r0:r0 + rb, :] = jnp.where(r_io == pos, 1.0, 0.0).astype(BF16)
        p = p_sc[...]
        for c0 in range(0, D, 512):
            hs_sc[:, c0:c0 + 512] = _dot(p, h_cols(c0)).astype(BF16)
        routed = _dot_nt(p, route_t)
        gate1 = routed[:, 0:1] + routed[:, 1:2] + routed[:, 2:3]
        gate2 = routed[:, 3:4] + routed[:, 4:5] + routed[:, 5:6]
        lane = lax.broadcasted_iota(jnp.int32, (TM, LANES), 1)
        cs_sc[...] = (jnp.where(lane == routed[:, 6:7].astype(jnp.int32), gate1, 0.0)
                      + jnp.where(lane == routed[:, 7:8].astype(jnp.int32), gate2, 0.0))
        o_ref[...] = jnp.zeros(o_ref.shape, F32)

    g = e // EXPERTS_PER_GROUP
    k = e % EXPERTS_PER_GROUP
    first, last = jnp.int32(CLASS_SPAN[0][0]), jnp.int32(CLASS_SPAN[0][1])
    for kk in range(1, EXPERTS_PER_GROUP):
        first = jnp.where(k == kk, CLASS_SPAN[kk][0], first)
        last = jnp.where(k == kk, CLASS_SPAN[kk][1], last)
    start = seg_sm[g * len(PAIR_CLASSES) + first]
    end = seg_sm[g * len(PAIR_CLASSES) + last + 1]
    c_lo = start // MOE_CHUNK
    c_hi = jnp.where(end > start, (end + MOE_CHUNK - 1) // MOE_CHUNK, c_lo)

    for cp in weight_copies(step):
        cp.wait()
    slot = step % W_SLOTS

    def expert_rows(c, n_chunks):
        m = n_chunks * MOE_CHUNK
        r0 = pl.multiple_of(c * MOE_CHUNK, MOE_CHUNK)
        rows = hs_sc[pl.ds(r0, m), :]
        lane = lax.broadcasted_iota(jnp.int32, (m, LANES), 1)
        w = jnp.sum(jnp.where(lane == e, cs_sc[pl.ds(r0, m), :], 0.0), axis=-1, keepdims=True)
        a = jax.nn.silu(_dot(rows, wg_sc[slot])) * _dot(rows, wu_sc[slot]) * w
        o_ref[pl.ds(r0, m), :] += _dot(a.astype(BF16), wd_sc[slot])

    def full(k, carry):
        expert_rows(c_lo + MOE_MAX_CHUNKS * k, MOE_MAX_CHUNKS)
        return carry

    n_chunks = c_hi - c_lo
    n_full = n_chunks // MOE_MAX_CHUNKS
    lax.fori_loop(0, n_full, full, 0)
    for rest in range(1, MOE_MAX_CHUNKS):
        @pl.when(n_chunks - n_full * MOE_MAX_CHUNKS == rest)
        def _(rest=rest):
            expert_rows(c_hi - rest, rest)

    @pl.when(e == n_exp - 1)
    def _():
        p = p_sc[...]
        for c0 in range(0, D, 512):
            ys = o_ref[:, c0:c0 + 512].astype(BF16)
            y = lax.dot_general(p, ys, (((0,), (0,)), ((), ())), preferred_element_type=F32)
            o_ref[:, c0:c0 + 512] = x_ref[:, c0:c0 + 512] + y


def _moe(x, norm_w, wg, bg, we, be, w_gate, w_up, w_down, layer, tm=1024):
    T, D = x.shape
    n_groups, n_exp = wg.shape[1], we.shape[1]
    ff = w_gate.shape[3]
    assert n_groups <= GROUP_ROWS and n_exp == n_groups * EXPERTS_PER_GROUP and T % tm == 0
    rows = GROUP_ROWS + n_exp
    cat_rows = -(-(n_groups * len(PAIR_CLASSES) + 1) // SUBLANES) * SUBLANES
    wrt = jnp.zeros((rows, D), F32).at[:n_groups].set(wg.T).at[GROUP_ROWS:].set(we.T).astype(BF16)
    brt = jnp.zeros((rows, 1), F32).at[:n_groups, 0].set(bg).at[GROUP_ROWS:, 0].set(be)
    tile = pl.BlockSpec((tm, D), lambda i, e: (i, 0))
    return pl.pallas_call(
        functools.partial(_moe_kernel, n_groups=n_groups, layer=layer),
        grid=(T // tm, n_exp),
        in_specs=[tile, _resident((1, D)), _resident((rows, D)), _resident((rows, 1)),
                  pl.BlockSpec(memory_space=pl.ANY), pl.BlockSpec(memory_space=pl.ANY),
                  pl.BlockSpec(memory_space=pl.ANY)],
        out_specs=tile,
        out_shape=jax.ShapeDtypeStruct((T, D), F32),
        scratch_shapes=[pltpu.VMEM((tm, D), BF16),
                        pltpu.VMEM((tm, tm), BF16),
                        pltpu.VMEM((tm, LANES), F32),
                        pltpu.SMEM((cat_rows,), jnp.int32),
                        pltpu.VMEM((W_SLOTS, D, ff), BF16),
                        pltpu.VMEM((W_SLOTS, D, ff), BF16),
                        pltpu.VMEM((W_SLOTS, ff, D), BF16),
                        pltpu.SemaphoreType.DMA((3, W_SLOTS))],
        compiler_params=pltpu.CompilerParams(dimension_semantics=("arbitrary", "arbitrary"),
                                             vmem_limit_bytes=MOE_VMEM_LIMIT),
    )(x, norm_w.reshape(1, D), wrt, brt, w_gate, w_up, w_down)


POOL_HALO = 16


def _pool_kernel(x_ref, nw_ref, pw_ref, ps_ref, o_ref, hb, pa, pb):
    TS, D = x_ref.shape
    n_groups, pc, _ = pw_ref.shape
    i = pl.program_id(1)
    nw = nw_ref[...]
    top = SUBLANES + POOL_HALO
    n_rows = top + TS

    @pl.when(jnp.logical_and(pl.program_id(0) == 0, i == 0))
    def _():
        hb[...] = jnp.zeros(hb.shape, F32)
        pa[0:SUBLANES] = jnp.zeros((SUBLANES, pc), F32)
        pb[0:SUBLANES] = jnp.zeros((SUBLANES, pc), F32)

    hb[SUBLANES:top] = jnp.where(i > 0, hb[n_rows - POOL_HALO:n_rows], 0.0)
    hb[top:] = _rms(x_ref[...], nw)
    t = i * TS + lax.broadcasted_iota(jnp.int32, (TS, 1), 0)
    for g, w in enumerate(POOL_WINDOWS):
        cs = slice(g * pc, (g + 1) * pc)
        cur = hb[top:, cs]
        src, cols, step, dst = hb, cs, 1, pa
        while step < w:
            dst[SUBLANES:n_rows, :] = src[SUBLANES:n_rows, cols] + src[SUBLANES - step:n_rows - step, cols]
            src, cols, step, dst = dst, slice(None), 2 * step, (pb if dst is pa else pa)
        s = src[top:n_rows, cols]
        count = jnp.minimum(t + 1, w).astype(F32)
        d = s / count - cur
        mixed = _dot(d.astype(BF16), pw_ref[g])
        o_ref[:, cs] = x_ref[:, cs] + ps_ref[:, cs] * mixed


def _pool_layer(x, norm_w, pool_w, pool_scale, B, S, TS=256):
    T, D = x.shape
    assert len(POOL_WINDOWS) == pool_w.shape[0] and max(POOL_WINDOWS) - 1 <= POOL_HALO
    ns = S // TS
    pw = pool_w.astype(BF16)
    return pl.pallas_call(
        _pool_kernel,
        grid=(B, ns),
        in_specs=[pl.BlockSpec((TS, D), lambda b, i: (b * ns + i, 0)),
                  _resident((1, D)), _resident(pw.shape), _resident((1, D))],
        out_specs=pl.BlockSpec((TS, D), lambda b, i: (b * ns + i, 0)),
        out_shape=jax.ShapeDtypeStruct((T, D), F32),
        scratch_shapes=[pltpu.VMEM((SUBLANES + POOL_HALO + TS, D), F32),
                        pltpu.VMEM((SUBLANES + POOL_HALO + TS, D // len(POOL_WINDOWS)), F32),
                        pltpu.VMEM((SUBLANES + POOL_HALO + TS, D // len(POOL_WINDOWS)), F32)],
        compiler_params=_params("arbitrary", "arbitrary"),
    )(x, norm_w.reshape(1, D), pw, pool_scale.reshape(1, D))


def _chunk(S):
    return 256 if S % 256 == 0 else 128


def kernel(x, rel_bias, mix_norm_e, w_in_e, q_norm_e, k_norm_e, conv_w_e, conv_b_e, conv_ln_g_e, conv_ln_b_e,
           w_out_e, mix_norm_o, pool_w_o, pool_scale_o, ffn_norm, router_group_w, router_group_b,
           router_expert_w, router_expert_b, w_gate, w_up, w_down):
    B, S, D = x.shape
    T = B * S
    depth = ffn_norm.shape[0]
    n_heads = rel_bias.shape[1]
    idx_heads = (w_in_e.shape[2] - n_heads * HEAD_DIM - 2 * HEAD_DIM - IDX_DIM - 2 * conv_w_e.shape[2]) \
        // (IDX_DIM + 1)
    k_top = min(INDEX_TOPK, S // 4)
    C = _chunk(S)
    xf = x.reshape(T, D)
    bias_tiles, bias_stat = _rel_bias_tiles(rel_bias, C)
    expert_w = None
    for l in range(depth):
        i = l // 2
        if l % 2 == 0:
            q, k, v, iq, ik, iw, u = _in_proj(xf, mix_norm_e[i], w_in_e[i], q_norm_e[i], k_norm_e[i],
                                              n_heads, idx_heads, conv_w_e.shape[2])
            cast = (w_gate, w_up, w_down) if expert_w is None else ()
            if any(_cast_slabs(w, B * (S // C)) is None for w in cast):
                cast = ()
            attn, done = _attention(q, k, v, iq, ik, iw, bias_tiles, bias_stat, B, S, k_top, C, cast)
            expert_w = done if cast else expert_w
            conv = _conv_module(u, conv_w_e[i], conv_b_e[i], conv_ln_g_e[i], conv_ln_b_e[i], B, S)
            xf = _out_proj(xf, attn, conv, w_out_e[i])
        else:
            xf = _pool_layer(xf, mix_norm_o[i], pool_w_o[i], pool_scale_o[i], B, S)
        if expert_w is None:
            expert_w = [w.astype(BF16) for w in (w_gate, w_up, w_down)]
        xf = _moe(xf, ffn_norm[l], router_group_w[l], router_group_b[l], router_expert_w[l], router_expert_b[l],
                  *expert_w, layer=l)
    return xf.reshape(B, S, D)
```

```python
import functools
import math

import jax
import jax.numpy as jnp
from jax import lax
from jax.experimental import pallas as pl
from jax.experimental.pallas import tpu as pltpu

F32 = jnp.float32
BF16 = jnp.bfloat16

NORM_EPS = 1e-6
HEAD_DIM = 128
IDX_DIM = 64
INDEX_TOPK = 256
REL_BUCKETS = 32
REL_MAX_DIST = 128
POOL_WINDOWS = (2, 4, 8, 16)
EXPERTS_PER_GROUP = 4
LANES = 128
VMEM_LIMIT = 56 * 1024 * 1024
NEG = -1e30
INT_MIN = -(2 ** 31)


def _dot(a, b):
    return jnp.dot(a, b, preferred_element_type=F32)


def _dot_nt(a, b):
    return lax.dot_general(a, b, (((1,), (1,)), ((), ())), preferred_element_type=F32)


def _rms(x, w):
    return x * lax.rsqrt(jnp.mean(x * x, axis=-1, keepdims=True) + NORM_EPS) * w


def _params(*sem):
    return pltpu.CompilerParams(dimension_semantics=sem, vmem_limit_bytes=VMEM_LIMIT)


def _resident(shape):
    nd = len(shape)
    return pl.BlockSpec(shape, lambda *_: (0,) * nd, pipeline_mode=pl.Buffered(1))


def _in_proj_kernel(x_ref, nw_ref, qn_ref, kn_ref, wq_ref, wkv_ref, wiq_ref, wikw_ref, wa_ref, wg_ref,
                    q_ref, k_ref, v_ref, iq_ref, ik_ref, iw_ref, u_ref, *, q_scale, iw_scale):
    h = _rms(x_ref[...], nw_ref[...]).astype(BF16)
    n_pairs = q_ref.shape[0] // 2
    qn = qn_ref[...] * q_scale
    for c in range(n_pairs):
        qq = _dot(h, wq_ref[:, c * 256:(c + 1) * 256])
        for s in range(2):
            qh = qq[:, s * HEAD_DIM:(s + 1) * HEAD_DIM]
            q_ref[2 * c + s] = _rms(qh, qn).astype(BF16)
    kv = _dot(h, wkv_ref[...])
    k_ref[...] = _rms(kv[:, :HEAD_DIM], kn_ref[...]).astype(BF16)
    v_ref[...] = kv[:, HEAD_DIM:].astype(BF16)
    for c in range(iq_ref.shape[0] // 2):
        r = _dot(h, wiq_ref[:, c * 256:(c + 1) * 256])
        iq_ref[2 * c] = r[:, :LANES].astype(BF16)
        iq_ref[2 * c + 1] = r[:, LANES:].astype(BF16)
    r = _dot(h, wikw_ref[...])
    ik_ref[...] = r[:, :LANES].astype(BF16)
    iw_ref[...] = r[:, LANES:] * iw_scale
    for c in range(u_ref.shape[1] // 256):
        cs = slice(c * 256, (c + 1) * 256)
        a = _dot(h, wa_ref[:, cs])
        g = _dot(h, wg_ref[:, cs])
        u_ref[:, cs] = (a * jax.nn.sigmoid(g)).astype(BF16)


def _split_w_in_kernel(w_ref, wq_ref, wkv_ref, wiq_ref, wikw_ref, wa_ref, wg_ref, *, idx_heads):
    o = 0
    for ref in (wq_ref, wkv_ref, wiq_ref):
        n = ref.shape[1]
        ref[...] = w_ref[:, o:o + n].astype(BF16)
        o += n
    wik = w_ref[:, o:o + IDX_DIM]
    wiw = w_ref[:, o + IDX_DIM:o + IDX_DIM + idx_heads]
    o += IDX_DIM + idx_heads
    pad = jnp.zeros((wik.shape[0], LANES - idx_heads), F32)
    wikw_ref[...] = jnp.concatenate([wik, wik, wiw, pad], axis=1).astype(BF16)
    for ref in (wa_ref, wg_ref):
        n = ref.shape[1]
        ref[...] = w_ref[:, o:o + n].astype(BF16)
        o += n


def _split_w_in(w_in, n_heads, idx_heads, conv_ch, tk=256):
    D, n_in = w_in.shape
    widths = (n_heads * HEAD_DIM, 2 * HEAD_DIM, idx_heads * IDX_DIM, 2 * LANES, conv_ch, conv_ch)
    assert n_in == sum(widths) - 2 * LANES + IDX_DIM + idx_heads and 2 * IDX_DIM == LANES and D % tk == 0
    return pl.pallas_call(
        functools.partial(_split_w_in_kernel, idx_heads=idx_heads),
        grid=(D // tk,),
        in_specs=[pl.BlockSpec((tk, n_in), lambda i: (i, 0))],
        out_specs=[pl.BlockSpec((tk, w), lambda i: (i, 0)) for w in widths],
        out_shape=[jax.ShapeDtypeStruct((D, w), BF16) for w in widths],
        compiler_params=_params("arbitrary"),
    )(w_in)


def _in_proj(x, norm_w, w_in, q_norm, k_norm, n_heads, idx_heads, conv_ch, tm=512):
    T, D = x.shape
    iq_w = idx_heads * IDX_DIM
    ws = _split_w_in(w_in, n_heads, idx_heads, conv_ch)
    row = lambda w: pl.BlockSpec((tm, w), lambda i: (i, 0))
    heads = lambda n: pl.BlockSpec((n, tm, LANES), lambda i: (0, i, 0))
    kern = functools.partial(_in_proj_kernel, q_scale=HEAD_DIM ** -0.5,
                             iw_scale=(idx_heads ** -0.5) * (IDX_DIM ** -0.5))
    return pl.pallas_call(
        kern,
        grid=(T // tm,),
        in_specs=[row(D), _resident((1, D)), _resident((1, HEAD_DIM)), _resident((1, HEAD_DIM))]
                 + [_resident(w.shape) for w in ws],
        out_specs=[heads(n_heads), row(HEAD_DIM), row(HEAD_DIM), heads(iq_w // LANES), row(LANES), row(LANES),
                   row(conv_ch)],
        out_shape=[jax.ShapeDtypeStruct((n_heads, T, HEAD_DIM), BF16),
                   jax.ShapeDtypeStruct((T, HEAD_DIM), BF16),
                   jax.ShapeDtypeStruct((T, HEAD_DIM), BF16),
                   jax.ShapeDtypeStruct((iq_w // LANES, T, LANES), BF16),
                   jax.ShapeDtypeStruct((T, LANES), BF16),
                   jax.ShapeDtypeStruct((T, LANES), F32),
                   jax.ShapeDtypeStruct((T, conv_ch), BF16)],
        compiler_params=_params("arbitrary"),
    )(x, norm_w.reshape(1, D), q_norm.reshape(1, HEAD_DIM), k_norm.reshape(1, HEAD_DIM), *ws)


def _rel_bias_kernel(rb_ref, o_ref, stat_ref):
    _, n_heads, C, _ = o_ref.shape
    tau = lax.broadcasted_iota(jnp.int32, (C, C), 0)
    sig = lax.broadcasted_iota(jnp.int32, (C, C), 1)
    max_exact = REL_BUCKETS // 2
    for kind in range(2):
        d = tau - sig + kind * C
        n = jnp.maximum(d, 0)
        nf = jnp.maximum(n, 1).astype(F32)
        large = max_exact + (jnp.log(nf / max_exact) / math.log(REL_MAX_DIST / max_exact)
                             * (REL_BUCKETS - max_exact)).astype(jnp.int32)
        large = jnp.minimum(large, REL_BUCKETS - 1)
        bucket = jnp.where(n < max_exact, n, large)
        for h in range(n_heads):
            b = jnp.zeros((C, C), F32)
            for bk in range(REL_BUCKETS):
                b = jnp.where(bucket == bk, rb_ref[bk, h], b)
            b = b - rb_ref[REL_BUCKETS - 1, h]
            if kind == 0:
                b = jnp.where(d < 0, NEG, b)
            o_ref[kind, h] = b
    for h in range(n_heads):
        hi = rb_ref[0, h]
        lo = rb_ref[0, h]
        for bk in range(1, REL_BUCKETS):
            hi = jnp.maximum(hi, rb_ref[bk, h])
            lo = jnp.minimum(lo, rb_ref[bk, h])
        stat_ref[0, h] = hi - rb_ref[REL_BUCKETS - 1, h]
        stat_ref[1, h] = lo - rb_ref[REL_BUCKETS - 1, h]


def _rel_bias_tiles(rel_bias, C):
    n_heads = rel_bias.shape[1]
    assert C >= REL_MAX_DIST
    return pl.pallas_call(
        _rel_bias_kernel,
        in_specs=[pl.BlockSpec(memory_space=pltpu.SMEM)],
        out_specs=[pl.BlockSpec(memory_space=pltpu.VMEM), pl.BlockSpec(memory_space=pltpu.SMEM)],
        out_shape=[jax.ShapeDtypeStruct((2, n_heads, C, C), F32), jax.ShapeDtypeStruct((2, n_heads), F32)],
        compiler_params=pltpu.CompilerParams(vmem_limit_bytes=VMEM_LIMIT),
    )(rel_bias)


SHIFT_SPAN_LIMIT = 60.0


TILE_GROUP = 8


def _grouped_loop(n, fn, group):
    def body(k, carry):
        for u in range(group):
            fn(group * k + u)
        return carry

    lax.fori_loop(0, n // group, body, 0)
    base = (n // group) * group
    size = group // 2
    while size >= 1:
        take = ((n - base) & size) != 0

        @pl.when(take)
        def _(base=base, size=size):
            for u in range(size):
                fn(base + u)
        base = base + jnp.where(take, size, 0)
        size //= 2


def _order_key(x):
    bits = pltpu.bitcast(x, jnp.int32)
    return bits ^ ((bits >> 31) & 0x7FFFFFFF)


def _row_to_col(row):
    C = row.shape[1]
    halves = []
    for part in (row >> 16, row & 0xFFFF):
        halves.append(jnp.broadcast_to(part.astype(F32), (LANES, C)).T[:, 0:1].astype(jnp.int32))
    return (halves[0] << 16) | halves[1]


def _attn_kernel(bstat_ref, q_ref, k_ref, v_ref, iq_ref, ik_ref, iw_ref, bias_ref, *rest, k_top, n_cast):
    cast_in, (o_ref, *cast_out) = rest[:n_cast], rest[n_cast:2 * n_cast + 1]
    key_sc, keyt_sc, keyt16_sc, iqm_sc, vx_sc, kmax_sc, shift_sc, acc_sc = rest[2 * n_cast + 1:]
    for src, dst in zip(cast_in, cast_out):
        dst[...] = src[...].astype(BF16)
    n_heads, C, _ = q_ref.shape
    idx_heads = iqm_sc.shape[0]
    idx_bits = max(1, (key_sc.shape[0] * C - 1).bit_length())
    qi = pl.program_id(1)
    nkv = qi + 1

    @pl.when(qi == 0)
    def _():
        vx_sc[:, :HEAD_DIM] = v_ref[...]
        vx_sc[:, HEAD_DIM:] = jnp.ones((vx_sc.shape[0], HEAD_DIM), BF16)
        kf = k_ref[...].astype(F32)
        k2 = jnp.sum(kf * kf, axis=-1, keepdims=True)
        kmax_sc[...] = jnp.broadcast_to(jnp.sqrt(jnp.max(k2, axis=0, keepdims=True)), kmax_sc.shape)

    lane = lax.broadcasted_iota(jnp.int32, (C, LANES), 1)
    for p in range(idx_heads // 2):
        qp = iq_ref[p].astype(F32)
        iqm_sc[2 * p] = jnp.where(lane < IDX_DIM, qp, 0.0).astype(BF16)
        iqm_sc[2 * p + 1] = jnp.where(lane >= IDX_DIM, qp, 0.0).astype(BF16)
    iw = iw_ref[...]
    tau = lax.broadcasted_iota(jnp.int32, (C, C), 0)
    sig = lax.broadcasted_iota(jnp.int32, (C, C), 1)

    def score_tile(j):
        off = pl.multiple_of(j * C, C)
        ikc = ik_ref[pl.ds(off, C), :]
        acc = jnp.zeros((C, C), F32)
        for hh in range(idx_heads):
            s = _dot_nt(iqm_sc[hh], ikc)
            acc = acc + jnp.maximum(s, 0.0) * iw[:, hh:hh + 1]
        acc = jnp.where(jnp.logical_and(j == qi, sig > tau), -jnp.inf, acc)
        key_sc[j] = _order_key(acc)
        kt = _order_key(acc.T)
        keyt_sc[j] = kt
        keyt16_sc[j] = (kt >> 16).astype(jnp.int16)

    _grouped_loop(nkv, score_tile, TILE_GROUP)

    SUB = 32

    def count(hit):
        def body(j, cnt):
            for r in range(C // SUB):
                cnt = cnt + hit(j, r)
            return cnt
        cnt = lax.fori_loop(0, nkv, body, jnp.zeros((SUB, C), F32))
        return jnp.sum(cnt, axis=0, keepdims=True)

    def keyt(j, r):
        return keyt_sc[j, pl.ds(r * SUB, SUB), :]

    def count16(cand16):
        one = jnp.ones((SUB, C), jnp.int16)
        zero = jnp.zeros((SUB, C), jnp.int16)

        def body(j, cnt):
            for r in range(C // SUB):
                cnt = cnt + jnp.where(keyt16_sc[j, pl.ds(r * SUB, SUB), :] >= cand16, one, zero)
            return cnt
        cnt = lax.fori_loop(0, nkv, body, zero)
        return jnp.sum(cnt.astype(jnp.int32).astype(F32), axis=0, keepdims=True)

    def bit16_body(b, carry):
        res, n_res = carry
        cand = res ^ lax.shift_left(jnp.int32(1), 31 - b)
        tot = count16((cand >> 16).astype(jnp.int16))
        ok = tot >= k_top
        return jnp.where(ok, cand, res), jnp.where(ok, tot, n_res)

    searched = nkv * C > k_top
    carry = (jnp.full((1, C), INT_MIN, jnp.int32), jnp.zeros((1, C), F32))
    top_row, n_top = lax.fori_loop(0, jnp.where(searched, 16, 0), bit16_body, carry)

    top16 = top_row >> 16

    def low_body(j, c):
        kt = keyt_sc[j]
        hi = kt >> 16
        lo = (kt & 0xFFFF) - 32768
        lo = jnp.where(hi == top16, lo, jnp.where(hi > top16, 32767, -32768))
        keyt16_sc[j] = lo.astype(jnp.int16)
        return c

    lax.fori_loop(0, jnp.where(searched, nkv, 0), low_body, 0)

    def bit16_low_body(b, carry):
        res, n_res = carry
        cand = res ^ lax.shift_left(jnp.int32(1), 31 - b)
        tot = count16(((cand & 0xFFFF) - 32768).astype(jnp.int16))
        ok = tot >= k_top
        return jnp.where(ok, cand, res), jnp.where(ok, tot, n_res)

    thr_row, n_thr = lax.fori_loop(16, jnp.where(searched, 32, 16), bit16_low_body, (top_row, n_top))
    thr = _row_to_col(thr_row)

    @pl.when(jnp.max(n_thr) > k_top)
    def _():
        need = k_top - count(lambda j, r: jnp.where(keyt(j, r) > thr_row, 1.0, 0.0))
        s_sub = lax.broadcasted_iota(jnp.int32, (SUB, C), 0)

        def idx_body(b, last):
            cand = last | lax.shift_left(jnp.int32(1), idx_bits - 1 - b)
            below = count(lambda j, r: jnp.where(
                keyt(j, r) == thr_row, jnp.where(j * C + r * SUB + s_sub < cand, 1.0, 0.0), 0.0))
            return jnp.where(below < need, cand, last)

        last = _row_to_col(lax.fori_loop(0, idx_bits, idx_body, jnp.zeros((1, C), jnp.int32)))

        def drop_body(j, carry):
            kk = key_sc[j]
            key_sc[j] = jnp.where(kk == thr, jnp.where(j * C + sig > last, kk - 1, kk), kk)
            return carry

        lax.fori_loop(0, nkv, drop_body, 0)

    kmax = kmax_sc[0:1, 0:1] * 1.001
    worst = jnp.zeros((C, 1), F32)
    for h in range(n_heads):
        qf = q_ref[h].astype(F32)
        bound = jnp.sqrt(jnp.sum(qf * qf, axis=-1, keepdims=True)) * kmax
        shift_sc[h] = jnp.broadcast_to(bound + bstat_ref[0, h], (C, LANES))
        worst = jnp.maximum(worst, 2.0 * bound + (bstat_ref[0, h] - bstat_ref[1, h]))
    loose = jnp.max(worst) > SHIFT_SPAN_LIMIT

    def logits(j, h, kind):
        off = pl.multiple_of(j * C, C)
        lg = _dot_nt(q_ref[h], k_ref[pl.ds(off, C), :])
        return lg if kind is None else lg + bias_ref[kind, h]

    def near_tiles(fn):
        @pl.when(qi > 0)
        def _():
            fn(qi - 1, 1)
            fn(qi, 0)

        @pl.when(qi == 0)
        def _():
            fn(qi, 0)

    def far_tiles(fn, group=TILE_GROUP):
        _grouped_loop(jnp.maximum(qi - 1, 0), lambda j: fn(j, None), group)

    @pl.when(loose)
    def _():
        for h in range(n_heads):
            shift_sc[h] = jnp.full((C, LANES), NEG, F32)

        def max_tile(j, kind):
            sel = key_sc[j] >= thr
            for h in range(n_heads):
                lg = jnp.where(sel, logits(j, h, kind), NEG)
                m = jnp.max(lg, axis=-1, keepdims=True)
                shift_sc[h] = jnp.maximum(shift_sc[h], jnp.broadcast_to(m, (C, LANES)))

        far_tiles(max_tile, group=1)
        near_tiles(max_tile)

    acc_sc[...] = jnp.zeros(acc_sc.shape, F32)

    def attn_tile(j, kind):
        off = pl.multiple_of(j * C, C)
        vx = vx_sc[pl.ds(off, C), :]
        sel = key_sc[j] >= thr
        for h in range(n_heads):
            sh = jnp.concatenate([shift_sc[h]] * (C // LANES), axis=1)
            p = jnp.where(sel, jnp.exp(logits(j, h, kind) - sh), 0.0)
            acc_sc[h] += _dot(p.astype(BF16), vx)

    far_tiles(attn_tile)
    near_tiles(attn_tile)
    for h in range(n_heads):
        a = acc_sc[h]
        o_ref[:, h * HEAD_DIM:(h + 1) * HEAD_DIM] = (a[:, :HEAD_DIM] / a[:, HEAD_DIM:]).astype(BF16)


CAST_SLAB_ELEMS = 1 << 20


def _cast_slabs(w, steps):
    rows = math.prod(w.shape[:-1])
    if rows % (steps * 16) != 0 or rows // steps * w.shape[-1] > CAST_SLAB_ELEMS:
        return None
    return w.reshape(steps, rows // steps, w.shape[-1])


def _attention(q, k, v, iq, ik, iw, bias_tiles, bias_stat, B, S, k_top, C, cast=()):
    n_heads, T, _ = q.shape
    n_pairs = iq.shape[0]
    nq = S // C
    slabs = [_cast_slabs(w, B * nq) for w in cast]
    kern = functools.partial(_attn_kernel, k_top=k_top, n_cast=len(slabs))
    heads = lambda n: pl.BlockSpec((n, C, LANES), lambda b, i: (0, b * nq + i, 0))
    seq = pl.BlockSpec((S, LANES), lambda b, i: (b, 0))
    slab_specs = [pl.BlockSpec((None,) + w.shape[1:], lambda b, i: (b * nq + i, 0, 0)) for w in slabs]
    outs = pl.pallas_call(
        kern,
        grid=(B, nq),
        in_specs=[pl.BlockSpec(memory_space=pltpu.SMEM),
                  heads(n_heads), seq, seq, heads(n_pairs), seq,
                  pl.BlockSpec((C, LANES), lambda b, i: (b * nq + i, 0)),
                  _resident(bias_tiles.shape)] + slab_specs,
        out_specs=[pl.BlockSpec((C, n_heads * HEAD_DIM), lambda b, i: (b * nq + i, 0))] + slab_specs,
        out_shape=[jax.ShapeDtypeStruct((T, n_heads * HEAD_DIM), BF16)]
                  + [jax.ShapeDtypeStruct(w.shape, BF16) for w in slabs],
        scratch_shapes=[pltpu.VMEM((nq, C, C), jnp.int32),
                        pltpu.VMEM((nq, C, C), jnp.int32),
                        pltpu.VMEM((nq, C, C), jnp.int16),
                        pltpu.VMEM((2 * n_pairs, C, LANES), BF16),
                        pltpu.VMEM((S, 2 * HEAD_DIM), BF16),
                        pltpu.VMEM((8, LANES), F32),
                        pltpu.VMEM((n_heads, C, LANES), F32),
                        pltpu.VMEM((n_heads, C, 2 * HEAD_DIM), F32)],
        compiler_params=_params("arbitrary", "arbitrary"),
    )(bias_stat, q, k, v, iq, ik, iw, bias_tiles, *slabs)
    return outs[0], [o.reshape(w.shape) for o, w in zip(outs[1:], cast)]


CONV_HALO = 32


SUBLANES = 8
CONV_SLAB = 64


def _conv_kernel(u_ref, w_ref, cb_ref, g_ref, b_ref, o_ref, cp_sc, y_sc, halo_sc, *, width):
    TS, CH = u_ref.shape
    n_rows = CONV_HALO + TS
    i = pl.program_id(1)

    @pl.when(jnp.logical_and(pl.program_id(0) == 0, i == 0))
    def _():
        halo_sc[...] = jnp.zeros(halo_sc.shape, F32)

    halo = jnp.where(i > 0, halo_sc[...], 0.0)
    halo_sc[...] = u_ref[TS - CONV_HALO:, :].astype(F32)
    pad = jnp.zeros((SUBLANES, LANES), F32)
    for c in range(CH // LANES):
        cs = slice(c * LANES, (c + 1) * LANES)
        col = jnp.concatenate([halo[:, cs], u_ref[:, cs].astype(F32), pad], axis=0)
        for r in range(SUBLANES):
            cp_sc[r, :, cs] = col[r:r + n_rows]

    base = CONV_HALO - (width - 1)
    phases = {}
    for j in range(width):
        q, r = divmod(base + j, SUBLANES)
        phases.setdefault(r, []).append((q, j))

    def slab(s, carry):
        t0 = pl.multiple_of(s * CONV_SLAB, CONV_SLAB)
        for c in range(CH // LANES):
            cs = slice(c * LANES, (c + 1) * LANES)
            acc = jnp.zeros((CONV_SLAB, LANES), F32)
            for r, taps in phases.items():
                q_lo, q_hi = taps[0][0], taps[-1][0]
                win = cp_sc[r, pl.ds(q_lo * SUBLANES + t0, CONV_SLAB + (q_hi - q_lo) * SUBLANES), cs]
                for q, j in taps:
                    off = (q - q_lo) * SUBLANES
                    acc = acc + w_ref[j:j + 1, cs] * win[off:off + CONV_SLAB]
            y_sc[pl.ds(t0, CONV_SLAB), cs] = acc + cb_ref[:, cs]
        return carry

    lax.fori_loop(0, TS // CONV_SLAB, slab, 0)
    y = y_sc[...]
    mu = jnp.mean(y, axis=-1, keepdims=True)
    yc = y - mu
    var = jnp.mean(yc * yc, axis=-1, keepdims=True)
    yn = yc * lax.rsqrt(var + NORM_EPS) * g_ref[...] + b_ref[...]
    o_ref[...] = (yn * jax.nn.sigmoid(yn)).astype(BF16)


def _conv_module(u, conv_w, conv_b, ln_g, ln_b, B, S, TS=512):
    T, CH = u.shape
    width = conv_w.shape[0]
    assert width - 1 <= CONV_HALO
    ns = S // TS
    wpad = jnp.zeros((CONV_HALO, CH), F32).at[:width].set(conv_w)
    vec = lambda a: a.reshape(1, CH)
    kern = functools.partial(_conv_kernel, width=width)
    return pl.pallas_call(
        kern,
        grid=(B, ns),
        in_specs=[pl.BlockSpec((TS, CH), lambda b, i: (b * ns + i, 0)),
                  _resident((CONV_HALO, CH)), _resident((1, CH)), _resident((1, CH)), _resident((1, CH))],
        out_specs=pl.BlockSpec((TS, CH), lambda b, i: (b * ns + i, 0)),
        out_shape=jax.ShapeDtypeStruct((T, CH), BF16),
        scratch_shapes=[pltpu.VMEM((SUBLANES, CONV_HALO + TS, CH), F32),
                        pltpu.VMEM((TS, CH), F32),
                        pltpu.VMEM((CONV_HALO, CH), F32)],
        compiler_params=_params("arbitrary", "arbitrary"),
    )(u, wpad, vec(conv_b), vec(ln_g), vec(ln_b))


def _out_proj_kernel(x_ref, a_ref, c_ref, wa_ref, wc_ref, o_ref):
    o_ref[...] = x_ref[...] + _dot(a_ref[...], wa_ref[...]) + _dot(c_ref[...], wc_ref[...])


def _out_proj(x, attn, conv, w_out, tm=512):
    T, D = x.shape
    aw, cw = attn.shape[1], conv.shape[1]
    wa = w_out[:aw].astype(BF16)
    wc = w_out[aw:].astype(BF16)
    row = lambda w: pl.BlockSpec((tm, w), lambda i: (i, 0))
    return pl.pallas_call(
        _out_proj_kernel,
        grid=(T // tm,),
        in_specs=[row(D), row(aw), row(cw), _resident(wa.shape), _resident(wc.shape)],
        out_specs=row(D),
        out_shape=jax.ShapeDtypeStruct((T, D), F32),
        compiler_params=_params("arbitrary"),
    )(x, attn, conv, wa, wc)


def _first_max(vals):
    m = vals[0]
    for v in vals[1:]:
        m = jnp.maximum(m, v)
    idx = jnp.full(m.shape, len(vals) - 1, jnp.int32)
    for k in range(len(vals) - 2, -1, -1):
        idx = jnp.where(vals[k] == m, k, idx)
    return m, idx


def _softmax_cols(cols):
    m = cols[0]
    for c in cols[1:]:
        m = jnp.maximum(m, c)
    e = [jnp.exp(c - m) for c in cols]
    s = e[0]
    for c in e[1:]:
        s = s + c
    return [c / s for c in e]


GROUP_ROWS = 8
MOE_CHUNK = 64
MOE_MAX_CHUNKS = 8


def _split3(a):
    hi = a.astype(BF16)
    r = a - hi.astype(F32)
    mid = r.astype(BF16)
    lo = (r - mid.astype(F32)).astype(BF16)
    return hi, mid, lo


PAIR_CLASSES = [(a, b) for a in range(EXPERTS_PER_GROUP) for b in range(a + 1, EXPERTS_PER_GROUP)]
CLASS_SPAN = [(min(c for c, p in enumerate(PAIR_CLASSES) if k in p), max(c for c, p in enumerate(PAIR_CLASSES) if k in p))
              for k in range(EXPERTS_PER_GROUP)]


W_SLOTS = 3
MOE_VMEM_LIMIT = 62 * 1024 * 1024


def _moe_kernel(x_ref, nw_ref, wrt_ref, brt_ref, wg_hbm, wu_hbm, wd_hbm, o_ref,
                hs_sc, p_sc, cs_sc, seg_sm, wg_sc, wu_sc, wd_sc, w_sem, *, n_groups, layer):
    TM, D = x_ref.shape
    e = pl.program_id(1)
    n_exp = pl.num_programs(1)
    step = pl.program_id(0) * n_exp + e
    n_steps = pl.num_programs(0) * n_exp

    def weight_copies(s):
        slot, expert = s % W_SLOTS, s % n_exp
        return [pltpu.make_async_copy(src.at[layer, expert], dst.at[slot], w_sem.at[k, slot])
                for k, (src, dst) in enumerate(((wg_hbm, wg_sc), (wu_hbm, wu_sc), (wd_hbm, wd_sc)))]

    @pl.when(step == 0)
    def _():
        for s in range(W_SLOTS - 1):
            for cp in weight_copies(s):
                cp.start()

    @pl.when(step + W_SLOTS - 1 < n_steps)
    def _():
        for cp in weight_copies(step + W_SLOTS - 1):
            cp.start()

    @pl.when(e == 0)
    def _():
        x = x_ref[...]
        scale = lax.rsqrt(jnp.mean(x * x, axis=-1, keepdims=True) + NORM_EPS)

        def h_cols(c0):
            cs = slice(c0, c0 + 512)
            return (x_ref[:, cs] * scale * nw_ref[:, cs]).astype(BF16)

        lg = brt_ref[...] + jnp.zeros((1, TM), F32)
        for c0 in range(0, D, 512):
            lg = lg + _dot_nt(wrt_ref[:, c0:c0 + 512], h_cols(c0))
        row = lambda k: lg[k:k + 1, :]
        g_prob = _softmax_cols([row(g) for g in range(n_groups)])
        g_p, g_idx = _first_max(g_prob)
        e_logit = []
        for k in range(EXPERTS_PER_GROUP):
            v = row(GROUP_ROWS + k)
            for g in range(1, n_groups):
                v = jnp.where(g_idx == g, row(GROUP_ROWS + g * EXPERTS_PER_GROUP + k), v)
            e_logit.append(v)
        e_prob = _softmax_cols(e_logit)
        p1, i1 = _first_max(e_prob)
        rest = [jnp.where(i1 == k, -1.0, e_prob[k]) for k in range(EXPERTS_PER_GROUP)]
        p2, i2 = _first_max(rest)
        den = p1 + p2
        base = g_idx * EXPERTS_PER_GROUP
        e_io = lax.broadcasted_iota(jnp.int32, (LANES, TM), 0)
        comb_t = (jnp.where(e_io == base + i1, g_p * (p1 / den), 0.0)
                  + jnp.where(e_io == base + i2, g_p * (p2 / den), 0.0))

        lo, hi = jnp.minimum(i1, i2), jnp.maximum(i1, i2)
        cls = jnp.zeros((1, TM), jnp.int32)
        for c, (a, b) in enumerate(PAIR_CLASSES):
            cls = jnp.where(lo * EXPERTS_PER_GROUP + hi == a * EXPERTS_PER_GROUP + b, c, cls)
        cat = g_idx * len(PAIR_CLASSES) + cls
        n_cat = n_groups * len(PAIR_CLASSES)
        cat_rows = seg_sm.shape[0]
        onehot_t = jnp.where(lax.broadcasted_iota(jnp.int32, (cat_rows, TM), 0) == cat, 1.0, 0.0)
        upper = jnp.where(lax.broadcasted_iota(jnp.int32, (LANES, LANES), 0)
                          < lax.broadcasted_iota(jnp.int32, (LANES, LANES), 1), 1.0, 0.0).astype(BF16)
        before = jnp.zeros((cat_rows, 1), F32)
        ranks = []
        for b0 in range(0, TM, LANES):
            blk = onehot_t[:, b0:b0 + LANES]
            ranks.append(_dot(blk.astype(BF16), upper) + before)
            before = before + jnp.sum(blk, axis=1, keepdims=True)
        rank = jnp.concatenate(ranks, axis=1)
        start = jnp.int32(0)
        pos = jnp.zeros((1, TM), F32)
        for c in range(n_cat):
            seg_sm[c] = start
            pos = jnp.where(cat == c, start.astype(F32) + rank[c:c + 1, :], pos)
            start = start + jnp.sum(onehot_t[c:c + 1, :]).astype(jnp.int32)
        seg_sm[n_cat] = start
        pos = pos.astype(jnp.int32)

        rb = 256
        for r0 in range(0, TM, rb):
            r_io = lax.broadcasted_iota(jnp.int32, (rb, TM), 0) + r0
            p_sc[r0:r0 + rb, :] = jnp.where(r_io == pos, 1.0, 0.0).astype(BF16)
        p = p_sc[...]
        for c0 in range(0, D, 512):
            hs_sc[:, c0:c0 + 512] = _dot(p, h_cols(c0)).astype(BF16)
        cs = jnp.zeros((TM, LANES), F32)
        for part in _split3(comb_t):
            cs = cs + _dot_nt(p, part)
        cs_sc[...] = cs
        o_ref[...] = jnp.zeros(o_ref.shape, F32)

    g = e // EXPERTS_PER_GROUP
    k = e % EXPERTS_PER_GROUP
    first, last = jnp.int32(CLASS_SPAN[0][0]), jnp.int32(CLASS_SPAN[0][1])
    for kk in range(1, EXPERTS_PER_GROUP):
        first = jnp.where(k == kk, CLASS_SPAN[kk][0], first)
        last = jnp.where(k == kk, CLASS_SPAN[kk][1], last)
    start = seg_sm[g * len(PAIR_CLASSES) + first]
    end = seg_sm[g * len(PAIR_CLASSES) + last + 1]
    c_lo = start // MOE_CHUNK
    c_hi = jnp.where(end > start, (end + MOE_CHUNK - 1) // MOE_CHUNK, c_lo)

    for cp in weight_copies(step):
        cp.wait()
    slot = step % W_SLOTS

    def expert_rows(c, n_chunks):
        m = n_chunks * MOE_CHUNK
        r0 = pl.multiple_of(c * MOE_CHUNK, MOE_CHUNK)
        rows = hs_sc[pl.ds(r0, m), :]
        lane = lax.broadcasted_iota(jnp.int32, (m, LANES), 1)
        w = jnp.sum(jnp.where(lane == e, cs_sc[pl.ds(r0, m), :], 0.0), axis=-1, keepdims=True)
        a = jax.nn.silu(_dot(rows, wg_sc[slot])) * _dot(rows, wu_sc[slot]) * w
        o_ref[pl.ds(r0, m), :] += _dot(a.astype(BF16), wd_sc[slot])

    def full(k, carry):
        expert_rows(c_lo + MOE_MAX_CHUNKS * k, MOE_MAX_CHUNKS)
        return carry

    n_chunks = c_hi - c_lo
    n_full = n_chunks // MOE_MAX_CHUNKS
    lax.fori_loop(0, n_full, full, 0)
    for rest in range(1, MOE_MAX_CHUNKS):
        @pl.when(n_chunks - n_full * MOE_MAX_CHUNKS == rest)
        def _(rest=rest):
            expert_rows(c_hi - rest, rest)

    @pl.when(e == n_exp - 1)
    def _():
        p = p_sc[...]
        for c0 in range(0, D, 512):
            ys = o_ref[:, c0:c0 + 512].astype(BF16)
            y = lax.dot_general(p, ys, (((0,), (0,)), ((), ())), preferred_element_type=F32)
            o_ref[:, c0:c0 + 512] = x_ref[:, c0:c0 + 512] + y


def _moe(x, norm_w, wg, bg, we, be, w_gate, w_up, w_down, layer, tm=1024):
    T, D = x.shape
    n_groups, n_exp = wg.shape[1], we.shape[1]
    ff = w_gate.shape[3]
    assert n_groups <= GROUP_ROWS and n_exp == n_groups * EXPERTS_PER_GROUP and T % tm == 0
    rows = GROUP_ROWS + n_exp
    cat_rows = -(-(n_groups * len(PAIR_CLASSES) + 1) // SUBLANES) * SUBLANES
    wrt = jnp.zeros((rows, D), F32).at[:n_groups].set(wg.T).at[GROUP_ROWS:].set(we.T).astype(BF16)
    brt = jnp.zeros((rows, 1), F32).at[:n_groups, 0].set(bg).at[GROUP_ROWS:, 0].set(be)
    tile = pl.BlockSpec((tm, D), lambda i, e: (i, 0))
    return pl.pallas_call(
        functools.partial(_moe_kernel, n_groups=n_groups, layer=layer),
        grid=(T // tm, n_exp),
        in_specs=[tile, _resident((1, D)), _resident((rows, D)), _resident((rows, 1)),
                  pl.BlockSpec(memory_space=pl.ANY), pl.BlockSpec(memory_space=pl.ANY),
                  pl.BlockSpec(memory_space=pl.ANY)],
        out_specs=tile,
        out_shape=jax.ShapeDtypeStruct((T, D), F32),
        scratch_shapes=[pltpu.VMEM((tm, D), BF16),
                        pltpu.VMEM((tm, tm), BF16),
                        pltpu.VMEM((tm, LANES), F32),
                        pltpu.SMEM((cat_rows,), jnp.int32),
                        pltpu.VMEM((W_SLOTS, D, ff), BF16),
                        pltpu.VMEM((W_SLOTS, D, ff), BF16),
                        pltpu.VMEM((W_SLOTS, ff, D), BF16),
                        pltpu.SemaphoreType.DMA((3, W_SLOTS))],
        compiler_params=pltpu.CompilerParams(dimension_semantics=("arbitrary", "arbitrary"),
                                             vmem_limit_bytes=MOE_VMEM_LIMIT),
    )(x, norm_w.reshape(1, D), wrt, brt, w_gate, w_up, w_down)


POOL_HALO = 16


def _pool_kernel(x_ref, nw_ref, pw_ref, ps_ref, o_ref, hb, pa, pb):
    TS, D = x_ref.shape
    n_groups, pc, _ = pw_ref.shape
    i = pl.program_id(1)
    nw = nw_ref[...]
    top = SUBLANES + POOL_HALO
    n_rows = top + TS

    @pl.when(jnp.logical_and(pl.program_id(0) == 0, i == 0))
    def _():
        hb[...] = jnp.zeros(hb.shape, F32)
        pa[0:SUBLANES] = jnp.zeros((SUBLANES, pc), F32)
        pb[0:SUBLANES] = jnp.zeros((SUBLANES, pc), F32)

    hb[SUBLANES:top] = jnp.where(i > 0, hb[n_rows - POOL_HALO:n_rows], 0.0)
    hb[top:] = _rms(x_ref[...], nw)
    t = i * TS + lax.broadcasted_iota(jnp.int32, (TS, 1), 0)
    for g, w in enumerate(POOL_WINDOWS):
        cs = slice(g * pc, (g + 1) * pc)
        cur = hb[top:, cs]
        src, cols, step, dst = hb, cs, 1, pa
        while step < w:
            dst[SUBLANES:n_rows, :] = src[SUBLANES:n_rows, cols] + src[SUBLANES - step:n_rows - step, cols]
            src, cols, step, dst = dst, slice(None), 2 * step, (pb if dst is pa else pa)
        s = src[top:n_rows, cols]
        count = jnp.minimum(t + 1, w).astype(F32)
        d = s / count - cur
        mixed = _dot(d.astype(BF16), pw_ref[g])
        o_ref[:, cs] = x_ref[:, cs] + ps_ref[:, cs] * mixed


def _pool_layer(x, norm_w, pool_w, pool_scale, B, S, TS=512):
    T, D = x.shape
    assert len(POOL_WINDOWS) == pool_w.shape[0] and max(POOL_WINDOWS) - 1 <= POOL_HALO
    ns = S // TS
    pw = pool_w.astype(BF16)
    return pl.pallas_call(
        _pool_kernel,
        grid=(B, ns),
        in_specs=[pl.BlockSpec((TS, D), lambda b, i: (b * ns + i, 0)),
                  _resident((1, D)), _resident(pw.shape), _resident((1, D))],
        out_specs=pl.BlockSpec((TS, D), lambda b, i: (b * ns + i, 0)),
        out_shape=jax.ShapeDtypeStruct((T, D), F32),
        scratch_shapes=[pltpu.VMEM((SUBLANES + POOL_HALO + TS, D), F32),
                        pltpu.VMEM((SUBLANES + POOL_HALO + TS, D // len(POOL_WINDOWS)), F32),
                        pltpu.VMEM((SUBLANES + POOL_HALO + TS, D // len(POOL_WINDOWS)), F32)],
        compiler_params=_params("arbitrary", "arbitrary"),
    )(x, norm_w.reshape(1, D), pw, pool_scale.reshape(1, D))


def _chunk(S):
    return 256 if S % 256 == 0 else 128


def kernel(x, rel_bias, mix_norm_e, w_in_e, q_norm_e, k_norm_e, conv_w_e, conv_b_e, conv_ln_g_e, conv_ln_b_e,
           w_out_e, mix_norm_o, pool_w_o, pool_scale_o, ffn_norm, router_group_w, router_group_b,
           router_expert_w, router_expert_b, w_gate, w_up, w_down):
    B, S, D = x.shape
    T = B * S
    depth = ffn_norm.shape[0]
    n_heads = rel_bias.shape[1]
    idx_heads = (w_in_e.shape[2] - n_heads * HEAD_DIM - 2 * HEAD_DIM - IDX_DIM - 2 * conv_w_e.shape[2]) \
        // (IDX_DIM + 1)
    k_top = min(INDEX_TOPK, S // 4)
    C = _chunk(S)
    xf = x.reshape(T, D)
    bias_tiles, bias_stat = _rel_bias_tiles(rel_bias, C)
    expert_w = None
    for l in range(depth):
        i = l // 2
        if l % 2 == 0:
            q, k, v, iq, ik, iw, u = _in_proj(xf, mix_norm_e[i], w_in_e[i], q_norm_e[i], k_norm_e[i],
                                              n_heads, idx_heads, conv_w_e.shape[2])
            cast = (w_gate, w_up, w_down) if expert_w is None else ()
            if any(_cast_slabs(w, B * (S // C)) is None for w in cast):
                cast = ()
            attn, done = _attention(q, k, v, iq, ik, iw, bias_tiles, bias_stat, B, S, k_top, C, cast)
            expert_w = done if cast else expert_w
            conv = _conv_module(u, conv_w_e[i], conv_b_e[i], conv_ln_g_e[i], conv_ln_b_e[i], B, S)
            xf = _out_proj(xf, attn, conv, w_out_e[i])
        else:
            xf = _pool_layer(xf, mix_norm_o[i], pool_w_o[i], pool_scale_o[i], B, S)
        if expert_w is None:
            expert_w = [w.astype(BF16) for w in (w_gate, w_up, w_down)]
        xf = _moe(xf, ffn_norm[l], router_group_w[l], router_group_b[l], router_expert_w[l], router_expert_b[l],
                  *expert_w, layer=l)
    return xf.reshape(B, S, D)
```

```python
import functools
import math

import jax
import jax.numpy as jnp
from jax import lax
from jax.experimental import pallas as pl
from jax.experimental.pallas import tpu as pltpu

F32 = jnp.float32
BF16 = jnp.bfloat16

NORM_EPS = 1e-6
HEAD_DIM = 128
IDX_DIM = 64
INDEX_TOPK = 256
REL_BUCKETS = 32
REL_MAX_DIST = 128
POOL_WINDOWS = (2, 4, 8, 16)
EXPERTS_PER_GROUP = 4
LANES = 128
VMEM_LIMIT = 56 * 1024 * 1024
NEG = -1e30
INT_MIN = -(2 ** 31)


def _dot(a, b):
    return jnp.dot(a, b, preferred_element_type=F32)


def _dot_nt(a, b):
    return lax.dot_general(a, b, (((1,), (1,)), ((), ())), preferred_element_type=F32)


def _rms(x, w):
    return x * lax.rsqrt(jnp.mean(x * x, axis=-1, keepdims=True) + NORM_EPS) * w


def _params(*sem):
    return pltpu.CompilerParams(dimension_semantics=sem, vmem_limit_bytes=VMEM_LIMIT)


def _resident(shape):
    nd = len(shape)
    return pl.BlockSpec(shape, lambda *_: (0,) * nd, pipeline_mode=pl.Buffered(1))


def _in_proj_kernel(x_ref, nw_ref, qn_ref, kn_ref, wq_ref, wkv_ref, wiq_ref, wikw_ref, wa_ref, wg_ref,
                    q_ref, k_ref, v_ref, iq_ref, ik_ref, iw_ref, u_ref, *, q_scale, iw_scale):
    h = _rms(x_ref[...], nw_ref[...]).astype(BF16)
    n_pairs = q_ref.shape[0] // 2
    qn = qn_ref[...] * q_scale
    for c in range(n_pairs):
        qq = _dot_nt(h, wq_ref[c * 256:(c + 1) * 256, :])
        for s in range(2):
            qh = qq[:, s * HEAD_DIM:(s + 1) * HEAD_DIM]
            q_ref[2 * c + s] = _rms(qh, qn).astype(BF16)
    kv = _dot_nt(h, wkv_ref[...])
    k_ref[...] = _rms(kv[:, :HEAD_DIM], kn_ref[...]).astype(BF16)
    v_ref[...] = kv[:, HEAD_DIM:].astype(BF16)
    for c in range(iq_ref.shape[0] // 2):
        r = _dot_nt(h, wiq_ref[c * 256:(c + 1) * 256, :])
        iq_ref[2 * c] = r[:, :LANES].astype(BF16)
        iq_ref[2 * c + 1] = r[:, LANES:].astype(BF16)
    r = _dot_nt(h, wikw_ref[...])
    ik_ref[...] = r[:, :LANES].astype(BF16)
    iw_ref[...] = r[:, LANES:] * iw_scale
    for c in range(u_ref.shape[1] // 256):
        cs = slice(c * 256, (c + 1) * 256)
        a = _dot_nt(h, wa_ref[cs, :])
        g = _dot_nt(h, wg_ref[cs, :])
        u_ref[:, cs] = (a * jax.nn.sigmoid(g)).astype(BF16)


def _split_w_in_kernel(w_ref, wq_ref, wkv_ref, wiq_ref, wikw_ref, wa_ref, wg_ref, *, idx_heads):
    o = 0
    for ref in (wq_ref, wkv_ref, wiq_ref):
        n = ref.shape[0]
        ref[...] = w_ref[o:o + n, :].astype(BF16)
        o += n
    wik = w_ref[o:o + IDX_DIM, :]
    wiw = w_ref[o + IDX_DIM:o + IDX_DIM + idx_heads, :]
    o += IDX_DIM + idx_heads
    pad = jnp.zeros((LANES - idx_heads, wik.shape[1]), F32)
    wikw_ref[...] = jnp.concatenate([wik, wik, wiw, pad], axis=0).astype(BF16)
    for ref in (wa_ref, wg_ref):
        n = ref.shape[0]
        ref[...] = w_ref[o:o + n, :].astype(BF16)
        o += n


def _split_w_in(w_in, n_heads, idx_heads, conv_ch, tk=256):
    D, n_in = w_in.shape
    widths = (n_heads * HEAD_DIM, 2 * HEAD_DIM, idx_heads * IDX_DIM, 2 * LANES, conv_ch, conv_ch)
    assert n_in == sum(widths) - 2 * LANES + IDX_DIM + idx_heads and 2 * IDX_DIM == LANES and D % tk == 0
    return pl.pallas_call(
        functools.partial(_split_w_in_kernel, idx_heads=idx_heads),
        grid=(D // tk,),
        in_specs=[pl.BlockSpec((n_in, tk), lambda i: (0, i))],
        out_specs=[pl.BlockSpec((w, tk), lambda i: (0, i)) for w in widths],
        out_shape=[jax.ShapeDtypeStruct((w, D), BF16) for w in widths],
        compiler_params=_params("arbitrary"),
    )(w_in.T)


def _in_proj(x, norm_w, w_in, q_norm, k_norm, n_heads, idx_heads, conv_ch, tm=512):
    T, D = x.shape
    iq_w = idx_heads * IDX_DIM
    ws = _split_w_in(w_in, n_heads, idx_heads, conv_ch)
    row = lambda w: pl.BlockSpec((tm, w), lambda i: (i, 0))
    heads = lambda n: pl.BlockSpec((n, tm, LANES), lambda i: (0, i, 0))
    kern = functools.partial(_in_proj_kernel, q_scale=HEAD_DIM ** -0.5,
                             iw_scale=(idx_heads ** -0.5) * (IDX_DIM ** -0.5))
    return pl.pallas_call(
        kern,
        grid=(T // tm,),
        in_specs=[row(D), _resident((1, D)), _resident((1, HEAD_DIM)), _resident((1, HEAD_DIM))]
                 + [_resident(w.shape) for w in ws],
        out_specs=[heads(n_heads), row(HEAD_DIM), row(HEAD_DIM), heads(iq_w // LANES), row(LANES), row(LANES),
                   row(conv_ch)],
        out_shape=[jax.ShapeDtypeStruct((n_heads, T, HEAD_DIM), BF16),
                   jax.ShapeDtypeStruct((T, HEAD_DIM), BF16),
                   jax.ShapeDtypeStruct((T, HEAD_DIM), BF16),
                   jax.ShapeDtypeStruct((iq_w // LANES, T, LANES), BF16),
                   jax.ShapeDtypeStruct((T, LANES), BF16),
                   jax.ShapeDtypeStruct((T, LANES), F32),
                   jax.ShapeDtypeStruct((T, conv_ch), BF16)],
        compiler_params=_params("arbitrary"),
    )(x, norm_w.reshape(1, D), q_norm.reshape(1, HEAD_DIM), k_norm.reshape(1, HEAD_DIM), *ws)


def _rel_bias_kernel(rb_ref, o_ref, stat_ref):
    _, n_heads, C, _ = o_ref.shape
    tau = lax.broadcasted_iota(jnp.int32, (C, C), 0)
    sig = lax.broadcasted_iota(jnp.int32, (C, C), 1)
    max_exact = REL_BUCKETS // 2
    for kind in range(2):
        d = tau - sig + kind * C
        n = jnp.maximum(d, 0)
        nf = jnp.maximum(n, 1).astype(F32)
        large = max_exact + (jnp.log(nf / max_exact) / math.log(REL_MAX_DIST / max_exact)
                             * (REL_BUCKETS - max_exact)).astype(jnp.int32)
        large = jnp.minimum(large, REL_BUCKETS - 1)
        bucket = jnp.where(n < max_exact, n, large)
        for h in range(n_heads):
            b = jnp.zeros((C, C), F32)
            for bk in range(REL_BUCKETS):
                b = jnp.where(bucket == bk, rb_ref[bk, h], b)
            b = b - rb_ref[REL_BUCKETS - 1, h]
            if kind == 0:
                b = jnp.where(d < 0, NEG, b)
            o_ref[kind, h] = b
    for h in range(n_heads):
        hi = rb_ref[0, h]
        lo = rb_ref[0, h]
        for bk in range(1, REL_BUCKETS):
            hi = jnp.maximum(hi, rb_ref[bk, h])
            lo = jnp.minimum(lo, rb_ref[bk, h])
        stat_ref[0, h] = hi - rb_ref[REL_BUCKETS - 1, h]
        stat_ref[1, h] = lo - rb_ref[REL_BUCKETS - 1, h]


def _rel_bias_tiles(rel_bias, C):
    n_heads = rel_bias.shape[1]
    assert C >= REL_MAX_DIST
    return pl.pallas_call(
        _rel_bias_kernel,
        in_specs=[pl.BlockSpec(memory_space=pltpu.SMEM)],
        out_specs=[pl.BlockSpec(memory_space=pltpu.VMEM), pl.BlockSpec(memory_space=pltpu.SMEM)],
        out_shape=[jax.ShapeDtypeStruct((2, n_heads, C, C), F32), jax.ShapeDtypeStruct((2, n_heads), F32)],
        compiler_params=pltpu.CompilerParams(vmem_limit_bytes=VMEM_LIMIT),
    )(rel_bias)


SHIFT_SPAN_LIMIT = 60.0


TILE_GROUP = 8


def _grouped_loop(n, fn, group):
    def body(k, carry):
        for u in range(group):
            fn(group * k + u)
        return carry

    lax.fori_loop(0, n // group, body, 0)
    base = (n // group) * group
    size = group // 2
    while size >= 1:
        take = ((n - base) & size) != 0

        @pl.when(take)
        def _(base=base, size=size):
            for u in range(size):
                fn(base + u)
        base = base + jnp.where(take, size, 0)
        size //= 2


def _order_key(x):
    bits = pltpu.bitcast(x, jnp.int32)
    return bits ^ ((bits >> 31) & 0x7FFFFFFF)


def _row_to_col(row):
    C = row.shape[1]
    halves = []
    for part in (row >> 16, row & 0xFFFF):
        halves.append(jnp.broadcast_to(part.astype(F32), (LANES, C)).T[:, 0:1].astype(jnp.int32))
    return (halves[0] << 16) | halves[1]


def _attn_kernel(bstat_ref, q_ref, k_ref, v_ref, iq_ref, ik_ref, iw_ref, bias_ref, *rest, k_top, n_cast):
    cast_in, (o_ref, *cast_out) = rest[:n_cast], rest[n_cast:2 * n_cast + 1]
    key_sc, keyt_sc, keyt16_sc, iqm_sc, vx_sc, kmax_sc, shift_sc, acc_sc = rest[2 * n_cast + 1:]
    for src, dst in zip(cast_in, cast_out):
        dst[...] = src[...].astype(BF16)
    n_heads, C, _ = q_ref.shape
    idx_heads = iqm_sc.shape[0]
    idx_bits = max(1, (key_sc.shape[0] * C - 1).bit_length())
    qi = pl.program_id(1)
    nkv = qi + 1

    @pl.when(qi == 0)
    def _():
        vx_sc[:, :HEAD_DIM] = v_ref[...]
        vx_sc[:, HEAD_DIM:] = jnp.ones((vx_sc.shape[0], HEAD_DIM), BF16)
        kf = k_ref[...].astype(F32)
        k2 = jnp.sum(kf * kf, axis=-1, keepdims=True)
        kmax_sc[...] = jnp.broadcast_to(jnp.sqrt(jnp.max(k2, axis=0, keepdims=True)), kmax_sc.shape)

    lane = lax.broadcasted_iota(jnp.int32, (C, LANES), 1)
    for p in range(idx_heads // 2):
        qp = iq_ref[p].astype(F32)
        iqm_sc[2 * p] = jnp.where(lane < IDX_DIM, qp, 0.0).astype(BF16)
        iqm_sc[2 * p + 1] = jnp.where(lane >= IDX_DIM, qp, 0.0).astype(BF16)
    iw = iw_ref[...]
    tau = lax.broadcasted_iota(jnp.int32, (C, C), 0)
    sig = lax.broadcasted_iota(jnp.int32, (C, C), 1)

    def score_tile(j):
        off = pl.multiple_of(j * C, C)
        ikc = ik_ref[pl.ds(off, C), :]
        acc = jnp.zeros((C, C), F32)
        for hh in range(idx_heads):
            s = _dot_nt(iqm_sc[hh], ikc)
            acc = acc + jnp.maximum(s, 0.0) * iw[:, hh:hh + 1]
        acc = jnp.where(jnp.logical_and(j == qi, sig > tau), -jnp.inf, acc)
        key_sc[j] = _order_key(acc)
        kt = _order_key(acc.T)
        keyt_sc[j] = kt
        keyt16_sc[j] = (kt >> 16).astype(jnp.int16)

    _grouped_loop(nkv, score_tile, TILE_GROUP)

    SUB = 32

    def count(hit):
        def body(j, cnt):
            for r in range(C // SUB):
                cnt = cnt + hit(j, r)
            return cnt
        cnt = lax.fori_loop(0, nkv, body, jnp.zeros((SUB, C), F32))
        return jnp.sum(cnt, axis=0, keepdims=True)

    def keyt(j, r):
        return keyt_sc[j, pl.ds(r * SUB, SUB), :]

    def count16(cand16):
        one = jnp.ones((SUB, C), jnp.int16)
        zero = jnp.zeros((SUB, C), jnp.int16)

        def body(j, cnt):
            for r in range(C // SUB):
                cnt = cnt + jnp.where(keyt16_sc[j, pl.ds(r * SUB, SUB), :] >= cand16, one, zero)
            return cnt
        cnt = lax.fori_loop(0, nkv, body, zero)
        return jnp.sum(cnt.astype(jnp.int32).astype(F32), axis=0, keepdims=True)

    def bit16_body(b, carry):
        res, n_res = carry
        cand = res ^ lax.shift_left(jnp.int32(1), 31 - b)
        tot = count16((cand >> 16).astype(jnp.int16))
        ok = tot >= k_top
        return jnp.where(ok, cand, res), jnp.where(ok, tot, n_res)

    searched = nkv * C > k_top
    carry = (jnp.full((1, C), INT_MIN, jnp.int32), jnp.zeros((1, C), F32))
    top_row, n_top = lax.fori_loop(0, jnp.where(searched, 16, 0), bit16_body, carry)

    top16 = top_row >> 16

    def low_body(j, c):
        kt = keyt_sc[j]
        hi = kt >> 16
        lo = (kt & 0xFFFF) - 32768
        lo = jnp.where(hi == top16, lo, jnp.where(hi > top16, 32767, -32768))
        keyt16_sc[j] = lo.astype(jnp.int16)
        return c

    lax.fori_loop(0, jnp.where(searched, nkv, 0), low_body, 0)

    def bit16_low_body(b, carry):
        res, n_res = carry
        cand = res ^ lax.shift_left(jnp.int32(1), 31 - b)
        tot = count16(((cand & 0xFFFF) - 32768).astype(jnp.int16))
        ok = tot >= k_top
        return jnp.where(ok, cand, res), jnp.where(ok, tot, n_res)

    thr_row, n_thr = lax.fori_loop(16, jnp.where(searched, 32, 16), bit16_low_body, (top_row, n_top))
    thr = _row_to_col(thr_row)

    @pl.when(jnp.max(n_thr) > k_top)
    def _():
        need = k_top - count(lambda j, r: jnp.where(keyt(j, r) > thr_row, 1.0, 0.0))
        s_sub = lax.broadcasted_iota(jnp.int32, (SUB, C), 0)

        def idx_body(b, last):
            cand = last | lax.shift_left(jnp.int32(1), idx_bits - 1 - b)
            below = count(lambda j, r: jnp.where(
                keyt(j, r) == thr_row, jnp.where(j * C + r * SUB + s_sub < cand, 1.0, 0.0), 0.0))
            return jnp.where(below < need, cand, last)

        last = _row_to_col(lax.fori_loop(0, idx_bits, idx_body, jnp.zeros((1, C), jnp.int32)))

        def drop_body(j, carry):
            kk = key_sc[j]
            key_sc[j] = jnp.where(kk == thr, jnp.where(j * C + sig > last, kk - 1, kk), kk)
            return carry

        lax.fori_loop(0, nkv, drop_body, 0)

    kmax = kmax_sc[0:1, 0:1] * 1.001
    worst = jnp.zeros((C, 1), F32)
    for h in range(n_heads):
        qf = q_ref[h].astype(F32)
        bound = jnp.sqrt(jnp.sum(qf * qf, axis=-1, keepdims=True)) * kmax
        shift_sc[h] = jnp.broadcast_to(bound + bstat_ref[0, h], (C, LANES))
        worst = jnp.maximum(worst, 2.0 * bound + (bstat_ref[0, h] - bstat_ref[1, h]))
    loose = jnp.max(worst) > SHIFT_SPAN_LIMIT

    def logits(j, h, kind):
        off = pl.multiple_of(j * C, C)
        lg = _dot_nt(q_ref[h], k_ref[pl.ds(off, C), :])
        return lg if kind is None else lg + bias_ref[kind, h]

    def near_tiles(fn):
        @pl.when(qi > 0)
        def _():
            fn(qi - 1, 1)
            fn(qi, 0)

        @pl.when(qi == 0)
        def _():
            fn(qi, 0)

    def far_tiles(fn, group=TILE_GROUP):
        _grouped_loop(jnp.maximum(qi - 1, 0), lambda j: fn(j, None), group)

    @pl.when(loose)
    def _():
        for h in range(n_heads):
            shift_sc[h] = jnp.full((C, LANES), NEG, F32)

        def max_tile(j, kind):
            sel = key_sc[j] >= thr
            for h in range(n_heads):
                lg = jnp.where(sel, logits(j, h, kind), NEG)
                m = jnp.max(lg, axis=-1, keepdims=True)
                shift_sc[h] = jnp.maximum(shift_sc[h], jnp.broadcast_to(m, (C, LANES)))

        far_tiles(max_tile, group=1)
        near_tiles(max_tile)

    acc_sc[...] = jnp.zeros(acc_sc.shape, F32)

    def attn_tile(j, kind):
        off = pl.multiple_of(j * C, C)
        vx = vx_sc[pl.ds(off, C), :]
        sel = key_sc[j] >= thr
        for h in range(n_heads):
            sh = jnp.concatenate([shift_sc[h]] * (C // LANES), axis=1)
            p = jnp.where(sel, jnp.exp(logits(j, h, kind) - sh), 0.0)
            acc_sc[h] += _dot(p.astype(BF16), vx)

    far_tiles(attn_tile)
    near_tiles(attn_tile)
    for h in range(n_heads):
        a = acc_sc[h]
        o_ref[:, h * HEAD_DIM:(h + 1) * HEAD_DIM] = (a[:, :HEAD_DIM] / a[:, HEAD_DIM:]).astype(BF16)


CAST_SLAB_ELEMS = 1 << 20


def _cast_slabs(w, steps):
    rows = math.prod(w.shape[:-1])
    if rows % (steps * 16) != 0 or rows // steps * w.shape[-1] > CAST_SLAB_ELEMS:
        return None
    return w.reshape(steps, rows // steps, w.shape[-1])


def _attention(q, k, v, iq, ik, iw, bias_tiles, bias_stat, B, S, k_top, C, cast=()):
    n_heads, T, _ = q.shape
    n_pairs = iq.shape[0]
    nq = S // C
    slabs = [_cast_slabs(w, B * nq) for w in cast]
    kern = functools.partial(_attn_kernel, k_top=k_top, n_cast=len(slabs))
    heads = lambda n: pl.BlockSpec((n, C, LANES), lambda b, i: (0, b * nq + i, 0))
    seq = pl.BlockSpec((S, LANES), lambda b, i: (b, 0))
    slab_specs = [pl.BlockSpec((None,) + w.shape[1:], lambda b, i: (b * nq + i, 0, 0)) for w in slabs]
    outs = pl.pallas_call(
        kern,
        grid=(B, nq),
        in_specs=[pl.BlockSpec(memory_space=pltpu.SMEM),
                  heads(n_heads), seq, seq, heads(n_pairs), seq,
                  pl.BlockSpec((C, LANES), lambda b, i: (b * nq + i, 0)),
                  _resident(bias_tiles.shape)] + slab_specs,
        out_specs=[pl.BlockSpec((C, n_heads * HEAD_DIM), lambda b, i: (b * nq + i, 0))] + slab_specs,
        out_shape=[jax.ShapeDtypeStruct((T, n_heads * HEAD_DIM), BF16)]
                  + [jax.ShapeDtypeStruct(w.shape, BF16) for w in slabs],
        scratch_shapes=[pltpu.VMEM((nq, C, C), jnp.int32),
                        pltpu.VMEM((nq, C, C), jnp.int32),
                        pltpu.VMEM((nq, C, C), jnp.int16),
                        pltpu.VMEM((2 * n_pairs, C, LANES), BF16),
                        pltpu.VMEM((S, 2 * HEAD_DIM), BF16),
                        pltpu.VMEM((8, LANES), F32),
                        pltpu.VMEM((n_heads, C, LANES), F32),
                        pltpu.VMEM((n_heads, C, 2 * HEAD_DIM), F32)],
        compiler_params=_params("arbitrary", "arbitrary"),
    )(bias_stat, q, k, v, iq, ik, iw, bias_tiles, *slabs)
    return outs[0], [o.reshape(w.shape) for o, w in zip(outs[1:], cast)]


CONV_HALO = 32


SUBLANES = 8
CONV_SLAB = 64


def _conv_kernel(u_ref, w_ref, cb_ref, g_ref, b_ref, o_ref, cp_sc, y_sc, halo_sc, *, width):
    TS, CH = u_ref.shape
    n_rows = CONV_HALO + TS
    i = pl.program_id(1)

    @pl.when(jnp.logical_and(pl.program_id(0) == 0, i == 0))
    def _():
        halo_sc[...] = jnp.zeros(halo_sc.shape, F32)

    halo = jnp.where(i > 0, halo_sc[...], 0.0)
    halo_sc[...] = u_ref[TS - CONV_HALO:, :].astype(F32)
    pad = jnp.zeros((SUBLANES, LANES), F32)
    for c in range(CH // LANES):
        cs = slice(c * LANES, (c + 1) * LANES)
        col = jnp.concatenate([halo[:, cs], u_ref[:, cs].astype(F32), pad], axis=0)
        for r in range(SUBLANES):
            cp_sc[r, :, cs] = col[r:r + n_rows]

    base = CONV_HALO - (width - 1)
    phases = {}
    for j in range(width):
        q, r = divmod(base + j, SUBLANES)
        phases.setdefault(r, []).append((q, j))

    def slab(s, carry):
        t0 = pl.multiple_of(s * CONV_SLAB, CONV_SLAB)
        for c in range(CH // LANES):
            cs = slice(c * LANES, (c + 1) * LANES)
            acc = jnp.zeros((CONV_SLAB, LANES), F32)
            for r, taps in phases.items():
                q_lo, q_hi = taps[0][0], taps[-1][0]
                win = cp_sc[r, pl.ds(q_lo * SUBLANES + t0, CONV_SLAB + (q_hi - q_lo) * SUBLANES), cs]
                for q, j in taps:
                    off = (q - q_lo) * SUBLANES
                    acc = acc + w_ref[j:j + 1, cs] * win[off:off + CONV_SLAB]
            y_sc[pl.ds(t0, CONV_SLAB), cs] = acc + cb_ref[:, cs]
        return carry

    lax.fori_loop(0, TS // CONV_SLAB, slab, 0)
    y = y_sc[...]
    mu = jnp.mean(y, axis=-1, keepdims=True)
    yc = y - mu
    var = jnp.mean(yc * yc, axis=-1, keepdims=True)
    yn = yc * lax.rsqrt(var + NORM_EPS) * g_ref[...] + b_ref[...]
    o_ref[...] = (yn * jax.nn.sigmoid(yn)).astype(BF16)


def _conv_module(u, conv_w, conv_b, ln_g, ln_b, B, S, TS=512):
    T, CH = u.shape
    width = conv_w.shape[0]
    assert width - 1 <= CONV_HALO
    ns = S // TS
    wpad = jnp.zeros((CONV_HALO, CH), F32).at[:width].set(conv_w)
    vec = lambda a: a.reshape(1, CH)
    kern = functools.partial(_conv_kernel, width=width)
    return pl.pallas_call(
        kern,
        grid=(B, ns),
        in_specs=[pl.BlockSpec((TS, CH), lambda b, i: (b * ns + i, 0)),
                  _resident((CONV_HALO, CH)), _resident((1, CH)), _resident((1, CH)), _resident((1, CH))],
        out_specs=pl.BlockSpec((TS, CH), lambda b, i: (b * ns + i, 0)),
        out_shape=jax.ShapeDtypeStruct((T, CH), BF16),
        scratch_shapes=[pltpu.VMEM((SUBLANES, CONV_HALO + TS, CH), F32),
                        pltpu.VMEM((TS, CH), F32),
                        pltpu.VMEM((CONV_HALO, CH), F32)],
        compiler_params=_params("arbitrary", "arbitrary"),
    )(u, wpad, vec(conv_b), vec(ln_g), vec(ln_b))


def _out_proj_kernel(x_ref, a_ref, c_ref, wa_ref, wc_ref, o_ref):
    o_ref[...] = x_ref[...] + _dot(a_ref[...], wa_ref[...]) + _dot(c_ref[...], wc_ref[...])


def _out_proj(x, attn, conv, w_out, tm=512):
    T, D = x.shape
    aw, cw = attn.shape[1], conv.shape[1]
    wa = w_out[:aw].astype(BF16)
    wc = w_out[aw:].astype(BF16)
    row = lambda w: pl.BlockSpec((tm, w), lambda i: (i, 0))
    return pl.pallas_call(
        _out_proj_kernel,
        grid=(T // tm,),
        in_specs=[row(D), row(aw), row(cw), _resident(wa.shape), _resident(wc.shape)],
        out_specs=row(D),
        out_shape=jax.ShapeDtypeStruct((T, D), F32),
        compiler_params=_params("arbitrary"),
    )(x, attn, conv, wa, wc)


def _first_max(vals):
    m = vals[0]
    for v in vals[1:]:
        m = jnp.maximum(m, v)
    idx = jnp.full(m.shape, len(vals) - 1, jnp.int32)
    for k in range(len(vals) - 2, -1, -1):
        idx = jnp.where(vals[k] == m, k, idx)
    return m, idx


def _softmax_cols(cols):
    m = cols[0]
    for c in cols[1:]:
        m = jnp.maximum(m, c)
    e = [jnp.exp(c - m) for c in cols]
    s = e[0]
    for c in e[1:]:
        s = s + c
    return [c / s for c in e]


GROUP_ROWS = 8
MOE_CHUNK = 64
MOE_MAX_CHUNKS = 8


def _split3(a):
    hi = a.astype(BF16)
    r = a - hi.astype(F32)
    mid = r.astype(BF16)
    lo = (r - mid.astype(F32)).astype(BF16)
    return hi, mid, lo


PAIR_CLASSES = [(a, b) for a in range(EXPERTS_PER_GROUP) for b in range(a + 1, EXPERTS_PER_GROUP)]
CLASS_SPAN = [(min(c for c, p in enumerate(PAIR_CLASSES) if k in p), max(c for c, p in enumerate(PAIR_CLASSES) if k in p))
              for k in range(EXPERTS_PER_GROUP)]


W_SLOTS = 3
MOE_VMEM_LIMIT = 62 * 1024 * 1024


def _moe_kernel(x_ref, nw_ref, wrt_ref, brt_ref, wg_hbm, wu_hbm, wd_hbm, o_ref,
                hs_sc, p_sc, cs_sc, seg_sm, wg_sc, wu_sc, wd_sc, w_sem, *, n_groups, layer):
    TM, D = x_ref.shape
    e = pl.program_id(1)
    n_exp = pl.num_programs(1)
    step = pl.program_id(0) * n_exp + e
    n_steps = pl.num_programs(0) * n_exp

    def weight_copies(s):
        slot, expert = s % W_SLOTS, s % n_exp
        return [pltpu.make_async_copy(src.at[layer, expert], dst.at[slot], w_sem.at[k, slot])
                for k, (src, dst) in enumerate(((wg_hbm, wg_sc), (wu_hbm, wu_sc), (wd_hbm, wd_sc)))]

    @pl.when(step == 0)
    def _():
        for s in range(W_SLOTS - 1):
            for cp in weight_copies(s):
                cp.start()

    @pl.when(step + W_SLOTS - 1 < n_steps)
    def _():
        for cp in weight_copies(step + W_SLOTS - 1):
            cp.start()

    @pl.when(e == 0)
    def _():
        x = x_ref[...]
        scale = lax.rsqrt(jnp.mean(x * x, axis=-1, keepdims=True) + NORM_EPS)

        def h_cols(c0):
            cs = slice(c0, c0 + 512)
            return (x_ref[:, cs] * scale * nw_ref[:, cs]).astype(BF16)

        lg = brt_ref[...] + jnp.zeros((1, TM), F32)
        for c0 in range(0, D, 512):
            lg = lg + _dot_nt(wrt_ref[:, c0:c0 + 512], h_cols(c0))
        row = lambda k: lg[k:k + 1, :]
        g_prob = _softmax_cols([row(g) for g in range(n_groups)])
        g_p, g_idx = _first_max(g_prob)
        e_logit = []
        for k in range(EXPERTS_PER_GROUP):
            v = row(GROUP_ROWS + k)
            for g in range(1, n_groups):
                v = jnp.where(g_idx == g, row(GROUP_ROWS + g * EXPERTS_PER_GROUP + k), v)
            e_logit.append(v)
        e_prob = _softmax_cols(e_logit)
        p1, i1 = _first_max(e_prob)
        rest = [jnp.where(i1 == k, -1.0, e_prob[k]) for k in range(EXPERTS_PER_GROUP)]
        p2, i2 = _first_max(rest)
        den = p1 + p2
        base = g_idx * EXPERTS_PER_GROUP
        e_io = lax.broadcasted_iota(jnp.int32, (LANES, TM), 0)
        comb_t = (jnp.where(e_io == base + i1, g_p * (p1 / den), 0.0)
                  + jnp.where(e_io == base + i2, g_p * (p2 / den), 0.0))

        lo, hi = jnp.minimum(i1, i2), jnp.maximum(i1, i2)
        cls = jnp.zeros((1, TM), jnp.int32)
        for c, (a, b) in enumerate(PAIR_CLASSES):
            cls = jnp.where(lo * EXPERTS_PER_GROUP + hi == a * EXPERTS_PER_GROUP + b, c, cls)
        cat = g_idx * len(PAIR_CLASSES) + cls
        n_cat = n_groups * len(PAIR_CLASSES)
        cat_rows = seg_sm.shape[0]
        onehot_t = jnp.where(lax.broadcasted_iota(jnp.int32, (cat_rows, TM), 0) == cat, 1.0, 0.0)
        upper = jnp.where(lax.broadcasted_iota(jnp.int32, (LANES, LANES), 0)
                          < lax.broadcasted_iota(jnp.int32, (LANES, LANES), 1), 1.0, 0.0).astype(BF16)
        before = jnp.zeros((cat_rows, 1), F32)
        ranks = []
        for b0 in range(0, TM, LANES):
            blk = onehot_t[:, b0:b0 + LANES]
            ranks.append(_dot(blk.astype(BF16), upper) + before)
            before = before + jnp.sum(blk, axis=1, keepdims=True)
        rank = jnp.concatenate(ranks, axis=1)
        start = jnp.int32(0)
        pos = jnp.zeros((1, TM), F32)
        for c in range(n_cat):
            seg_sm[c] = start
            pos = jnp.where(cat == c, start.astype(F32) + rank[c:c + 1, :], pos)
            start = start + jnp.sum(onehot_t[c:c + 1, :]).astype(jnp.int32)
        seg_sm[n_cat] = start
        pos = pos.astype(jnp.int32)

        rb = 256
        for r0 in range(0, TM, rb):
            r_io = lax.broadcasted_iota(jnp.int32, (rb, TM), 0) + r0
            p_sc[r0:r0 + rb, :] = jnp.where(r_io == pos, 1.0, 0.0).astype(BF16)
        p = p_sc[...]
        for c0 in range(0, D, 512):
            hs_sc[:, c0:c0 + 512] = _dot(p, h_cols(c0)).astype(BF16)
        cs = jnp.zeros((TM, LANES), F32)
        for part in _split3(comb_t):
            cs = cs + _dot_nt(p, part)
        cs_sc[...] = cs
        o_ref[...] = jnp.zeros(o_ref.shape, F32)

    g = e // EXPERTS_PER_GROUP
    k = e % EXPERTS_PER_GROUP
    first, last = jnp.int32(CLASS_SPAN[0][0]), jnp.int32(CLASS_SPAN[0][1])
    for kk in range(1, EXPERTS_PER_GROUP):
        first = jnp.where(k == kk, CLASS_SPAN[kk][0], first)
        last = jnp.where(k == kk, CLASS_SPAN[kk][1], last)
    start = seg_sm[g * len(PAIR_CLASSES) + first]
    end = seg_sm[g * len(PAIR_CLASSES) + last + 1]
    c_lo = start // MOE_CHUNK
    c_hi = jnp.where(end > start, (end + MOE_CHUNK - 1) // MOE_CHUNK, c_lo)

    for cp in weight_copies(step):
        cp.wait()
    slot = step % W_SLOTS

    def expert_rows(c, n_chunks):
        m = n_chunks * MOE_CHUNK
        r0 = pl.multiple_of(c * MOE_CHUNK, MOE_CHUNK)
        rows = hs_sc[pl.ds(r0, m), :]
        lane = lax.broadcasted_iota(jnp.int32, (m, LANES), 1)
        w = jnp.sum(jnp.where(lane == e, cs_sc[pl.ds(r0, m), :], 0.0), axis=-1, keepdims=True)
        a = jax.nn.silu(_dot(rows, wg_sc[slot])) * _dot(rows, wu_sc[slot]) * w
        o_ref[pl.ds(r0, m), :] += _dot(a.astype(BF16), wd_sc[slot])

    def full(k, carry):
        expert_rows(c_lo + MOE_MAX_CHUNKS * k, MOE_MAX_CHUNKS)
        return carry

    n_chunks = c_hi - c_lo
    n_full = n_chunks // MOE_MAX_CHUNKS
    lax.fori_loop(0, n_full, full, 0)
    for rest in range(1, MOE_MAX_CHUNKS):
        @pl.when(n_chunks - n_full * MOE_MAX_CHUNKS == rest)
        def _(rest=rest):
            expert_rows(c_hi - rest, rest)

    @pl.when(e == n_exp - 1)
    def _():
        p = p_sc[...]
        for c0 in range(0, D, 512):
            ys = o_ref[:, c0:c0 + 512].astype(BF16)
            y = lax.dot_general(p, ys, (((0,), (0,)), ((), ())), preferred_element_type=F32)
            o_ref[:, c0:c0 + 512] = x_ref[:, c0:c0 + 512] + y


def _moe(x, norm_w, wg, bg, we, be, w_gate, w_up, w_down, layer, tm=1024):
    T, D = x.shape
    n_groups, n_exp = wg.shape[1], we.shape[1]
    ff = w_gate.shape[3]
    assert n_groups <= GROUP_ROWS and n_exp == n_groups * EXPERTS_PER_GROUP and T % tm == 0
    rows = GROUP_ROWS + n_exp
    cat_rows = -(-(n_groups * len(PAIR_CLASSES) + 1) // SUBLANES) * SUBLANES
    wrt = jnp.zeros((rows, D), F32).at[:n_groups].set(wg.T).at[GROUP_ROWS:].set(we.T).astype(BF16)
    brt = jnp.zeros((rows, 1), F32).at[:n_groups, 0].set(bg).at[GROUP_ROWS:, 0].set(be)
    tile = pl.BlockSpec((tm, D), lambda i, e: (i, 0))
    return pl.pallas_call(
        functools.partial(_moe_kernel, n_groups=n_groups, layer=layer),
        grid=(T // tm, n_exp),
        in_specs=[tile, _resident((1, D)), _resident((rows, D)), _resident((rows, 1)),
                  pl.BlockSpec(memory_space=pl.ANY), pl.BlockSpec(memory_space=pl.ANY),
                  pl.BlockSpec(memory_space=pl.ANY)],
        out_specs=tile,
        out_shape=jax.ShapeDtypeStruct((T, D), F32),
        scratch_shapes=[pltpu.VMEM((tm, D), BF16),
                        pltpu.VMEM((tm, tm), BF16),
                        pltpu.VMEM((tm, LANES), F32),
                        pltpu.SMEM((cat_rows,), jnp.int32),
                        pltpu.VMEM((W_SLOTS, D, ff), BF16),
                        pltpu.VMEM((W_SLOTS, D, ff), BF16),
                        pltpu.VMEM((W_SLOTS, ff, D), BF16),
                        pltpu.SemaphoreType.DMA((3, W_SLOTS))],
        compiler_params=pltpu.CompilerParams(dimension_semantics=("arbitrary", "arbitrary"),
                                             vmem_limit_bytes=MOE_VMEM_LIMIT),
    )(x, norm_w.reshape(1, D), wrt, brt, w_gate, w_up, w_down)


POOL_HALO = 16


def _pool_kernel(x_ref, nw_ref, pw_ref, ps_ref, o_ref, hb, pa, pb):
    TS, D = x_ref.shape
    n_groups, pc, _ = pw_ref.shape
    i = pl.program_id(1)
    nw = nw_ref[...]
    top = SUBLANES + POOL_HALO
    n_rows = top + TS

    @pl.when(jnp.logical_and(pl.program_id(0) == 0, i == 0))
    def _():
        hb[...] = jnp.zeros(hb.shape, F32)
        pa[0:SUBLANES] = jnp.zeros((SUBLANES, pc), F32)
        pb[0:SUBLANES] = jnp.zeros((SUBLANES, pc), F32)

    hb[SUBLANES:top] = jnp.where(i > 0, hb[n_rows - POOL_HALO:n_rows], 0.0)
    hb[top:] = _rms(x_ref[...], nw)
    t = i * TS + lax.broadcasted_iota(jnp.int32, (TS, 1), 0)
    for g, w in enumerate(POOL_WINDOWS):
        cs = slice(g * pc, (g + 1) * pc)
        cur = hb[top:, cs]
        src, cols, step, dst = hb, cs, 1, pa
        while step < w:
            dst[SUBLANES:n_rows, :] = src[SUBLANES:n_rows, cols] + src[SUBLANES - step:n_rows - step, cols]
            src, cols, step, dst = dst, slice(None), 2 * step, (pb if dst is pa else pa)
        s = src[top:n_rows, cols]
        count = jnp.minimum(t + 1, w).astype(F32)
        d = s / count - cur
        mixed = _dot(d.astype(BF16), pw_ref[g])
        o_ref[:, cs] = x_ref[:, cs] + ps_ref[:, cs] * mixed


def _pool_layer(x, norm_w, pool_w, pool_scale, B, S, TS=512):
    T, D = x.shape
    assert len(POOL_WINDOWS) == pool_w.shape[0] and max(POOL_WINDOWS) - 1 <= POOL_HALO
    ns = S // TS
    pw = pool_w.astype(BF16)
    return pl.pallas_call(
        _pool_kernel,
        grid=(B, ns),
        in_specs=[pl.BlockSpec((TS, D), lambda b, i: (b * ns + i, 0)),
                  _resident((1, D)), _resident(pw.shape), _resident((1, D))],
        out_specs=pl.BlockSpec((TS, D), lambda b, i: (b * ns + i, 0)),
        out_shape=jax.ShapeDtypeStruct((T, D), F32),
        scratch_shapes=[pltpu.VMEM((SUBLANES + POOL_HALO + TS, D), F32),
                        pltpu.VMEM((SUBLANES + POOL_HALO + TS, D // len(POOL_WINDOWS)), F32),
                        pltpu.VMEM((SUBLANES + POOL_HALO + TS, D // len(POOL_WINDOWS)), F32)],
        compiler_params=_params("arbitrary", "arbitrary"),
    )(x, norm_w.reshape(1, D), pw, pool_scale.reshape(1, D))


def _chunk(S):
    return 256 if S % 256 == 0 else 128


def kernel(x, rel_bias, mix_norm_e, w_in_e, q_norm_e, k_norm_e, conv_w_e, conv_b_e, conv_ln_g_e, conv_ln_b_e,
           w_out_e, mix_norm_o, pool_w_o, pool_scale_o, ffn_norm, router_group_w, router_group_b,
           router_expert_w, router_expert_b, w_gate, w_up, w_down):
    B, S, D = x.shape
    T = B * S
    depth = ffn_norm.shape[0]
    n_heads = rel_bias.shape[1]
    idx_heads = (w_in_e.shape[2] - n_heads * HEAD_DIM - 2 * HEAD_DIM - IDX_DIM - 2 * conv_w_e.shape[2]) \
        // (IDX_DIM + 1)
    k_top = min(INDEX_TOPK, S // 4)
    C = _chunk(S)
    xf = x.reshape(T, D)
    bias_tiles, bias_stat = _rel_bias_tiles(rel_bias, C)
    expert_w = None
    for l in range(depth):
        i = l // 2
        if l % 2 == 0:
            q, k, v, iq, ik, iw, u = _in_proj(xf, mix_norm_e[i], w_in_e[i], q_norm_e[i], k_norm_e[i],
                                              n_heads, idx_heads, conv_w_e.shape[2])
            cast = (w_gate, w_up, w_down) if expert_w is None else ()
            if any(_cast_slabs(w, B * (S // C)) is None for w in cast):
                cast = ()
            attn, done = _attention(q, k, v, iq, ik, iw, bias_tiles, bias_stat, B, S, k_top, C, cast)
            expert_w = done if cast else expert_w
            conv = _conv_module(u, conv_w_e[i], conv_b_e[i], conv_ln_g_e[i], conv_ln_b_e[i], B, S)
            xf = _out_proj(xf, attn, conv, w_out_e[i])
        else:
            xf = _pool_layer(xf, mix_norm_o[i], pool_w_o[i], pool_scale_o[i], B, S)
        if expert_w is None:
            expert_w = [w.astype(BF16) for w in (w_gate, w_up, w_down)]
        xf = _moe(xf, ffn_norm[l], router_group_w[l], router_group_b[l], router_expert_w[l], router_expert_b[l],
                  *expert_w, layer=l)
    return xf.reshape(B, S, D)
```
